```python
import math
import jax, jax.numpy as jnp
from jax import lax
import numpy as np

D_MODEL = 1024
BATCH = 2
SEQ = 8192
DEPTH = 1
DEC_BATCH = 8
DEC_SEQ = 32
PAST_LEN = 1024

CHUNK = 64
H_A = 4
HD_A = 64
W_A = H_A * 2 * HD_A
H_B = 8
HD_B = 64
W_B = H_B * HD_B
BAND_CHUNKS = 8
BAND_PAST = BAND_CHUNKS * CHUNK
REL_CLIP_B = 128
T5_BUCKETS = 32
T5_MAX_EXACT = 8
T5_MAX_DIST = 128
Q_BLOCK = 128
EPS = 1e-6
NEG = -1e30
IN_COLS = 4 * W_A + 4 * W_B + 2 * D_MODEL
SPLITS = [W_A, 2 * W_A, 3 * W_A, 4 * W_A, 4 * W_A + W_B, 4 * W_A + 2 * W_B, 4 * W_A + 3 * W_B, 4 * W_A + 4 * W_B]

kernel_name = 'hybrid_diffattn_chunkband_stream_step'


def rms_norm(x, g):
    xf = x.astype(jnp.float32)
    y = xf * lax.rsqrt(jnp.mean(xf * xf, axis=-1, keepdims=True) + EPS)
    return (y * g.astype(jnp.float32)).astype(x.dtype)


def t5_bucket(rel):
    half = T5_BUCKETS // 2
    ret = jnp.where(rel > 0, half, 0)
    n = jnp.abs(rel)
    nf = jnp.maximum(n, 1).astype(jnp.float32)
    large = T5_MAX_EXACT + (jnp.log(nf / T5_MAX_EXACT) / math.log(T5_MAX_DIST / T5_MAX_EXACT) * (half - T5_MAX_EXACT)).astype(jnp.int32)
    large = jnp.minimum(large, half - 1)
    return ret + jnp.where(n < T5_MAX_EXACT, n, large)


def diff_attn(q, k, v, qpos, kpos, t5_bias, lam, g_subln, lam_init):
    logits = jnp.einsum('bqhmd,bkhmd->bmhqk', q, k).astype(jnp.float32) * (HD_A ** -0.5)
    bias = jnp.transpose(t5_bias[t5_bucket(kpos[None, :] - qpos[:, None])], (2, 0, 1)).astype(jnp.float32)
    visible = (kpos[None, :] // CHUNK) <= (qpos[:, None] // CHUNK)
    p = jax.nn.softmax(jnp.where(visible, logits + bias, NEG), axis=-1)
    w = (p[:, 0] - lam * p[:, 1]).astype(v.dtype)
    o = jnp.einsum('bhqk,bkhe->bqhe', w, v)
    return rms_norm(o, g_subln) * (1.0 - lam_init)


def band_attn(q, k, v, qpos, kpos, rel_bias):
    logits = jnp.einsum('bnqhd,bnkhd->bnhqk', q, k).astype(jnp.float32) * (HD_B ** -0.5)
    rel = kpos[:, None, :] - qpos[:, :, None]
    bias = jnp.transpose(rel_bias[jnp.clip(rel, -REL_CLIP_B, REL_CLIP_B) + REL_CLIP_B], (0, 3, 1, 2)).astype(jnp.float32)
    qc = (qpos // CHUNK)[:, :, None]
    kc = (kpos // CHUNK)[:, None, :]
    visible = (kpos[:, None, :] >= 0) & (kc <= qc) & (kc >= qc - BAND_CHUNKS)
    p = jax.nn.softmax(jnp.where(visible[:, None], logits + bias, NEG), axis=-1)
    return jnp.einsum('bnhqk,bnkhd->bnqhd', p.astype(v.dtype), v)


def modulate_and_project(x, c, g_norm, w_ada, b_ada, w_in, g_qa, g_ka, g_qb, g_kb):
    B, T = x.shape[0], x.shape[1]
    mod = jax.nn.silu(c) @ w_ada + b_ada
    shift, scale, gate = jnp.split(mod, 3, axis=-1)
    h = rms_norm(x, g_norm) * (1.0 + scale[:, None]) + shift[:, None]
    qa, ka, va, ga, qb, kb, vb, gb, mg = jnp.split(h @ w_in, SPLITS, axis=-1)
    qa = rms_norm(qa.reshape(B, T, H_A, 2, HD_A), g_qa)
    ka = rms_norm(ka.reshape(B, T, H_A, 2, HD_A), g_ka)
    va = va.reshape(B, T, H_A, 2 * HD_A)
    qb = rms_norm(qb.reshape(B, T, H_B, HD_B), g_qb)
    kb = rms_norm(kb.reshape(B, T, H_B, HD_B), g_kb)
    vb = vb.reshape(B, T, H_B, HD_B)
    return gate, qa, ka, va, ga, qb, kb, vb, gb, mg


def merge_output(x, gate, oa, ga, ob, gb, mg, w_oa, w_ob, w_out):
    ya = (oa * jax.nn.silu(ga)) @ w_oa
    yb = (ob * jax.nn.silu(gb)) @ w_ob
    mga, mgb = jnp.split(mg, 2, axis=-1)
    m = jax.nn.sigmoid(mga) * ya + jax.nn.sigmoid(mgb) * yb
    return x + gate[:, None] * (m @ w_out)


def diff_attn_prompt(qa, ka, va, t5_bias, lam, g_subln, lam_init):
    B, S = qa.shape[0], qa.shape[1]
    nb = S // Q_BLOCK
    qblocks = jnp.moveaxis(qa.reshape(B, nb, Q_BLOCK, H_A, 2, HD_A), 1, 0)
    kpos = jnp.arange(S)

    def one_block(args):
        qi, i = args
        qpos = i * Q_BLOCK + jnp.arange(Q_BLOCK)
        return diff_attn(qi, ka, va, qpos, kpos, t5_bias, lam, g_subln, lam_init)

    o = lax.map(one_block, (qblocks, jnp.arange(nb)))
    return jnp.moveaxis(o, 0, 1).reshape(B, S, W_A)


def band_attn_prompt(qb, kb, vb, rel_bias):
    B, S = qb.shape[0], qb.shape[1]
    nc = S // CHUNK
    pad = ((0, 0), (BAND_CHUNKS, 0), (0, 0), (0, 0), (0, 0))
    kp = jnp.pad(kb.reshape(B, nc, CHUNK, H_B, HD_B), pad)
    vp = jnp.pad(vb.reshape(B, nc, CHUNK, H_B, HD_B), pad)
    idx = jnp.arange(nc)[:, None] + jnp.arange(BAND_CHUNKS + 1)[None, :]
    band_len = (BAND_CHUNKS + 1) * CHUNK
    kband = kp[:, idx].reshape(B, nc, band_len, H_B, HD_B)
    vband = vp[:, idx].reshape(B, nc, band_len, H_B, HD_B)
    kpos = ((idx - BAND_CHUNKS)[:, :, None] * CHUNK + jnp.arange(CHUNK)[None, None, :]).reshape(nc, band_len)
    qpos = jnp.arange(nc)[:, None] * CHUNK + jnp.arange(CHUNK)[None, :]
    o = band_attn(qb.reshape(B, nc, CHUNK, H_B, HD_B), kband, vband, qpos, kpos, rel_bias)
    return o.reshape(B, S, W_B)


def setup_inputs(seed: int = 0) -> dict:
    key = jax.random.key(seed)
    ks = jax.random.split(key, 32)
    f32 = jnp.float32
    nrm = lambda k, shp, s: jax.random.normal(k, shp, f32) * s
    lb = min(BAND_PAST, PAST_LEN)
    return {
        'x_prompt': nrm(ks[0], (BATCH, SEQ, D_MODEL), 1.0),
        'x_sample': nrm(ks[1], (DEC_BATCH, DEC_SEQ, D_MODEL), 1.0),
        'cache_a_k': nrm(ks[2], (DEPTH, DEC_BATCH, PAST_LEN, H_A, 2 * HD_A), 1.0),
        'cache_a_v': nrm(ks[3], (DEPTH, DEC_BATCH, PAST_LEN, H_A, 2 * HD_A), 1.0),
        'cache_b_k': nrm(ks[4], (DEPTH, DEC_BATCH, lb, H_B, HD_B), 1.0),
        'cache_b_v': nrm(ks[5], (DEPTH, DEC_BATCH, lb, H_B, HD_B), 1.0),
        'c_prompt': nrm(ks[6], (BATCH, D_MODEL), 1.0),
        'c_sample': nrm(ks[7], (DEC_BATCH, D_MODEL), 1.0),
        'g_norm': 1.0 + nrm(ks[8], (DEPTH, D_MODEL), 0.01),
        'w_ada': nrm(ks[9], (DEPTH, D_MODEL, 3 * D_MODEL), 0.5 * D_MODEL ** -0.5),
        'b_ada': nrm(ks[10], (DEPTH, 3 * D_MODEL), 0.01),
        'w_in': nrm(ks[11], (DEPTH, D_MODEL, IN_COLS), D_MODEL ** -0.5),
        'g_qa': 1.0 + nrm(ks[12], (DEPTH, HD_A), 0.01),
        'g_ka': 1.0 + nrm(ks[13], (DEPTH, HD_A), 0.01),
        'lam_q1': nrm(ks[14], (DEPTH, HD_A), 0.1),
        'lam_k1': nrm(ks[15], (DEPTH, HD_A), 0.1),
        'lam_q2': nrm(ks[16], (DEPTH, HD_A), 0.1),
        'lam_k2': nrm(ks[17], (DEPTH, HD_A), 0.1),
        'g_subln': 1.0 + nrm(ks[18], (DEPTH, 2 * HD_A), 0.01),
        't5_bias': nrm(ks[19], (T5_BUCKETS, H_A), 0.1),
        'g_qb': 1.0 + nrm(ks[20], (DEPTH, HD_B), 0.01),
        'g_kb': 1.0 + nrm(ks[21], (DEPTH, HD_B), 0.01),
        'rel_bias_b': nrm(ks[22], (DEPTH, 2 * REL_CLIP_B + 1, H_B), 0.1),
        'w_oa': nrm(ks[23], (DEPTH, W_A, D_MODEL), W_A ** -0.5),
        'w_ob': nrm(ks[24], (DEPTH, W_B, D_MODEL), W_B ** -0.5),
        'w_out': nrm(ks[25], (DEPTH, D_MODEL, D_MODEL), D_MODEL ** -0.5),
    }


def reference(x_prompt, x_sample, cache_a_k, cache_a_v, cache_b_k, cache_b_v, c_prompt, c_sample, g_norm, w_ada, b_ada, w_in, g_qa, g_ka, lam_q1, lam_k1, lam_q2, lam_k2, g_subln, t5_bias, g_qb, g_kb, rel_bias_b, w_oa, w_ob, w_out):
    xp, xs = x_prompt, x_sample
    BP, S = xp.shape[0], xp.shape[1]
    BS, T = xs.shape[0], xs.shape[1]
    past_len = cache_a_k.shape[2]
    n_keep = min(BAND_PAST, S)
    akp, avp, bkp, bvp, aks, avs, bks, bvs = [], [], [], [], [], [], [], []
    for l in range(DEPTH):
        lam_init = 0.8 - 0.6 * math.exp(-0.3 * l)
        lam = (jnp.exp(jnp.sum((lam_q1[l] * lam_k1[l]).astype(jnp.float32)))
               - jnp.exp(jnp.sum((lam_q2[l] * lam_k2[l]).astype(jnp.float32))) + lam_init)
        proj = (g_norm[l], w_ada[l], b_ada[l], w_in[l], g_qa[l], g_ka[l], g_qb[l], g_kb[l])
        outp = (w_oa[l], w_ob[l], w_out[l])

        gate, qa, ka, va, ga, qb, kb, vb, gb, mg = modulate_and_project(xp, c_prompt, *proj)
        oa = diff_attn_prompt(qa, ka, va, t5_bias, lam, g_subln[l], lam_init)
        ob = band_attn_prompt(qb, kb, vb, rel_bias_b[l])
        xp = merge_output(xp, gate, oa, ga, ob, gb, mg, *outp)
        akp.append(ka.reshape(BP, S, H_A, 2 * HD_A))
        avp.append(va)
        bkp.append(kb[:, S - n_keep:])
        bvp.append(vb[:, S - n_keep:])

        gate, qa, ka, va, ga, qb, kb, vb, gb, mg = modulate_and_project(xs, c_sample, *proj)
        k_all = jnp.concatenate([cache_a_k[l].reshape(BS, past_len, H_A, 2, HD_A), ka], axis=1)
        v_all = jnp.concatenate([cache_a_v[l], va], axis=1)
        qpos = past_len + jnp.arange(T)
        kpos = jnp.arange(past_len + T)
        oa = diff_attn(qa, k_all, v_all, qpos, kpos, t5_bias, lam, g_subln[l], lam_init).reshape(BS, T, W_A)
        lb = cache_b_k.shape[2]
        kb_all = jnp.concatenate([cache_b_k[l], kb], axis=1)[:, None]
        vb_all = jnp.concatenate([cache_b_v[l], vb], axis=1)[:, None]
        kpos_b = jnp.concatenate([past_len - lb + jnp.arange(lb), qpos])[None]
        ob = band_attn(qb[:, None], kb_all, vb_all, qpos[None], kpos_b, rel_bias_b[l]).reshape(BS, T, W_B)
        xs = merge_output(xs, gate, oa, ga, ob, gb, mg, *outp)
        aks.append(ka.reshape(BS, T, H_A, 2 * HD_A))
        avs.append(va)
        bks.append(kb)
        bvs.append(vb)

    return (xp, xs, jnp.stack(akp), jnp.stack(avp), jnp.stack(bkp), jnp.stack(bvp), jnp.stack(aks), jnp.stack(avs), jnp.stack(bks), jnp.stack(bvs))
```

```python
import functools
import math

import jax
import jax.numpy as jnp
from jax import lax
from jax.experimental import pallas as pl
from jax.experimental.pallas import tpu as pltpu

CHUNK = 64
H_A = 4
HD_A = 64
W_A = H_A * 2 * HD_A
H_B = 8
HD_B = 64
W_B = H_B * HD_B
BAND_CHUNKS = 8
BAND_PAST = BAND_CHUNKS * CHUNK
REL_CLIP_B = 128
T5_BUCKETS = 32
T5_MAX_EXACT = 8
T5_MAX_DIST = 128
EPS = 1e-6
NEG = -1e30

LANES = 128
SEG = 512
NORM_GROUP = 256
VMEM_LIMIT = 56 * 1024 * 1024

F32 = jnp.float32
BF16 = jnp.bfloat16


def _t5_bucket(rel):
    half = T5_BUCKETS // 2
    ret = jnp.where(rel > 0, half, 0)
    n = jnp.abs(rel)
    nf = jnp.maximum(n, 1).astype(jnp.float32)
    large = T5_MAX_EXACT + (jnp.log(nf / T5_MAX_EXACT) / math.log(T5_MAX_DIST / T5_MAX_EXACT) * (half - T5_MAX_EXACT)).astype(jnp.int32)
    large = jnp.minimum(large, half - 1)
    return ret + jnp.where(n < T5_MAX_EXACT, n, large)


def _far_bucket(n):
    half = T5_BUCKETS // 2
    return min(T5_MAX_EXACT + int(math.log(n / T5_MAX_EXACT) / math.log(T5_MAX_DIST / T5_MAX_EXACT) * (half - T5_MAX_EXACT)), half - 1)


def _nt(a, b):
    return lax.dot_general(a, b, (((1,), (1,)), ((), ())), preferred_element_type=F32)


def _silu(x):
    return x * (1.0 / (1.0 + jnp.exp(-x)))


def _sigmoid(x):
    return 1.0 / (1.0 + jnp.exp(-x))


def _mod_kernel(c_ref, w_ref, b_ref, o_ref):
    c = c_ref[...]
    o_ref[...] = jnp.dot(_silu(c), w_ref[...], preferred_element_type=F32,
                         precision=lax.Precision.HIGHEST) + b_ref[...]


def _modulation(c_all, w_ada, b_ada):
    rows, d = c_all.shape
    n_out = w_ada.shape[1]
    tn = d
    return pl.pallas_call(
        _mod_kernel,
        grid=(n_out // tn,),
        in_specs=[pl.BlockSpec((rows, d), lambda j: (0, 0)),
                  pl.BlockSpec((d, tn), lambda j: (0, j)),
                  pl.BlockSpec((1, tn), lambda j: (0, j))],
        out_specs=pl.BlockSpec((rows, tn), lambda j: (0, j)),
        out_shape=jax.ShapeDtypeStruct((rows, n_out), F32),
        name="adaln_mod",
    )(c_all, w_ada, b_ada.reshape(1, n_out))


def _proj_kernel(x_ref, shift_ref, scale_ref, gn_ref, w_ref, gqa_ref, gka_ref, gqb_ref, gkb_ref, pm_ref,
                 qa_ref, ka32_ref, kab_ref, va32_ref, vab_ref, ga_ref,
                 qb_ref, kb32_ref, kbb_ref, vb32_ref, vbb_ref, gb_ref, mg_ref):
    nb, t, d = x_ref.shape
    rows = nb * t
    x = x_ref[...]
    ms = jnp.mean(x * x, axis=-1, keepdims=True)
    xn = x * lax.rsqrt(ms + EPS) * gn_ref[...]
    h = xn * (1.0 + scale_ref[...]) + shift_ref[...]
    hb = h.reshape(rows, d).astype(BF16)

    def seg(c, width=SEG):
        return jnp.dot(hb, w_ref[:, c * SEG:c * SEG + width], preferred_element_type=F32)

    def head_norm(y, g_ref):
        sq = y * y
        hi = sq.astype(BF16)
        lo = (sq - hi.astype(F32)).astype(BF16)
        pm = pm_ref[...]
        parts = []
        for c in range(SEG // NORM_GROUP):
            sl = slice(c * NORM_GROUP, (c + 1) * NORM_GROUP)
            gms = jnp.dot(hi[:, sl], pm, preferred_element_type=F32) + jnp.dot(lo[:, sl], pm, preferred_element_type=F32)
            parts.append(y[:, sl] * lax.rsqrt(gms + EPS))
        return jnp.concatenate(parts, axis=1) * g_ref[...]

    def put(ref, y):
        ref[...] = y.astype(ref.dtype).reshape(ref.shape)

    put(qa_ref, head_norm(seg(0), gqa_ref) * (HD_A ** -0.5))
    ka = head_norm(seg(1), gka_ref)
    put(ka32_ref, ka)
    put(kab_ref, ka)
    va = seg(2)
    put(va32_ref, va)
    put(vab_ref, va)
    put(ga_ref, seg(3))
    put(qb_ref, head_norm(seg(4), gqb_ref) * (HD_B ** -0.5))
    kb = head_norm(seg(5), gkb_ref)
    put(kb32_ref, kb)
    put(kbb_ref, kb)
    vb = seg(6)
    put(vb32_ref, vb)
    put(vbb_ref, vb)
    put(gb_ref, seg(7))
    for c in range(8, 12):
        mg_ref[:, :, (c - 8) * SEG:(c - 7) * SEG] = seg(c).astype(mg_ref.dtype).reshape(nb, t, SEG)


def _project(x, shift, scale, g_norm, w_in_bf, gqa, gka, gqb, gkb, pm, nb, t):
    bx, sx, d = x.shape
    n_cols = w_in_bf.shape[1]
    grid = (bx // nb, sx // t)
    tok = lambda w: pl.BlockSpec((nb, t, w), lambda b, i: (b, i, 0))
    per_b = pl.BlockSpec((nb, 1, d), lambda b, i: (b, 0, 0))
    const = lambda shp: pl.BlockSpec(shp, lambda b, i: (0,) * len(shp))
    sds = lambda w, dt: jax.ShapeDtypeStruct((bx, sx, w), dt)
    out_shape = (sds(SEG, BF16), sds(SEG, F32), sds(SEG, BF16), sds(SEG, F32), sds(SEG, BF16), sds(SEG, BF16),
                 sds(SEG, BF16), sds(SEG, F32), sds(SEG, BF16), sds(SEG, F32), sds(SEG, BF16), sds(SEG, BF16),
                 sds(4 * SEG, BF16))
    out_specs = tuple(tok(s.shape[-1]) for s in out_shape)
    return pl.pallas_call(
        _proj_kernel,
        grid=grid,
        in_specs=[tok(d), per_b, per_b, const((1, d)),
                  pl.BlockSpec((d, n_cols), lambda b, i: (0, 0), pipeline_mode=pl.Buffered(1)),
                  const((1, SEG)), const((1, SEG)), const((1, SEG)), const((1, SEG)),
                  const((NORM_GROUP, NORM_GROUP))],
        out_specs=out_specs,
        out_shape=out_shape,
        compiler_params=pltpu.CompilerParams(dimension_semantics=("arbitrary", "arbitrary"),
                                             vmem_limit_bytes=VMEM_LIMIT),
        name="in_proj",
    )(x, shift, scale, g_norm.reshape(1, d), w_in_bf, gqa, gka, gqb, gkb, pm)


def _bias_tile_kernel(gen_ref, o_ref, *, tq, nk, qbase, kbase0, kstep, band, nvalid):
    v = pl.program_id(1)
    g = gen_ref[0, 0]
    x = jnp.broadcast_to(g, (tq, g.shape[-1]))
    y = pltpu.roll(x, 0, 1, stride=1, stride_axis=0)
    y = y[:, :nk]
    row = lax.broadcasted_iota(jnp.int32, (tq, nk), 0)
    col = lax.broadcasted_iota(jnp.int32, (tq, nk), 1)
    qc = (qbase + row) // CHUNK
    kc = (kbase0 - v * kstep + col) // CHUNK
    y = jnp.where(kc <= qc, y, NEG)
    if band:
        y = jnp.where(kc >= qc - BAND_CHUNKS, y, NEG)
    y = jnp.where(col < nvalid, y, NEG)
    o_ref[0, 0] = y


def _bias_tiles(table_fn, n_heads, tq, nk, n_var, qbase, kbase0, kstep, band, nvalid):
    length = 1 << (tq + nk - 1).bit_length()
    assert length >= tq + nk - 1 and min(kbase0 - (n_var - 1) * kstep, qbase) >= 0
    r = jnp.arange(length, dtype=jnp.int32)
    d = jnp.where(r < nk, r, r - length)
    kbase = kbase0 - kstep * jnp.arange(n_var, dtype=jnp.int32)
    rel = (kbase[:, None] - qbase) + d[None, :]
    gen = jnp.transpose(table_fn(rel), (2, 0, 1)).astype(F32)
    gen = gen.reshape(n_heads, n_var, 1, length)
    return pl.pallas_call(
        functools.partial(_bias_tile_kernel, tq=tq, nk=nk, qbase=qbase, kbase0=kbase0, kstep=kstep,
                          band=band, nvalid=nvalid),
        grid=(n_heads, n_var),
        in_specs=[pl.BlockSpec((1, 1, 1, length), lambda h, v: (h, v, 0, 0))],
        out_specs=pl.BlockSpec((1, 1, tq, nk), lambda h, v: (h, v, 0, 0)),
        out_shape=jax.ShapeDtypeStruct((n_heads, n_var, tq, nk), F32),
        name="bias_tiles",
    )(gen)


def _split_halves(q):
    lane = lax.broadcasted_iota(jnp.int32, q.shape, 1)
    zero = jnp.zeros_like(q)
    return jnp.where(lane < LANES // 2, q, zero), jnp.where(lane >= LANES // 2, q, zero)


def _online_update(carry, s, v):
    m, l, acc = carry
    m_new = jnp.maximum(m, jnp.max(s, axis=-1, keepdims=True))
    alpha = jnp.exp(m - m_new)
    p = jnp.exp(s - m_new)
    l = alpha * l + jnp.sum(p, axis=-1, keepdims=True)
    acc = alpha * acc + jnp.dot(p.astype(BF16), v, preferred_element_type=F32)
    return m_new, l, acc


def _diff_init(tq):
    return (jnp.full((tq, 1), -jnp.inf, F32), jnp.zeros((tq, 1), F32), jnp.zeros((tq, LANES), F32))


def _diff_finish(c0, c1, lamv_ref, gsub_ref, g, lam_init):
    lamv = lamv_ref[...]
    e1 = jnp.exp(jnp.sum(lamv[0:1] * lamv[1:2], axis=-1, keepdims=True))
    e2 = jnp.exp(jnp.sum(lamv[2:3] * lamv[3:4], axis=-1, keepdims=True))
    lam = e1 - e2 + lam_init
    o = c0[2] / c0[1] - lam * (c1[2] / c1[1])
    o = o * lax.rsqrt(jnp.mean(o * o, axis=-1, keepdims=True) + EPS) * gsub_ref[...]
    o = o * (1.0 - lam_init)
    return o * _silu(g.astype(F32))


def _attn_a_prompt_kernel(cfar_ref, q_ref, k_ref, v_ref, g_ref, bias_ref, lamv_ref, gsub_ref, o_ref, *, tq, n_var, lam_init):
    h = pl.program_id(1)
    i = pl.program_id(2)
    q0, q1 = _split_halves(q_ref[0])
    cfar = cfar_ref[h]
    n_far = jnp.maximum(i - (n_var - 1), 0)

    def far(j, carry):
        c0, c1 = carry
        start = pl.multiple_of(j * tq, tq)
        k = k_ref[0, pl.ds(start, tq), :]
        v = v_ref[0, pl.ds(start, tq), :]
        return _online_update(c0, _nt(q0, k) + cfar, v), _online_update(c1, _nt(q1, k) + cfar, v)

    c0, c1 = lax.fori_loop(0, n_far, far, (_diff_init(tq), _diff_init(tq)))
    ws = pl.multiple_of(n_far * tq, tq)
    k = k_ref[0, pl.ds(ws, n_var * tq), :]
    v = v_ref[0, pl.ds(ws, n_var * tq), :]
    bias = bias_ref[0, 0]
    c0 = _online_update(c0, _nt(q0, k) + bias, v)
    c1 = _online_update(c1, _nt(q1, k) + bias, v)
    o_ref[0] = _diff_finish(c0, c1, lamv_ref, gsub_ref, g_ref[0], lam_init).astype(o_ref.dtype)


def _attn_a_prompt(qa, kab, vab, ga, bias, cfar, lamv, gsub, tq, lam_init):
    b, s, _ = qa.shape
    n_var = bias.shape[1]
    assert bias.shape[3] == n_var * tq
    tile = pl.BlockSpec((1, tq, LANES), lambda bi, h, i: (bi, i, h))
    full = pl.BlockSpec((1, s, LANES), lambda bi, h, i: (bi, 0, h))
    return pl.pallas_call(
        functools.partial(_attn_a_prompt_kernel, tq=tq, n_var=n_var, lam_init=lam_init),
        grid=(b, H_A, s // tq),
        in_specs=[pl.BlockSpec(memory_space=pltpu.SMEM), tile, full, full, tile,
                  pl.BlockSpec((1, 1, tq, n_var * tq), lambda bi, h, i: (h, jnp.minimum(i, n_var - 1), 0, 0)),
                  pl.BlockSpec((4, HD_A), lambda bi, h, i: (0, 0)),
                  pl.BlockSpec((1, LANES), lambda bi, h, i: (0, 0))],
        out_specs=tile,
        out_shape=jax.ShapeDtypeStruct((b, s, W_A), BF16),
        compiler_params=pltpu.CompilerParams(dimension_semantics=("arbitrary", "arbitrary", "arbitrary"),
                                             vmem_limit_bytes=VMEM_LIMIT),
        name="diff_attn_prompt",
    )(cfar, qa, kab, vab, ga, bias, lamv, gsub)


def _window_kv(cache_ref, new_ref, nk):
    past = cache_ref.shape[1]
    t = new_ref.shape[1]
    pad = jnp.zeros((nk - past - t, LANES), BF16)
    return jnp.concatenate([cache_ref[0].astype(BF16), new_ref[0], pad], axis=0)


def _attn_a_sample_kernel(q_ref, kc_ref, vc_ref, kn_ref, vn_ref, g_ref, bias_ref, lamv_ref, gsub_ref, o_ref, *, lam_init):
    tq = q_ref.shape[1]
    nk = bias_ref.shape[3]
    q0, q1 = _split_halves(q_ref[0])
    k = _window_kv(kc_ref, kn_ref, nk)
    v = _window_kv(vc_ref, vn_ref, nk)
    bias = bias_ref[0, 0]
    c0 = _online_update(_diff_init(tq), _nt(q0, k) + bias, v)
    c1 = _online_update(_diff_init(tq), _nt(q1, k) + bias, v)
    o_ref[0] = _diff_finish(c0, c1, lamv_ref, gsub_ref, g_ref[0], lam_init).astype(o_ref.dtype)


def _attn_a_sample(qa, kab, vab, ga, cache_k, cache_v, bias, lamv, gsub, lam_init):
    b, t, _ = qa.shape
    past = cache_k.shape[1]
    nk = bias.shape[3]
    new = pl.BlockSpec((1, t, LANES), lambda bi, h: (bi, 0, h))
    old = pl.BlockSpec((1, past, LANES), lambda bi, h: (bi, 0, h))
    return pl.pallas_call(
        functools.partial(_attn_a_sample_kernel, lam_init=lam_init),
        grid=(b, H_A),
        in_specs=[new, old, old, new, new, new,
                  pl.BlockSpec((1, 1, t, nk), lambda bi, h: (h, 0, 0, 0)),
                  pl.BlockSpec((4, HD_A), lambda bi, h: (0, 0)),
                  pl.BlockSpec((1, LANES), lambda bi, h: (0, 0))],
        out_specs=new,
        out_shape=jax.ShapeDtypeStruct((b, t, W_A), BF16),
        name="diff_attn_sample",
    )(qa, cache_k, cache_v, kab, vab, ga, bias, lamv, gsub)


def _band_pair(q, k, v, g, bias_ref):
    q0, q1 = _split_halves(q)
    v0, v1 = _split_halves(v)
    o = None
    for hh, (qh, vh) in enumerate(((q0, v0), (q1, v1))):
        s = _nt(qh, k) + bias_ref[hh, 0]
        m = jnp.max(s, axis=-1, keepdims=True)
        p = jnp.exp(s - m)
        l = jnp.sum(p, axis=-1, keepdims=True)
        oh = jnp.dot(p.astype(BF16), vh, preferred_element_type=F32) / l
        o = oh if o is None else o + oh
    return o * _silu(g.astype(F32))


def _attn_b_prompt_kernel(q_ref, k_ref, v_ref, g_ref, bias_ref, o_ref, *, tq, n_var):
    i = pl.program_id(2)
    ws = pl.multiple_of(jnp.maximum(i - (n_var - 1), 0) * tq, tq)
    k = k_ref[0, pl.ds(ws, n_var * tq), :]
    v = v_ref[0, pl.ds(ws, n_var * tq), :]
    o_ref[0] = _band_pair(q_ref[0], k, v, g_ref[0], bias_ref).astype(o_ref.dtype)


def _attn_b_prompt(qb, kbb, vbb, gb, bias, tq):
    b, s, _ = qb.shape
    n_var = bias.shape[1]
    assert bias.shape[3] == n_var * tq
    tile = pl.BlockSpec((1, tq, LANES), lambda bi, p, i: (bi, i, p))
    full = pl.BlockSpec((1, s, LANES), lambda bi, p, i: (bi, 0, p))
    return pl.pallas_call(
        functools.partial(_attn_b_prompt_kernel, tq=tq, n_var=n_var),
        grid=(b, H_B // 2, s // tq),
        in_specs=[tile, full, full, tile,
                  pl.BlockSpec((2, 1, tq, n_var * tq), lambda bi, p, i: (p, jnp.minimum(i, n_var - 1), 0, 0))],
        out_specs=tile,
        out_shape=jax.ShapeDtypeStruct((b, s, W_B), BF16),
        compiler_params=pltpu.CompilerParams(dimension_semantics=("arbitrary", "arbitrary", "arbitrary"),
                                             vmem_limit_bytes=VMEM_LIMIT),
        name="band_attn_prompt",
    )(qb, kbb, vbb, gb, bias)


def _attn_b_sample_kernel(q_ref, kc_ref, vc_ref, kn_ref, vn_ref, g_ref, bias_ref, o_ref):
    nk = bias_ref.shape[3]
    k = _window_kv(kc_ref, kn_ref, nk)
    v = _window_kv(vc_ref, vn_ref, nk)
    o_ref[0] = _band_pair(q_ref[0], k, v, g_ref[0], bias_ref).astype(o_ref.dtype)


def _attn_b_sample(qb, kbb, vbb, gb, cache_k, cache_v, bias):
    b, t, _ = qb.shape
    past = cache_k.shape[1]
    nk = bias.shape[3]
    new = pl.BlockSpec((1, t, LANES), lambda bi, p: (bi, 0, p))
    old = pl.BlockSpec((1, past, LANES), lambda bi, p: (bi, 0, p))
    return pl.pallas_call(
        _attn_b_sample_kernel,
        grid=(b, H_B // 2),
        in_specs=[new, old, old, new, new, new,
                  pl.BlockSpec((2, 1, t, nk), lambda bi, p: (p, 0, 0, 0))],
        out_specs=new,
        out_shape=jax.ShapeDtypeStruct((b, t, W_B), BF16),
        name="band_attn_sample",
    )(qb, cache_k, cache_v, kbb, vbb, gb, bias)


def _merge_kernel(x_ref, gate_ref, oa_ref, ob_ref, mg_ref, woa_ref, wob_ref, wout_ref, o_ref):
    nb, t, d = x_ref.shape
    rows = nb * t
    ya = jnp.dot(oa_ref[...].reshape(rows, W_A), woa_ref[...], preferred_element_type=F32)
    yb = jnp.dot(ob_ref[...].reshape(rows, W_B), wob_ref[...], preferred_element_type=F32)
    mg = mg_ref[...].reshape(rows, 2 * d).astype(F32)
    m = _sigmoid(mg[:, :d]) * ya + _sigmoid(mg[:, d:]) * yb
    y = jnp.dot(m.astype(BF16), wout_ref[...], preferred_element_type=F32)
    o_ref[...] = x_ref[...] + gate_ref[...] * y.reshape(nb, t, d)


def _merge(x, gate, oa, ob, mg, w_oa_bf, w_ob_bf, w_out_bf, nb, t):
    bx, sx, d = x.shape
    tok = lambda w: pl.BlockSpec((nb, t, w), lambda b, i: (b, i, 0))
    const = lambda shp: pl.BlockSpec(shp, lambda b, i: (0,) * len(shp))
    return pl.pallas_call(
        _merge_kernel,
        grid=(bx // nb, sx // t),
        in_specs=[tok(d), pl.BlockSpec((nb, 1, d), lambda b, i: (b, 0, 0)), tok(W_A), tok(W_B), tok(2 * d),
                  const((W_A, d)), const((W_B, d)), const((d, d))],
        out_specs=tok(d),
        out_shape=jax.ShapeDtypeStruct((bx, sx, d), F32),
        compiler_params=pltpu.CompilerParams(dimension_semantics=("arbitrary", "arbitrary"),
                                             vmem_limit_bytes=VMEM_LIMIT),
        name="merge_out",
    )(x, gate, oa, ob, mg, w_oa_bf, w_ob_bf, w_out_bf)


TQ_A = 256
TQ_B = 256
TM_PROJ = 512
QBASE = 1024


def kernel(x_prompt, x_sample, cache_a_k, cache_a_v, cache_b_k, cache_b_v, c_prompt, c_sample, g_norm, w_ada, b_ada, w_in, g_qa, g_ka, lam_q1, lam_k1, lam_q2, lam_k2, g_subln, t5_bias, g_qb, g_kb, rel_bias_b, w_oa, w_ob, w_out):
    xp, xs = x_prompt, x_sample
    bp, s, d = xp.shape
    bs, t, _ = xs.shape
    depth = w_in.shape[0]
    past = cache_a_k.shape[2]
    lb = cache_b_k.shape[2]
    n_keep = min(BAND_PAST, s)
    assert s % TM_PROJ == 0 and s % TQ_A == 0 and s % TQ_B == 0 and TQ_A % CHUNK == 0 and TQ_B % CHUNK == 0
    assert 2 * TQ_B >= BAND_PAST and past >= lb
    assert _far_bucket(TQ_A + 1) == T5_BUCKETS // 2 - 1

    r = jnp.arange(NORM_GROUP)
    pm = ((r[:, None] // HD_A) == (r[None, :] // HD_A)).astype(BF16) * (1.0 / HD_A)
    c_rows = bp + bs
    c_pad = -(-c_rows // 8) * 8
    c_all = jnp.concatenate([c_prompt, c_sample, jnp.zeros((c_pad - c_rows, d), F32)], axis=0)

    nk_as = -(-(past + t) // LANES) * LANES
    nk_bs = -(-(lb + t) // LANES) * LANES
    t5_fn = lambda rel: t5_bias[_t5_bucket(rel)]
    outs = [[] for _ in range(8)]
    for l in range(depth):
        lam_init = 0.8 - 0.6 * math.exp(-0.3 * l)
        rel_fn = lambda rel: rel_bias_b[l][jnp.clip(rel, -REL_CLIP_B, REL_CLIP_B) + REL_CLIP_B]
        bias_ap = _bias_tiles(t5_fn, H_A, TQ_A, 2 * TQ_A, 2, QBASE, QBASE, TQ_A, False, 2 * TQ_A)
        bias_bp = _bias_tiles(rel_fn, H_B, TQ_B, 3 * TQ_B, 3, QBASE, QBASE, TQ_B, True, 3 * TQ_B)
        bias_as = _bias_tiles(t5_fn, H_A, t, nk_as, 1, past, 0, 0, False, past + t)
        bias_bs = _bias_tiles(rel_fn, H_B, t, nk_bs, 1, past, past - lb, 0, True, lb + t)
        cfar = t5_bias[_t5_bucket(jnp.int32(-(TQ_A + 1)))]

        mod = _modulation(c_all, w_ada[l], b_ada[l])
        shift = mod[:, :d].reshape(c_pad, 1, d)
        scale = mod[:, d:2 * d].reshape(c_pad, 1, d)
        gate = mod[:, 2 * d:].reshape(c_pad, 1, d)
        w_in_bf = w_in[l].astype(BF16)
        w_oa_bf, w_ob_bf, w_out_bf = w_oa[l].astype(BF16), w_ob[l].astype(BF16), w_out[l].astype(BF16)
        tile8 = lambda g: jnp.tile(g, SEG // g.shape[0]).reshape(1, SEG)
        gains = (tile8(g_qa[l]), tile8(g_ka[l]), tile8(g_qb[l]), tile8(g_kb[l]))
        lamv = jnp.stack([lam_q1[l], lam_k1[l], lam_q2[l], lam_k2[l]])
        gsub = g_subln[l].reshape(1, LANES)

        (qa, ka32, kab, va32, vab, ga, qb, kb32, kbb, vb32, vbb, gb, mg) = _project(
            xp, shift[:bp], scale[:bp], g_norm[l], w_in_bf, *gains, pm, 1, TM_PROJ)
        oa = _attn_a_prompt(qa, kab, vab, ga, bias_ap, cfar, lamv, gsub, TQ_A, lam_init)
        ob = _attn_b_prompt(qb, kbb, vbb, gb, bias_bp, TQ_B)
        xp = _merge(xp, gate[:bp], oa, ob, mg, w_oa_bf, w_ob_bf, w_out_bf, 1, TM_PROJ)
        outs[0].append(ka32.reshape(bp, s, H_A, 2 * HD_A))
        outs[1].append(va32.reshape(bp, s, H_A, 2 * HD_A))
        outs[2].append(kb32[:, s - n_keep:].reshape(bp, n_keep, H_B, HD_B))
        outs[3].append(vb32[:, s - n_keep:].reshape(bp, n_keep, H_B, HD_B))

        (qa, ka32, kab, va32, vab, ga, qb, kb32, kbb, vb32, vbb, gb, mg) = _project(
            xs, shift[bp:c_rows], scale[bp:c_rows], g_norm[l], w_in_bf, *gains, pm, bs, t)
        oa = _attn_a_sample(qa, kab, vab, ga, cache_a_k[l].reshape(bs, past, W_A), cache_a_v[l].reshape(bs, past, W_A),
                            bias_as, lamv, gsub, lam_init)
        ob = _attn_b_sample(qb, kbb, vbb, gb, cache_b_k[l].reshape(bs, lb, W_B), cache_b_v[l].reshape(bs, lb, W_B), bias_bs)
        xs = _merge(xs, gate[bp:c_rows], oa, ob, mg, w_oa_bf, w_ob_bf, w_out_bf, bs, t)
        outs[4].append(ka32.reshape(bs, t, H_A, 2 * HD_A))
        outs[5].append(va32.reshape(bs, t, H_A, 2 * HD_A))
        outs[6].append(kb32.reshape(bs, t, H_B, HD_B))
        outs[7].append(vb32.reshape(bs, t, H_B, HD_B))

    return (xp, xs) + tuple(jnp.stack(o) for o in outs)
```

```python
import functools
import math

import jax
import jax.numpy as jnp
from jax import lax
from jax.experimental import pallas as pl
from jax.experimental.pallas import tpu as pltpu

CHUNK = 64
H_A = 4
HD_A = 64
W_A = H_A * 2 * HD_A
H_B = 8
HD_B = 64
W_B = H_B * HD_B
BAND_CHUNKS = 8
BAND_PAST = BAND_CHUNKS * CHUNK
REL_CLIP_B = 128
T5_BUCKETS = 32
T5_MAX_EXACT = 8
T5_MAX_DIST = 128
EPS = 1e-6
NEG = -1e30

LANES = 128
SEG = 512
NORM_GROUP = 256
VMEM_LIMIT = 56 * 1024 * 1024

LOG2E = math.log2(math.e)

F32 = jnp.float32
BF16 = jnp.bfloat16


def _t5_bucket(rel):
    half = T5_BUCKETS // 2
    ret = jnp.where(rel > 0, half, 0)
    n = jnp.abs(rel)
    nf = jnp.maximum(n, 1).astype(jnp.float32)
    large = T5_MAX_EXACT + (jnp.log(nf / T5_MAX_EXACT) / math.log(T5_MAX_DIST / T5_MAX_EXACT) * (half - T5_MAX_EXACT)).astype(jnp.int32)
    large = jnp.minimum(large, half - 1)
    return ret + jnp.where(n < T5_MAX_EXACT, n, large)


def _far_bucket(n):
    half = T5_BUCKETS // 2
    return min(T5_MAX_EXACT + int(math.log(n / T5_MAX_EXACT) / math.log(T5_MAX_DIST / T5_MAX_EXACT) * (half - T5_MAX_EXACT)), half - 1)


def _nt(a, b):
    return lax.dot_general(a, b, (((1,), (1,)), ((), ())), preferred_element_type=F32)


def _silu(x):
    return x * (1.0 / (1.0 + jnp.exp(-x)))


def _sigmoid(x):
    return 1.0 / (1.0 + jnp.exp(-x))


def _mod_kernel(c_ref, w_ref, b_ref, o_ref):
    c = c_ref[...]
    o_ref[...] = jnp.dot(_silu(c), w_ref[...], preferred_element_type=F32,
                         precision=lax.Precision.HIGHEST) + b_ref[...]


def _modulation(c_all, w_ada, b_ada):
    rows, d = c_all.shape
    n_out = w_ada.shape[1]
    tn = d
    return pl.pallas_call(
        _mod_kernel,
        grid=(n_out // tn,),
        in_specs=[pl.BlockSpec((rows, d), lambda j: (0, 0)),
                  pl.BlockSpec((d, tn), lambda j: (0, j)),
                  pl.BlockSpec((1, tn), lambda j: (0, j))],
        out_specs=pl.BlockSpec((rows, tn), lambda j: (0, j)),
        out_shape=jax.ShapeDtypeStruct((rows, n_out), F32),
        name="adaln_mod",
    )(c_all, w_ada, b_ada.reshape(1, n_out))


def _proj_kernel(x_ref, shift_ref, scale_ref, gn_ref, w_ref, gqa_ref, gka_ref, gqb_ref, gkb_ref, pm_ref,
                 qa_ref, ka32_ref, kab_ref, va32_ref, vab_ref, ga_ref,
                 qb_ref, kb32_ref, kbb_ref, vb32_ref, vbb_ref, gb_ref, mg_ref):
    nb, t, d = x_ref.shape
    rows = nb * t
    x = x_ref[...]
    ms = jnp.mean(x * x, axis=-1, keepdims=True)
    xn = x * lax.rsqrt(ms + EPS) * gn_ref[...]
    h = xn * (1.0 + scale_ref[...]) + shift_ref[...]
    hb = h.reshape(rows, d).astype(BF16)

    def seg(c, width=SEG):
        return jnp.dot(hb, w_ref[:, c * SEG:c * SEG + width], preferred_element_type=F32)

    def head_norm(y, g_ref):
        sq = y * y
        hi = sq.astype(BF16)
        lo = (sq - hi.astype(F32)).astype(BF16)
        pm = pm_ref[...]
        parts = []
        for c in range(SEG // NORM_GROUP):
            sl = slice(c * NORM_GROUP, (c + 1) * NORM_GROUP)
            gms = jnp.dot(hi[:, sl], pm, preferred_element_type=F32) + jnp.dot(lo[:, sl], pm, preferred_element_type=F32)
            parts.append(y[:, sl] * lax.rsqrt(gms + EPS))
        return jnp.concatenate(parts, axis=1) * g_ref[...]

    def put(ref, y):
        ref[...] = y.astype(ref.dtype).reshape(ref.shape)

    put(qa_ref, head_norm(seg(0), gqa_ref) * (HD_A ** -0.5 * LOG2E))
    ka = head_norm(seg(1), gka_ref)
    put(ka32_ref, ka)
    put(kab_ref, ka)
    va = seg(2)
    put(va32_ref, va)
    put(vab_ref, va)
    put(ga_ref, seg(3))
    put(qb_ref, head_norm(seg(4), gqb_ref) * (HD_B ** -0.5 * LOG2E))
    kb = head_norm(seg(5), gkb_ref)
    put(kb32_ref, kb)
    put(kbb_ref, kb)
    vb = seg(6)
    put(vb32_ref, vb)
    put(vbb_ref, vb)
    put(gb_ref, seg(7))
    for c in range(8, 12):
        mg_ref[:, :, (c - 8) * SEG:(c - 7) * SEG] = seg(c).astype(mg_ref.dtype).reshape(nb, t, SEG)


def _project(x, shift, scale, g_norm, w_in_bf, gqa, gka, gqb, gkb, pm, nb, t):
    bx, sx, d = x.shape
    n_cols = w_in_bf.shape[1]
    grid = (bx // nb, sx // t)
    tok = lambda w: pl.BlockSpec((nb, t, w), lambda b, i: (b, i, 0))
    per_b = pl.BlockSpec((nb, 1, d), lambda b, i: (b, 0, 0))
    const = lambda shp: pl.BlockSpec(shp, lambda b, i: (0,) * len(shp))
    sds = lambda w, dt: jax.ShapeDtypeStruct((bx, sx, w), dt)
    out_shape = (sds(SEG, BF16), sds(SEG, F32), sds(SEG, BF16), sds(SEG, F32), sds(SEG, BF16), sds(SEG, BF16),
                 sds(SEG, BF16), sds(SEG, F32), sds(SEG, BF16), sds(SEG, F32), sds(SEG, BF16), sds(SEG, BF16),
                 sds(4 * SEG, BF16))
    out_specs = tuple(tok(s.shape[-1]) for s in out_shape)
    return pl.pallas_call(
        _proj_kernel,
        grid=grid,
        in_specs=[tok(d), per_b, per_b, const((1, d)),
                  pl.BlockSpec((d, n_cols), lambda b, i: (0, 0), pipeline_mode=pl.Buffered(1)),
                  const((1, SEG)), const((1, SEG)), const((1, SEG)), const((1, SEG)),
                  const((NORM_GROUP, NORM_GROUP))],
        out_specs=out_specs,
        out_shape=out_shape,
        compiler_params=pltpu.CompilerParams(dimension_semantics=("arbitrary", "arbitrary"),
                                             vmem_limit_bytes=VMEM_LIMIT),
        name="in_proj",
    )(x, shift, scale, g_norm.reshape(1, d), w_in_bf, gqa, gka, gqb, gkb, pm)


def _bias_tile_kernel(off_ref, gen_ref, o_ref, *, tq, nk, qbase, kbase0, kstep, band, nvalid):
    v = pl.program_id(1)
    g = gen_ref[0, 0]
    x = jnp.broadcast_to(g, (tq, g.shape[-1]))
    y = pltpu.roll(x, 0, 1, stride=1, stride_axis=0)
    y = (y[:, :nk] - off_ref[pl.program_id(0)]) * LOG2E
    row = lax.broadcasted_iota(jnp.int32, (tq, nk), 0)
    col = lax.broadcasted_iota(jnp.int32, (tq, nk), 1)
    qc = (qbase + row) // CHUNK
    kc = (kbase0 - v * kstep + col) // CHUNK
    y = jnp.where(kc <= qc, y, NEG)
    if band:
        y = jnp.where(kc >= qc - BAND_CHUNKS, y, NEG)
    y = jnp.where(col < nvalid, y, NEG)
    o_ref[0, 0] = y


def _bias_tiles(table_fn, offset, n_heads, tq, nk, n_var, qbase, kbase0, kstep, band, nvalid):
    length = 1 << (tq + nk - 1).bit_length()
    assert length >= tq + nk - 1 and min(kbase0 - (n_var - 1) * kstep, qbase) >= 0
    r = jnp.arange(length, dtype=jnp.int32)
    d = jnp.where(r < nk, r, r - length)
    kbase = kbase0 - kstep * jnp.arange(n_var, dtype=jnp.int32)
    rel = (kbase[:, None] - qbase) + d[None, :]
    gen = jnp.transpose(table_fn(rel), (2, 0, 1)).astype(F32)
    gen = gen.reshape(n_heads, n_var, 1, length)
    return pl.pallas_call(
        functools.partial(_bias_tile_kernel, tq=tq, nk=nk, qbase=qbase, kbase0=kbase0, kstep=kstep,
                          band=band, nvalid=nvalid),
        grid=(n_heads, n_var),
        in_specs=[pl.BlockSpec(memory_space=pltpu.SMEM),
                  pl.BlockSpec((1, 1, 1, length), lambda h, v: (h, v, 0, 0))],
        out_specs=pl.BlockSpec((1, 1, tq, nk), lambda h, v: (h, v, 0, 0)),
        out_shape=jax.ShapeDtypeStruct((n_heads, n_var, tq, nk), F32),
        name="bias_tiles",
    )(offset.astype(F32), gen)


def _split_halves(q):
    lane = lax.broadcasted_iota(jnp.int32, q.shape, 1)
    zero = jnp.zeros_like(q)
    return jnp.where(lane < LANES // 2, q, zero), jnp.where(lane >= LANES // 2, q, zero)


def _online_update(carry, s, v):
    m, l, acc = carry
    m_new = jnp.maximum(m, jnp.max(s, axis=-1, keepdims=True))
    alpha = jnp.exp2(m - m_new)
    p = jnp.exp2(s - m_new)
    l = alpha * l + jnp.sum(p, axis=-1, keepdims=True)
    acc = alpha * acc + jnp.dot(p.astype(BF16), v, preferred_element_type=F32)
    return m_new, l, acc


def _diff_init(tq):
    return (jnp.full((tq, 1), -jnp.inf, F32), jnp.zeros((tq, 1), F32), jnp.zeros((tq, LANES), F32))


def _diff_finish(c0, c1, lamv_ref, gsub_ref, g, lam_init):
    lamv = lamv_ref[...]
    e1 = jnp.exp(jnp.sum(lamv[0:1] * lamv[1:2], axis=-1, keepdims=True))
    e2 = jnp.exp(jnp.sum(lamv[2:3] * lamv[3:4], axis=-1, keepdims=True))
    lam = e1 - e2 + lam_init
    o = c0[2] / c0[1] - lam * (c1[2] / c1[1])
    o = o * lax.rsqrt(jnp.mean(o * o, axis=-1, keepdims=True) + EPS) * gsub_ref[...]
    o = o * (1.0 - lam_init)
    return o * _silu(g.astype(F32))


def _attn_a_prompt_kernel(q_ref, k_ref, v_ref, g_ref, bias_ref, lamv_ref, gsub_ref, o_ref, *, tq, n_var, lam_init):
    i = pl.program_id(2)
    q0, q1 = _split_halves(q_ref[0])
    n_far = jnp.maximum(i - (n_var - 1), 0)

    def far(j, carry):
        c0, c1 = carry
        start = pl.multiple_of(j * tq, tq)
        k = k_ref[0, pl.ds(start, tq), :]
        v = v_ref[0, pl.ds(start, tq), :]
        return _online_update(c0, _nt(q0, k), v), _online_update(c1, _nt(q1, k), v)

    c0, c1 = lax.fori_loop(0, n_far, far, (_diff_init(tq), _diff_init(tq)))
    ws = pl.multiple_of(n_far * tq, tq)
    k = k_ref[0, pl.ds(ws, n_var * tq), :]
    v = v_ref[0, pl.ds(ws, n_var * tq), :]
    bias = bias_ref[0, 0]
    c0 = _online_update(c0, _nt(q0, k) + bias, v)
    c1 = _online_update(c1, _nt(q1, k) + bias, v)
    o_ref[0] = _diff_finish(c0, c1, lamv_ref, gsub_ref, g_ref[0], lam_init).astype(o_ref.dtype)


def _attn_a_prompt(qa, kab, vab, ga, bias, lamv, gsub, tq, lam_init):
    b, s, _ = qa.shape
    n_var = bias.shape[1]
    assert bias.shape[3] == n_var * tq
    tile = pl.BlockSpec((1, tq, LANES), lambda bi, h, i: (bi, i, h))
    full = pl.BlockSpec((1, s, LANES), lambda bi, h, i: (bi, 0, h))
    return pl.pallas_call(
        functools.partial(_attn_a_prompt_kernel, tq=tq, n_var=n_var, lam_init=lam_init),
        grid=(b, H_A, s // tq),
        in_specs=[tile, full, full, tile,
                  pl.BlockSpec((1, 1, tq, n_var * tq), lambda bi, h, i: (h, jnp.minimum(i, n_var - 1), 0, 0)),
                  pl.BlockSpec((4, HD_A), lambda bi, h, i: (0, 0)),
                  pl.BlockSpec((1, LANES), lambda bi, h, i: (0, 0))],
        out_specs=tile,
        out_shape=jax.ShapeDtypeStruct((b, s, W_A), BF16),
        compiler_params=pltpu.CompilerParams(dimension_semantics=("arbitrary", "arbitrary", "arbitrary"),
                                             vmem_limit_bytes=VMEM_LIMIT),
        name="diff_attn_prompt",
    )(qa, kab, vab, ga, bias, lamv, gsub)


def _window_kv(cache_ref, new_ref, nk):
    past = cache_ref.shape[1]
    t = new_ref.shape[1]
    pad = jnp.zeros((nk - past - t, LANES), BF16)
    return jnp.concatenate([cache_ref[0].astype(BF16), new_ref[0], pad], axis=0)


def _attn_a_sample_kernel(q_ref, kc_ref, vc_ref, kn_ref, vn_ref, g_ref, bias_ref, lamv_ref, gsub_ref, o_ref, *, lam_init):
    tq = q_ref.shape[1]
    nk = bias_ref.shape[3]
    q0, q1 = _split_halves(q_ref[0])
    k = _window_kv(kc_ref, kn_ref, nk)
    v = _window_kv(vc_ref, vn_ref, nk)
    bias = bias_ref[0, 0]
    c0 = _online_update(_diff_init(tq), _nt(q0, k) + bias, v)
    c1 = _online_update(_diff_init(tq), _nt(q1, k) + bias, v)
    o_ref[0] = _diff_finish(c0, c1, lamv_ref, gsub_ref, g_ref[0], lam_init).astype(o_ref.dtype)


def _attn_a_sample(qa, kab, vab, ga, cache_k, cache_v, bias, lamv, gsub, lam_init):
    b, t, _ = qa.shape
    past = cache_k.shape[1]
    nk = bias.shape[3]
    new = pl.BlockSpec((1, t, LANES), lambda bi, h: (bi, 0, h))
    old = pl.BlockSpec((1, past, LANES), lambda bi, h: (bi, 0, h))
    return pl.pallas_call(
        functools.partial(_attn_a_sample_kernel, lam_init=lam_init),
        grid=(b, H_A),
        in_specs=[new, old, old, new, new, new,
                  pl.BlockSpec((1, 1, t, nk), lambda bi, h: (h, 0, 0, 0)),
                  pl.BlockSpec((4, HD_A), lambda bi, h: (0, 0)),
                  pl.BlockSpec((1, LANES), lambda bi, h: (0, 0))],
        out_specs=new,
        out_shape=jax.ShapeDtypeStruct((b, t, W_A), BF16),
        name="diff_attn_sample",
    )(qa, cache_k, cache_v, kab, vab, ga, bias, lamv, gsub)


def _band_pair(q, k, v, g, bias_ref):
    q0, q1 = _split_halves(q)
    v0, v1 = _split_halves(v)
    o = None
    for hh, (qh, vh) in enumerate(((q0, v0), (q1, v1))):
        s = _nt(qh, k) + bias_ref[hh, 0]
        m = jnp.max(s, axis=-1, keepdims=True)
        p = jnp.exp2(s - m)
        l = jnp.sum(p, axis=-1, keepdims=True)
        oh = jnp.dot(p.astype(BF16), vh, preferred_element_type=F32) / l
        o = oh if o is None else o + oh
    return o * _silu(g.astype(F32))


def _attn_b_prompt_kernel(q_ref, k_ref, v_ref, g_ref, bias_ref, o_ref, *, tq, n_var):
    i = pl.program_id(2)
    ws = pl.multiple_of(jnp.maximum(i - (n_var - 1), 0) * tq, tq)
    k = k_ref[0, pl.ds(ws, n_var * tq), :]
    v = v_ref[0, pl.ds(ws, n_var * tq), :]
    o_ref[0] = _band_pair(q_ref[0], k, v, g_ref[0], bias_ref).astype(o_ref.dtype)


def _attn_b_prompt(qb, kbb, vbb, gb, bias, tq):
    b, s, _ = qb.shape
    n_var = bias.shape[1]
    assert bias.shape[3] == n_var * tq
    tile = pl.BlockSpec((1, tq, LANES), lambda bi, p, i: (bi, i, p))
    full = pl.BlockSpec((1, s, LANES), lambda bi, p, i: (bi, 0, p))
    return pl.pallas_call(
        functools.partial(_attn_b_prompt_kernel, tq=tq, n_var=n_var),
        grid=(b, H_B // 2, s // tq),
        in_specs=[tile, full, full, tile,
                  pl.BlockSpec((2, 1, tq, n_var * tq), lambda bi, p, i: (p, jnp.minimum(i, n_var - 1), 0, 0))],
        out_specs=tile,
        out_shape=jax.ShapeDtypeStruct((b, s, W_B), BF16),
        compiler_params=pltpu.CompilerParams(dimension_semantics=("arbitrary", "arbitrary", "arbitrary"),
                                             vmem_limit_bytes=VMEM_LIMIT),
        name="band_attn_prompt",
    )(qb, kbb, vbb, gb, bias)


def _attn_b_sample_kernel(q_ref, kc_ref, vc_ref, kn_ref, vn_ref, g_ref, bias_ref, o_ref):
    nk = bias_ref.shape[3]
    k = _window_kv(kc_ref, kn_ref, nk)
    v = _window_kv(vc_ref, vn_ref, nk)
    o_ref[0] = _band_pair(q_ref[0], k, v, g_ref[0], bias_ref).astype(o_ref.dtype)


def _attn_b_sample(qb, kbb, vbb, gb, cache_k, cache_v, bias):
    b, t, _ = qb.shape
    past = cache_k.shape[1]
    nk = bias.shape[3]
    new = pl.BlockSpec((1, t, LANES), lambda bi, p: (bi, 0, p))
    old = pl.BlockSpec((1, past, LANES), lambda bi, p: (bi, 0, p))
    return pl.pallas_call(
        _attn_b_sample_kernel,
        grid=(b, H_B // 2),
        in_specs=[new, old, old, new, new, new,
                  pl.BlockSpec((2, 1, t, nk), lambda bi, p: (p, 0, 0, 0))],
        out_specs=new,
        out_shape=jax.ShapeDtypeStruct((b, t, W_B), BF16),
        name="band_attn_sample",
    )(qb, cache_k, cache_v, kbb, vbb, gb, bias)


def _merge_kernel(x_ref, gate_ref, oa_ref, ob_ref, mg_ref, woa_ref, wob_ref, wout_ref, o_ref):
    nb, t, d = x_ref.shape
    rows = nb * t
    ya = jnp.dot(oa_ref[...].reshape(rows, W_A), woa_ref[...], preferred_element_type=F32)
    yb = jnp.dot(ob_ref[...].reshape(rows, W_B), wob_ref[...], preferred_element_type=F32)
    mg = mg_ref[...].reshape(rows, 2 * d).astype(F32)
    m = _sigmoid(mg[:, :d]) * ya + _sigmoid(mg[:, d:]) * yb
    y = jnp.dot(m.astype(BF16), wout_ref[...], preferred_element_type=F32)
    o_ref[...] = x_ref[...] + gate_ref[...] * y.reshape(nb, t, d)


def _merge(x, gate, oa, ob, mg, w_oa_bf, w_ob_bf, w_out_bf, nb, t):
    bx, sx, d = x.shape
    tok = lambda w: pl.BlockSpec((nb, t, w), lambda b, i: (b, i, 0))
    const = lambda shp: pl.BlockSpec(shp, lambda b, i: (0,) * len(shp))
    return pl.pallas_call(
        _merge_kernel,
        grid=(bx // nb, sx // t),
        in_specs=[tok(d), pl.BlockSpec((nb, 1, d), lambda b, i: (b, 0, 0)), tok(W_A), tok(W_B), tok(2 * d),
                  const((W_A, d)), const((W_B, d)), const((d, d))],
        out_specs=tok(d),
        out_shape=jax.ShapeDtypeStruct((bx, sx, d), F32),
        compiler_params=pltpu.CompilerParams(dimension_semantics=("arbitrary", "arbitrary"),
                                             vmem_limit_bytes=VMEM_LIMIT),
        name="merge_out",
    )(x, gate, oa, ob, mg, w_oa_bf, w_ob_bf, w_out_bf)


TQ_A = 512
TQ_B = 256
TM_PROJ = 512
QBASE = 2048


def kernel(x_prompt, x_sample, cache_a_k, cache_a_v, cache_b_k, cache_b_v, c_prompt, c_sample, g_norm, w_ada, b_ada, w_in, g_qa, g_ka, lam_q1, lam_k1, lam_q2, lam_k2, g_subln, t5_bias, g_qb, g_kb, rel_bias_b, w_oa, w_ob, w_out):
    xp, xs = x_prompt, x_sample
    bp, s, d = xp.shape
    bs, t, _ = xs.shape
    depth = w_in.shape[0]
    past = cache_a_k.shape[2]
    lb = cache_b_k.shape[2]
    n_keep = min(BAND_PAST, s)
    assert s % TM_PROJ == 0 and s % TQ_A == 0 and s % TQ_B == 0 and TQ_A % CHUNK == 0 and TQ_B % CHUNK == 0
    assert 2 * TQ_B >= BAND_PAST and past >= lb
    assert _far_bucket(TQ_A + 1) == T5_BUCKETS // 2 - 1

    r = jnp.arange(NORM_GROUP)
    pm = ((r[:, None] // HD_A) == (r[None, :] // HD_A)).astype(BF16) * (1.0 / HD_A)
    c_rows = bp + bs
    c_pad = -(-c_rows // 8) * 8
    c_all = jnp.concatenate([c_prompt, c_sample, jnp.zeros((c_pad - c_rows, d), F32)], axis=0)

    nk_as = -(-(past + t) // LANES) * LANES
    nk_bs = -(-(lb + t) // LANES) * LANES
    t5_fn = lambda rel: t5_bias[_t5_bucket(rel)]
    outs = [[] for _ in range(8)]
    for l in range(depth):
        lam_init = 0.8 - 0.6 * math.exp(-0.3 * l)
        rel_fn = lambda rel: rel_bias_b[l][jnp.clip(rel, -REL_CLIP_B, REL_CLIP_B) + REL_CLIP_B]
        cfar = t5_bias[_t5_bucket(jnp.int32(-(TQ_A + 1)))]
        bias_ap = _bias_tiles(t5_fn, cfar, H_A, TQ_A, 2 * TQ_A, 2, QBASE, QBASE, TQ_A, False, 2 * TQ_A)
        bias_bp = _bias_tiles(rel_fn, jnp.zeros((H_B,)), H_B, TQ_B, 3 * TQ_B, 3, QBASE, QBASE, TQ_B, True, 3 * TQ_B)
        bias_as = _bias_tiles(t5_fn, jnp.zeros((H_A,)), H_A, t, nk_as, 1, past, 0, 0, False, past + t)
        bias_bs = _bias_tiles(rel_fn, jnp.zeros((H_B,)), H_B, t, nk_bs, 1, past, past - lb, 0, True, lb + t)

        mod = _modulation(c_all, w_ada[l], b_ada[l])
        shift = mod[:, :d].reshape(c_pad, 1, d)
        scale = mod[:, d:2 * d].reshape(c_pad, 1, d)
        gate = mod[:, 2 * d:].reshape(c_pad, 1, d)
        w_in_bf = w_in[l].astype(BF16)
        w_oa_bf, w_ob_bf, w_out_bf = w_oa[l].astype(BF16), w_ob[l].astype(BF16), w_out[l].astype(BF16)
        tile8 = lambda g: jnp.tile(g, SEG // g.shape[0]).reshape(1, SEG)
        gains = (tile8(g_qa[l]), tile8(g_ka[l]), tile8(g_qb[l]), tile8(g_kb[l]))
        lamv = jnp.stack([lam_q1[l], lam_k1[l], lam_q2[l], lam_k2[l]])
        gsub = g_subln[l].reshape(1, LANES)

        (qa, ka32, kab, va32, vab, ga, qb, kb32, kbb, vb32, vbb, gb, mg) = _project(
            xp, shift[:bp], scale[:bp], g_norm[l], w_in_bf, *gains, pm, 1, TM_PROJ)
        oa = _attn_a_prompt(qa, kab, vab, ga, bias_ap, lamv, gsub, TQ_A, lam_init)
        ob = _attn_b_prompt(qb, kbb, vbb, gb, bias_bp, TQ_B)
        xp = _merge(xp, gate[:bp], oa, ob, mg, w_oa_bf, w_ob_bf, w_out_bf, 1, TM_PROJ)
        outs[0].append(ka32.reshape(bp, s, H_A, 2 * HD_A))
        outs[1].append(va32.reshape(bp, s, H_A, 2 * HD_A))
        outs[2].append(kb32[:, s - n_keep:].reshape(bp, n_keep, H_B, HD_B))
        outs[3].append(vb32[:, s - n_keep:].reshape(bp, n_keep, H_B, HD_B))

        (qa, ka32, kab, va32, vab, ga, qb, kb32, kbb, vb32, vbb, gb, mg) = _project(
            xs, shift[bp:c_rows], scale[bp:c_rows], g_norm[l], w_in_bf, *gains, pm, bs, t)
        oa = _attn_a_sample(qa, kab, vab, ga, cache_a_k[l].reshape(bs, past, W_A), cache_a_v[l].reshape(bs, past, W_A),
                            bias_as, lamv, gsub, lam_init)
        ob = _attn_b_sample(qb, kbb, vbb, gb, cache_b_k[l].reshape(bs, lb, W_B), cache_b_v[l].reshape(bs, lb, W_B), bias_bs)
        xs = _merge(xs, gate[bp:c_rows], oa, ob, mg, w_oa_bf, w_ob_bf, w_out_bf, bs, t)
        outs[4].append(ka32.reshape(bs, t, H_A, 2 * HD_A))
        outs[5].append(va32.reshape(bs, t, H_A, 2 * HD_A))
        outs[6].append(kb32.reshape(bs, t, H_B, HD_B))
        outs[7].append(vb32.reshape(bs, t, H_B, HD_B))

    return (xp, xs) + tuple(jnp.stack(o) for o in outs)
```

```python
import functools
import math

import jax
import jax.numpy as jnp
from jax import lax
from jax.experimental import pallas as pl
from jax.experimental.pallas import tpu as pltpu

CHUNK = 64
H_A = 4
HD_A = 64
W_A = H_A * 2 * HD_A
H_B = 8
HD_B = 64
W_B = H_B * HD_B
BAND_CHUNKS = 8
BAND_PAST = BAND_CHUNKS * CHUNK
REL_CLIP_B = 128
T5_BUCKETS = 32
T5_MAX_EXACT = 8
T5_MAX_DIST = 128
EPS = 1e-6
NEG = -1e30

LANES = 128
SEG = 512
NORM_GROUP = 256
MAX_DIRECT_LOGIT = 60.0
BF16_ROUND_MARGIN = 1.02
VMEM_LIMIT = 56 * 1024 * 1024

LOG2E = math.log2(math.e)

F32 = jnp.float32
BF16 = jnp.bfloat16


def _t5_bucket(rel):
    half = T5_BUCKETS // 2
    ret = jnp.where(rel > 0, half, 0)
    n = jnp.abs(rel)
    nf = jnp.maximum(n, 1).astype(jnp.float32)
    large = T5_MAX_EXACT + (jnp.log(nf / T5_MAX_EXACT) / math.log(T5_MAX_DIST / T5_MAX_EXACT) * (half - T5_MAX_EXACT)).astype(jnp.int32)
    large = jnp.minimum(large, half - 1)
    return ret + jnp.where(n < T5_MAX_EXACT, n, large)


def _far_bucket(n):
    half = T5_BUCKETS // 2
    return min(T5_MAX_EXACT + int(math.log(n / T5_MAX_EXACT) / math.log(T5_MAX_DIST / T5_MAX_EXACT) * (half - T5_MAX_EXACT)), half - 1)


def _nt(a, b):
    return lax.dot_general(a, b, (((1,), (1,)), ((), ())), preferred_element_type=F32)


def _silu(x):
    return x * (1.0 / (1.0 + jnp.exp(-x)))


def _sigmoid(x):
    return 1.0 / (1.0 + jnp.exp(-x))


def _mod_kernel(c_ref, w_ref, b_ref, o_ref):
    c = c_ref[...]
    o_ref[...] = jnp.dot(_silu(c), w_ref[...], preferred_element_type=F32,
                         precision=lax.Precision.HIGHEST) + b_ref[...]


def _modulation(c_all, w_ada, b_ada):
    rows, d = c_all.shape
    n_out = w_ada.shape[1]
    tn = d
    return pl.pallas_call(
        _mod_kernel,
        grid=(n_out // tn,),
        in_specs=[pl.BlockSpec((rows, d), lambda j: (0, 0)),
                  pl.BlockSpec((d, tn), lambda j: (0, j)),
                  pl.BlockSpec((1, tn), lambda j: (0, j))],
        out_specs=pl.BlockSpec((rows, tn), lambda j: (0, j)),
        out_shape=jax.ShapeDtypeStruct((rows, n_out), F32),
        name="adaln_mod",
    )(c_all, w_ada, b_ada.reshape(1, n_out))


def _proj_kernel(x_ref, shift_ref, scale_ref, gn_ref, w_ref, gqa_ref, gka_ref, gqb_ref, gkb_ref, pm_ref,
                 qa_ref, ka32_ref, kab_ref, va32_ref, vab_ref, ga_ref,
                 qb_ref, kb32_ref, kbb_ref, vb32_ref, vbb_ref, gb_ref, mg_ref):
    nb, t, d = x_ref.shape
    rows = nb * t
    x = x_ref[...]
    ms = jnp.mean(x * x, axis=-1, keepdims=True)
    xn = x * lax.rsqrt(ms + EPS) * gn_ref[...]
    h = xn * (1.0 + scale_ref[...]) + shift_ref[...]
    hb = h.reshape(rows, d).astype(BF16)

    def seg(c, width=SEG):
        return jnp.dot(hb, w_ref[:, c * SEG:c * SEG + width], preferred_element_type=F32)

    def head_norm(y, g_ref):
        sq = y * y
        hi = sq.astype(BF16)
        lo = (sq - hi.astype(F32)).astype(BF16)
        pm = pm_ref[...]
        parts = []
        for c in range(SEG // NORM_GROUP):
            sl = slice(c * NORM_GROUP, (c + 1) * NORM_GROUP)
            gms = jnp.dot(hi[:, sl], pm, preferred_element_type=F32) + jnp.dot(lo[:, sl], pm, preferred_element_type=F32)
            parts.append(y[:, sl] * lax.rsqrt(gms + EPS))
        return jnp.concatenate(parts, axis=1) * g_ref[...]

    def put(ref, y):
        ref[...] = y.astype(ref.dtype).reshape(ref.shape)

    put(qa_ref, head_norm(seg(0), gqa_ref) * (HD_A ** -0.5 * LOG2E))
    ka = head_norm(seg(1), gka_ref)
    put(ka32_ref, ka)
    put(kab_ref, ka)
    va = seg(2)
    put(va32_ref, va)
    put(vab_ref, va)
    put(ga_ref, seg(3))
    put(qb_ref, head_norm(seg(4), gqb_ref) * (HD_B ** -0.5 * LOG2E))
    kb = head_norm(seg(5), gkb_ref)
    put(kb32_ref, kb)
    put(kbb_ref, kb)
    vb = seg(6)
    put(vb32_ref, vb)
    put(vbb_ref, vb)
    put(gb_ref, seg(7))
    for c in range(8, 12):
        mg_ref[:, :, (c - 8) * SEG:(c - 7) * SEG] = seg(c).astype(mg_ref.dtype).reshape(nb, t, SEG)


def _project(x, shift, scale, g_norm, w_in_bf, gqa, gka, gqb, gkb, pm, nb, t):
    bx, sx, d = x.shape
    n_cols = w_in_bf.shape[1]
    grid = (bx // nb, sx // t)
    tok = lambda w: pl.BlockSpec((nb, t, w), lambda b, i: (b, i, 0))
    per_b = pl.BlockSpec((nb, 1, d), lambda b, i: (b, 0, 0))
    const = lambda shp: pl.BlockSpec(shp, lambda b, i: (0,) * len(shp))
    sds = lambda w, dt: jax.ShapeDtypeStruct((bx, sx, w), dt)
    out_shape = (sds(SEG, BF16), sds(SEG, F32), sds(SEG, BF16), sds(SEG, F32), sds(SEG, BF16), sds(SEG, BF16),
                 sds(SEG, BF16), sds(SEG, F32), sds(SEG, BF16), sds(SEG, F32), sds(SEG, BF16), sds(SEG, BF16),
                 sds(4 * SEG, BF16))
    out_specs = tuple(tok(s.shape[-1]) for s in out_shape)
    return pl.pallas_call(
        _proj_kernel,
        grid=grid,
        in_specs=[tok(d), per_b, per_b, const((1, d)),
                  pl.BlockSpec((d, n_cols), lambda b, i: (0, 0), pipeline_mode=pl.Buffered(1)),
                  const((1, SEG)), const((1, SEG)), const((1, SEG)), const((1, SEG)),
                  const((NORM_GROUP, NORM_GROUP))],
        out_specs=out_specs,
        out_shape=out_shape,
        compiler_params=pltpu.CompilerParams(dimension_semantics=("arbitrary", "arbitrary"),
                                             vmem_limit_bytes=VMEM_LIMIT),
        name="in_proj",
    )(x, shift, scale, g_norm.reshape(1, d), w_in_bf, gqa, gka, gqb, gkb, pm)


def _bias_tile_kernel(off_ref, gen_ref, o_ref, *, tq, nk, qbase, kbase0, kstep, band, nvalid):
    v = pl.program_id(1)
    g = gen_ref[0, 0]
    x = jnp.broadcast_to(g, (tq, g.shape[-1]))
    y = pltpu.roll(x, 0, 1, stride=1, stride_axis=0)
    y = (y[:, :nk] - off_ref[pl.program_id(0)]) * LOG2E
    row = lax.broadcasted_iota(jnp.int32, (tq, nk), 0)
    col = lax.broadcasted_iota(jnp.int32, (tq, nk), 1)
    qc = (qbase + row) // CHUNK
    kc = (kbase0 - v * kstep + col) // CHUNK
    y = jnp.where(kc <= qc, y, NEG)
    if band:
        y = jnp.where(kc >= qc - BAND_CHUNKS, y, NEG)
    y = jnp.where(col < nvalid, y, NEG)
    o_ref[0, 0] = y


def _bias_tiles(table_fn, offset, n_heads, tq, nk, n_var, qbase, kbase0, kstep, band, nvalid):
    length = 1 << (tq + nk - 1).bit_length()
    assert length >= tq + nk - 1 and min(kbase0 - (n_var - 1) * kstep, qbase) >= 0
    r = jnp.arange(length, dtype=jnp.int32)
    d = jnp.where(r < nk, r, r - length)
    kbase = kbase0 - kstep * jnp.arange(n_var, dtype=jnp.int32)
    rel = (kbase[:, None] - qbase) + d[None, :]
    gen = jnp.transpose(table_fn(rel), (2, 0, 1)).astype(F32)
    gen = gen.reshape(n_heads, n_var, 1, length)
    return pl.pallas_call(
        functools.partial(_bias_tile_kernel, tq=tq, nk=nk, qbase=qbase, kbase0=kbase0, kstep=kstep,
                          band=band, nvalid=nvalid),
        grid=(n_heads, n_var),
        in_specs=[pl.BlockSpec(memory_space=pltpu.SMEM),
                  pl.BlockSpec((1, 1, 1, length), lambda h, v: (h, v, 0, 0))],
        out_specs=pl.BlockSpec((1, 1, tq, nk), lambda h, v: (h, v, 0, 0)),
        out_shape=jax.ShapeDtypeStruct((n_heads, n_var, tq, nk), F32),
        name="bias_tiles",
    )(offset.astype(F32), gen)


def _split_halves(q):
    lane = lax.broadcasted_iota(jnp.int32, q.shape, 1)
    zero = jnp.zeros_like(q)
    return jnp.where(lane < LANES // 2, q, zero), jnp.where(lane >= LANES // 2, q, zero)


def _online_update(carry, s, v):
    m, l, acc = carry
    m_new = jnp.maximum(m, jnp.max(s, axis=-1, keepdims=True))
    alpha = jnp.exp2(m - m_new)
    p = jnp.exp2(s - m_new)
    l = alpha * l + jnp.sum(p, axis=-1, keepdims=True)
    acc = alpha * acc + jnp.dot(p.astype(BF16), v, preferred_element_type=F32)
    return m_new, l, acc


def _diff_init(tq):
    return (jnp.full((tq, 1), -jnp.inf, F32), jnp.zeros((tq, 1), F32), jnp.zeros((tq, LANES), F32))


def _diff_finish(c0, c1, lamv_ref, gsub_ref, g, lam_init):
    lamv = lamv_ref[...]
    e1 = jnp.exp(jnp.sum(lamv[0:1] * lamv[1:2], axis=-1, keepdims=True))
    e2 = jnp.exp(jnp.sum(lamv[2:3] * lamv[3:4], axis=-1, keepdims=True))
    lam = e1 - e2 + lam_init
    o = c0[2] / c0[1] - lam * (c1[2] / c1[1])
    o = o * lax.rsqrt(jnp.mean(o * o, axis=-1, keepdims=True) + EPS) * gsub_ref[...]
    o = o * (1.0 - lam_init)
    return o * _silu(g.astype(F32))


def _lane_partial_sum(p):
    out = p[:, :LANES]
    for c in range(1, p.shape[1] // LANES):
        out = out + p[:, c * LANES:(c + 1) * LANES]
    return out


def _bounded_update(carry, s, v):
    l, acc = carry
    p = jnp.exp2(s)
    return l + _lane_partial_sum(p), acc + jnp.dot(p.astype(BF16), v, preferred_element_type=F32)


def _attn_a_prompt_kernel(par_ref, q_ref, k_ref, v_ref, g_ref, bias_ref, lamv_ref, gsub_ref, o_ref, *, tq, n_var, lam_init):
    i = pl.program_id(2)
    n_far = jnp.maximum(i - (n_var - 1), 0)
    ws = pl.multiple_of(n_far * tq, tq)
    bounded = par_ref[0] > 0.5

    def run(update, init, finish):
        q0, q1 = _split_halves(q_ref[0])

        def far(j, carry):
            c0, c1 = carry
            start = pl.multiple_of(j * tq, tq)
            k = k_ref[0, pl.ds(start, tq), :]
            v = v_ref[0, pl.ds(start, tq), :]
            return update(c0, _nt(q0, k), v), update(c1, _nt(q1, k), v)

        c0, c1 = lax.fori_loop(0, n_far, far, (init, init))
        k = k_ref[0, pl.ds(ws, n_var * tq), :]
        v = v_ref[0, pl.ds(ws, n_var * tq), :]
        bias = bias_ref[0, 0]
        c0 = finish(update(c0, _nt(q0, k) + bias, v))
        c1 = finish(update(c1, _nt(q1, k) + bias, v))
        o_ref[0] = _diff_finish(c0, c1, lamv_ref, gsub_ref, g_ref[0], lam_init).astype(o_ref.dtype)

    @pl.when(bounded)
    def _():
        zeros = jnp.zeros((tq, LANES), F32)
        run(_bounded_update, (zeros, zeros), lambda c: (None, jnp.sum(c[0], axis=-1, keepdims=True), c[1]))

    @pl.when(jnp.logical_not(bounded))
    def _():
        run(_online_update, _diff_init(tq), lambda c: c)


def _logits_bounded(g_q, g_k, head_dim, table, offset):
    qk = math.sqrt(head_dim) * LOG2E * jnp.max(jnp.abs(g_q)) * jnp.max(jnp.abs(g_k)) * BF16_ROUND_MARGIN
    bias = jnp.max(jnp.abs(table - offset[None, :])) * LOG2E
    return (qk + bias <= MAX_DIRECT_LOGIT).astype(F32).reshape(1)


def _attn_a_prompt(qa, kab, vab, ga, bias, bounded, lamv, gsub, tq, lam_init):
    b, s, _ = qa.shape
    n_var = bias.shape[1]
    assert bias.shape[3] == n_var * tq
    tile = pl.BlockSpec((1, tq, LANES), lambda bi, h, i: (bi, i, h))
    full = pl.BlockSpec((1, s, LANES), lambda bi, h, i: (bi, 0, h))
    return pl.pallas_call(
        functools.partial(_attn_a_prompt_kernel, tq=tq, n_var=n_var, lam_init=lam_init),
        grid=(b, H_A, s // tq),
        in_specs=[pl.BlockSpec(memory_space=pltpu.SMEM), tile, full, full, tile,
                  pl.BlockSpec((1, 1, tq, n_var * tq), lambda bi, h, i: (h, jnp.minimum(i, n_var - 1), 0, 0)),
                  pl.BlockSpec((4, HD_A), lambda bi, h, i: (0, 0)),
                  pl.BlockSpec((1, LANES), lambda bi, h, i: (0, 0))],
        out_specs=tile,
        out_shape=jax.ShapeDtypeStruct((b, s, W_A), BF16),
        compiler_params=pltpu.CompilerParams(dimension_semantics=("arbitrary", "arbitrary", "arbitrary"),
                                             vmem_limit_bytes=VMEM_LIMIT),
        name="diff_attn_prompt",
    )(bounded, qa, kab, vab, ga, bias, lamv, gsub)


def _window_kv(cache_ref, new_ref, nk):
    past = cache_ref.shape[1]
    t = new_ref.shape[1]
    pad = jnp.zeros((nk - past - t, LANES), BF16)
    return jnp.concatenate([cache_ref[0].astype(BF16), new_ref[0], pad], axis=0)


def _attn_a_sample_kernel(q_ref, kc_ref, vc_ref, kn_ref, vn_ref, g_ref, bias_ref, lamv_ref, gsub_ref, o_ref, *, lam_init):
    tq = q_ref.shape[1]
    nk = bias_ref.shape[3]
    q0, q1 = _split_halves(q_ref[0])
    k = _window_kv(kc_ref, kn_ref, nk)
    v = _window_kv(vc_ref, vn_ref, nk)
    bias = bias_ref[0, 0]
    c0 = _online_update(_diff_init(tq), _nt(q0, k) + bias, v)
    c1 = _online_update(_diff_init(tq), _nt(q1, k) + bias, v)
    o_ref[0] = _diff_finish(c0, c1, lamv_ref, gsub_ref, g_ref[0], lam_init).astype(o_ref.dtype)


def _attn_a_sample(qa, kab, vab, ga, cache_k, cache_v, bias, lamv, gsub, lam_init):
    b, t, _ = qa.shape
    past = cache_k.shape[1]
    nk = bias.shape[3]
    new = pl.BlockSpec((1, t, LANES), lambda bi, h: (bi, 0, h))
    old = pl.BlockSpec((1, past, LANES), lambda bi, h: (bi, 0, h))
    return pl.pallas_call(
        functools.partial(_attn_a_sample_kernel, lam_init=lam_init),
        grid=(b, H_A),
        in_specs=[new, old, old, new, new, new,
                  pl.BlockSpec((1, 1, t, nk), lambda bi, h: (h, 0, 0, 0)),
                  pl.BlockSpec((4, HD_A), lambda bi, h: (0, 0)),
                  pl.BlockSpec((1, LANES), lambda bi, h: (0, 0))],
        out_specs=new,
        out_shape=jax.ShapeDtypeStruct((b, t, W_A), BF16),
        name="diff_attn_sample",
    )(qa, cache_k, cache_v, kab, vab, ga, bias, lamv, gsub)


def _band_pair(q, k, v, g, bias_ref, bounded=False):
    q0, q1 = _split_halves(q)
    v0, v1 = _split_halves(v)
    o = None
    for hh, (qh, vh) in enumerate(((q0, v0), (q1, v1))):
        s = _nt(qh, k) + bias_ref[hh, 0]
        if not bounded:
            s = s - jnp.max(s, axis=-1, keepdims=True)
        p = jnp.exp2(s)
        l = jnp.sum(p, axis=-1, keepdims=True)
        oh = jnp.dot(p.astype(BF16), vh, preferred_element_type=F32) / l
        o = oh if o is None else o + oh
    return o * _silu(g.astype(F32))


def _attn_b_prompt_kernel(par_ref, q_ref, k_ref, v_ref, g_ref, bias_ref, o_ref, *, tq, n_var):
    i = pl.program_id(2)
    ws = pl.multiple_of(jnp.maximum(i - (n_var - 1), 0) * tq, tq)
    bounded = par_ref[0] > 0.5

    def run(direct):
        k = k_ref[0, pl.ds(ws, n_var * tq), :]
        v = v_ref[0, pl.ds(ws, n_var * tq), :]
        o_ref[0] = _band_pair(q_ref[0], k, v, g_ref[0], bias_ref, direct).astype(o_ref.dtype)

    pl.when(bounded)(lambda: run(True))
    pl.when(jnp.logical_not(bounded))(lambda: run(False))


def _attn_b_prompt(qb, kbb, vbb, gb, bias, bounded, tq):
    b, s, _ = qb.shape
    n_var = bias.shape[1]
    assert bias.shape[3] == n_var * tq
    tile = pl.BlockSpec((1, tq, LANES), lambda bi, p, i: (bi, i, p))
    full = pl.BlockSpec((1, s, LANES), lambda bi, p, i: (bi, 0, p))
    return pl.pallas_call(
        functools.partial(_attn_b_prompt_kernel, tq=tq, n_var=n_var),
        grid=(b, H_B // 2, s // tq),
        in_specs=[pl.BlockSpec(memory_space=pltpu.SMEM), tile, full, full, tile,
                  pl.BlockSpec((2, 1, tq, n_var * tq), lambda bi, p, i: (p, jnp.minimum(i, n_var - 1), 0, 0))],
        out_specs=tile,
        out_shape=jax.ShapeDtypeStruct((b, s, W_B), BF16),
        compiler_params=pltpu.CompilerParams(dimension_semantics=("arbitrary", "arbitrary", "arbitrary"),
                                             vmem_limit_bytes=VMEM_LIMIT),
        name="band_attn_prompt",
    )(bounded, qb, kbb, vbb, gb, bias)


def _attn_b_sample_kernel(q_ref, kc_ref, vc_ref, kn_ref, vn_ref, g_ref, bias_ref, o_ref):
    nk = bias_ref.shape[3]
    k = _window_kv(kc_ref, kn_ref, nk)
    v = _window_kv(vc_ref, vn_ref, nk)
    o_ref[0] = _band_pair(q_ref[0], k, v, g_ref[0], bias_ref).astype(o_ref.dtype)


def _attn_b_sample(qb, kbb, vbb, gb, cache_k, cache_v, bias):
    b, t, _ = qb.shape
    past = cache_k.shape[1]
    nk = bias.shape[3]
    new = pl.BlockSpec((1, t, LANES), lambda bi, p: (bi, 0, p))
    old = pl.BlockSpec((1, past, LANES), lambda bi, p: (bi, 0, p))
    return pl.pallas_call(
        _attn_b_sample_kernel,
        grid=(b, H_B // 2),
        in_specs=[new, old, old, new, new, new,
                  pl.BlockSpec((2, 1, t, nk), lambda bi, p: (p, 0, 0, 0))],
        out_specs=new,
        out_shape=jax.ShapeDtypeStruct((b, t, W_B), BF16),
        name="band_attn_sample",
    )(qb, cache_k, cache_v, kbb, vbb, gb, bias)


def _merge_kernel(x_ref, gate_ref, oa_ref, ob_ref, mg_ref, woa_ref, wob_ref, wout_ref, o_ref):
    nb, t, d = x_ref.shape
    rows = nb * t
    ya = jnp.dot(oa_ref[...].reshape(rows, W_A), woa_ref[...], preferred_element_type=F32)
    yb = jnp.dot(ob_ref[...].reshape(rows, W_B), wob_ref[...], preferred_element_type=F32)
    mg = mg_ref[...].reshape(rows, 2 * d).astype(F32)
    m = _sigmoid(mg[:, :d]) * ya + _sigmoid(mg[:, d:]) * yb
    y = jnp.dot(m.astype(BF16), wout_ref[...], preferred_element_type=F32)
    o_ref[...] = x_ref[...] + gate_ref[...] * y.reshape(nb, t, d)


def _merge(x, gate, oa, ob, mg, w_oa_bf, w_ob_bf, w_out_bf, nb, t):
    bx, sx, d = x.shape
    tok = lambda w: pl.BlockSpec((nb, t, w), lambda b, i: (b, i, 0))
    const = lambda shp: pl.BlockSpec(shp, lambda b, i: (0,) * len(shp))
    return pl.pallas_call(
        _merge_kernel,
        grid=(bx // nb, sx // t),
        in_specs=[tok(d), pl.BlockSpec((nb, 1, d), lambda b, i: (b, 0, 0)), tok(W_A), tok(W_B), tok(2 * d),
                  const((W_A, d)), const((W_B, d)), const((d, d))],
        out_specs=tok(d),
        out_shape=jax.ShapeDtypeStruct((bx, sx, d), F32),
        compiler_params=pltpu.CompilerParams(dimension_semantics=("arbitrary", "arbitrary"),
                                             vmem_limit_bytes=VMEM_LIMIT),
        name="merge_out",
    )(x, gate, oa, ob, mg, w_oa_bf, w_ob_bf, w_out_bf)


TQ_A = 512
TQ_B = 256
TM_PROJ = 512
QBASE = 2048


def kernel(x_prompt, x_sample, cache_a_k, cache_a_v, cache_b_k, cache_b_v, c_prompt, c_sample, g_norm, w_ada, b_ada, w_in, g_qa, g_ka, lam_q1, lam_k1, lam_q2, lam_k2, g_subln, t5_bias, g_qb, g_kb, rel_bias_b, w_oa, w_ob, w_out):
    xp, xs = x_prompt, x_sample
    bp, s, d = xp.shape
    bs, t, _ = xs.shape
    depth = w_in.shape[0]
    past = cache_a_k.shape[2]
    lb = cache_b_k.shape[2]
    n_keep = min(BAND_PAST, s)
    assert s % TM_PROJ == 0 and s % TQ_A == 0 and s % TQ_B == 0 and TQ_A % CHUNK == 0 and TQ_B % CHUNK == 0
    assert 2 * TQ_B >= BAND_PAST and past >= lb
    assert _far_bucket(TQ_A + 1) == T5_BUCKETS // 2 - 1

    r = jnp.arange(NORM_GROUP)
    pm = ((r[:, None] // HD_A) == (r[None, :] // HD_A)).astype(BF16) * (1.0 / HD_A)
    c_rows = bp + bs
    c_pad = -(-c_rows // 8) * 8
    c_all = jnp.concatenate([c_prompt, c_sample, jnp.zeros((c_pad - c_rows, d), F32)], axis=0)

    nk_as = -(-(past + t) // LANES) * LANES
    nk_bs = -(-(lb + t) // LANES) * LANES
    t5_fn = lambda rel: t5_bias[_t5_bucket(rel)]
    outs = [[] for _ in range(8)]
    for l in range(depth):
        lam_init = 0.8 - 0.6 * math.exp(-0.3 * l)
        rel_fn = lambda rel: rel_bias_b[l][jnp.clip(rel, -REL_CLIP_B, REL_CLIP_B) + REL_CLIP_B]
        cfar = t5_bias[_t5_bucket(jnp.int32(-(TQ_A + 1)))]
        bias_ap = _bias_tiles(t5_fn, cfar, H_A, TQ_A, 2 * TQ_A, 2, QBASE, QBASE, TQ_A, False, 2 * TQ_A)
        bias_bp = _bias_tiles(rel_fn, jnp.zeros((H_B,)), H_B, TQ_B, 3 * TQ_B, 3, QBASE, QBASE, TQ_B, True, 3 * TQ_B)
        bias_as = _bias_tiles(t5_fn, jnp.zeros((H_A,)), H_A, t, nk_as, 1, past, 0, 0, False, past + t)
        bias_bs = _bias_tiles(rel_fn, jnp.zeros((H_B,)), H_B, t, nk_bs, 1, past, past - lb, 0, True, lb + t)

        mod = _modulation(c_all, w_ada[l], b_ada[l])
        shift = mod[:, :d].reshape(c_pad, 1, d)
        scale = mod[:, d:2 * d].reshape(c_pad, 1, d)
        gate = mod[:, 2 * d:].reshape(c_pad, 1, d)
        w_in_bf = w_in[l].astype(BF16)
        w_oa_bf, w_ob_bf, w_out_bf = w_oa[l].astype(BF16), w_ob[l].astype(BF16), w_out[l].astype(BF16)
        tile8 = lambda g: jnp.tile(g, SEG // g.shape[0]).reshape(1, SEG)
        gains = (tile8(g_qa[l]), tile8(g_ka[l]), tile8(g_qb[l]), tile8(g_kb[l]))
        lamv = jnp.stack([lam_q1[l], lam_k1[l], lam_q2[l], lam_k2[l]])
        gsub = g_subln[l].reshape(1, LANES)

        (qa, ka32, kab, va32, vab, ga, qb, kb32, kbb, vb32, vbb, gb, mg) = _project(
            xp, shift[:bp], scale[:bp], g_norm[l], w_in_bf, *gains, pm, 1, TM_PROJ)
        bounded_a = _logits_bounded(g_qa[l], g_ka[l], HD_A, t5_bias, cfar)
        bounded_b = _logits_bounded(g_qb[l], g_kb[l], HD_B, rel_bias_b[l], jnp.zeros((H_B,)))
        oa = _attn_a_prompt(qa, kab, vab, ga, bias_ap, bounded_a, lamv, gsub, TQ_A, lam_init)
        ob = _attn_b_prompt(qb, kbb, vbb, gb, bias_bp, bounded_b, TQ_B)
        xp = _merge(xp, gate[:bp], oa, ob, mg, w_oa_bf, w_ob_bf, w_out_bf, 1, TM_PROJ)
        outs[0].append(ka32.reshape(bp, s, H_A, 2 * HD_A))
        outs[1].append(va32.reshape(bp, s, H_A, 2 * HD_A))
        outs[2].append(kb32[:, s - n_keep:].reshape(bp, n_keep, H_B, HD_B))
        outs[3].append(vb32[:, s - n_keep:].reshape(bp, n_keep, H_B, HD_B))

        (qa, ka32, kab, va32, vab, ga, qb, kb32, kbb, vb32, vbb, gb, mg) = _project(
            xs, shift[bp:c_rows], scale[bp:c_rows], g_norm[l], w_in_bf, *gains, pm, bs, t)
        oa = _attn_a_sample(qa, kab, vab, ga, cache_a_k[l].reshape(bs, past, W_A), cache_a_v[l].reshape(bs, past, W_A),
                            bias_as, lamv, gsub, lam_init)
        ob = _attn_b_sample(qb, kbb, vbb, gb, cache_b_k[l].reshape(bs, lb, W_B), cache_b_v[l].reshape(bs, lb, W_B), bias_bs)
        xs = _merge(xs, gate[bp:c_rows], oa, ob, mg, w_oa_bf, w_ob_bf, w_out_bf, bs, t)
        outs[4].append(ka32.reshape(bs, t, H_A, 2 * HD_A))
        outs[5].append(va32.reshape(bs, t, H_A, 2 * HD_A))
        outs[6].append(kb32.reshape(bs, t, H_B, HD_B))
        outs[7].append(vb32.reshape(bs, t, H_B, HD_B))

    return (xp, xs) + tuple(jnp.stack(o) for o in outs)
```

```python
import functools
import math

import jax
import jax.numpy as jnp
from jax import lax
from jax.experimental import pallas as pl
from jax.experimental.pallas import tpu as pltpu

CHUNK = 64
H_A = 4
HD_A = 64
W_A = H_A * 2 * HD_A
H_B = 8
HD_B = 64
W_B = H_B * HD_B
BAND_CHUNKS = 8
BAND_PAST = BAND_CHUNKS * CHUNK
REL_CLIP_B = 128
T5_BUCKETS = 32
T5_MAX_EXACT = 8
T5_MAX_DIST = 128
EPS = 1e-6
NEG = -1e30

LANES = 128
SEG = 512
NORM_GROUP = 256
MAX_DIRECT_LOGIT = 60.0
BF16_ROUND_MARGIN = 1.02
VMEM_LIMIT = 56 * 1024 * 1024

LOG2E = math.log2(math.e)

F32 = jnp.float32
BF16 = jnp.bfloat16


def _t5_bucket(rel):
    half = T5_BUCKETS // 2
    ret = jnp.where(rel > 0, half, 0)
    n = jnp.abs(rel)
    nf = jnp.maximum(n, 1).astype(jnp.float32)
    large = T5_MAX_EXACT + (jnp.log(nf / T5_MAX_EXACT) / math.log(T5_MAX_DIST / T5_MAX_EXACT) * (half - T5_MAX_EXACT)).astype(jnp.int32)
    large = jnp.minimum(large, half - 1)
    return ret + jnp.where(n < T5_MAX_EXACT, n, large)


def _far_bucket(n):
    half = T5_BUCKETS // 2
    return min(T5_MAX_EXACT + int(math.log(n / T5_MAX_EXACT) / math.log(T5_MAX_DIST / T5_MAX_EXACT) * (half - T5_MAX_EXACT)), half - 1)


def _nt(a, b):
    return lax.dot_general(a, b, (((1,), (1,)), ((), ())), preferred_element_type=F32)


def _silu(x):
    return x * (1.0 / (1.0 + jnp.exp(-x)))


def _sigmoid(x):
    return 1.0 / (1.0 + jnp.exp(-x))


def _mod_kernel(c_ref, w_ref, b_ref, o_ref):
    c = c_ref[...]
    o_ref[...] = jnp.dot(_silu(c), w_ref[...], preferred_element_type=F32,
                         precision=lax.Precision.HIGHEST) + b_ref[...]


def _modulation(c_all, w_ada, b_ada):
    rows, d = c_all.shape
    n_out = w_ada.shape[1]
    tn = d
    return pl.pallas_call(
        _mod_kernel,
        grid=(n_out // tn,),
        in_specs=[pl.BlockSpec((rows, d), lambda j: (0, 0)),
                  pl.BlockSpec((d, tn), lambda j: (0, j)),
                  pl.BlockSpec((1, tn), lambda j: (0, j))],
        out_specs=pl.BlockSpec((rows, tn), lambda j: (0, j)),
        out_shape=jax.ShapeDtypeStruct((rows, n_out), F32),
        name="adaln_mod",
    )(c_all, w_ada, b_ada.reshape(1, n_out))


def _proj_kernel(x_ref, shift_ref, scale_ref, gn_ref, w_ref, gqa_ref, gka_ref, gqb_ref, gkb_ref, pm_ref,
                 qa_ref, ka32_ref, kab_ref, va32_ref, vab_ref, ga_ref,
                 qb_ref, kb32_ref, kbb_ref, vb32_ref, vbb_ref, gb_ref, mg_ref):
    nb, t, d = x_ref.shape
    rows = nb * t
    x = x_ref[...]
    ms = jnp.mean(x * x, axis=-1, keepdims=True)
    xn = x * lax.rsqrt(ms + EPS) * gn_ref[...]
    h = xn * (1.0 + scale_ref[...]) + shift_ref[...]
    hb = h.reshape(rows, d).astype(BF16)

    def seg(c, width=SEG):
        return jnp.dot(hb, w_ref[:, c * SEG:c * SEG + width], preferred_element_type=F32)

    def head_norm(y, g_ref):
        sq = y * y
        hi = sq.astype(BF16)
        lo = (sq - hi.astype(F32)).astype(BF16)
        pm = pm_ref[...]
        parts = []
        for c in range(SEG // NORM_GROUP):
            sl = slice(c * NORM_GROUP, (c + 1) * NORM_GROUP)
            gms = jnp.dot(hi[:, sl], pm, preferred_element_type=F32) + jnp.dot(lo[:, sl], pm, preferred_element_type=F32)
            parts.append(y[:, sl] * lax.rsqrt(gms + EPS))
        return jnp.concatenate(parts, axis=1) * g_ref[...]

    def put(ref, y):
        ref[...] = y.astype(ref.dtype).reshape(ref.shape)

    def put_by_head(ref, y):
        for hh in range(H_A):
            ref[:, pl.ds(hh, t, stride=H_A), :] = y[:, hh * LANES:(hh + 1) * LANES].reshape(nb, t, LANES)

    put(qa_ref, head_norm(seg(0), gqa_ref) * (HD_A ** -0.5 * LOG2E))
    ka = head_norm(seg(1), gka_ref)
    put_by_head(ka32_ref, ka)
    put(kab_ref, ka)
    va = seg(2)
    put_by_head(va32_ref, va)
    put(vab_ref, va)
    put(ga_ref, seg(3))
    put(qb_ref, head_norm(seg(4), gqb_ref) * (HD_B ** -0.5 * LOG2E))
    kb = head_norm(seg(5), gkb_ref)
    put(kb32_ref, kb)
    put(kbb_ref, kb)
    vb = seg(6)
    put(vb32_ref, vb)
    put(vbb_ref, vb)
    put(gb_ref, seg(7))
    for c in range(8, 12):
        mg_ref[:, :, (c - 8) * SEG:(c - 7) * SEG] = seg(c).astype(mg_ref.dtype).reshape(nb, t, SEG)


def _project(x, shift, scale, g_norm, w_in_bf, gqa, gka, gqb, gkb, pm, nb, t):
    bx, sx, d = x.shape
    n_cols = w_in_bf.shape[1]
    grid = (bx // nb, sx // t)
    tok = lambda w: pl.BlockSpec((nb, t, w), lambda b, i: (b, i, 0))
    per_b = pl.BlockSpec((nb, 1, d), lambda b, i: (b, 0, 0))
    const = lambda shp: pl.BlockSpec(shp, lambda b, i: (0,) * len(shp))
    sds = lambda w, dt: jax.ShapeDtypeStruct((bx, sx, w), dt)
    by_head = jax.ShapeDtypeStruct((bx, sx * H_A, LANES), F32)
    out_shape = (sds(SEG, BF16), by_head, sds(SEG, BF16), by_head, sds(SEG, BF16), sds(SEG, BF16),
                 sds(SEG, BF16), sds(SEG, F32), sds(SEG, BF16), sds(SEG, F32), sds(SEG, BF16), sds(SEG, BF16),
                 sds(4 * SEG, BF16))
    out_specs = tuple(pl.BlockSpec((nb, s.shape[1] // (sx // t), s.shape[2]), lambda b, i: (b, i, 0)) for s in out_shape)
    return pl.pallas_call(
        _proj_kernel,
        grid=grid,
        in_specs=[tok(d), per_b, per_b, const((1, d)),
                  pl.BlockSpec((d, n_cols), lambda b, i: (0, 0), pipeline_mode=pl.Buffered(1)),
                  const((1, SEG)), const((1, SEG)), const((1, SEG)), const((1, SEG)),
                  const((NORM_GROUP, NORM_GROUP))],
        out_specs=out_specs,
        out_shape=out_shape,
        compiler_params=pltpu.CompilerParams(dimension_semantics=("arbitrary", "arbitrary"),
                                             vmem_limit_bytes=VMEM_LIMIT),
        name="in_proj",
    )(x, shift, scale, g_norm.reshape(1, d), w_in_bf, gqa, gka, gqb, gkb, pm)


def _bias_tile_kernel(off_ref, gen_ref, o_ref, *, tq, nk, qbase, kbase0, kstep, band, nvalid):
    v = pl.program_id(1)
    g = gen_ref[0, 0]
    x = jnp.broadcast_to(g, (tq, g.shape[-1]))
    y = pltpu.roll(x, 0, 1, stride=1, stride_axis=0)
    y = (y[:, :nk] - off_ref[pl.program_id(0)]) * LOG2E
    row = lax.broadcasted_iota(jnp.int32, (tq, nk), 0)
    col = lax.broadcasted_iota(jnp.int32, (tq, nk), 1)
    qc = (qbase + row) // CHUNK
    kc = (kbase0 - v * kstep + col) // CHUNK
    y = jnp.where(kc <= qc, y, NEG)
    if band:
        y = jnp.where(kc >= qc - BAND_CHUNKS, y, NEG)
    y = jnp.where(col < nvalid, y, NEG)
    o_ref[0, 0] = y


def _bias_tiles(table_fn, offset, n_heads, tq, nk, n_var, qbase, kbase0, kstep, band, nvalid):
    length = 1 << (tq + nk - 1).bit_length()
    assert length >= tq + nk - 1 and min(kbase0 - (n_var - 1) * kstep, qbase) >= 0
    r = jnp.arange(length, dtype=jnp.int32)
    d = jnp.where(r < nk, r, r - length)
    kbase = kbase0 - kstep * jnp.arange(n_var, dtype=jnp.int32)
    rel = (kbase[:, None] - qbase) + d[None, :]
    gen = jnp.transpose(table_fn(rel), (2, 0, 1)).astype(F32)
    gen = gen.reshape(n_heads, n_var, 1, length)
    return pl.pallas_call(
        functools.partial(_bias_tile_kernel, tq=tq, nk=nk, qbase=qbase, kbase0=kbase0, kstep=kstep,
                          band=band, nvalid=nvalid),
        grid=(n_heads, n_var),
        in_specs=[pl.BlockSpec(memory_space=pltpu.SMEM),
                  pl.BlockSpec((1, 1, 1, length), lambda h, v: (h, v, 0, 0))],
        out_specs=pl.BlockSpec((1, 1, tq, nk), lambda h, v: (h, v, 0, 0)),
        out_shape=jax.ShapeDtypeStruct((n_heads, n_var, tq, nk), F32),
        name="bias_tiles",
    )(offset.astype(F32), gen)


def _split_halves(q):
    lane = lax.broadcasted_iota(jnp.int32, q.shape, 1)
    zero = jnp.zeros_like(q)
    return jnp.where(lane < LANES // 2, q, zero), jnp.where(lane >= LANES // 2, q, zero)


def _online_update(carry, s, v):
    m, l, acc = carry
    m_new = jnp.maximum(m, jnp.max(s, axis=-1, keepdims=True))
    alpha = jnp.exp2(m - m_new)
    p = jnp.exp2(s - m_new)
    l = alpha * l + jnp.sum(p, axis=-1, keepdims=True)
    acc = alpha * acc + jnp.dot(p.astype(BF16), v, preferred_element_type=F32)
    return m_new, l, acc


def _diff_init(tq):
    return (jnp.full((tq, 1), -jnp.inf, F32), jnp.zeros((tq, 1), F32), jnp.zeros((tq, LANES), F32))


def _diff_finish(c0, c1, lamv_ref, gsub_ref, g, lam_init):
    lamv = lamv_ref[...]
    e1 = jnp.exp(jnp.sum(lamv[0:1] * lamv[1:2], axis=-1, keepdims=True))
    e2 = jnp.exp(jnp.sum(lamv[2:3] * lamv[3:4], axis=-1, keepdims=True))
    lam = e1 - e2 + lam_init
    o = c0[2] / c0[1] - lam * (c1[2] / c1[1])
    o = o * lax.rsqrt(jnp.mean(o * o, axis=-1, keepdims=True) + EPS) * gsub_ref[...]
    o = o * (1.0 - lam_init)
    return o * _silu(g.astype(F32))


def _lane_partial_sum(p):
    out = p[:, :LANES]
    for c in range(1, p.shape[1] // LANES):
        out = out + p[:, c * LANES:(c + 1) * LANES]
    return out


def _bounded_update(carry, s, v):
    l, acc = carry
    p = jnp.exp2(s)
    return l + _lane_partial_sum(p), acc + jnp.dot(p.astype(BF16), v, preferred_element_type=F32)


def _attn_a_prompt_kernel(par_ref, q_ref, k_ref, v_ref, g_ref, bias_ref, lamv_ref, gsub_ref, o_ref, *, tq, n_var, lam_init):
    i = pl.program_id(2)
    n_far = jnp.maximum(i - (n_var - 1), 0)
    ws = pl.multiple_of(n_far * tq, tq)
    bounded = par_ref[0] > 0.5

    n_heads = q_ref.shape[2] // LANES

    def run(update, init, finish):
        hs = [slice(hh * LANES, (hh + 1) * LANES) for hh in range(n_heads)]
        qs = [_split_halves(q_ref[0, :, sl]) for sl in hs]

        def far(j, carry):
            start = pl.multiple_of(j * tq, tq)
            out = []
            for (q0, q1), sl, (c0, c1) in zip(qs, hs, carry):
                k = k_ref[0, pl.ds(start, tq), sl]
                v = v_ref[0, pl.ds(start, tq), sl]
                out.append((update(c0, _nt(q0, k), v), update(c1, _nt(q1, k), v)))
            return tuple(out)

        carry = lax.fori_loop(0, n_far, far, ((init, init),) * n_heads)
        for hh, ((q0, q1), sl, (c0, c1)) in enumerate(zip(qs, hs, carry)):
            k = k_ref[0, pl.ds(ws, n_var * tq), sl]
            v = v_ref[0, pl.ds(ws, n_var * tq), sl]
            bias = bias_ref[hh, 0]
            c0 = finish(update(c0, _nt(q0, k) + bias, v))
            c1 = finish(update(c1, _nt(q1, k) + bias, v))
            o_ref[0, :, sl] = _diff_finish(c0, c1, lamv_ref, gsub_ref, g_ref[0, :, sl], lam_init).astype(o_ref.dtype)

    @pl.when(bounded)
    def _():
        zeros = jnp.zeros((tq, LANES), F32)
        run(_bounded_update, (zeros, zeros), lambda c: (None, jnp.sum(c[0], axis=-1, keepdims=True), c[1]))

    @pl.when(jnp.logical_not(bounded))
    def _():
        run(_online_update, _diff_init(tq), lambda c: c)


def _logits_bounded(g_q, g_k, head_dim, table, offset):
    qk = math.sqrt(head_dim) * LOG2E * jnp.max(jnp.abs(g_q)) * jnp.max(jnp.abs(g_k)) * BF16_ROUND_MARGIN
    bias = jnp.max(jnp.abs(table - offset[None, :])) * LOG2E
    return (qk + bias <= MAX_DIRECT_LOGIT).astype(F32).reshape(1)


def _attn_a_prompt(qa, kab, vab, ga, bias, bounded, lamv, gsub, tq, lam_init):
    b, s, _ = qa.shape
    n_var = bias.shape[1]
    assert bias.shape[3] == n_var * tq
    hps = HEADS_PER_STEP_A
    tile = pl.BlockSpec((1, tq, hps * LANES), lambda bi, h, i: (bi, i, h))
    full = pl.BlockSpec((1, s, hps * LANES), lambda bi, h, i: (bi, 0, h))
    return pl.pallas_call(
        functools.partial(_attn_a_prompt_kernel, tq=tq, n_var=n_var, lam_init=lam_init),
        grid=(b, H_A // hps, s // tq),
        in_specs=[pl.BlockSpec(memory_space=pltpu.SMEM), tile, full, full, tile,
                  pl.BlockSpec((hps, 1, tq, n_var * tq), lambda bi, h, i: (h, jnp.minimum(i, n_var - 1), 0, 0)),
                  pl.BlockSpec((4, HD_A), lambda bi, h, i: (0, 0)),
                  pl.BlockSpec((1, LANES), lambda bi, h, i: (0, 0))],
        out_specs=tile,
        out_shape=jax.ShapeDtypeStruct((b, s, W_A), BF16),
        compiler_params=pltpu.CompilerParams(dimension_semantics=("arbitrary", "arbitrary", "arbitrary"),
                                             vmem_limit_bytes=VMEM_LIMIT),
        name="diff_attn_prompt",
    )(bounded, qa, kab, vab, ga, bias, lamv, gsub)


def _window_kv(cache_ref, new_ref, nk):
    past = cache_ref.shape[1]
    t = new_ref.shape[1]
    pad = jnp.zeros((nk - past - t, LANES), BF16)
    return jnp.concatenate([cache_ref[0].astype(BF16), new_ref[0], pad], axis=0)


def _attn_a_sample_kernel(q_ref, kc_ref, vc_ref, kn_ref, vn_ref, g_ref, bias_ref, lamv_ref, gsub_ref, o_ref, *, lam_init):
    tq = q_ref.shape[1]
    nk = bias_ref.shape[3]
    past = kc_ref.shape[1] // H_A
    pad = jnp.zeros((nk - past - tq, LANES), BF16)
    for hh in range(H_A):
        sl = slice(hh * LANES, (hh + 1) * LANES)
        rows = pl.ds(hh, past, stride=H_A)
        q0, q1 = _split_halves(q_ref[0, :, sl])
        k = jnp.concatenate([kc_ref[0, rows, :].astype(BF16), kn_ref[0, :, sl], pad], axis=0)
        v = jnp.concatenate([vc_ref[0, rows, :].astype(BF16), vn_ref[0, :, sl], pad], axis=0)
        bias = bias_ref[hh, 0]
        c0 = _online_update(_diff_init(tq), _nt(q0, k) + bias, v)
        c1 = _online_update(_diff_init(tq), _nt(q1, k) + bias, v)
        o_ref[0, :, sl] = _diff_finish(c0, c1, lamv_ref, gsub_ref, g_ref[0, :, sl], lam_init).astype(o_ref.dtype)


def _attn_a_sample(qa, kab, vab, ga, cache_k, cache_v, bias, lamv, gsub, lam_init):
    b, t, _ = qa.shape
    rows = cache_k.shape[1]
    nk = bias.shape[3]
    new = pl.BlockSpec((1, t, W_A), lambda bi: (bi, 0, 0))
    old = pl.BlockSpec((1, rows, LANES), lambda bi: (bi, 0, 0))
    return pl.pallas_call(
        functools.partial(_attn_a_sample_kernel, lam_init=lam_init),
        grid=(b,),
        in_specs=[new, old, old, new, new, new,
                  pl.BlockSpec((H_A, 1, t, nk), lambda bi: (0, 0, 0, 0)),
                  pl.BlockSpec((4, HD_A), lambda bi: (0, 0)),
                  pl.BlockSpec((1, LANES), lambda bi: (0, 0))],
        out_specs=new,
        out_shape=jax.ShapeDtypeStruct((b, t, W_A), BF16),
        name="diff_attn_sample",
    )(qa, cache_k, cache_v, kab, vab, ga, bias, lamv, gsub)


def _band_pair(q, k, v, g, bias_ref, bounded=False, first_head=0):
    q0, q1 = _split_halves(q)
    v0, v1 = _split_halves(v)
    o = None
    for hh, (qh, vh) in enumerate(((q0, v0), (q1, v1))):
        s = _nt(qh, k) + bias_ref[first_head + hh, 0]
        if not bounded:
            s = s - jnp.max(s, axis=-1, keepdims=True)
        p = jnp.exp2(s)
        l = jnp.sum(p, axis=-1, keepdims=True)
        oh = jnp.dot(p.astype(BF16), vh, preferred_element_type=F32) / l
        o = oh if o is None else o + oh
    return o * _silu(g.astype(F32))


def _attn_b_prompt_kernel(par_ref, q_ref, k_ref, v_ref, g_ref, bias_ref, o_ref, *, tq, n_var):
    i = pl.program_id(2)
    ws = pl.multiple_of(jnp.maximum(i - (n_var - 1), 0) * tq, tq)
    bounded = par_ref[0] > 0.5

    def run(direct):
        for pp in range(q_ref.shape[2] // LANES):
            sl = slice(pp * LANES, (pp + 1) * LANES)
            k = k_ref[0, pl.ds(ws, n_var * tq), sl]
            v = v_ref[0, pl.ds(ws, n_var * tq), sl]
            o = _band_pair(q_ref[0, :, sl], k, v, g_ref[0, :, sl], bias_ref, direct, 2 * pp)
            o_ref[0, :, sl] = o.astype(o_ref.dtype)

    pl.when(bounded)(lambda: run(True))
    pl.when(jnp.logical_not(bounded))(lambda: run(False))


def _attn_b_prompt(qb, kbb, vbb, gb, bias, bounded, tq):
    b, s, w = qb.shape
    n_var = bias.shape[1]
    assert bias.shape[3] == n_var * tq
    tile = pl.BlockSpec((1, tq, w), lambda bi, p, i: (bi, i, p))
    full = pl.BlockSpec((1, s, w), lambda bi, p, i: (bi, 0, p), pipeline_mode=pl.Buffered(1))
    return pl.pallas_call(
        functools.partial(_attn_b_prompt_kernel, tq=tq, n_var=n_var),
        grid=(b, 1, s // tq),
        in_specs=[pl.BlockSpec(memory_space=pltpu.SMEM), tile, full, full, tile,
                  pl.BlockSpec((H_B, 1, tq, n_var * tq), lambda bi, p, i: (p, jnp.minimum(i, n_var - 1), 0, 0))],
        out_specs=tile,
        out_shape=jax.ShapeDtypeStruct((b, s, W_B), BF16),
        compiler_params=pltpu.CompilerParams(dimension_semantics=("arbitrary", "arbitrary", "arbitrary"),
                                             vmem_limit_bytes=VMEM_LIMIT),
        name="band_attn_prompt",
    )(bounded, qb, kbb, vbb, gb, bias)


def _attn_b_sample_kernel(q_ref, kc_ref, vc_ref, kn_ref, vn_ref, g_ref, bias_ref, o_ref):
    nk = bias_ref.shape[3]
    k = _window_kv(kc_ref, kn_ref, nk)
    v = _window_kv(vc_ref, vn_ref, nk)
    o_ref[0] = _band_pair(q_ref[0], k, v, g_ref[0], bias_ref).astype(o_ref.dtype)


def _attn_b_sample(qb, kbb, vbb, gb, cache_k, cache_v, bias):
    b, t, _ = qb.shape
    past = cache_k.shape[1]
    nk = bias.shape[3]
    new = pl.BlockSpec((1, t, LANES), lambda bi, p: (bi, 0, p))
    old = pl.BlockSpec((1, past, LANES), lambda bi, p: (bi, 0, p))
    return pl.pallas_call(
        _attn_b_sample_kernel,
        grid=(b, H_B // 2),
        in_specs=[new, old, old, new, new, new,
                  pl.BlockSpec((2, 1, t, nk), lambda bi, p: (p, 0, 0, 0))],
        out_specs=new,
        out_shape=jax.ShapeDtypeStruct((b, t, W_B), BF16),
        name="band_attn_sample",
    )(qb, cache_k, cache_v, kbb, vbb, gb, bias)


def _merge_kernel(x_ref, gate_ref, oa_ref, ob_ref, mg_ref, woa_ref, wob_ref, wout_ref, o_ref):
    nb, t, d = x_ref.shape
    rows = nb * t
    ya = jnp.dot(oa_ref[...].reshape(rows, W_A), woa_ref[...], preferred_element_type=F32)
    yb = jnp.dot(ob_ref[...].reshape(rows, W_B), wob_ref[...], preferred_element_type=F32)
    mg = mg_ref[...].reshape(rows, 2 * d).astype(F32)
    m = _sigmoid(mg[:, :d]) * ya + _sigmoid(mg[:, d:]) * yb
    y = jnp.dot(m.astype(BF16), wout_ref[...], preferred_element_type=F32)
    o_ref[...] = x_ref[...] + gate_ref[...] * y.reshape(nb, t, d)


def _merge(x, gate, oa, ob, mg, w_oa_bf, w_ob_bf, w_out_bf, nb, t):
    bx, sx, d = x.shape
    tok = lambda w: pl.BlockSpec((nb, t, w), lambda b, i: (b, i, 0))
    const = lambda shp: pl.BlockSpec(shp, lambda b, i: (0,) * len(shp))
    return pl.pallas_call(
        _merge_kernel,
        grid=(bx // nb, sx // t),
        in_specs=[tok(d), pl.BlockSpec((nb, 1, d), lambda b, i: (b, 0, 0)), tok(W_A), tok(W_B), tok(2 * d),
                  const((W_A, d)), const((W_B, d)), const((d, d))],
        out_specs=tok(d),
        out_shape=jax.ShapeDtypeStruct((bx, sx, d), F32),
        compiler_params=pltpu.CompilerParams(dimension_semantics=("arbitrary", "arbitrary"),
                                             vmem_limit_bytes=VMEM_LIMIT),
        name="merge_out",
    )(x, gate, oa, ob, mg, w_oa_bf, w_ob_bf, w_out_bf)


TQ_A = 512
HEADS_PER_STEP_A = 2
TQ_B = 256
TM_PROJ = 512
QBASE = 2048


def kernel(x_prompt, x_sample, cache_a_k, cache_a_v, cache_b_k, cache_b_v, c_prompt, c_sample, g_norm, w_ada, b_ada, w_in, g_qa, g_ka, lam_q1, lam_k1, lam_q2, lam_k2, g_subln, t5_bias, g_qb, g_kb, rel_bias_b, w_oa, w_ob, w_out):
    xp, xs = x_prompt, x_sample
    bp, s, d = xp.shape
    bs, t, _ = xs.shape
    depth = w_in.shape[0]
    past = cache_a_k.shape[2]
    lb = cache_b_k.shape[2]
    n_keep = min(BAND_PAST, s)
    assert s % TM_PROJ == 0 and s % TQ_A == 0 and s % TQ_B == 0 and TQ_A % CHUNK == 0 and TQ_B % CHUNK == 0
    assert 2 * TQ_B >= BAND_PAST and past >= lb
    assert _far_bucket(TQ_A + 1) == T5_BUCKETS // 2 - 1

    r = jnp.arange(NORM_GROUP)
    pm = ((r[:, None] // HD_A) == (r[None, :] // HD_A)).astype(BF16) * (1.0 / HD_A)
    c_rows = bp + bs
    c_pad = -(-c_rows // 8) * 8
    c_all = jnp.concatenate([c_prompt, c_sample, jnp.zeros((c_pad - c_rows, d), F32)], axis=0)

    nk_as = -(-(past + t) // LANES) * LANES
    nk_bs = -(-(lb + t) // LANES) * LANES
    t5_fn = lambda rel: t5_bias[_t5_bucket(rel)]
    outs = [[] for _ in range(8)]
    for l in range(depth):
        lam_init = 0.8 - 0.6 * math.exp(-0.3 * l)
        rel_fn = lambda rel: rel_bias_b[l][jnp.clip(rel, -REL_CLIP_B, REL_CLIP_B) + REL_CLIP_B]
        cfar = t5_bias[_t5_bucket(jnp.int32(-(TQ_A + 1)))]
        bias_ap = _bias_tiles(t5_fn, cfar, H_A, TQ_A, 2 * TQ_A, 2, QBASE, QBASE, TQ_A, False, 2 * TQ_A)
        bias_bp = _bias_tiles(rel_fn, jnp.zeros((H_B,)), H_B, TQ_B, 3 * TQ_B, 3, QBASE, QBASE, TQ_B, True, 3 * TQ_B)
        bias_as = _bias_tiles(t5_fn, jnp.zeros((H_A,)), H_A, t, nk_as, 1, past, 0, 0, False, past + t)
        bias_bs = _bias_tiles(rel_fn, jnp.zeros((H_B,)), H_B, t, nk_bs, 1, past, past - lb, 0, True, lb + t)

        mod = _modulation(c_all, w_ada[l], b_ada[l])
        shift = mod[:, :d].reshape(c_pad, 1, d)
        scale = mod[:, d:2 * d].reshape(c_pad, 1, d)
        gate = mod[:, 2 * d:].reshape(c_pad, 1, d)
        w_in_bf = w_in[l].astype(BF16)
        w_oa_bf, w_ob_bf, w_out_bf = w_oa[l].astype(BF16), w_ob[l].astype(BF16), w_out[l].astype(BF16)
        tile8 = lambda g: jnp.tile(g, SEG // g.shape[0]).reshape(1, SEG)
        gains = (tile8(g_qa[l]), tile8(g_ka[l]), tile8(g_qb[l]), tile8(g_kb[l]))
        lamv = jnp.stack([lam_q1[l], lam_k1[l], lam_q2[l], lam_k2[l]])
        gsub = g_subln[l].reshape(1, LANES)

        (qa, ka32, kab, va32, vab, ga, qb, kb32, kbb, vb32, vbb, gb, mg) = _project(
            xp, shift[:bp], scale[:bp], g_norm[l], w_in_bf, *gains, pm, 1, TM_PROJ)
        bounded_a = _logits_bounded(g_qa[l], g_ka[l], HD_A, t5_bias, cfar)
        bounded_b = _logits_bounded(g_qb[l], g_kb[l], HD_B, rel_bias_b[l], jnp.zeros((H_B,)))
        oa = _attn_a_prompt(qa, kab, vab, ga, bias_ap, bounded_a, lamv, gsub, TQ_A, lam_init)
        ob = _attn_b_prompt(qb, kbb, vbb, gb, bias_bp, bounded_b, TQ_B)
        xp = _merge(xp, gate[:bp], oa, ob, mg, w_oa_bf, w_ob_bf, w_out_bf, 1, TM_PROJ)
        outs[0].append(ka32.reshape(bp, s, H_A, 2 * HD_A))
        outs[1].append(va32.reshape(bp, s, H_A, 2 * HD_A))
        outs[2].append(kb32[:, s - n_keep:].reshape(bp, n_keep, H_B, HD_B))
        outs[3].append(vb32[:, s - n_keep:].reshape(bp, n_keep, H_B, HD_B))

        (qa, ka32, kab, va32, vab, ga, qb, kb32, kbb, vb32, vbb, gb, mg) = _project(
            xs, shift[bp:c_rows], scale[bp:c_rows], g_norm[l], w_in_bf, *gains, pm, bs, t)
        oa = _attn_a_sample(qa, kab, vab, ga, cache_a_k[l].reshape(bs, past * H_A, LANES), cache_a_v[l].reshape(bs, past * H_A, LANES),
                            bias_as, lamv, gsub, lam_init)
        ob = _attn_b_sample(qb, kbb, vbb, gb, cache_b_k[l].reshape(bs, lb, W_B), cache_b_v[l].reshape(bs, lb, W_B), bias_bs)
        xs = _merge(xs, gate[bp:c_rows], oa, ob, mg, w_oa_bf, w_ob_bf, w_out_bf, bs, t)
        outs[4].append(ka32.reshape(bs, t, H_A, 2 * HD_A))
        outs[5].append(va32.reshape(bs, t, H_A, 2 * HD_A))
        outs[6].append(kb32.reshape(bs, t, H_B, HD_B))
        outs[7].append(vb32.reshape(bs, t, H_B, HD_B))

    return (xp, xs) + tuple(jnp.stack(o) for o in outs)
```

```python
import functools
import math

import jax
import jax.numpy as jnp
from jax import lax
from jax.experimental import pallas as pl
from jax.experimental.pallas import tpu as pltpu

CHUNK = 64
H_A = 4
HD_A = 64
W_A = H_A * 2 * HD_A
H_B = 8
HD_B = 64
W_B = H_B * HD_B
BAND_CHUNKS = 8
BAND_PAST = BAND_CHUNKS * CHUNK
REL_CLIP_B = 128
T5_BUCKETS = 32
T5_MAX_EXACT = 8
T5_MAX_DIST = 128
EPS = 1e-6
NEG = -1e30

LANES = 128
SEG = 512
NORM_GROUP = 256
MAX_DIRECT_LOGIT = 60.0
BF16_ROUND_MARGIN = 1.02
VMEM_LIMIT = 56 * 1024 * 1024

LOG2E = math.log2(math.e)

F32 = jnp.float32
BF16 = jnp.bfloat16


def _t5_bucket(rel):
    half = T5_BUCKETS // 2
    ret = jnp.where(rel > 0, half, 0)
    n = jnp.abs(rel)
    nf = jnp.maximum(n, 1).astype(jnp.float32)
    large = T5_MAX_EXACT + (jnp.log(nf / T5_MAX_EXACT) / math.log(T5_MAX_DIST / T5_MAX_EXACT) * (half - T5_MAX_EXACT)).astype(jnp.int32)
    large = jnp.minimum(large, half - 1)
    return ret + jnp.where(n < T5_MAX_EXACT, n, large)


def _far_bucket(n):
    half = T5_BUCKETS // 2
    return min(T5_MAX_EXACT + int(math.log(n / T5_MAX_EXACT) / math.log(T5_MAX_DIST / T5_MAX_EXACT) * (half - T5_MAX_EXACT)), half - 1)


def _nt(a, b):
    return lax.dot_general(a, b, (((1,), (1,)), ((), ())), preferred_element_type=F32)


def _silu(x):
    return x * (1.0 / (1.0 + jnp.exp(-x)))


def _sigmoid(x):
    return 1.0 / (1.0 + jnp.exp(-x))


def _mod_kernel(c_ref, w_ref, b_ref, o_ref):
    c = c_ref[...]
    o_ref[...] = jnp.dot(_silu(c), w_ref[...], preferred_element_type=F32,
                         precision=lax.Precision.HIGHEST) + b_ref[...]


def _modulation(c_all, w_ada, b_ada):
    rows, d = c_all.shape
    n_out = w_ada.shape[1]
    tn = d
    return pl.pallas_call(
        _mod_kernel,
        grid=(n_out // tn,),
        in_specs=[pl.BlockSpec((rows, d), lambda j: (0, 0)),
                  pl.BlockSpec((d, tn), lambda j: (0, j)),
                  pl.BlockSpec((1, tn), lambda j: (0, j))],
        out_specs=pl.BlockSpec((rows, tn), lambda j: (0, j)),
        out_shape=jax.ShapeDtypeStruct((rows, n_out), F32),
        name="adaln_mod",
    )(c_all, w_ada, b_ada.reshape(1, n_out))


def _proj_kernel(x_ref, shift_ref, scale_ref, gn_ref, w_ref, gqa_ref, gka_ref, gqb_ref, gkb_ref, pm_ref,
                 qa_ref, ka32_ref, kab_ref, va32_ref, vab_ref, ga_ref,
                 qb_ref, kb32_ref, kbb_ref, vb32_ref, vbb_ref, gb_ref, mg_ref):
    nb, t, d = x_ref.shape
    rows = nb * t
    x = x_ref[...]
    ms = jnp.mean(x * x, axis=-1, keepdims=True)
    xn = x * lax.rsqrt(ms + EPS) * gn_ref[...]
    h = xn * (1.0 + scale_ref[...]) + shift_ref[...]
    hb = h.reshape(rows, d).astype(BF16)

    def seg(c, width=SEG):
        return jnp.dot(hb, w_ref[:, c * SEG:c * SEG + width], preferred_element_type=F32)

    def head_norm(y, g_ref):
        sq = y * y
        hi = sq.astype(BF16)
        lo = (sq - hi.astype(F32)).astype(BF16)
        pm = pm_ref[...]
        parts = []
        for c in range(SEG // NORM_GROUP):
            sl = slice(c * NORM_GROUP, (c + 1) * NORM_GROUP)
            gms = jnp.dot(hi[:, sl], pm, preferred_element_type=F32) + jnp.dot(lo[:, sl], pm, preferred_element_type=F32)
            parts.append(y[:, sl] * lax.rsqrt(gms + EPS))
        return jnp.concatenate(parts, axis=1) * g_ref[...]

    def put(ref, y):
        ref[...] = y.astype(ref.dtype).reshape(ref.shape)

    def put_by_head(ref, y):
        for hh in range(H_A):
            ref[:, pl.ds(hh, t, stride=H_A), :] = y[:, hh * LANES:(hh + 1) * LANES].reshape(nb, t, LANES)

    put(qa_ref, head_norm(seg(0), gqa_ref) * (HD_A ** -0.5 * LOG2E))
    ka = head_norm(seg(1), gka_ref)
    put_by_head(ka32_ref, ka)
    put(kab_ref, ka)
    va = seg(2)
    put_by_head(va32_ref, va)
    put(vab_ref, va)
    put(ga_ref, seg(3))
    put(qb_ref, head_norm(seg(4), gqb_ref) * (HD_B ** -0.5 * LOG2E))
    kb = head_norm(seg(5), gkb_ref)
    put(kb32_ref, kb)
    put(kbb_ref, kb)
    vb = seg(6)
    put(vb32_ref, vb)
    put(vbb_ref, vb)
    put(gb_ref, seg(7))
    for c in range(8, 12):
        mg_ref[:, :, (c - 8) * SEG:(c - 7) * SEG] = seg(c).astype(mg_ref.dtype).reshape(nb, t, SEG)


def _project(x, shift, scale, g_norm, w_in_bf, gqa, gka, gqb, gkb, pm, nb, t):
    bx, sx, d = x.shape
    n_cols = w_in_bf.shape[1]
    grid = (bx // nb, sx // t)
    tok = lambda w: pl.BlockSpec((nb, t, w), lambda b, i: (b, i, 0))
    per_b = pl.BlockSpec((nb, 1, d), lambda b, i: (b, 0, 0))
    const = lambda shp: pl.BlockSpec(shp, lambda b, i: (0,) * len(shp))
    sds = lambda w, dt: jax.ShapeDtypeStruct((bx, sx, w), dt)
    by_head = jax.ShapeDtypeStruct((bx, sx * H_A, LANES), F32)
    out_shape = (sds(SEG, BF16), by_head, sds(SEG, BF16), by_head, sds(SEG, BF16), sds(SEG, BF16),
                 sds(SEG, BF16), sds(SEG, F32), sds(SEG, BF16), sds(SEG, F32), sds(SEG, BF16), sds(SEG, BF16),
                 sds(4 * SEG, BF16))
    out_specs = tuple(pl.BlockSpec((nb, s.shape[1] // (sx // t), s.shape[2]), lambda b, i: (b, i, 0)) for s in out_shape)
    return pl.pallas_call(
        _proj_kernel,
        grid=grid,
        in_specs=[tok(d), per_b, per_b, const((1, d)),
                  pl.BlockSpec((d, n_cols), lambda b, i: (0, 0), pipeline_mode=pl.Buffered(1)),
                  const((1, SEG)), const((1, SEG)), const((1, SEG)), const((1, SEG)),
                  const((NORM_GROUP, NORM_GROUP))],
        out_specs=out_specs,
        out_shape=out_shape,
        compiler_params=pltpu.CompilerParams(dimension_semantics=("arbitrary", "arbitrary"),
                                             vmem_limit_bytes=VMEM_LIMIT),
        name="in_proj",
    )(x, shift, scale, g_norm.reshape(1, d), w_in_bf, gqa, gka, gqb, gkb, pm)


def _bias_tile_kernel(off_ref, gen_ref, o_ref, *, tq, nk, qbase, kbase0, kstep, band, nvalid):
    v = pl.program_id(1)
    g = gen_ref[0, 0]
    x = jnp.broadcast_to(g, (tq, g.shape[-1]))
    y = pltpu.roll(x, 0, 1, stride=1, stride_axis=0)
    y = (y[:, :nk] - off_ref[pl.program_id(0)]) * LOG2E
    row = lax.broadcasted_iota(jnp.int32, (tq, nk), 0)
    col = lax.broadcasted_iota(jnp.int32, (tq, nk), 1)
    qc = (qbase + row) // CHUNK
    kc = (kbase0 - v * kstep + col) // CHUNK
    y = jnp.where(kc <= qc, y, NEG)
    if band:
        y = jnp.where(kc >= qc - BAND_CHUNKS, y, NEG)
    y = jnp.where(col < nvalid, y, NEG)
    o_ref[0, 0] = y


def _bias_tiles(table_fn, offset, n_heads, tq, nk, n_var, qbase, kbase0, kstep, band, nvalid):
    length = 1 << (tq + nk - 1).bit_length()
    assert length >= tq + nk - 1 and min(kbase0 - (n_var - 1) * kstep, qbase) >= 0
    r = jnp.arange(length, dtype=jnp.int32)
    d = jnp.where(r < nk, r, r - length)
    kbase = kbase0 - kstep * jnp.arange(n_var, dtype=jnp.int32)
    rel = (kbase[:, None] - qbase) + d[None, :]
    gen = jnp.transpose(table_fn(rel), (2, 0, 1)).astype(F32)
    gen = gen.reshape(n_heads, n_var, 1, length)
    return pl.pallas_call(
        functools.partial(_bias_tile_kernel, tq=tq, nk=nk, qbase=qbase, kbase0=kbase0, kstep=kstep,
                          band=band, nvalid=nvalid),
        grid=(n_heads, n_var),
        in_specs=[pl.BlockSpec(memory_space=pltpu.SMEM),
                  pl.BlockSpec((1, 1, 1, length), lambda h, v: (h, v, 0, 0))],
        out_specs=pl.BlockSpec((1, 1, tq, nk), lambda h, v: (h, v, 0, 0)),
        out_shape=jax.ShapeDtypeStruct((n_heads, n_var, tq, nk), F32),
        name="bias_tiles",
    )(offset.astype(F32), gen)


def _split_halves(q):
    lane = lax.broadcasted_iota(jnp.int32, q.shape, 1)
    zero = jnp.zeros_like(q)
    return jnp.where(lane < LANES // 2, q, zero), jnp.where(lane >= LANES // 2, q, zero)


def _online_update(carry, s, v):
    m, l, acc = carry
    m_new = jnp.maximum(m, jnp.max(s, axis=-1, keepdims=True))
    alpha = jnp.exp2(m - m_new)
    p = jnp.exp2(s - m_new)
    l = alpha * l + jnp.sum(p, axis=-1, keepdims=True)
    acc = alpha * acc + jnp.dot(p.astype(BF16), v, preferred_element_type=F32)
    return m_new, l, acc


def _diff_init(tq):
    return (jnp.full((tq, 1), -jnp.inf, F32), jnp.zeros((tq, 1), F32), jnp.zeros((tq, LANES), F32))


def _diff_finish(c0, c1, lamv_ref, gsub_ref, g, lam_init):
    lamv = lamv_ref[...]
    e1 = jnp.exp(jnp.sum(lamv[0:1] * lamv[1:2], axis=-1, keepdims=True))
    e2 = jnp.exp(jnp.sum(lamv[2:3] * lamv[3:4], axis=-1, keepdims=True))
    lam = e1 - e2 + lam_init
    o = c0[2] / c0[1] - lam * (c1[2] / c1[1])
    o = o * lax.rsqrt(jnp.mean(o * o, axis=-1, keepdims=True) + EPS) * gsub_ref[...]
    o = o * (1.0 - lam_init)
    return o * _silu(g.astype(F32))


def _lane_partial_sum(p):
    out = p[:, :LANES]
    for c in range(1, p.shape[1] // LANES):
        out = out + p[:, c * LANES:(c + 1) * LANES]
    return out


def _bounded_update(carry, s, v):
    l, acc = carry
    p = jnp.exp2(s)
    return l + _lane_partial_sum(p), acc + jnp.dot(p.astype(BF16), v, preferred_element_type=F32)


def _attn_a_prompt_kernel(par_ref, q_ref, k_ref, v_ref, g_ref, bias_ref, lamv_ref, gsub_ref, o_ref, l_ref, acc_ref,
                          *, tq, n_var, lam_init):
    i = pl.program_id(2)
    n_far = jnp.maximum(i - (n_var - 1), 0)
    ws = pl.multiple_of(n_far * tq, tq)
    bounded = par_ref[0] > 0.5

    n_heads = q_ref.shape[2] // LANES

    def run(update, init, finish):
        hs = [slice(hh * LANES, (hh + 1) * LANES) for hh in range(n_heads)]
        qs = [_split_halves(q_ref[0, :, sl]) for sl in hs]

        def far(j, carry):
            start = pl.multiple_of(j * tq, tq)
            out = []
            for (q0, q1), sl, (c0, c1) in zip(qs, hs, carry):
                k = k_ref[0, pl.ds(start, tq), sl]
                v = v_ref[0, pl.ds(start, tq), sl]
                out.append((update(c0, _nt(q0, k), v), update(c1, _nt(q1, k), v)))
            return tuple(out)

        carry = lax.fori_loop(0, n_far, far, ((init, init),) * n_heads)
        for hh, ((q0, q1), sl, (c0, c1)) in enumerate(zip(qs, hs, carry)):
            k = k_ref[0, pl.ds(ws, n_var * tq), sl]
            v = v_ref[0, pl.ds(ws, n_var * tq), sl]
            bias = bias_ref[hh, 0]
            c0 = finish(update(c0, _nt(q0, k) + bias, v))
            c1 = finish(update(c1, _nt(q1, k) + bias, v))
            o_ref[0, :, sl] = _diff_finish(c0, c1, lamv_ref, gsub_ref, g_ref[0, :, sl], lam_init).astype(o_ref.dtype)

    def run_bounded():
        hs = [slice(hh * LANES, (hh + 1) * LANES) for hh in range(n_heads)]
        qs = [_split_halves(q_ref[0, :, sl]) for sl in hs]
        l_ref[...] = jnp.zeros(l_ref.shape, F32)
        acc_ref[...] = jnp.zeros(acc_ref.shape, F32)

        def accumulate(hh, start, nk, bias):
            k = k_ref[0, pl.ds(start, nk), hs[hh]]
            v = v_ref[0, pl.ds(start, nk), hs[hh]]
            for mp in range(2):
                s = _nt(qs[hh][mp], k)
                p = jnp.exp2(s if bias is None else s + bias)
                l_ref[hh, mp] += _lane_partial_sum(p)
                acc_ref[hh, mp] += jnp.dot(p.astype(BF16), v, preferred_element_type=F32)

        def far(j, carry):
            for hh in range(n_heads):
                accumulate(hh, pl.multiple_of(j * tq, tq), tq, None)
            return carry

        lax.fori_loop(0, n_far, far, 0)
        for hh in range(n_heads):
            accumulate(hh, ws, n_var * tq, bias_ref[hh, 0])
            c0 = (None, jnp.sum(l_ref[hh, 0], axis=-1, keepdims=True), acc_ref[hh, 0])
            c1 = (None, jnp.sum(l_ref[hh, 1], axis=-1, keepdims=True), acc_ref[hh, 1])
            o_ref[0, :, hs[hh]] = _diff_finish(c0, c1, lamv_ref, gsub_ref, g_ref[0, :, hs[hh]], lam_init).astype(o_ref.dtype)

    pl.when(bounded)(run_bounded)

    @pl.when(jnp.logical_not(bounded))
    def _():
        run(_online_update, _diff_init(tq), lambda c: c)


def _logits_bounded(g_q, g_k, head_dim, table, offset):
    qk = math.sqrt(head_dim) * LOG2E * jnp.max(jnp.abs(g_q)) * jnp.max(jnp.abs(g_k)) * BF16_ROUND_MARGIN
    bias = jnp.max(jnp.abs(table - offset[None, :])) * LOG2E
    return (qk + bias <= MAX_DIRECT_LOGIT).astype(F32).reshape(1)


def _attn_a_prompt(qa, kab, vab, ga, bias, bounded, lamv, gsub, tq, lam_init):
    b, s, _ = qa.shape
    n_var = bias.shape[1]
    assert bias.shape[3] == n_var * tq
    hps = HEADS_PER_STEP_A
    tile = pl.BlockSpec((1, tq, hps * LANES), lambda bi, h, i: (bi, i, h))
    full = pl.BlockSpec((1, s, hps * LANES), lambda bi, h, i: (bi, 0, h))
    return pl.pallas_call(
        functools.partial(_attn_a_prompt_kernel, tq=tq, n_var=n_var, lam_init=lam_init),
        grid=(b, H_A // hps, s // tq),
        in_specs=[pl.BlockSpec(memory_space=pltpu.SMEM), tile, full, full, tile,
                  pl.BlockSpec((hps, 1, tq, n_var * tq), lambda bi, h, i: (h, jnp.minimum(i, n_var - 1), 0, 0)),
                  pl.BlockSpec((4, HD_A), lambda bi, h, i: (0, 0)),
                  pl.BlockSpec((1, LANES), lambda bi, h, i: (0, 0))],
        out_specs=tile,
        out_shape=jax.ShapeDtypeStruct((b, s, W_A), BF16),
        scratch_shapes=[pltpu.VMEM((hps, 2, tq, LANES), F32), pltpu.VMEM((hps, 2, tq, LANES), F32)],
        compiler_params=pltpu.CompilerParams(dimension_semantics=("arbitrary", "arbitrary", "arbitrary"),
                                             vmem_limit_bytes=VMEM_LIMIT),
        name="diff_attn_prompt",
    )(bounded, qa, kab, vab, ga, bias, lamv, gsub)


def _window_kv(cache_ref, new_ref, nk):
    past = cache_ref.shape[1]
    t = new_ref.shape[1]
    pad = jnp.zeros((nk - past - t, LANES), BF16)
    return jnp.concatenate([cache_ref[0].astype(BF16), new_ref[0], pad], axis=0)


def _attn_a_sample_kernel(q_ref, kc_ref, vc_ref, kn_ref, vn_ref, g_ref, bias_ref, lamv_ref, gsub_ref, o_ref, *, lam_init):
    tq = q_ref.shape[1]
    nk = bias_ref.shape[3]
    past = kc_ref.shape[1] // H_A
    pad = jnp.zeros((nk - past - tq, LANES), BF16)
    for hh in range(H_A):
        sl = slice(hh * LANES, (hh + 1) * LANES)
        rows = pl.ds(hh, past, stride=H_A)
        q0, q1 = _split_halves(q_ref[0, :, sl])
        k = jnp.concatenate([kc_ref[0, rows, :].astype(BF16), kn_ref[0, :, sl], pad], axis=0)
        v = jnp.concatenate([vc_ref[0, rows, :].astype(BF16), vn_ref[0, :, sl], pad], axis=0)
        bias = bias_ref[hh, 0]
        c0 = _online_update(_diff_init(tq), _nt(q0, k) + bias, v)
        c1 = _online_update(_diff_init(tq), _nt(q1, k) + bias, v)
        o_ref[0, :, sl] = _diff_finish(c0, c1, lamv_ref, gsub_ref, g_ref[0, :, sl], lam_init).astype(o_ref.dtype)


def _attn_a_sample(qa, kab, vab, ga, cache_k, cache_v, bias, lamv, gsub, lam_init):
    b, t, _ = qa.shape
    rows = cache_k.shape[1]
    nk = bias.shape[3]
    new = pl.BlockSpec((1, t, W_A), lambda bi: (bi, 0, 0))
    old = pl.BlockSpec((1, rows, LANES), lambda bi: (bi, 0, 0))
    return pl.pallas_call(
        functools.partial(_attn_a_sample_kernel, lam_init=lam_init),
        grid=(b,),
        in_specs=[new, old, old, new, new, new,
                  pl.BlockSpec((H_A, 1, t, nk), lambda bi: (0, 0, 0, 0)),
                  pl.BlockSpec((4, HD_A), lambda bi: (0, 0)),
                  pl.BlockSpec((1, LANES), lambda bi: (0, 0))],
        out_specs=new,
        out_shape=jax.ShapeDtypeStruct((b, t, W_A), BF16),
        name="diff_attn_sample",
    )(qa, cache_k, cache_v, kab, vab, ga, bias, lamv, gsub)


def _band_pair(q, k, v, g, bias_ref, bounded=False, first_head=0):
    q0, q1 = _split_halves(q)
    v0, v1 = _split_halves(v)
    o = None
    for hh, (qh, vh) in enumerate(((q0, v0), (q1, v1))):
        s = _nt(qh, k) + bias_ref[first_head + hh, 0]
        if not bounded:
            s = s - jnp.max(s, axis=-1, keepdims=True)
        p = jnp.exp2(s)
        l = jnp.sum(p, axis=-1, keepdims=True)
        oh = jnp.dot(p.astype(BF16), vh, preferred_element_type=F32) / l
        o = oh if o is None else o + oh
    return o * _silu(g.astype(F32))


def _attn_b_prompt_kernel(par_ref, q_ref, k_ref, v_ref, g_ref, bias_ref, o_ref, *, tq, n_var):
    i = pl.program_id(2)
    ws = pl.multiple_of(jnp.maximum(i - (n_var - 1), 0) * tq, tq)
    bounded = par_ref[0] > 0.5

    def run(direct):
        for pp in range(q_ref.shape[2] // LANES):
            sl = slice(pp * LANES, (pp + 1) * LANES)
            k = k_ref[0, pl.ds(ws, n_var * tq), sl]
            v = v_ref[0, pl.ds(ws, n_var * tq), sl]
            o = _band_pair(q_ref[0, :, sl], k, v, g_ref[0, :, sl], bias_ref, direct, 2 * pp)
            o_ref[0, :, sl] = o.astype(o_ref.dtype)

    pl.when(bounded)(lambda: run(True))
    pl.when(jnp.logical_not(bounded))(lambda: run(False))


def _attn_b_prompt(qb, kbb, vbb, gb, bias, bounded, tq):
    b, s, w = qb.shape
    n_var = bias.shape[1]
    assert bias.shape[3] == n_var * tq
    tile = pl.BlockSpec((1, tq, w), lambda bi, p, i: (bi, i, p))
    full = pl.BlockSpec((1, s, w), lambda bi, p, i: (bi, 0, p), pipeline_mode=pl.Buffered(1))
    return pl.pallas_call(
        functools.partial(_attn_b_prompt_kernel, tq=tq, n_var=n_var),
        grid=(b, 1, s // tq),
        in_specs=[pl.BlockSpec(memory_space=pltpu.SMEM), tile, full, full, tile,
                  pl.BlockSpec((H_B, 1, tq, n_var * tq), lambda bi, p, i: (p, jnp.minimum(i, n_var - 1), 0, 0))],
        out_specs=tile,
        out_shape=jax.ShapeDtypeStruct((b, s, W_B), BF16),
        compiler_params=pltpu.CompilerParams(dimension_semantics=("arbitrary", "arbitrary", "arbitrary"),
                                             vmem_limit_bytes=VMEM_LIMIT),
        name="band_attn_prompt",
    )(bounded, qb, kbb, vbb, gb, bias)


def _attn_b_sample_kernel(q_ref, kc_ref, vc_ref, kn_ref, vn_ref, g_ref, bias_ref, o_ref):
    nk = bias_ref.shape[3]
    k = _window_kv(kc_ref, kn_ref, nk)
    v = _window_kv(vc_ref, vn_ref, nk)
    o_ref[0] = _band_pair(q_ref[0], k, v, g_ref[0], bias_ref).astype(o_ref.dtype)


def _attn_b_sample(qb, kbb, vbb, gb, cache_k, cache_v, bias):
    b, t, _ = qb.shape
    past = cache_k.shape[1]
    nk = bias.shape[3]
    new = pl.BlockSpec((1, t, LANES), lambda bi, p: (bi, 0, p))
    old = pl.BlockSpec((1, past, LANES), lambda bi, p: (bi, 0, p))
    return pl.pallas_call(
        _attn_b_sample_kernel,
        grid=(b, H_B // 2),
        in_specs=[new, old, old, new, new, new,
                  pl.BlockSpec((2, 1, t, nk), lambda bi, p: (p, 0, 0, 0))],
        out_specs=new,
        out_shape=jax.ShapeDtypeStruct((b, t, W_B), BF16),
        name="band_attn_sample",
    )(qb, cache_k, cache_v, kbb, vbb, gb, bias)


def _merge_kernel(x_ref, gate_ref, oa_ref, ob_ref, mg_ref, woa_ref, wob_ref, wout_ref, o_ref):
    nb, t, d = x_ref.shape
    rows = nb * t
    ya = jnp.dot(oa_ref[...].reshape(rows, W_A), woa_ref[...], preferred_element_type=F32)
    yb = jnp.dot(ob_ref[...].reshape(rows, W_B), wob_ref[...], preferred_element_type=F32)
    mg = mg_ref[...].reshape(rows, 2 * d).astype(F32)
    m = _sigmoid(mg[:, :d]) * ya + _sigmoid(mg[:, d:]) * yb
    y = jnp.dot(m.astype(BF16), wout_ref[...], preferred_element_type=F32)
    o_ref[...] = x_ref[...] + gate_ref[...] * y.reshape(nb, t, d)


def _merge(x, gate, oa, ob, mg, w_oa_bf, w_ob_bf, w_out_bf, nb, t):
    bx, sx, d = x.shape
    tok = lambda w: pl.BlockSpec((nb, t, w), lambda b, i: (b, i, 0))
    const = lambda shp: pl.BlockSpec(shp, lambda b, i: (0,) * len(shp))
    return pl.pallas_call(
        _merge_kernel,
        grid=(bx // nb, sx // t),
        in_specs=[tok(d), pl.BlockSpec((nb, 1, d), lambda b, i: (b, 0, 0)), tok(W_A), tok(W_B), tok(2 * d),
                  const((W_A, d)), const((W_B, d)), const((d, d))],
        out_specs=tok(d),
        out_shape=jax.ShapeDtypeStruct((bx, sx, d), F32),
        compiler_params=pltpu.CompilerParams(dimension_semantics=("arbitrary", "arbitrary"),
                                             vmem_limit_bytes=VMEM_LIMIT),
        name="merge_out",
    )(x, gate, oa, ob, mg, w_oa_bf, w_ob_bf, w_out_bf)


TQ_A = 512
HEADS_PER_STEP_A = 2
TQ_B = 256
TM_PROJ = 512
QBASE = 2048


def kernel(x_prompt, x_sample, cache_a_k, cache_a_v, cache_b_k, cache_b_v, c_prompt, c_sample, g_norm, w_ada, b_ada, w_in, g_qa, g_ka, lam_q1, lam_k1, lam_q2, lam_k2, g_subln, t5_bias, g_qb, g_kb, rel_bias_b, w_oa, w_ob, w_out):
    xp, xs = x_prompt, x_sample
    bp, s, d = xp.shape
    bs, t, _ = xs.shape
    depth = w_in.shape[0]
    past = cache_a_k.shape[2]
    lb = cache_b_k.shape[2]
    n_keep = min(BAND_PAST, s)
    assert s % TM_PROJ == 0 and s % TQ_A == 0 and s % TQ_B == 0 and TQ_A % CHUNK == 0 and TQ_B % CHUNK == 0
    assert 2 * TQ_B >= BAND_PAST and past >= lb
    assert _far_bucket(TQ_A + 1) == T5_BUCKETS // 2 - 1

    r = jnp.arange(NORM_GROUP)
    pm = ((r[:, None] // HD_A) == (r[None, :] // HD_A)).astype(BF16) * (1.0 / HD_A)
    c_rows = bp + bs
    c_pad = -(-c_rows // 8) * 8
    c_all = jnp.concatenate([c_prompt, c_sample, jnp.zeros((c_pad - c_rows, d), F32)], axis=0)

    nk_as = -(-(past + t) // LANES) * LANES
    nk_bs = -(-(lb + t) // LANES) * LANES
    t5_fn = lambda rel: t5_bias[_t5_bucket(rel)]
    outs = [[] for _ in range(8)]
    for l in range(depth):
        lam_init = 0.8 - 0.6 * math.exp(-0.3 * l)
        rel_fn = lambda rel: rel_bias_b[l][jnp.clip(rel, -REL_CLIP_B, REL_CLIP_B) + REL_CLIP_B]
        cfar = t5_bias[_t5_bucket(jnp.int32(-(TQ_A + 1)))]
        bias_ap = _bias_tiles(t5_fn, cfar, H_A, TQ_A, 2 * TQ_A, 2, QBASE, QBASE, TQ_A, False, 2 * TQ_A)
        bias_bp = _bias_tiles(rel_fn, jnp.zeros((H_B,)), H_B, TQ_B, 3 * TQ_B, 3, QBASE, QBASE, TQ_B, True, 3 * TQ_B)
        bias_as = _bias_tiles(t5_fn, jnp.zeros((H_A,)), H_A, t, nk_as, 1, past, 0, 0, False, past + t)
        bias_bs = _bias_tiles(rel_fn, jnp.zeros((H_B,)), H_B, t, nk_bs, 1, past, past - lb, 0, True, lb + t)

        mod = _modulation(c_all, w_ada[l], b_ada[l])
        shift = mod[:, :d].reshape(c_pad, 1, d)
        scale = mod[:, d:2 * d].reshape(c_pad, 1, d)
        gate = mod[:, 2 * d:].reshape(c_pad, 1, d)
        w_in_bf = w_in[l].astype(BF16)
        w_oa_bf, w_ob_bf, w_out_bf = w_oa[l].astype(BF16), w_ob[l].astype(BF16), w_out[l].astype(BF16)
        tile8 = lambda g: jnp.tile(g, SEG // g.shape[0]).reshape(1, SEG)
        gains = (tile8(g_qa[l]), tile8(g_ka[l]), tile8(g_qb[l]), tile8(g_kb[l]))
        lamv = jnp.stack([lam_q1[l], lam_k1[l], lam_q2[l], lam_k2[l]])
        gsub = g_subln[l].reshape(1, LANES)

        (qa, ka32, kab, va32, vab, ga, qb, kb32, kbb, vb32, vbb, gb, mg) = _project(
            xp, shift[:bp], scale[:bp], g_norm[l], w_in_bf, *gains, pm, 1, TM_PROJ)
        bounded_a = _logits_bounded(g_qa[l], g_ka[l], HD_A, t5_bias, cfar)
        bounded_b = _logits_bounded(g_qb[l], g_kb[l], HD_B, rel_bias_b[l], jnp.zeros((H_B,)))
        oa = _attn_a_prompt(qa, kab, vab, ga, bias_ap, bounded_a, lamv, gsub, TQ_A, lam_init)
        ob = _attn_b_prompt(qb, kbb, vbb, gb, bias_bp, bounded_b, TQ_B)
        xp = _merge(xp, gate[:bp], oa, ob, mg, w_oa_bf, w_ob_bf, w_out_bf, 1, TM_PROJ)
        outs[0].append(ka32.reshape(bp, s, H_A, 2 * HD_A))
        outs[1].append(va32.reshape(bp, s, H_A, 2 * HD_A))
        outs[2].append(kb32[:, s - n_keep:].reshape(bp, n_keep, H_B, HD_B))
        outs[3].append(vb32[:, s - n_keep:].reshape(bp, n_keep, H_B, HD_B))

        (qa, ka32, kab, va32, vab, ga, qb, kb32, kbb, vb32, vbb, gb, mg) = _project(
            xs, shift[bp:c_rows], scale[bp:c_rows], g_norm[l], w_in_bf, *gains, pm, bs, t)
        oa = _attn_a_sample(qa, kab, vab, ga, cache_a_k[l].reshape(bs, past * H_A, LANES), cache_a_v[l].reshape(bs, past * H_A, LANES),
                            bias_as, lamv, gsub, lam_init)
        ob = _attn_b_sample(qb, kbb, vbb, gb, cache_b_k[l].reshape(bs, lb, W_B), cache_b_v[l].reshape(bs, lb, W_B), bias_bs)
        xs = _merge(xs, gate[bp:c_rows], oa, ob, mg, w_oa_bf, w_ob_bf, w_out_bf, bs, t)
        outs[4].append(ka32.reshape(bs, t, H_A, 2 * HD_A))
        outs[5].append(va32.reshape(bs, t, H_A, 2 * HD_A))
        outs[6].append(kb32.reshape(bs, t, H_B, HD_B))
        outs[7].append(vb32.reshape(bs, t, H_B, HD_B))

    return (xp, xs) + tuple(jnp.stack(o) for o in outs)
```

```python
import functools
import math

import jax
import jax.numpy as jnp
from jax import lax
from jax.experimental import pallas as pl
from jax.experimental.pallas import tpu as pltpu

CHUNK = 64
H_A = 4
HD_A = 64
W_A = H_A * 2 * HD_A
H_B = 8
HD_B = 64
W_B = H_B * HD_B
BAND_CHUNKS = 8
BAND_PAST = BAND_CHUNKS * CHUNK
REL_CLIP_B = 128
T5_BUCKETS = 32
T5_MAX_EXACT = 8
T5_MAX_DIST = 128
EPS = 1e-6
NEG = -1e30

LANES = 128
SEG = 512
NORM_GROUP = 256
MAX_DIRECT_LOGIT = 60.0
BF16_ROUND_MARGIN = 1.02
VMEM_LIMIT = 56 * 1024 * 1024

LOG2E = math.log2(math.e)

F32 = jnp.float32
BF16 = jnp.bfloat16


def _t5_bucket(rel):
    half = T5_BUCKETS // 2
    assert (T5_MAX_DIST // T5_MAX_EXACT) ** 2 == 2 ** (half - T5_MAX_EXACT)
    ret = jnp.where(rel > 0, half, 0)
    n = jnp.abs(rel)
    large = T5_MAX_EXACT + sum((n * n >= T5_MAX_EXACT ** 2 * 2 ** j).astype(jnp.int32)
                               for j in range(1, half - T5_MAX_EXACT))
    large = jnp.minimum(large, half - 1)
    return ret + jnp.where(n < T5_MAX_EXACT, n, large)


def _t5_run(t5_bias, first_rel, count):
    assert max(abs(first_rel), abs(first_rel + count)) < 2 ** 15
    rel = first_rel + jnp.arange(count, dtype=jnp.int32)
    hit = _t5_bucket(rel)[:, None, None] == jnp.arange(T5_BUCKETS, dtype=jnp.int32)[None, :, None]
    return jnp.sum(jnp.where(hit, t5_bias[None], 0.0), axis=1)


def _clipped_run(table, first_rel, count):
    n = table.shape[0]
    first = first_rel + (n - 1) // 2
    n_lo = min(max(-first, 0), count)
    n_hi = min(max(first + count - n, 0), count)
    mid = count - n_lo - n_hi
    parts = [jnp.broadcast_to(table[:1], (n_lo, table.shape[1])),
             table[first + n_lo:first + n_lo + mid] if mid > 0 else table[:0],
             jnp.broadcast_to(table[n - 1:], (n_hi, table.shape[1]))]
    return jnp.concatenate(parts, axis=0)


def _far_bucket(n):
    half = T5_BUCKETS // 2
    return min(T5_MAX_EXACT + int(math.log(n / T5_MAX_EXACT) / math.log(T5_MAX_DIST / T5_MAX_EXACT) * (half - T5_MAX_EXACT)), half - 1)


def _nt(a, b):
    return lax.dot_general(a, b, (((1,), (1,)), ((), ())), preferred_element_type=F32)


def _silu(x):
    return x * (1.0 / (1.0 + jnp.exp(-x)))


def _sigmoid(x):
    return 1.0 / (1.0 + jnp.exp(-x))


def _mod_kernel(c_ref, w_ref, b_ref, o_ref):
    c = c_ref[...]
    o_ref[...] = jnp.dot(_silu(c), w_ref[...], preferred_element_type=F32,
                         precision=lax.Precision.HIGHEST) + b_ref[...]


def _modulation(c_all, w_ada, b_ada):
    rows, d = c_all.shape
    n_out = w_ada.shape[1]
    tn = d
    return pl.pallas_call(
        _mod_kernel,
        grid=(n_out // tn,),
        in_specs=[pl.BlockSpec((rows, d), lambda j: (0, 0)),
                  pl.BlockSpec((d, tn), lambda j: (0, j)),
                  pl.BlockSpec((1, tn), lambda j: (0, j))],
        out_specs=pl.BlockSpec((rows, tn), lambda j: (0, j)),
        out_shape=jax.ShapeDtypeStruct((rows, n_out), F32),
        name="adaln_mod",
    )(c_all, w_ada, b_ada.reshape(1, n_out))


def _proj_kernel(x_ref, shift_ref, scale_ref, gn_ref, w_ref, gqa_ref, gka_ref, gqb_ref, gkb_ref, pm_ref,
                 qa_ref, ka32_ref, kab_ref, va32_ref, vab_ref, ga_ref,
                 qb_ref, kb32_ref, kbb_ref, vb32_ref, vbb_ref, gb_ref, mg_ref):
    nb, t, d = x_ref.shape
    rows = nb * t
    x = x_ref[...]
    ms = jnp.mean(x * x, axis=-1, keepdims=True)
    xn = x * lax.rsqrt(ms + EPS) * gn_ref[...]
    h = xn * (1.0 + scale_ref[...]) + shift_ref[...]
    hb = h.reshape(rows, d).astype(BF16)

    def seg(c, width=SEG):
        return jnp.dot(hb, w_ref[:, c * SEG:c * SEG + width], preferred_element_type=F32)

    def head_norm(y, g_ref):
        sq = y * y
        hi = sq.astype(BF16)
        pm = pm_ref[...]
        parts = []
        for c in range(SEG // NORM_GROUP):
            sl = slice(c * NORM_GROUP, (c + 1) * NORM_GROUP)
            gms = jnp.dot(hi[:, sl], pm, preferred_element_type=F32)
            parts.append(y[:, sl] * lax.rsqrt(gms + EPS))
        return jnp.concatenate(parts, axis=1) * g_ref[...]

    def put(ref, y):
        ref[...] = y.astype(ref.dtype).reshape(ref.shape)

    def put_by_head(ref, y):
        for hh in range(H_A):
            ref[:, pl.ds(hh, t, stride=H_A), :] = y[:, hh * LANES:(hh + 1) * LANES].reshape(nb, t, LANES)

    put(qa_ref, head_norm(seg(0), gqa_ref) * (HD_A ** -0.5 * LOG2E))
    ka = head_norm(seg(1), gka_ref)
    put_by_head(ka32_ref, ka)
    put(kab_ref, ka)
    va = seg(2)
    put_by_head(va32_ref, va)
    put(vab_ref, va)
    put(ga_ref, seg(3))
    put(qb_ref, head_norm(seg(4), gqb_ref) * (HD_B ** -0.5 * LOG2E))
    kb = head_norm(seg(5), gkb_ref)
    put(kb32_ref, kb)
    put(kbb_ref, kb)
    vb = seg(6)
    put(vb32_ref, vb)
    put(vbb_ref, vb)
    put(gb_ref, seg(7))
    for c in range(8, 12):
        mg_ref[:, :, (c - 8) * SEG:(c - 7) * SEG] = seg(c).astype(mg_ref.dtype).reshape(nb, t, SEG)


def _project(x, shift, scale, g_norm, w_in_bf, gqa, gka, gqb, gkb, pm, nb, t):
    bx, sx, d = x.shape
    n_cols = w_in_bf.shape[1]
    grid = (bx // nb, sx // t)
    tok = lambda w: pl.BlockSpec((nb, t, w), lambda b, i: (b, i, 0))
    per_b = pl.BlockSpec((nb, 1, d), lambda b, i: (b, 0, 0))
    const = lambda shp: pl.BlockSpec(shp, lambda b, i: (0,) * len(shp))
    sds = lambda w, dt: jax.ShapeDtypeStruct((bx, sx, w), dt)
    by_head = jax.ShapeDtypeStruct((bx, sx * H_A, LANES), F32)
    out_shape = (sds(SEG, BF16), by_head, sds(SEG, BF16), by_head, sds(SEG, BF16), sds(SEG, BF16),
                 sds(SEG, BF16), sds(SEG, F32), sds(SEG, BF16), sds(SEG, F32), sds(SEG, BF16), sds(SEG, BF16),
                 sds(4 * SEG, BF16))
    out_specs = tuple(pl.BlockSpec((nb, s.shape[1] // (sx // t), s.shape[2]), lambda b, i: (b, i, 0)) for s in out_shape)
    return pl.pallas_call(
        _proj_kernel,
        grid=grid,
        in_specs=[tok(d), per_b, per_b, const((1, d)),
                  pl.BlockSpec((d, n_cols), lambda b, i: (0, 0), pipeline_mode=pl.Buffered(1)),
                  const((1, SEG)), const((1, SEG)), const((1, SEG)), const((1, SEG)),
                  const((NORM_GROUP, NORM_GROUP))],
        out_specs=out_specs,
        out_shape=out_shape,
        compiler_params=pltpu.CompilerParams(dimension_semantics=("arbitrary", "arbitrary"),
                                             vmem_limit_bytes=VMEM_LIMIT),
        name="in_proj",
    )(x, shift, scale, g_norm.reshape(1, d), w_in_bf, gqa, gka, gqb, gkb, pm)


def _bias_tile_kernel(off_ref, gen_ref, o_ref, *, tq, nk, n_var, qbase, kbase0, kstep, lead_step, band, nvalid):
    v = pl.program_id(1)
    g = gen_ref[0, 0]
    x = jnp.broadcast_to(g, (tq, g.shape[-1]))
    y = pltpu.roll(x, 0, 1, stride=1, stride_axis=0)
    y = (y[:, :nk] - off_ref[pl.program_id(0)]) * LOG2E
    row = lax.broadcasted_iota(jnp.int32, (tq, nk), 0)
    col = lax.broadcasted_iota(jnp.int32, (tq, nk), 1)
    qc = (qbase + row) // CHUNK
    kc = (kbase0 - v * kstep + col) // CHUNK
    y = jnp.where(kc <= qc, y, NEG)
    if band:
        y = jnp.where(kc >= qc - BAND_CHUNKS, y, NEG)
    y = jnp.where(col < nvalid, y, NEG)
    if lead_step:
        y = jnp.where(col >= (n_var - 1 - v) * lead_step, y, NEG)
    o_ref[0, 0] = y


def _bias_tiles(table_fn, offset, n_heads, tq, nk, n_var, qbase, kbase0, kstep, band, nvalid, lead_step=0):
    length = 1 << (tq + nk - 1).bit_length()
    assert length >= tq + nk - 1 and min(kbase0 - (n_var - 1) * kstep, qbase) >= 0
    gen = jnp.stack([jnp.concatenate([table_fn(kbase0 - v * kstep - qbase, nk),
                                      table_fn(kbase0 - v * kstep - qbase + nk - length, length - nk)], axis=0)
                     for v in range(n_var)])
    gen = jnp.transpose(gen, (2, 0, 1)).astype(F32).reshape(n_heads, n_var, 1, length)
    return pl.pallas_call(
        functools.partial(_bias_tile_kernel, tq=tq, nk=nk, n_var=n_var, qbase=qbase, kbase0=kbase0, kstep=kstep,
                          lead_step=lead_step, band=band, nvalid=nvalid),
        grid=(n_heads, n_var),
        in_specs=[pl.BlockSpec(memory_space=pltpu.SMEM),
                  pl.BlockSpec((1, 1, 1, length), lambda h, v: (h, v, 0, 0))],
        out_specs=pl.BlockSpec((1, 1, tq, nk), lambda h, v: (h, v, 0, 0)),
        out_shape=jax.ShapeDtypeStruct((n_heads, n_var, tq, nk), F32),
        name="bias_tiles",
    )(offset.astype(F32), gen)


def _split_halves(q):
    lane = lax.broadcasted_iota(jnp.int32, q.shape, 1)
    zero = jnp.zeros_like(q)
    return jnp.where(lane < LANES // 2, q, zero), jnp.where(lane >= LANES // 2, q, zero)


def _online_update(carry, s, v):
    m, l, acc = carry
    m_new = jnp.maximum(m, jnp.max(s, axis=-1, keepdims=True))
    alpha = jnp.exp2(m - m_new)
    p = jnp.exp2(s - m_new)
    l = alpha * l + jnp.sum(p, axis=-1, keepdims=True)
    acc = alpha * acc + jnp.dot(p.astype(BF16), v, preferred_element_type=F32)
    return m_new, l, acc


def _diff_init(tq):
    return (jnp.full((tq, 1), -jnp.inf, F32), jnp.zeros((tq, 1), F32), jnp.zeros((tq, LANES), F32))


def _diff_finish(c0, c1, lamv_ref, gsub_ref, g, lam_init):
    lamv = lamv_ref[...]
    e1 = jnp.exp(jnp.sum(lamv[0:1] * lamv[1:2], axis=-1, keepdims=True))
    e2 = jnp.exp(jnp.sum(lamv[2:3] * lamv[3:4], axis=-1, keepdims=True))
    lam = e1 - e2 + lam_init
    o = c0[2] / c0[1] - lam * (c1[2] / c1[1])
    o = o * lax.rsqrt(jnp.mean(o * o, axis=-1, keepdims=True) + EPS) * gsub_ref[...]
    o = o * (1.0 - lam_init)
    return o * _silu(g.astype(F32))


def _lane_partial_sum(p):
    out = p[:, :LANES]
    for c in range(1, p.shape[1] // LANES):
        out = out + p[:, c * LANES:(c + 1) * LANES]
    return out


def _bounded_update(carry, s, v):
    l, acc = carry
    p = jnp.exp2(s)
    return l + _lane_partial_sum(p), acc + jnp.dot(p.astype(BF16), v, preferred_element_type=F32)


def _attn_a_prompt_kernel(par_ref, q_ref, k_ref, v_ref, g_ref, bias_ref, lamv_ref, gsub_ref, o_ref, l_ref, acc_ref,
                          *, tq, n_var, lam_init):
    i = pl.program_id(2)
    n_far = jnp.maximum(i - 1, 0)
    near = pl.multiple_of(n_far * tq, tq)
    diag = pl.multiple_of(i * tq, tq)
    half = tq // 2
    bounded = par_ref[0] > 0.5
    n_heads = q_ref.shape[2] // LANES
    hs = [slice(hh * LANES, (hh + 1) * LANES) for hh in range(n_heads)]

    def finish_head(hh, c0, c1):
        o = _diff_finish(c0, c1, lamv_ref, gsub_ref, g_ref[0, :, hs[hh]], lam_init)
        o_ref[0, :, hs[hh]] = o.astype(o_ref.dtype)

    def run_online():
        qs = [_split_halves(q_ref[0, :, sl]) for sl in hs]

        def tile(carry, start, bias):
            out = []
            for hh, (c0, c1) in enumerate(carry):
                k = k_ref[0, pl.ds(start, tq), hs[hh]]
                v = v_ref[0, pl.ds(start, tq), hs[hh]]
                s0, s1 = _nt(qs[hh][0], k), _nt(qs[hh][1], k)
                if bias is not None:
                    s0, s1 = s0 + bias_ref[hh, 0, :, bias], s1 + bias_ref[hh, 0, :, bias]
                out.append((_online_update(c0, s0, v), _online_update(c1, s1, v)))
            return tuple(out)

        init = _diff_init(tq)
        carry = lax.fori_loop(0, n_far, lambda j, c: tile(c, pl.multiple_of(j * tq, tq), None), ((init, init),) * n_heads)
        carry = tile(tile(carry, near, slice(0, tq)), diag, slice(tq, 2 * tq))
        for hh, (c0, c1) in enumerate(carry):
            finish_head(hh, c0, c1)

    def run_bounded():
        qs = [_split_halves(q_ref[0, :, sl]) for sl in hs]
        l_ref[...] = jnp.zeros(l_ref.shape, F32)
        acc_ref[...] = jnp.zeros(acc_ref.shape, F32)

        def accumulate(hh, start, nk, rows=slice(None), bias_cols=None):
            k = k_ref[0, pl.ds(start, nk), hs[hh]]
            v = v_ref[0, pl.ds(start, nk), hs[hh]]
            for mp in range(2):
                s = _nt(qs[hh][mp][rows], k)
                if bias_cols is not None:
                    s = s + bias_ref[hh, 0, rows, bias_cols]
                p = jnp.exp2(s)
                l_ref[hh, mp, rows, :] += _lane_partial_sum(p)
                acc_ref[hh, mp, rows, :] += jnp.dot(p.astype(BF16), v, preferred_element_type=F32)

        def far(j, carry):
            for hh in range(n_heads):
                accumulate(hh, pl.multiple_of(j * tq, tq), tq)
            return carry

        lax.fori_loop(0, n_far, far, 0)
        for hh in range(n_heads):
            accumulate(hh, near, tq, bias_cols=slice(0, tq))
            accumulate(hh, diag, half, slice(0, half), slice(tq, tq + half))
            accumulate(hh, diag, tq, slice(half, tq), slice(tq, 2 * tq))
            c0 = (None, jnp.sum(l_ref[hh, 0], axis=-1, keepdims=True), acc_ref[hh, 0])
            c1 = (None, jnp.sum(l_ref[hh, 1], axis=-1, keepdims=True), acc_ref[hh, 1])
            finish_head(hh, c0, c1)

    pl.when(bounded)(run_bounded)
    pl.when(jnp.logical_not(bounded))(run_online)


def _logits_bounded(g_q, g_k, head_dim, table, offset):
    qk = math.sqrt(head_dim) * LOG2E * jnp.max(jnp.abs(g_q)) * jnp.max(jnp.abs(g_k)) * BF16_ROUND_MARGIN
    bias = jnp.max(jnp.abs(table - offset[None, :])) * LOG2E
    return (qk + bias <= MAX_DIRECT_LOGIT).astype(F32).reshape(1)


def _attn_a_prompt(qa, kab, vab, ga, bias, bounded, lamv, gsub, tq, lam_init):
    b, s, _ = qa.shape
    n_var = bias.shape[1]
    assert bias.shape[3] == n_var * tq
    hps = HEADS_PER_STEP_A
    tile = pl.BlockSpec((1, tq, hps * LANES), lambda bi, h, i: (bi, i, h))
    full = pl.BlockSpec((1, s, hps * LANES), lambda bi, h, i: (bi, 0, h))
    return pl.pallas_call(
        functools.partial(_attn_a_prompt_kernel, tq=tq, n_var=n_var, lam_init=lam_init),
        grid=(b, H_A // hps, s // tq),
        in_specs=[pl.BlockSpec(memory_space=pltpu.SMEM), tile, full, full, tile,
                  pl.BlockSpec((hps, 1, tq, n_var * tq), lambda bi, h, i: (h, jnp.minimum(i, n_var - 1), 0, 0)),
                  pl.BlockSpec((4, HD_A), lambda bi, h, i: (0, 0)),
                  pl.BlockSpec((1, LANES), lambda bi, h, i: (0, 0))],
        out_specs=tile,
        out_shape=jax.ShapeDtypeStruct((b, s, W_A), BF16),
        scratch_shapes=[pltpu.VMEM((hps, 2, tq, LANES), F32), pltpu.VMEM((hps, 2, tq, LANES), F32)],
        compiler_params=pltpu.CompilerParams(dimension_semantics=("arbitrary", "arbitrary", "arbitrary"),
                                             vmem_limit_bytes=VMEM_LIMIT),
        name="diff_attn_prompt",
    )(bounded, qa, kab, vab, ga, bias, lamv, gsub)


def _attn_a_sample_kernel(q_ref, kc_ref, vc_ref, kn_ref, vn_ref, g_ref, bias_ref, lamv_ref, gsub_ref, o_ref, *, lam_init):
    tq = q_ref.shape[1]
    nk = bias_ref.shape[3]
    past = kc_ref.shape[1] // H_A
    pad = jnp.zeros((nk - past - tq, LANES), BF16)
    for hh in range(H_A):
        sl = slice(hh * LANES, (hh + 1) * LANES)
        rows = pl.ds(hh, past, stride=H_A)
        qst = jnp.concatenate(_split_halves(q_ref[0, :, sl]), axis=0)
        k = jnp.concatenate([kc_ref[0, rows, :].astype(BF16), kn_ref[0, :, sl], pad], axis=0)
        v = jnp.concatenate([vc_ref[0, rows, :].astype(BF16), vn_ref[0, :, sl], pad], axis=0)
        bias = bias_ref[hh, 0]
        c = _online_update(_diff_init(2 * tq), _nt(qst, k) + jnp.concatenate([bias, bias], axis=0), v)
        c0, c1 = tuple(x[:tq] for x in c), tuple(x[tq:] for x in c)
        o_ref[0, :, sl] = _diff_finish(c0, c1, lamv_ref, gsub_ref, g_ref[0, :, sl], lam_init).astype(o_ref.dtype)


def _attn_a_sample(qa, kab, vab, ga, cache_k, cache_v, bias, lamv, gsub, lam_init):
    b, t, _ = qa.shape
    rows = cache_k.shape[1]
    nk = bias.shape[3]
    new = pl.BlockSpec((1, t, W_A), lambda bi: (bi, 0, 0))
    old = pl.BlockSpec((1, rows, LANES), lambda bi: (bi, 0, 0))
    return pl.pallas_call(
        functools.partial(_attn_a_sample_kernel, lam_init=lam_init),
        grid=(b,),
        in_specs=[new, old, old, new, new, new,
                  pl.BlockSpec((H_A, 1, t, nk), lambda bi: (0, 0, 0, 0)),
                  pl.BlockSpec((4, HD_A), lambda bi: (0, 0)),
                  pl.BlockSpec((1, LANES), lambda bi: (0, 0))],
        out_specs=new,
        out_shape=jax.ShapeDtypeStruct((b, t, W_A), BF16),
        name="diff_attn_sample",
    )(qa, cache_k, cache_v, kab, vab, ga, bias, lamv, gsub)


def _band_pair(q, k, v, g, bias_ref, bounded=False, first_head=0):
    t = q.shape[0]
    qst = jnp.concatenate(_split_halves(q), axis=0)
    s = _nt(qst, k) + jnp.concatenate([bias_ref[first_head, 0], bias_ref[first_head + 1, 0]], axis=0)
    if not bounded:
        s = s - jnp.max(s, axis=-1, keepdims=True)
    p = jnp.exp2(s)
    l = jnp.sum(p, axis=-1, keepdims=True)
    o2 = jnp.dot(p.astype(BF16), v, preferred_element_type=F32) / l
    lane = lax.broadcasted_iota(jnp.int32, (t, LANES), 1)
    o = jnp.where(lane < LANES // 2, o2[:t], o2[t:])
    return o * _silu(g.astype(F32))


def _attn_b_prompt_kernel(par_ref, q_ref, k_ref, v_ref, g_ref, bias_ref, o_ref, *, tq, n_var):
    i = pl.program_id(2)
    ws = pl.multiple_of(jnp.maximum(i - (n_var - 1), 0) * tq, tq)
    bounded = par_ref[0] > 0.5

    def run(direct):
        for pp in range(q_ref.shape[2] // LANES):
            sl = slice(pp * LANES, (pp + 1) * LANES)
            k = k_ref[0, pl.ds(ws, n_var * tq), sl]
            v = v_ref[0, pl.ds(ws, n_var * tq), sl]
            o = _band_pair(q_ref[0, :, sl], k, v, g_ref[0, :, sl], bias_ref, direct, 2 * pp)
            o_ref[0, :, sl] = o.astype(o_ref.dtype)

    pl.when(bounded)(lambda: run(True))
    pl.when(jnp.logical_not(bounded))(lambda: run(False))


def _attn_b_prompt(qb, kbb, vbb, gb, bias, bounded, tq):
    b, s, w = qb.shape
    n_var = bias.shape[1]
    assert bias.shape[3] == n_var * tq
    tile = pl.BlockSpec((1, tq, w), lambda bi, p, i: (bi, i, p))
    full = pl.BlockSpec((1, s, w), lambda bi, p, i: (bi, 0, p), pipeline_mode=pl.Buffered(1))
    return pl.pallas_call(
        functools.partial(_attn_b_prompt_kernel, tq=tq, n_var=n_var),
        grid=(b, 1, s // tq),
        in_specs=[pl.BlockSpec(memory_space=pltpu.SMEM), tile, full, full, tile,
                  pl.BlockSpec((H_B, 1, tq, n_var * tq), lambda bi, p, i: (p, jnp.minimum(i, n_var - 1), 0, 0))],
        out_specs=tile,
        out_shape=jax.ShapeDtypeStruct((b, s, W_B), BF16),
        compiler_params=pltpu.CompilerParams(dimension_semantics=("arbitrary", "arbitrary", "arbitrary"),
                                             vmem_limit_bytes=VMEM_LIMIT),
        name="band_attn_prompt",
    )(bounded, qb, kbb, vbb, gb, bias)


def _attn_b_sample_kernel(q_ref, kc_ref, vc_ref, kn_ref, vn_ref, g_ref, bias_ref, o_ref):
    nk = bias_ref.shape[3]
    past, t = kc_ref.shape[1], kn_ref.shape[1]
    pad = jnp.zeros((nk - past - t, LANES), BF16)
    for pp in range(q_ref.shape[2] // LANES):
        sl = slice(pp * LANES, (pp + 1) * LANES)
        k = jnp.concatenate([kc_ref[0, :, sl].astype(BF16), kn_ref[0, :, sl], pad], axis=0)
        v = jnp.concatenate([vc_ref[0, :, sl].astype(BF16), vn_ref[0, :, sl], pad], axis=0)
        o_ref[0, :, sl] = _band_pair(q_ref[0, :, sl], k, v, g_ref[0, :, sl], bias_ref, False, 2 * pp).astype(o_ref.dtype)


def _attn_b_sample(qb, kbb, vbb, gb, cache_k, cache_v, bias):
    b, t, w = qb.shape
    past = cache_k.shape[1]
    nk = bias.shape[3]
    new = pl.BlockSpec((1, t, w), lambda bi: (bi, 0, 0))
    old = pl.BlockSpec((1, past, w), lambda bi: (bi, 0, 0))
    return pl.pallas_call(
        _attn_b_sample_kernel,
        grid=(b,),
        in_specs=[new, old, old, new, new, new,
                  pl.BlockSpec((H_B, 1, t, nk), lambda bi: (0, 0, 0, 0))],
        out_specs=new,
        out_shape=jax.ShapeDtypeStruct((b, t, W_B), BF16),
        name="band_attn_sample",
    )(qb, cache_k, cache_v, kbb, vbb, gb, bias)


def _merge_kernel(x_ref, gate_ref, oa_ref, ob_ref, mg_ref, woa_ref, wob_ref, wout_ref, o_ref):
    nb, t, d = x_ref.shape
    rows = nb * t
    ya = jnp.dot(oa_ref[...].reshape(rows, W_A), woa_ref[...], preferred_element_type=F32)
    yb = jnp.dot(ob_ref[...].reshape(rows, W_B), wob_ref[...], preferred_element_type=F32)
    mg = mg_ref[...].reshape(rows, 2 * d).astype(F32)
    m = _sigmoid(mg[:, :d]) * ya + _sigmoid(mg[:, d:]) * yb
    y = jnp.dot(m.astype(BF16), wout_ref[...], preferred_element_type=F32)
    o_ref[...] = x_ref[...] + gate_ref[...] * y.reshape(nb, t, d)


def _merge(x, gate, oa, ob, mg, w_oa_bf, w_ob_bf, w_out_bf, nb, t):
    bx, sx, d = x.shape
    tok = lambda w: pl.BlockSpec((nb, t, w), lambda b, i: (b, i, 0))
    const = lambda shp: pl.BlockSpec(shp, lambda b, i: (0,) * len(shp))
    return pl.pallas_call(
        _merge_kernel,
        grid=(bx // nb, sx // t),
        in_specs=[tok(d), pl.BlockSpec((nb, 1, d), lambda b, i: (b, 0, 0)), tok(W_A), tok(W_B), tok(2 * d),
                  const((W_A, d)), const((W_B, d)), const((d, d))],
        out_specs=tok(d),
        out_shape=jax.ShapeDtypeStruct((bx, sx, d), F32),
        compiler_params=pltpu.CompilerParams(dimension_semantics=("arbitrary", "arbitrary"),
                                             vmem_limit_bytes=VMEM_LIMIT),
        name="merge_out",
    )(x, gate, oa, ob, mg, w_oa_bf, w_ob_bf, w_out_bf)


TQ_A = 512
HEADS_PER_STEP_A = 2
TQ_B = 256
TM_PROJ = 512
QBASE = 2048


def kernel(x_prompt, x_sample, cache_a_k, cache_a_v, cache_b_k, cache_b_v, c_prompt, c_sample, g_norm, w_ada, b_ada, w_in, g_qa, g_ka, lam_q1, lam_k1, lam_q2, lam_k2, g_subln, t5_bias, g_qb, g_kb, rel_bias_b, w_oa, w_ob, w_out):
    xp, xs = x_prompt, x_sample
    bp, s, d = xp.shape
    bs, t, _ = xs.shape
    depth = w_in.shape[0]
    past = cache_a_k.shape[2]
    lb = cache_b_k.shape[2]
    n_keep = min(BAND_PAST, s)
    assert s % TM_PROJ == 0 and s % TQ_A == 0 and s % TQ_B == 0 and (TQ_A // 2) % CHUNK == 0 and TQ_B % CHUNK == 0
    assert 2 * TQ_B >= BAND_PAST and past >= lb
    assert _far_bucket(TQ_A + 1) == T5_BUCKETS // 2 - 1

    r = jnp.arange(NORM_GROUP)
    pm = ((r[:, None] // HD_A) == (r[None, :] // HD_A)).astype(BF16) * (1.0 / HD_A)
    c_rows = bp + bs
    c_pad = -(-c_rows // 8) * 8
    c_all = jnp.concatenate([c_prompt, c_sample, jnp.zeros((c_pad - c_rows, d), F32)], axis=0)

    nk_as = -(-(past + t) // LANES) * LANES
    nk_bs = -(-(lb + t) // LANES) * LANES
    t5_fn = functools.partial(_t5_run, t5_bias)
    outs = [[] for _ in range(8)]
    for l in range(depth):
        lam_init = 0.8 - 0.6 * math.exp(-0.3 * l)
        rel_fn = functools.partial(_clipped_run, rel_bias_b[l])
        cfar = t5_bias[_t5_bucket(jnp.int32(-(TQ_A + 1)))]
        bias_ap = _bias_tiles(t5_fn, cfar, H_A, TQ_A, 2 * TQ_A, 2, QBASE, QBASE - TQ_A, 0, False, 2 * TQ_A, lead_step=TQ_A)
        bias_bp = _bias_tiles(rel_fn, jnp.zeros((H_B,)), H_B, TQ_B, 3 * TQ_B, 3, QBASE, QBASE, TQ_B, True, 3 * TQ_B)
        bias_as = _bias_tiles(t5_fn, jnp.zeros((H_A,)), H_A, t, nk_as, 1, past, 0, 0, False, past + t)
        bias_bs = _bias_tiles(rel_fn, jnp.zeros((H_B,)), H_B, t, nk_bs, 1, past, past - lb, 0, True, lb + t)

        mod = _modulation(c_all, w_ada[l], b_ada[l])
        shift = mod[:, :d].reshape(c_pad, 1, d)
        scale = mod[:, d:2 * d].reshape(c_pad, 1, d)
        gate = mod[:, 2 * d:].reshape(c_pad, 1, d)
        w_in_bf = w_in[l].astype(BF16)
        w_oa_bf, w_ob_bf, w_out_bf = w_oa[l].astype(BF16), w_ob[l].astype(BF16), w_out[l].astype(BF16)
        tile8 = lambda g: jnp.tile(g, SEG // g.shape[0]).reshape(1, SEG)
        gains = (tile8(g_qa[l]), tile8(g_ka[l]), tile8(g_qb[l]), tile8(g_kb[l]))
        lamv = jnp.stack([lam_q1[l], lam_k1[l], lam_q2[l], lam_k2[l]])
        gsub = g_subln[l].reshape(1, LANES)

        (qa, ka32, kab, va32, vab, ga, qb, kb32, kbb, vb32, vbb, gb, mg) = _project(
            xp, shift[:bp], scale[:bp], g_norm[l], w_in_bf, *gains, pm, 1, TM_PROJ)
        bounded_a = _logits_bounded(g_qa[l], g_ka[l], HD_A, t5_bias, cfar)
        bounded_b = _logits_bounded(g_qb[l], g_kb[l], HD_B, rel_bias_b[l], jnp.zeros((H_B,)))
        oa = _attn_a_prompt(qa, kab, vab, ga, bias_ap, bounded_a, lamv, gsub, TQ_A, lam_init)
        ob = _attn_b_prompt(qb, kbb, vbb, gb, bias_bp, bounded_b, TQ_B)
        xp = _merge(xp, gate[:bp], oa, ob, mg, w_oa_bf, w_ob_bf, w_out_bf, 1, TM_PROJ)
        outs[0].append(ka32.reshape(bp, s, H_A, 2 * HD_A))
        outs[1].append(va32.reshape(bp, s, H_A, 2 * HD_A))
        outs[2].append(kb32[:, s - n_keep:].reshape(bp, n_keep, H_B, HD_B))
        outs[3].append(vb32[:, s - n_keep:].reshape(bp, n_keep, H_B, HD_B))

        (qa, ka32, kab, va32, vab, ga, qb, kb32, kbb, vb32, vbb, gb, mg) = _project(
            xs, shift[bp:c_rows], scale[bp:c_rows], g_norm[l], w_in_bf, *gains, pm, bs, t)
        oa = _attn_a_sample(qa, kab, vab, ga, cache_a_k[l].reshape(bs, past * H_A, LANES), cache_a_v[l].reshape(bs, past * H_A, LANES),
                            bias_as, lamv, gsub, lam_init)
        ob = _attn_b_sample(qb, kbb, vbb, gb, cache_b_k[l].reshape(bs, lb, W_B), cache_b_v[l].reshape(bs, lb, W_B), bias_bs)
        xs = _merge(xs, gate[bp:c_rows], oa, ob, mg, w_oa_bf, w_ob_bf, w_out_bf, bs, t)
        outs[4].append(ka32.reshape(bs, t, H_A, 2 * HD_A))
        outs[5].append(va32.reshape(bs, t, H_A, 2 * HD_A))
        outs[6].append(kb32.reshape(bs, t, H_B, HD_B))
        outs[7].append(vb32.reshape(bs, t, H_B, HD_B))

    return (xp, xs) + tuple(jnp.stack(o) for o in outs)
```

```python
import functools
import math

import jax
import jax.numpy as jnp
from jax import lax
from jax.experimental import pallas as pl
from jax.experimental.pallas import tpu as pltpu

CHUNK = 64
H_A = 4
HD_A = 64
W_A = H_A * 2 * HD_A
H_B = 8
HD_B = 64
W_B = H_B * HD_B
BAND_CHUNKS = 8
BAND_PAST = BAND_CHUNKS * CHUNK
REL_CLIP_B = 128
T5_BUCKETS = 32
T5_MAX_EXACT = 8
T5_MAX_DIST = 128
EPS = 1e-6
NEG = -1e30

LANES = 128
SUBLANES = 8
SEG = 512
NORM_GROUP = 256
MAX_DIRECT_LOGIT = 60.0
BF16_ROUND_MARGIN = 1.02
VMEM_LIMIT = 56 * 1024 * 1024

LOG2E = math.log2(math.e)

F32 = jnp.float32
BF16 = jnp.bfloat16


def _t5_bucket(rel):
    half = T5_BUCKETS // 2
    assert (T5_MAX_DIST // T5_MAX_EXACT) ** 2 == 2 ** (half - T5_MAX_EXACT)
    ret = jnp.where(rel > 0, half, 0)
    n = jnp.abs(rel)
    large = T5_MAX_EXACT + sum((n * n >= T5_MAX_EXACT ** 2 * 2 ** j).astype(jnp.int32)
                               for j in range(1, half - T5_MAX_EXACT))
    large = jnp.minimum(large, half - 1)
    return ret + jnp.where(n < T5_MAX_EXACT, n, large)


def _t5_run(t5_bias, first_rel, count):
    assert max(abs(first_rel), abs(first_rel + count)) < 2 ** 15
    rel = first_rel + jnp.arange(count, dtype=jnp.int32)
    hit = _t5_bucket(rel)[:, None, None] == jnp.arange(T5_BUCKETS, dtype=jnp.int32)[None, :, None]
    return jnp.sum(jnp.where(hit, t5_bias[None], 0.0), axis=1)


def _clipped_run(table, first_rel, count):
    n = table.shape[0]
    first = first_rel + (n - 1) // 2
    n_lo = min(max(-first, 0), count)
    n_hi = min(max(first + count - n, 0), count)
    mid = count - n_lo - n_hi
    parts = [jnp.broadcast_to(table[:1], (n_lo, table.shape[1])),
             table[first + n_lo:first + n_lo + mid] if mid > 0 else table[:0],
             jnp.broadcast_to(table[n - 1:], (n_hi, table.shape[1]))]
    return jnp.concatenate(parts, axis=0)


def _far_bucket(n):
    half = T5_BUCKETS // 2
    return min(T5_MAX_EXACT + int(math.log(n / T5_MAX_EXACT) / math.log(T5_MAX_DIST / T5_MAX_EXACT) * (half - T5_MAX_EXACT)), half - 1)


def _nt(a, b):
    return lax.dot_general(a, b, (((1,), (1,)), ((), ())), preferred_element_type=F32)


def _silu(x):
    return x * (1.0 / (1.0 + jnp.exp(-x)))


def _sigmoid(x):
    return 1.0 / (1.0 + jnp.exp(-x))


def _mod_kernel(c_ref, w_ref, b_ref, o_ref):
    c = c_ref[...]
    o_ref[...] = jnp.dot(_silu(c), w_ref[...], preferred_element_type=F32,
                         precision=lax.Precision.HIGHEST) + b_ref[...]


def _modulation(c_all, w_ada, b_ada):
    rows, d = c_all.shape
    n_out = w_ada.shape[1]
    tn = d
    return pl.pallas_call(
        _mod_kernel,
        grid=(n_out // tn,),
        in_specs=[pl.BlockSpec((rows, d), lambda j: (0, 0)),
                  pl.BlockSpec((d, tn), lambda j: (0, j)),
                  pl.BlockSpec((1, tn), lambda j: (0, j))],
        out_specs=pl.BlockSpec((rows, tn), lambda j: (0, j)),
        out_shape=jax.ShapeDtypeStruct((rows, n_out), F32),
        name="adaln_mod",
    )(c_all, w_ada, b_ada.reshape(1, n_out))


def _proj_kernel(x_ref, shift_ref, scale_ref, gn_ref, w_ref, gqa_ref, gka_ref, gqb_ref, gkb_ref, pm_ref,
                 qa_ref, ka32_ref, kab_ref, va32_ref, vab_ref, ga_ref,
                 qb_ref, kb32_ref, kbb_ref, vb32_ref, vbb_ref, gb_ref, mg_ref):
    nb, t, d = x_ref.shape
    rows = nb * t
    x = x_ref[...]
    ms = jnp.mean(x * x, axis=-1, keepdims=True)
    xn = x * lax.rsqrt(ms + EPS) * gn_ref[...]
    h = xn * (1.0 + scale_ref[...]) + shift_ref[...]
    hb = h.reshape(rows, d).astype(BF16)

    def seg(c, width=SEG):
        return jnp.dot(hb, w_ref[:, c * SEG:c * SEG + width], preferred_element_type=F32)

    def head_norm(y, g_ref):
        sq = y * y
        hi = sq.astype(BF16)
        pm = pm_ref[...]
        parts = []
        for c in range(SEG // NORM_GROUP):
            sl = slice(c * NORM_GROUP, (c + 1) * NORM_GROUP)
            gms = jnp.dot(hi[:, sl], pm, preferred_element_type=F32)
            parts.append(y[:, sl] * lax.rsqrt(gms + EPS))
        return jnp.concatenate(parts, axis=1) * g_ref[...]

    def put(ref, y):
        ref[...] = y.astype(ref.dtype).reshape(ref.shape)

    def put_by_head(ref, y):
        for hh in range(H_A):
            ref[:, pl.ds(hh, t, stride=H_A), :] = y[:, hh * LANES:(hh + 1) * LANES].reshape(nb, t, LANES)

    put(qa_ref, head_norm(seg(0), gqa_ref) * (HD_A ** -0.5 * LOG2E))
    ka = head_norm(seg(1), gka_ref)
    put_by_head(ka32_ref, ka)
    put(kab_ref, ka)
    va = seg(2)
    put_by_head(va32_ref, va)
    put(vab_ref, va)
    put(ga_ref, seg(3))
    put(qb_ref, head_norm(seg(4), gqb_ref) * (HD_B ** -0.5 * LOG2E))
    kb = head_norm(seg(5), gkb_ref)
    put(kb32_ref, kb)
    put(kbb_ref, kb)
    vb = seg(6)
    put(vb32_ref, vb)
    put(vbb_ref, vb)
    put(gb_ref, seg(7))
    for c in range(8, 12):
        mg_ref[:, :, (c - 8) * SEG:(c - 7) * SEG] = seg(c).astype(mg_ref.dtype).reshape(nb, t, SEG)


def _project(x, shift, scale, g_norm, w_in_bf, gqa, gka, gqb, gkb, pm, nb, t):
    bx, sx, d = x.shape
    n_cols = w_in_bf.shape[1]
    grid = (bx // nb, sx // t)
    tok = lambda w: pl.BlockSpec((nb, t, w), lambda b, i: (b, i, 0))
    per_b = pl.BlockSpec((nb, 1, d), lambda b, i: (b, 0, 0))
    const = lambda shp: pl.BlockSpec(shp, lambda b, i: (0,) * len(shp))
    sds = lambda w, dt: jax.ShapeDtypeStruct((bx, sx, w), dt)
    by_head = jax.ShapeDtypeStruct((bx, sx * H_A, LANES), F32)
    out_shape = (sds(SEG, BF16), by_head, sds(SEG, BF16), by_head, sds(SEG, BF16), sds(SEG, BF16),
                 sds(SEG, BF16), sds(SEG, F32), sds(SEG, BF16), sds(SEG, F32), sds(SEG, BF16), sds(SEG, BF16),
                 sds(4 * SEG, BF16))
    out_specs = tuple(pl.BlockSpec((nb, s.shape[1] // (sx // t), s.shape[2]), lambda b, i: (b, i, 0)) for s in out_shape)
    return pl.pallas_call(
        _proj_kernel,
        grid=grid,
        in_specs=[tok(d), per_b, per_b, const((1, d)),
                  pl.BlockSpec((d, n_cols), lambda b, i: (0, 0), pipeline_mode=pl.Buffered(1)),
                  const((1, SEG)), const((1, SEG)), const((1, SEG)), const((1, SEG)),
                  const((NORM_GROUP, NORM_GROUP))],
        out_specs=out_specs,
        out_shape=out_shape,
        compiler_params=pltpu.CompilerParams(dimension_semantics=("arbitrary", "arbitrary"),
                                             vmem_limit_bytes=VMEM_LIMIT),
        name="in_proj",
    )(x, shift, scale, g_norm.reshape(1, d), w_in_bf, gqa, gka, gqb, gkb, pm)


def _bias_tile_kernel(off_ref, gen_ref, o_ref, *, tq, nk, n_var, qbase, kbase0, kstep, lead_step, band, nvalid):
    v = pl.program_id(1)
    g = gen_ref[0, 0]
    x = jnp.broadcast_to(g, (tq, g.shape[-1]))
    y = pltpu.roll(x, 0, 1, stride=1, stride_axis=0)
    y = (y[:, :nk] - off_ref[pl.program_id(0)]) * LOG2E
    row = lax.broadcasted_iota(jnp.int32, (tq, nk), 0)
    col = lax.broadcasted_iota(jnp.int32, (tq, nk), 1)
    qc = (qbase + row) // CHUNK
    kc = (kbase0 - v * kstep + col) // CHUNK
    y = jnp.where(kc <= qc, y, NEG)
    if band:
        y = jnp.where(kc >= qc - BAND_CHUNKS, y, NEG)
    y = jnp.where(col < nvalid, y, NEG)
    if lead_step:
        y = jnp.where(col >= (n_var - 1 - v) * lead_step, y, NEG)
    o_ref[0, 0] = y


def _bias_tiles(table_fn, offset, n_heads, tq, nk, n_var, qbase, kbase0, kstep, band, nvalid, lead_step=0):
    length = 1 << (tq + nk - 1).bit_length()
    assert length >= tq + nk - 1 and min(kbase0 - (n_var - 1) * kstep, qbase) >= 0
    gen = jnp.stack([jnp.concatenate([table_fn(kbase0 - v * kstep - qbase, nk),
                                      table_fn(kbase0 - v * kstep - qbase + nk - length, length - nk)], axis=0)
                     for v in range(n_var)])
    gen = jnp.transpose(gen, (2, 0, 1)).astype(F32).reshape(n_heads, n_var, 1, length)
    return pl.pallas_call(
        functools.partial(_bias_tile_kernel, tq=tq, nk=nk, n_var=n_var, qbase=qbase, kbase0=kbase0, kstep=kstep,
                          lead_step=lead_step, band=band, nvalid=nvalid),
        grid=(n_heads, n_var),
        in_specs=[pl.BlockSpec(memory_space=pltpu.SMEM),
                  pl.BlockSpec((1, 1, 1, length), lambda h, v: (h, v, 0, 0))],
        out_specs=pl.BlockSpec((1, 1, tq, nk), lambda h, v: (h, v, 0, 0)),
        out_shape=jax.ShapeDtypeStruct((n_heads, n_var, tq, nk), F32),
        name="bias_tiles",
    )(offset.astype(F32), gen)


def _split_halves(q):
    lane = lax.broadcasted_iota(jnp.int32, q.shape, 1)
    zero = jnp.zeros_like(q)
    return jnp.where(lane < LANES // 2, q, zero), jnp.where(lane >= LANES // 2, q, zero)


def _online_update(carry, s, v):
    m, l, acc = carry
    m_new = jnp.maximum(m, jnp.max(s, axis=-1, keepdims=True))
    alpha = jnp.exp2(m - m_new)
    p = jnp.exp2(s - m_new)
    l = alpha * l + jnp.sum(p, axis=-1, keepdims=True)
    acc = alpha * acc + jnp.dot(p.astype(BF16), v, preferred_element_type=F32)
    return m_new, l, acc


def _diff_init(tq):
    return (jnp.full((tq, 1), -jnp.inf, F32), jnp.zeros((tq, 1), F32), jnp.zeros((tq, LANES), F32))


def _diff_finish(c0, c1, lamv_ref, gsub_ref, g, lam_init):
    lamv = lamv_ref[...]
    e1 = jnp.exp(jnp.sum(lamv[0:1] * lamv[1:2], axis=-1, keepdims=True))
    e2 = jnp.exp(jnp.sum(lamv[2:3] * lamv[3:4], axis=-1, keepdims=True))
    lam = e1 - e2 + lam_init
    o = c0[2] / c0[1] - lam * (c1[2] / c1[1])
    o = o * lax.rsqrt(jnp.mean(o * o, axis=-1, keepdims=True) + EPS) * gsub_ref[...]
    o = o * (1.0 - lam_init)
    return o * _silu(g.astype(F32))


def _lane_partial_sum(p):
    out = p[:, :LANES]
    for c in range(1, p.shape[1] // LANES):
        out = out + p[:, c * LANES:(c + 1) * LANES]
    return out


def _attn_a_prompt_kernel(par_ref, q_ref, k_ref, v_ref, g_ref, bias_ref, lamv_ref, gsub_ref, o_ref, l_ref, acc_ref,
                          *, tq, lam_init):
    i = pl.program_id(2)
    n_far = jnp.maximum(i - 1, 0)
    near = pl.multiple_of(n_far * tq, tq)
    diag = pl.multiple_of(i * tq, tq)
    half = tq // 2
    bounded = par_ref[0] > 0.5
    n_heads = q_ref.shape[2] // LANES
    hs = [slice(hh * LANES, (hh + 1) * LANES) for hh in range(n_heads)]

    def finish_head(hh, c0, c1):
        o = _diff_finish(c0, c1, lamv_ref, gsub_ref, g_ref[0, :, hs[hh]], lam_init)
        o_ref[0, :, hs[hh]] = o.astype(o_ref.dtype)

    def run_online():
        qs = [_split_halves(q_ref[0, :, sl]) for sl in hs]

        def tile(carry, start, bias):
            out = []
            for hh, (c0, c1) in enumerate(carry):
                k = k_ref[0, pl.ds(start, tq), hs[hh]]
                v = v_ref[0, pl.ds(start, tq), hs[hh]]
                s0, s1 = _nt(qs[hh][0], k), _nt(qs[hh][1], k)
                if bias is not None:
                    s0, s1 = s0 + bias_ref[hh, 0, :, bias], s1 + bias_ref[hh, 0, :, bias]
                out.append((_online_update(c0, s0, v), _online_update(c1, s1, v)))
            return tuple(out)

        init = _diff_init(tq)
        carry = lax.fori_loop(0, n_far, lambda j, c: tile(c, pl.multiple_of(j * tq, tq), None), ((init, init),) * n_heads)
        carry = tile(tile(carry, near, slice(0, tq)), diag, slice(tq, 2 * tq))
        for hh, (c0, c1) in enumerate(carry):
            finish_head(hh, c0, c1)

    def run_bounded():
        qs = [_split_halves(q_ref[0, :, sl]) for sl in hs]
        l_ref[...] = jnp.zeros(l_ref.shape, F32)
        acc_ref[...] = jnp.zeros(acc_ref.shape, F32)

        def accumulate(hh, start, nk, rows=slice(None), bias_cols=None):
            k = k_ref[0, pl.ds(start, nk), hs[hh]]
            v = v_ref[0, pl.ds(start, nk), hs[hh]]
            for mp in range(2):
                s = _nt(qs[hh][mp][rows], k)
                if bias_cols is not None:
                    s = s + bias_ref[hh, 0, rows, bias_cols]
                p = jnp.exp2(s)
                l_ref[hh, mp, rows, :] += _lane_partial_sum(p)
                acc_ref[hh, mp, rows, :] += jnp.dot(p.astype(BF16), v, preferred_element_type=F32)

        def far(j, carry):
            for hh in range(n_heads):
                accumulate(hh, pl.multiple_of(j * tq, tq), tq)
            return carry

        lax.fori_loop(0, n_far, far, 0)
        for hh in range(n_heads):
            accumulate(hh, near, tq, bias_cols=slice(0, tq))
            accumulate(hh, diag, half, slice(0, half), slice(tq, tq + half))
            accumulate(hh, diag, tq, slice(half, tq), slice(tq, 2 * tq))
            c0 = (None, jnp.sum(l_ref[hh, 0], axis=-1, keepdims=True), acc_ref[hh, 0])
            c1 = (None, jnp.sum(l_ref[hh, 1], axis=-1, keepdims=True), acc_ref[hh, 1])
            finish_head(hh, c0, c1)

    pl.when(bounded)(run_bounded)
    pl.when(jnp.logical_not(bounded))(run_online)


def _logits_bounded(g_q, g_k, head_dim, table, offset):
    qk = math.sqrt(head_dim) * LOG2E * jnp.max(jnp.abs(g_q)) * jnp.max(jnp.abs(g_k)) * BF16_ROUND_MARGIN
    bias = jnp.max(jnp.abs(table - offset[None, :])) * LOG2E
    return (qk + bias <= MAX_DIRECT_LOGIT).astype(F32).reshape(1)


def _attn_a_prompt(qa, kab, vab, ga, bias, bounded, lamv, gsub, tq, lam_init):
    b, s, _ = qa.shape
    n_var = bias.shape[1]
    assert bias.shape[3] == n_var * tq
    hps = HEADS_PER_STEP_A
    tile = pl.BlockSpec((1, tq, hps * LANES), lambda bi, h, i: (bi, i, h))
    full = pl.BlockSpec((1, s, hps * LANES), lambda bi, h, i: (bi, 0, h))
    return pl.pallas_call(
        functools.partial(_attn_a_prompt_kernel, tq=tq, lam_init=lam_init),
        grid=(b, H_A // hps, s // tq),
        in_specs=[pl.BlockSpec(memory_space=pltpu.SMEM), tile, full, full, tile,
                  pl.BlockSpec((hps, 1, tq, n_var * tq), lambda bi, h, i: (h, jnp.minimum(i, n_var - 1), 0, 0)),
                  pl.BlockSpec((4, HD_A), lambda bi, h, i: (0, 0)),
                  pl.BlockSpec((1, LANES), lambda bi, h, i: (0, 0))],
        out_specs=tile,
        out_shape=jax.ShapeDtypeStruct((b, s, W_A), BF16),
        scratch_shapes=[pltpu.VMEM((hps, 2, tq, LANES), F32), pltpu.VMEM((hps, 2, tq, LANES), F32)],
        compiler_params=pltpu.CompilerParams(dimension_semantics=("arbitrary", "arbitrary", "arbitrary"),
                                             vmem_limit_bytes=VMEM_LIMIT),
        name="diff_attn_prompt",
    )(bounded, qa, kab, vab, ga, bias, lamv, gsub)


def _attn_a_sample_kernel(q_ref, kc_ref, vc_ref, kn_ref, vn_ref, g_ref, bias_ref, lamv_ref, gsub_ref, o_ref, *, lam_init):
    tq = q_ref.shape[1]
    nk = bias_ref.shape[3]
    past = kc_ref.shape[1] // H_A
    pad = jnp.zeros((nk - past - tq, LANES), BF16)
    for hh in range(H_A):
        sl = slice(hh * LANES, (hh + 1) * LANES)
        rows = pl.ds(hh, past, stride=H_A)
        qst = jnp.concatenate(_split_halves(q_ref[0, :, sl]), axis=0)
        k = jnp.concatenate([kc_ref[0, rows, :].astype(BF16), kn_ref[0, :, sl], pad], axis=0)
        v = jnp.concatenate([vc_ref[0, rows, :].astype(BF16), vn_ref[0, :, sl], pad], axis=0)
        bias = bias_ref[hh, 0]
        c = _online_update(_diff_init(2 * tq), _nt(qst, k) + jnp.concatenate([bias, bias], axis=0), v)
        c0, c1 = tuple(x[:tq] for x in c), tuple(x[tq:] for x in c)
        o_ref[0, :, sl] = _diff_finish(c0, c1, lamv_ref, gsub_ref, g_ref[0, :, sl], lam_init).astype(o_ref.dtype)


def _attn_a_sample(qa, kab, vab, ga, cache_k, cache_v, bias, lamv, gsub, lam_init):
    b, t, _ = qa.shape
    rows = cache_k.shape[1]
    nk = bias.shape[3]
    new = pl.BlockSpec((1, t, W_A), lambda bi: (bi, 0, 0))
    old = pl.BlockSpec((1, rows, LANES), lambda bi: (bi, 0, 0))
    return pl.pallas_call(
        functools.partial(_attn_a_sample_kernel, lam_init=lam_init),
        grid=(b,),
        in_specs=[new, old, old, new, new, new,
                  pl.BlockSpec((H_A, 1, t, nk), lambda bi: (0, 0, 0, 0)),
                  pl.BlockSpec((4, HD_A), lambda bi: (0, 0)),
                  pl.BlockSpec((1, LANES), lambda bi: (0, 0))],
        out_specs=new,
        out_shape=jax.ShapeDtypeStruct((b, t, W_A), BF16),
        name="diff_attn_sample",
    )(qa, cache_k, cache_v, kab, vab, ga, bias, lamv, gsub)


def _band_pair(q, k, v, g, bias_ref, bounded=False, first_head=0):
    t = q.shape[0]
    qst = jnp.concatenate(_split_halves(q), axis=0)
    s = _nt(qst, k) + jnp.concatenate([bias_ref[first_head, 0], bias_ref[first_head + 1, 0]], axis=0)
    if not bounded:
        s = s - jnp.max(s, axis=-1, keepdims=True)
    p = jnp.exp2(s)
    l = jnp.sum(p, axis=-1, keepdims=True)
    o2 = jnp.dot(p.astype(BF16), v, preferred_element_type=F32) / l
    lane = lax.broadcasted_iota(jnp.int32, (t, LANES), 1)
    o = jnp.where(lane < LANES // 2, o2[:t], o2[t:])
    return o * _silu(g.astype(F32))


def _attn_b_prompt_kernel(par_ref, q_ref, k_ref, v_ref, g_ref, bias_ref, o_ref, *, tq, n_var):
    i = pl.program_id(2)
    ws = pl.multiple_of(jnp.maximum(i - (n_var - 1), 0) * tq, tq)
    bounded = par_ref[0] > 0.5

    def run(direct):
        for pp in range(q_ref.shape[2] // LANES):
            sl = slice(pp * LANES, (pp + 1) * LANES)
            k = k_ref[0, pl.ds(ws, n_var * tq), sl]
            v = v_ref[0, pl.ds(ws, n_var * tq), sl]
            o = _band_pair(q_ref[0, :, sl], k, v, g_ref[0, :, sl], bias_ref, direct, 2 * pp)
            o_ref[0, :, sl] = o.astype(o_ref.dtype)

    pl.when(bounded)(lambda: run(True))
    pl.when(jnp.logical_not(bounded))(lambda: run(False))


def _attn_b_prompt(qb, kbb, vbb, gb, bias, bounded, tq):
    b, s, w = qb.shape
    n_var = bias.shape[1]
    assert bias.shape[3] == n_var * tq
    tile = pl.BlockSpec((1, tq, w), lambda bi, p, i: (bi, i, p))
    full = pl.BlockSpec((1, s, w), lambda bi, p, i: (bi, 0, p))
    return pl.pallas_call(
        functools.partial(_attn_b_prompt_kernel, tq=tq, n_var=n_var),
        grid=(b, 1, s // tq),
        in_specs=[pl.BlockSpec(memory_space=pltpu.SMEM), tile, full, full, tile,
                  pl.BlockSpec((H_B, 1, tq, n_var * tq), lambda bi, p, i: (p, jnp.minimum(i, n_var - 1), 0, 0))],
        out_specs=tile,
        out_shape=jax.ShapeDtypeStruct((b, s, W_B), BF16),
        compiler_params=pltpu.CompilerParams(dimension_semantics=("arbitrary", "arbitrary", "arbitrary"),
                                             vmem_limit_bytes=VMEM_LIMIT),
        name="band_attn_prompt",
    )(bounded, qb, kbb, vbb, gb, bias)


def _attn_b_sample_kernel(q_ref, kc_ref, vc_ref, kn_ref, vn_ref, g_ref, bias_ref, o_ref):
    nk = bias_ref.shape[3]
    past, t = kc_ref.shape[2], kn_ref.shape[1]
    pad = jnp.zeros((nk - past - t, LANES), BF16)
    lane = lax.broadcasted_iota(jnp.int32, (t, LANES), 1)
    for pp in range(q_ref.shape[2] // LANES):
        sl = slice(pp * LANES, (pp + 1) * LANES)
        qst = jnp.concatenate(_split_halves(q_ref[0, :, sl]), axis=0)
        k_new = jnp.concatenate([kn_ref[0, :, sl], pad], axis=0)
        v_new = jnp.concatenate([vn_ref[0, :, sl], pad], axis=0)
        s = jnp.concatenate([jnp.dot(qst, kc_ref[0, sl, :].astype(BF16), preferred_element_type=F32),
                             _nt(qst, k_new)], axis=1)
        s = s + jnp.concatenate([bias_ref[2 * pp, 0], bias_ref[2 * pp + 1, 0]], axis=0)
        p = jnp.exp2(s - jnp.max(s, axis=-1, keepdims=True))
        l = jnp.sum(p, axis=-1, keepdims=True)
        pb = p.astype(BF16)
        o2 = (_nt(pb[:, :past], vc_ref[0, sl, :].astype(BF16))
              + jnp.dot(pb[:, past:], v_new, preferred_element_type=F32)) / l
        o = jnp.where(lane < LANES // 2, o2[:t], o2[t:])
        o_ref[0, :, sl] = (o * _silu(g_ref[0, :, sl].astype(F32))).astype(o_ref.dtype)


def _attn_b_sample(qb, kbb, vbb, gb, cache_kt, cache_vt, bias):
    b, t, w = qb.shape
    past = cache_kt.shape[2]
    nk = bias.shape[3]
    assert past % LANES == 0
    new = pl.BlockSpec((1, t, w), lambda bi: (bi, 0, 0))
    old = pl.BlockSpec((1, w, past), lambda bi: (bi, 0, 0))
    return pl.pallas_call(
        _attn_b_sample_kernel,
        grid=(b,),
        in_specs=[new, old, old, new, new, new,
                  pl.BlockSpec((H_B, 1, t, nk), lambda bi: (0, 0, 0, 0))],
        out_specs=new,
        out_shape=jax.ShapeDtypeStruct((b, t, W_B), BF16),
        name="band_attn_sample",
    )(qb, cache_kt, cache_vt, kbb, vbb, gb, bias)


def _merge_kernel(x_ref, gate_ref, oa_ref, ob_ref, mg_ref, woa_ref, wob_ref, wout_ref, o_ref):
    nb, t, d = x_ref.shape
    rows = nb * t
    ya = jnp.dot(oa_ref[...].reshape(rows, W_A), woa_ref[...], preferred_element_type=F32)
    yb = jnp.dot(ob_ref[...].reshape(rows, W_B), wob_ref[...], preferred_element_type=F32)
    mg = mg_ref[...].reshape(rows, 2 * d).astype(F32)
    m = _sigmoid(mg[:, :d]) * ya + _sigmoid(mg[:, d:]) * yb
    y = jnp.dot(m.astype(BF16), wout_ref[...], preferred_element_type=F32)
    o_ref[...] = x_ref[...] + gate_ref[...] * y.reshape(nb, t, d)


def _merge(x, gate, oa, ob, mg, w_oa_bf, w_ob_bf, w_out_bf, nb, t):
    bx, sx, d = x.shape
    tok = lambda w: pl.BlockSpec((nb, t, w), lambda b, i: (b, i, 0))
    const = lambda shp: pl.BlockSpec(shp, lambda b, i: (0,) * len(shp))
    return pl.pallas_call(
        _merge_kernel,
        grid=(bx // nb, sx // t),
        in_specs=[tok(d), pl.BlockSpec((nb, 1, d), lambda b, i: (b, 0, 0)), tok(W_A), tok(W_B), tok(2 * d),
                  const((W_A, d)), const((W_B, d)), const((d, d))],
        out_specs=tok(d),
        out_shape=jax.ShapeDtypeStruct((bx, sx, d), F32),
        compiler_params=pltpu.CompilerParams(dimension_semantics=("arbitrary", "arbitrary"),
                                             vmem_limit_bytes=VMEM_LIMIT),
        name="merge_out",
    )(x, gate, oa, ob, mg, w_oa_bf, w_ob_bf, w_out_bf)


TQ_A = 512
HEADS_PER_STEP_A = 2
TQ_B = 256
TM_PROJ = 512
QBASE = 2048


def kernel(x_prompt, x_sample, cache_a_k, cache_a_v, cache_b_k, cache_b_v, c_prompt, c_sample, g_norm, w_ada, b_ada, w_in, g_qa, g_ka, lam_q1, lam_k1, lam_q2, lam_k2, g_subln, t5_bias, g_qb, g_kb, rel_bias_b, w_oa, w_ob, w_out):
    xp, xs = x_prompt, x_sample
    bp, s, d = xp.shape
    bs, t, _ = xs.shape
    depth = w_in.shape[0]
    past = cache_a_k.shape[2]
    lb = cache_b_k.shape[2]
    n_keep = min(BAND_PAST, s)
    assert s % TM_PROJ == 0 and s % TQ_A == 0 and s % TQ_B == 0 and (TQ_A // 2) % CHUNK == 0 and TQ_B % CHUNK == 0
    assert 2 * TQ_B >= BAND_PAST and past >= lb
    assert _far_bucket(TQ_A + 1) == T5_BUCKETS // 2 - 1

    r = jnp.arange(NORM_GROUP)
    pm = ((r[:, None] // HD_A) == (r[None, :] // HD_A)).astype(BF16) * (1.0 / HD_A)
    c_rows = bp + bs
    c_pad = -(-c_rows // SUBLANES) * SUBLANES
    c_all = jnp.concatenate([c_prompt, c_sample, jnp.zeros((c_pad - c_rows, d), F32)], axis=0)

    nk_as = -(-(past + t) // LANES) * LANES
    nk_bs = -(-(lb + t) // LANES) * LANES
    t5_fn = functools.partial(_t5_run, t5_bias)
    outs = [[] for _ in range(8)]
    for l in range(depth):
        lam_init = 0.8 - 0.6 * math.exp(-0.3 * l)
        rel_fn = functools.partial(_clipped_run, rel_bias_b[l])
        cfar = t5_bias[_t5_bucket(jnp.int32(-(TQ_A + 1)))]
        bias_ap = _bias_tiles(t5_fn, cfar, H_A, TQ_A, 2 * TQ_A, 2, QBASE, QBASE - TQ_A, 0, False, 2 * TQ_A, lead_step=TQ_A)
        bias_bp = _bias_tiles(rel_fn, jnp.zeros((H_B,)), H_B, TQ_B, 3 * TQ_B, 3, QBASE, QBASE, TQ_B, True, 3 * TQ_B)
        bias_as = _bias_tiles(t5_fn, jnp.zeros((H_A,)), H_A, t, nk_as, 1, past, 0, 0, False, past + t)
        bias_bs = _bias_tiles(rel_fn, jnp.zeros((H_B,)), H_B, t, nk_bs, 1, past, past - lb, 0, True, lb + t)

        mod = _modulation(c_all, w_ada[l], b_ada[l])
        shift = mod[:, :d].reshape(c_pad, 1, d)
        scale = mod[:, d:2 * d].reshape(c_pad, 1, d)
        gate = mod[:, 2 * d:].reshape(c_pad, 1, d)
        w_in_bf = w_in[l].astype(BF16)
        w_oa_bf, w_ob_bf, w_out_bf = w_oa[l].astype(BF16), w_ob[l].astype(BF16), w_out[l].astype(BF16)
        tile8 = lambda g: jnp.tile(g, SEG // g.shape[0]).reshape(1, SEG)
        gains = (tile8(g_qa[l]), tile8(g_ka[l]), tile8(g_qb[l]), tile8(g_kb[l]))
        lamv = jnp.stack([lam_q1[l], lam_k1[l], lam_q2[l], lam_k2[l]])
        gsub = g_subln[l].reshape(1, LANES)

        (qa, ka32, kab, va32, vab, ga, qb, kb32, kbb, vb32, vbb, gb, mg) = _project(
            xp, shift[:bp], scale[:bp], g_norm[l], w_in_bf, *gains, pm, 1, TM_PROJ)
        bounded_a = _logits_bounded(g_qa[l], g_ka[l], HD_A, t5_bias, cfar)
        bounded_b = _logits_bounded(g_qb[l], g_kb[l], HD_B, rel_bias_b[l], jnp.zeros((H_B,)))
        oa = _attn_a_prompt(qa, kab, vab, ga, bias_ap, bounded_a, lamv, gsub, TQ_A, lam_init)
        ob = _attn_b_prompt(qb, kbb, vbb, gb, bias_bp, bounded_b, TQ_B)
        xp = _merge(xp, gate[:bp], oa, ob, mg, w_oa_bf, w_ob_bf, w_out_bf, 1, TM_PROJ)
        outs[0].append(ka32.reshape(bp, s, H_A, 2 * HD_A))
        outs[1].append(va32.reshape(bp, s, H_A, 2 * HD_A))
        outs[2].append(kb32[:, s - n_keep:].reshape(bp, n_keep, H_B, HD_B))
        outs[3].append(vb32[:, s - n_keep:].reshape(bp, n_keep, H_B, HD_B))

        (qa, ka32, kab, va32, vab, ga, qb, kb32, kbb, vb32, vbb, gb, mg) = _project(
            xs, shift[bp:c_rows], scale[bp:c_rows], g_norm[l], w_in_bf, *gains, pm, bs, t)
        oa = _attn_a_sample(qa, kab, vab, ga, cache_a_k[l].reshape(bs, past * H_A, LANES), cache_a_v[l].reshape(bs, past * H_A, LANES),
                            bias_as, lamv, gsub, lam_init)
        to_rows = lambda c: jnp.transpose(c, (0, 2, 3, 1)).reshape(bs, W_B, lb)
        ob = _attn_b_sample(qb, kbb, vbb, gb, to_rows(cache_b_k[l]), to_rows(cache_b_v[l]), bias_bs)
        xs = _merge(xs, gate[bp:c_rows], oa, ob, mg, w_oa_bf, w_ob_bf, w_out_bf, bs, t)
        outs[4].append(ka32.reshape(bs, t, H_A, 2 * HD_A))
        outs[5].append(va32.reshape(bs, t, H_A, 2 * HD_A))
        outs[6].append(kb32.reshape(bs, t, H_B, HD_B))
        outs[7].append(vb32.reshape(bs, t, H_B, HD_B))

    return (xp, xs) + tuple(jnp.stack(o) for o in outs)
```

```python
import functools
import math

import jax
import jax.numpy as jnp
from jax import lax
from jax.experimental import pallas as pl
from jax.experimental.pallas import tpu as pltpu

CHUNK = 64
H_A = 4
HD_A = 64
W_A = H_A * 2 * HD_A
H_B = 8
HD_B = 64
W_B = H_B * HD_B
BAND_CHUNKS = 8
BAND_PAST = BAND_CHUNKS * CHUNK
REL_CLIP_B = 128
T5_BUCKETS = 32
T5_MAX_EXACT = 8
T5_MAX_DIST = 128
EPS = 1e-6
NEG = -1e30

LANES = 128
SUBLANES = 8
SEG = 512
NORM_GROUP = 256
MAX_DIRECT_LOGIT = 60.0
BF16_ROUND_MARGIN = 1.02
VMEM_LIMIT = 56 * 1024 * 1024

LOG2E = math.log2(math.e)

F32 = jnp.float32
BF16 = jnp.bfloat16


def _t5_bucket(rel):
    half = T5_BUCKETS // 2
    assert (T5_MAX_DIST // T5_MAX_EXACT) ** 2 == 2 ** (half - T5_MAX_EXACT)
    ret = jnp.where(rel > 0, half, 0)
    n = jnp.abs(rel)
    large = T5_MAX_EXACT + sum((n * n >= T5_MAX_EXACT ** 2 * 2 ** j).astype(jnp.int32)
                               for j in range(1, half - T5_MAX_EXACT))
    large = jnp.minimum(large, half - 1)
    return ret + jnp.where(n < T5_MAX_EXACT, n, large)


def _t5_run(t5_bias, first_rel, count):
    assert max(abs(first_rel), abs(first_rel + count)) < 2 ** 15
    rel = first_rel + jnp.arange(count, dtype=jnp.int32)
    hit = _t5_bucket(rel)[:, None, None] == jnp.arange(T5_BUCKETS, dtype=jnp.int32)[None, :, None]
    return jnp.sum(jnp.where(hit, t5_bias[None], 0.0), axis=1)


def _clipped_run(table, first_rel, count):
    n = table.shape[0]
    first = first_rel + (n - 1) // 2
    n_lo = min(max(-first, 0), count)
    n_hi = min(max(first + count - n, 0), count)
    mid = count - n_lo - n_hi
    parts = [jnp.broadcast_to(table[:1], (n_lo, table.shape[1])),
             table[first + n_lo:first + n_lo + mid] if mid > 0 else table[:0],
             jnp.broadcast_to(table[n - 1:], (n_hi, table.shape[1]))]
    return jnp.concatenate(parts, axis=0)


def _far_bucket(n):
    half = T5_BUCKETS // 2
    return min(T5_MAX_EXACT + int(math.log(n / T5_MAX_EXACT) / math.log(T5_MAX_DIST / T5_MAX_EXACT) * (half - T5_MAX_EXACT)), half - 1)


def _nt(a, b):
    return lax.dot_general(a, b, (((1,), (1,)), ((), ())), preferred_element_type=F32)


def _silu(x):
    return x * (1.0 / (1.0 + jnp.exp(-x)))


def _sigmoid(x):
    return 1.0 / (1.0 + jnp.exp(-x))


def _mod_kernel(c_ref, w_ref, b_ref, o_ref):
    c = c_ref[...]
    o_ref[...] = jnp.dot(_silu(c), w_ref[...], preferred_element_type=F32,
                         precision=lax.Precision.HIGHEST) + b_ref[...]


def _modulation(c_all, w_ada, b_ada):
    rows, d = c_all.shape
    n_out = w_ada.shape[1]
    tn = d
    return pl.pallas_call(
        _mod_kernel,
        grid=(n_out // tn,),
        in_specs=[pl.BlockSpec((rows, d), lambda j: (0, 0)),
                  pl.BlockSpec((d, tn), lambda j: (0, j)),
                  pl.BlockSpec((1, tn), lambda j: (0, j))],
        out_specs=pl.BlockSpec((rows, tn), lambda j: (0, j)),
        out_shape=jax.ShapeDtypeStruct((rows, n_out), F32),
        name="adaln_mod",
    )(c_all, w_ada, b_ada.reshape(1, n_out))


def _proj_kernel(x_ref, shift_ref, scale_ref, gn_ref, w_ref, gqa_ref, gka_ref, gqb_ref, gkb_ref, pm_ref,
                 qa_ref, ka32_ref, kab_ref, va32_ref, vab_ref, ga_ref,
                 qb_ref, kb32_ref, kbb_ref, vb32_ref, vbb_ref, gb_ref, mg_ref):
    nb, t, d = x_ref.shape
    rows = nb * t
    x = x_ref[...]
    ms = jnp.mean(x * x, axis=-1, keepdims=True)
    xn = x * lax.rsqrt(ms + EPS) * gn_ref[...]
    h = xn * (1.0 + scale_ref[...]) + shift_ref[...]
    hb = h.reshape(rows, d).astype(BF16)

    def seg(c, width=SEG):
        return jnp.dot(hb, w_ref[:, c * SEG:c * SEG + width], preferred_element_type=F32)

    def head_norm(y, g_ref):
        sq = y * y
        hi = sq.astype(BF16)
        pm = pm_ref[...]
        parts = []
        for c in range(SEG // NORM_GROUP):
            sl = slice(c * NORM_GROUP, (c + 1) * NORM_GROUP)
            gms = jnp.dot(hi[:, sl], pm, preferred_element_type=F32)
            parts.append(y[:, sl] * lax.rsqrt(gms + EPS))
        return jnp.concatenate(parts, axis=1) * g_ref[...]

    def put(ref, y):
        ref[...] = y.astype(ref.dtype).reshape(ref.shape)

    def put_by_head(ref, y):
        for hh in range(H_A):
            ref[:, pl.ds(hh, t, stride=H_A), :] = y[:, hh * LANES:(hh + 1) * LANES].reshape(nb, t, LANES)

    put(qa_ref, head_norm(seg(0), gqa_ref) * (HD_A ** -0.5 * LOG2E))
    ka = head_norm(seg(1), gka_ref)
    put_by_head(ka32_ref, ka)
    put(kab_ref, ka)
    va = seg(2)
    put_by_head(va32_ref, va)
    put(vab_ref, va)
    put(ga_ref, seg(3))
    put(qb_ref, head_norm(seg(4), gqb_ref) * (HD_B ** -0.5 * LOG2E))
    kb = head_norm(seg(5), gkb_ref)
    put(kb32_ref, kb)
    put(kbb_ref, kb)
    vb = seg(6)
    put(vb32_ref, vb)
    put(vbb_ref, vb)
    put(gb_ref, seg(7))
    for c in range(8, 12):
        mg_ref[:, :, (c - 8) * SEG:(c - 7) * SEG] = seg(c).astype(mg_ref.dtype).reshape(nb, t, SEG)


def _project(x, shift, scale, g_norm, w_in_bf, gqa, gka, gqb, gkb, pm, nb, t):
    bx, sx, d = x.shape
    n_cols = w_in_bf.shape[1]
    grid = (bx // nb, sx // t)
    tok = lambda w: pl.BlockSpec((nb, t, w), lambda b, i: (b, i, 0))
    per_b = pl.BlockSpec((nb, 1, d), lambda b, i: (b, 0, 0))
    const = lambda shp: pl.BlockSpec(shp, lambda b, i: (0,) * len(shp))
    sds = lambda w, dt: jax.ShapeDtypeStruct((bx, sx, w), dt)
    by_head = jax.ShapeDtypeStruct((bx, sx * H_A, LANES), F32)
    out_shape = (sds(SEG, BF16), by_head, sds(SEG, BF16), by_head, sds(SEG, BF16), sds(SEG, BF16),
                 sds(SEG, BF16), sds(SEG, F32), sds(SEG, BF16), sds(SEG, F32), sds(SEG, BF16), sds(SEG, BF16),
                 sds(4 * SEG, BF16))
    out_specs = tuple(pl.BlockSpec((nb, s.shape[1] // (sx // t), s.shape[2]), lambda b, i: (b, i, 0)) for s in out_shape)
    return pl.pallas_call(
        _proj_kernel,
        grid=grid,
        in_specs=[tok(d), per_b, per_b, const((1, d)),
                  pl.BlockSpec((d, n_cols), lambda b, i: (0, 0), pipeline_mode=pl.Buffered(1)),
                  const((1, SEG)), const((1, SEG)), const((1, SEG)), const((1, SEG)),
                  const((NORM_GROUP, NORM_GROUP))],
        out_specs=out_specs,
        out_shape=out_shape,
        compiler_params=pltpu.CompilerParams(dimension_semantics=("arbitrary", "arbitrary"),
                                             vmem_limit_bytes=VMEM_LIMIT),
        name="in_proj",
    )(x, shift, scale, g_norm.reshape(1, d), w_in_bf, gqa, gka, gqb, gkb, pm)


def _bias_tile_kernel(off_ref, gen_ref, o_ref, *, tq, nk, n_var, qbase, kbase0, kstep, lead_step, band, nvalid):
    v = pl.program_id(1)
    g = gen_ref[0, 0]
    x = jnp.broadcast_to(g, (tq, g.shape[-1]))
    y = pltpu.roll(x, 0, 1, stride=1, stride_axis=0)
    y = (y[:, :nk] - off_ref[pl.program_id(0)]) * LOG2E
    row = lax.broadcasted_iota(jnp.int32, (tq, nk), 0)
    col = lax.broadcasted_iota(jnp.int32, (tq, nk), 1)
    qc = (qbase + row) // CHUNK
    kc = (kbase0 - v * kstep + col) // CHUNK
    y = jnp.where(kc <= qc, y, NEG)
    if band:
        y = jnp.where(kc >= qc - BAND_CHUNKS, y, NEG)
    y = jnp.where(col < nvalid, y, NEG)
    if lead_step:
        y = jnp.where(col >= (n_var - 1 - v) * lead_step, y, NEG)
    o_ref[0, 0] = y


def _bias_tiles(table_fn, offset, n_heads, tq, nk, n_var, qbase, kbase0, kstep, band, nvalid, lead_step=0):
    length = 1 << (tq + nk - 1).bit_length()
    assert length >= tq + nk - 1 and min(kbase0 - (n_var - 1) * kstep, qbase) >= 0
    gen = jnp.stack([jnp.concatenate([table_fn(kbase0 - v * kstep - qbase, nk),
                                      table_fn(kbase0 - v * kstep - qbase + nk - length, length - nk)], axis=0)
                     for v in range(n_var)])
    gen = jnp.transpose(gen, (2, 0, 1)).astype(F32).reshape(n_heads, n_var, 1, length)
    return pl.pallas_call(
        functools.partial(_bias_tile_kernel, tq=tq, nk=nk, n_var=n_var, qbase=qbase, kbase0=kbase0, kstep=kstep,
                          lead_step=lead_step, band=band, nvalid=nvalid),
        grid=(n_heads, n_var),
        in_specs=[pl.BlockSpec(memory_space=pltpu.SMEM),
                  pl.BlockSpec((1, 1, 1, length), lambda h, v: (h, v, 0, 0))],
        out_specs=pl.BlockSpec((1, 1, tq, nk), lambda h, v: (h, v, 0, 0)),
        out_shape=jax.ShapeDtypeStruct((n_heads, n_var, tq, nk), F32),
        name="bias_tiles",
    )(offset.astype(F32), gen)


def _split_halves(q):
    lane = lax.broadcasted_iota(jnp.int32, q.shape, 1)
    zero = jnp.zeros_like(q)
    return jnp.where(lane < LANES // 2, q, zero), jnp.where(lane >= LANES // 2, q, zero)


def _online_update(carry, s, v):
    m, l, acc = carry
    m_new = jnp.maximum(m, jnp.max(s, axis=-1, keepdims=True))
    alpha = jnp.exp2(m - m_new)
    p = jnp.exp2(s - m_new)
    l = alpha * l + jnp.sum(p, axis=-1, keepdims=True)
    acc = alpha * acc + jnp.dot(p.astype(BF16), v, preferred_element_type=F32)
    return m_new, l, acc


def _diff_init(tq):
    return (jnp.full((tq, 1), -jnp.inf, F32), jnp.zeros((tq, 1), F32), jnp.zeros((tq, LANES), F32))


def _diff_finish(c0, c1, lamv_ref, gsub_ref, g, lam_init):
    lamv = lamv_ref[...]
    e1 = jnp.exp(jnp.sum(lamv[0:1] * lamv[1:2], axis=-1, keepdims=True))
    e2 = jnp.exp(jnp.sum(lamv[2:3] * lamv[3:4], axis=-1, keepdims=True))
    lam = e1 - e2 + lam_init
    o = c0[2] / c0[1] - lam * (c1[2] / c1[1])
    o = o * lax.rsqrt(jnp.mean(o * o, axis=-1, keepdims=True) + EPS) * gsub_ref[...]
    o = o * (1.0 - lam_init)
    return o * _silu(g.astype(F32))


def _lane_partial_sum(p):
    out = p[:, :LANES]
    for c in range(1, p.shape[1] // LANES):
        out = out + p[:, c * LANES:(c + 1) * LANES]
    return out


def _attn_a_prompt_kernel(par_ref, q_ref, k_ref, v_ref, g_ref, bias_ref, lamv_ref, gsub_ref, o_ref, l_ref, acc_ref,
                          *, tq, lam_init):
    i = pl.program_id(2)
    n_far = jnp.maximum(i - 1, 0)
    near = pl.multiple_of(n_far * tq, tq)
    diag = pl.multiple_of(i * tq, tq)
    half = tq // 2
    bounded = par_ref[0] > 0.5
    n_heads = q_ref.shape[2] // LANES
    hs = [slice(hh * LANES, (hh + 1) * LANES) for hh in range(n_heads)]

    def finish_head(hh, c0, c1):
        o = _diff_finish(c0, c1, lamv_ref, gsub_ref, g_ref[0, :, hs[hh]], lam_init)
        o_ref[0, :, hs[hh]] = o.astype(o_ref.dtype)

    def run_online():
        for hh in range(n_heads):
            q0, q1 = _split_halves(q_ref[0, :, hs[hh]])

            def tile(carry, start, bias):
                c0, c1 = carry
                k = k_ref[0, pl.ds(start, tq), hs[hh]]
                v = v_ref[0, pl.ds(start, tq), hs[hh]]
                s0, s1 = _nt(q0, k), _nt(q1, k)
                if bias is not None:
                    s0, s1 = s0 + bias_ref[hh, 0, :, bias], s1 + bias_ref[hh, 0, :, bias]
                return _online_update(c0, s0, v), _online_update(c1, s1, v)

            init = _diff_init(tq)
            carry = lax.fori_loop(0, n_far, lambda j, c: tile(c, pl.multiple_of(j * tq, tq), None), (init, init))
            c0, c1 = tile(tile(carry, near, slice(0, tq)), diag, slice(tq, 2 * tq))
            finish_head(hh, c0, c1)

    def run_bounded():
        qs = [_split_halves(q_ref[0, :, sl]) for sl in hs]
        l_ref[...] = jnp.zeros(l_ref.shape, F32)
        acc_ref[...] = jnp.zeros(acc_ref.shape, F32)

        def accumulate(hh, start, nk, rows=slice(None), bias_cols=None):
            k = k_ref[0, pl.ds(start, nk), hs[hh]]
            v = v_ref[0, pl.ds(start, nk), hs[hh]]
            for mp in range(2):
                s = _nt(qs[hh][mp][rows], k)
                if bias_cols is not None:
                    s = s + bias_ref[hh, 0, rows, bias_cols]
                p = jnp.exp2(s)
                l_ref[hh, mp, rows, :] += _lane_partial_sum(p)
                acc_ref[hh, mp, rows, :] += jnp.dot(p.astype(BF16), v, preferred_element_type=F32)

        def far(j, carry):
            for hh in range(n_heads):
                accumulate(hh, pl.multiple_of(j * tq, tq), tq)
            return carry

        lax.fori_loop(0, n_far, far, 0)
        for hh in range(n_heads):
            accumulate(hh, near, tq, bias_cols=slice(0, tq))
            accumulate(hh, diag, half, slice(0, half), slice(tq, tq + half))
            accumulate(hh, diag, tq, slice(half, tq), slice(tq, 2 * tq))
            c0 = (None, jnp.sum(l_ref[hh, 0], axis=-1, keepdims=True), acc_ref[hh, 0])
            c1 = (None, jnp.sum(l_ref[hh, 1], axis=-1, keepdims=True), acc_ref[hh, 1])
            finish_head(hh, c0, c1)

    pl.when(bounded)(run_bounded)
    pl.when(jnp.logical_not(bounded))(run_online)


def _logits_bounded(g_q, g_k, head_dim, table, offset):
    qk = math.sqrt(head_dim) * LOG2E * jnp.max(jnp.abs(g_q)) * jnp.max(jnp.abs(g_k)) * BF16_ROUND_MARGIN
    bias = jnp.max(jnp.abs(table - offset[None, :])) * LOG2E
    return (qk + bias <= MAX_DIRECT_LOGIT).astype(F32).reshape(1)


def _attn_a_prompt(qa, kab, vab, ga, bias, bounded, lamv, gsub, tq, lam_init):
    b, s, _ = qa.shape
    n_var = bias.shape[1]
    assert bias.shape[3] == n_var * tq
    hps = HEADS_PER_STEP_A
    tile = pl.BlockSpec((1, tq, hps * LANES), lambda bi, h, i: (bi, i, h))
    single = pl.Buffered(1)
    full = pl.BlockSpec((1, s, hps * LANES), lambda bi, h, i: (bi, 0, h), pipeline_mode=single)
    return pl.pallas_call(
        functools.partial(_attn_a_prompt_kernel, tq=tq, lam_init=lam_init),
        grid=(b, H_A // hps, s // tq),
        in_specs=[pl.BlockSpec(memory_space=pltpu.SMEM), tile, full, full, tile,
                  pl.BlockSpec((hps, 1, tq, n_var * tq), lambda bi, h, i: (h, jnp.minimum(i, n_var - 1), 0, 0),
                               pipeline_mode=single),
                  pl.BlockSpec((4, HD_A), lambda bi, h, i: (0, 0)),
                  pl.BlockSpec((1, LANES), lambda bi, h, i: (0, 0))],
        out_specs=tile,
        out_shape=jax.ShapeDtypeStruct((b, s, W_A), BF16),
        scratch_shapes=[pltpu.VMEM((hps, 2, tq, LANES), F32), pltpu.VMEM((hps, 2, tq, LANES), F32)],
        compiler_params=pltpu.CompilerParams(dimension_semantics=("arbitrary", "arbitrary", "arbitrary"),
                                             vmem_limit_bytes=VMEM_LIMIT),
        name="diff_attn_prompt",
    )(bounded, qa, kab, vab, ga, bias, lamv, gsub)


def _attn_a_sample_kernel(q_ref, kc_ref, vc_ref, kn_ref, vn_ref, g_ref, bias_ref, lamv_ref, gsub_ref, o_ref, *, lam_init):
    tq = q_ref.shape[1]
    nk = bias_ref.shape[3]
    past = kc_ref.shape[1] // H_A
    pad = jnp.zeros((nk - past - tq, LANES), BF16)
    for hh in range(H_A):
        sl = slice(hh * LANES, (hh + 1) * LANES)
        rows = pl.ds(hh, past, stride=H_A)
        qst = jnp.concatenate(_split_halves(q_ref[0, :, sl]), axis=0)
        k = jnp.concatenate([kc_ref[0, rows, :].astype(BF16), kn_ref[0, :, sl], pad], axis=0)
        v = jnp.concatenate([vc_ref[0, rows, :].astype(BF16), vn_ref[0, :, sl], pad], axis=0)
        bias = bias_ref[hh, 0]
        c = _online_update(_diff_init(2 * tq), _nt(qst, k) + jnp.concatenate([bias, bias], axis=0), v)
        c0, c1 = tuple(x[:tq] for x in c), tuple(x[tq:] for x in c)
        o_ref[0, :, sl] = _diff_finish(c0, c1, lamv_ref, gsub_ref, g_ref[0, :, sl], lam_init).astype(o_ref.dtype)


def _attn_a_sample(qa, kab, vab, ga, cache_k, cache_v, bias, lamv, gsub, lam_init):
    b, t, _ = qa.shape
    rows = cache_k.shape[1]
    nk = bias.shape[3]
    new = pl.BlockSpec((1, t, W_A), lambda bi: (bi, 0, 0))
    old = pl.BlockSpec((1, rows, LANES), lambda bi: (bi, 0, 0))
    return pl.pallas_call(
        functools.partial(_attn_a_sample_kernel, lam_init=lam_init),
        grid=(b,),
        in_specs=[new, old, old, new, new, new,
                  pl.BlockSpec((H_A, 1, t, nk), lambda bi: (0, 0, 0, 0)),
                  pl.BlockSpec((4, HD_A), lambda bi: (0, 0)),
                  pl.BlockSpec((1, LANES), lambda bi: (0, 0))],
        out_specs=new,
        out_shape=jax.ShapeDtypeStruct((b, t, W_A), BF16),
        name="diff_attn_sample",
    )(qa, cache_k, cache_v, kab, vab, ga, bias, lamv, gsub)


def _band_pair(q, k, v, g, bias_ref, bounded=False, first_head=0):
    t = q.shape[0]
    qst = jnp.concatenate(_split_halves(q), axis=0)
    s = _nt(qst, k) + jnp.concatenate([bias_ref[first_head, 0], bias_ref[first_head + 1, 0]], axis=0)
    if not bounded:
        s = s - jnp.max(s, axis=-1, keepdims=True)
    p = jnp.exp2(s)
    l = jnp.sum(p, axis=-1, keepdims=True)
    o2 = jnp.dot(p.astype(BF16), v, preferred_element_type=F32) / l
    lane = lax.broadcasted_iota(jnp.int32, (t, LANES), 1)
    o = jnp.where(lane < LANES // 2, o2[:t], o2[t:])
    return o * _silu(g.astype(F32))


def _attn_b_prompt_kernel(par_ref, q_ref, k_ref, v_ref, g_ref, bias_ref, o_ref, *, tq, n_var):
    i = pl.program_id(2)
    ws = pl.multiple_of(jnp.maximum(i - (n_var - 1), 0) * tq, tq)
    bounded = par_ref[0] > 0.5

    def run(direct):
        for pp in range(q_ref.shape[2] // LANES):
            sl = slice(pp * LANES, (pp + 1) * LANES)
            k = k_ref[0, pl.ds(ws, n_var * tq), sl]
            v = v_ref[0, pl.ds(ws, n_var * tq), sl]
            o = _band_pair(q_ref[0, :, sl], k, v, g_ref[0, :, sl], bias_ref, direct, 2 * pp)
            o_ref[0, :, sl] = o.astype(o_ref.dtype)

    pl.when(bounded)(lambda: run(True))
    pl.when(jnp.logical_not(bounded))(lambda: run(False))


def _attn_b_prompt(qb, kbb, vbb, gb, bias, bounded, tq):
    b, s, w = qb.shape
    n_var = bias.shape[1]
    assert bias.shape[3] == n_var * tq
    tile = pl.BlockSpec((1, tq, w), lambda bi, p, i: (bi, i, p))
    full = pl.BlockSpec((1, s, w), lambda bi, p, i: (bi, 0, p))
    return pl.pallas_call(
        functools.partial(_attn_b_prompt_kernel, tq=tq, n_var=n_var),
        grid=(b, 1, s // tq),
        in_specs=[pl.BlockSpec(memory_space=pltpu.SMEM), tile, full, full, tile,
                  pl.BlockSpec((H_B, 1, tq, n_var * tq), lambda bi, p, i: (p, jnp.minimum(i, n_var - 1), 0, 0))],
        out_specs=tile,
        out_shape=jax.ShapeDtypeStruct((b, s, W_B), BF16),
        compiler_params=pltpu.CompilerParams(dimension_semantics=("arbitrary", "arbitrary", "arbitrary"),
                                             vmem_limit_bytes=VMEM_LIMIT),
        name="band_attn_prompt",
    )(bounded, qb, kbb, vbb, gb, bias)


def _attn_b_sample_kernel(q_ref, kc_ref, vc_ref, kn_ref, vn_ref, g_ref, bias_ref, o_ref):
    nk = bias_ref.shape[3]
    past, t = kc_ref.shape[2], kn_ref.shape[1]
    pad = jnp.zeros((nk - past - t, LANES), BF16)
    lane = lax.broadcasted_iota(jnp.int32, (t, LANES), 1)
    for pp in range(q_ref.shape[2] // LANES):
        sl = slice(pp * LANES, (pp + 1) * LANES)
        qst = jnp.concatenate(_split_halves(q_ref[0, :, sl]), axis=0)
        k_new = jnp.concatenate([kn_ref[0, :, sl], pad], axis=0)
        v_new = jnp.concatenate([vn_ref[0, :, sl], pad], axis=0)
        s = jnp.concatenate([jnp.dot(qst, kc_ref[0, sl, :].astype(BF16), preferred_element_type=F32),
                             _nt(qst, k_new)], axis=1)
        s = s + jnp.concatenate([bias_ref[2 * pp, 0], bias_ref[2 * pp + 1, 0]], axis=0)
        p = jnp.exp2(s - jnp.max(s, axis=-1, keepdims=True))
        l = jnp.sum(p, axis=-1, keepdims=True)
        pb = p.astype(BF16)
        o2 = (_nt(pb[:, :past], vc_ref[0, sl, :].astype(BF16))
              + jnp.dot(pb[:, past:], v_new, preferred_element_type=F32)) / l
        o = jnp.where(lane < LANES // 2, o2[:t], o2[t:])
        o_ref[0, :, sl] = (o * _silu(g_ref[0, :, sl].astype(F32))).astype(o_ref.dtype)


def _attn_b_sample(qb, kbb, vbb, gb, cache_kt, cache_vt, bias):
    b, t, w = qb.shape
    past = cache_kt.shape[2]
    nk = bias.shape[3]
    assert past % LANES == 0
    new = pl.BlockSpec((1, t, w), lambda bi: (bi, 0, 0))
    old = pl.BlockSpec((1, w, past), lambda bi: (bi, 0, 0))
    return pl.pallas_call(
        _attn_b_sample_kernel,
        grid=(b,),
        in_specs=[new, old, old, new, new, new,
                  pl.BlockSpec((H_B, 1, t, nk), lambda bi: (0, 0, 0, 0))],
        out_specs=new,
        out_shape=jax.ShapeDtypeStruct((b, t, W_B), BF16),
        name="band_attn_sample",
    )(qb, cache_kt, cache_vt, kbb, vbb, gb, bias)


def _merge_kernel(x_ref, gate_ref, oa_ref, ob_ref, mg_ref, woa_ref, wob_ref, wout_ref, o_ref):
    nb, t, d = x_ref.shape
    rows = nb * t
    ya = jnp.dot(oa_ref[...].reshape(rows, W_A), woa_ref[...], preferred_element_type=F32)
    yb = jnp.dot(ob_ref[...].reshape(rows, W_B), wob_ref[...], preferred_element_type=F32)
    mg = mg_ref[...].reshape(rows, 2 * d).astype(F32)
    m = _sigmoid(mg[:, :d]) * ya + _sigmoid(mg[:, d:]) * yb
    y = jnp.dot(m.astype(BF16), wout_ref[...], preferred_element_type=F32)
    o_ref[...] = x_ref[...] + gate_ref[...] * y.reshape(nb, t, d)


def _merge(x, gate, oa, ob, mg, w_oa_bf, w_ob_bf, w_out_bf, nb, t):
    bx, sx, d = x.shape
    tok = lambda w: pl.BlockSpec((nb, t, w), lambda b, i: (b, i, 0))
    const = lambda shp: pl.BlockSpec(shp, lambda b, i: (0,) * len(shp))
    return pl.pallas_call(
        _merge_kernel,
        grid=(bx // nb, sx // t),
        in_specs=[tok(d), pl.BlockSpec((nb, 1, d), lambda b, i: (b, 0, 0)), tok(W_A), tok(W_B), tok(2 * d),
                  const((W_A, d)), const((W_B, d)), const((d, d))],
        out_specs=tok(d),
        out_shape=jax.ShapeDtypeStruct((bx, sx, d), F32),
        compiler_params=pltpu.CompilerParams(dimension_semantics=("arbitrary", "arbitrary"),
                                             vmem_limit_bytes=VMEM_LIMIT),
        name="merge_out",
    )(x, gate, oa, ob, mg, w_oa_bf, w_ob_bf, w_out_bf)


TQ_A = 512
HEADS_PER_STEP_A = 4
TQ_B = 256
TM_PROJ = 512
QBASE = 2048


def kernel(x_prompt, x_sample, cache_a_k, cache_a_v, cache_b_k, cache_b_v, c_prompt, c_sample, g_norm, w_ada, b_ada, w_in, g_qa, g_ka, lam_q1, lam_k1, lam_q2, lam_k2, g_subln, t5_bias, g_qb, g_kb, rel_bias_b, w_oa, w_ob, w_out):
    xp, xs = x_prompt, x_sample
    bp, s, d = xp.shape
    bs, t, _ = xs.shape
    depth = w_in.shape[0]
    past = cache_a_k.shape[2]
    lb = cache_b_k.shape[2]
    n_keep = min(BAND_PAST, s)
    assert s % TM_PROJ == 0 and s % TQ_A == 0 and s % TQ_B == 0 and (TQ_A // 2) % CHUNK == 0 and TQ_B % CHUNK == 0
    assert 2 * TQ_B >= BAND_PAST and past >= lb
    assert _far_bucket(TQ_A + 1) == T5_BUCKETS // 2 - 1

    r = jnp.arange(NORM_GROUP)
    pm = ((r[:, None] // HD_A) == (r[None, :] // HD_A)).astype(BF16) * (1.0 / HD_A)
    c_rows = bp + bs
    c_pad = -(-c_rows // SUBLANES) * SUBLANES
    c_all = jnp.concatenate([c_prompt, c_sample, jnp.zeros((c_pad - c_rows, d), F32)], axis=0)

    nk_as = -(-(past + t) // LANES) * LANES
    nk_bs = -(-(lb + t) // LANES) * LANES
    t5_fn = functools.partial(_t5_run, t5_bias)
    outs = [[] for _ in range(8)]
    for l in range(depth):
        lam_init = 0.8 - 0.6 * math.exp(-0.3 * l)
        rel_fn = functools.partial(_clipped_run, rel_bias_b[l])
        cfar = t5_bias[_t5_bucket(jnp.int32(-(TQ_A + 1)))]
        bias_ap = _bias_tiles(t5_fn, cfar, H_A, TQ_A, 2 * TQ_A, 2, QBASE, QBASE - TQ_A, 0, False, 2 * TQ_A, lead_step=TQ_A)
        bias_bp = _bias_tiles(rel_fn, jnp.zeros((H_B,)), H_B, TQ_B, 3 * TQ_B, 3, QBASE, QBASE, TQ_B, True, 3 * TQ_B)
        bias_as = _bias_tiles(t5_fn, jnp.zeros((H_A,)), H_A, t, nk_as, 1, past, 0, 0, False, past + t)
        bias_bs = _bias_tiles(rel_fn, jnp.zeros((H_B,)), H_B, t, nk_bs, 1, past, past - lb, 0, True, lb + t)

        mod = _modulation(c_all, w_ada[l], b_ada[l])
        shift = mod[:, :d].reshape(c_pad, 1, d)
        scale = mod[:, d:2 * d].reshape(c_pad, 1, d)
        gate = mod[:, 2 * d:].reshape(c_pad, 1, d)
        w_in_bf = w_in[l].astype(BF16)
        w_oa_bf, w_ob_bf, w_out_bf = w_oa[l].astype(BF16), w_ob[l].astype(BF16), w_out[l].astype(BF16)
        tile8 = lambda g: jnp.tile(g, SEG // g.shape[0]).reshape(1, SEG)
        gains = (tile8(g_qa[l]), tile8(g_ka[l]), tile8(g_qb[l]), tile8(g_kb[l]))
        lamv = jnp.stack([lam_q1[l], lam_k1[l], lam_q2[l], lam_k2[l]])
        gsub = g_subln[l].reshape(1, LANES)

        (qa, ka32, kab, va32, vab, ga, qb, kb32, kbb, vb32, vbb, gb, mg) = _project(
            xp, shift[:bp], scale[:bp], g_norm[l], w_in_bf, *gains, pm, 1, TM_PROJ)
        bounded_a = _logits_bounded(g_qa[l], g_ka[l], HD_A, t5_bias, cfar)
        bounded_b = _logits_bounded(g_qb[l], g_kb[l], HD_B, rel_bias_b[l], jnp.zeros((H_B,)))
        oa = _attn_a_prompt(qa, kab, vab, ga, bias_ap, bounded_a, lamv, gsub, TQ_A, lam_init)
        ob = _attn_b_prompt(qb, kbb, vbb, gb, bias_bp, bounded_b, TQ_B)
        xp = _merge(xp, gate[:bp], oa, ob, mg, w_oa_bf, w_ob_bf, w_out_bf, 1, TM_PROJ)
        outs[0].append(ka32.reshape(bp, s, H_A, 2 * HD_A))
        outs[1].append(va32.reshape(bp, s, H_A, 2 * HD_A))
        outs[2].append(kb32[:, s - n_keep:].reshape(bp, n_keep, H_B, HD_B))
        outs[3].append(vb32[:, s - n_keep:].reshape(bp, n_keep, H_B, HD_B))

        (qa, ka32, kab, va32, vab, ga, qb, kb32, kbb, vb32, vbb, gb, mg) = _project(
            xs, shift[bp:c_rows], scale[bp:c_rows], g_norm[l], w_in_bf, *gains, pm, bs, t)
        oa = _attn_a_sample(qa, kab, vab, ga, cache_a_k[l].reshape(bs, past * H_A, LANES), cache_a_v[l].reshape(bs, past * H_A, LANES),
                            bias_as, lamv, gsub, lam_init)
        to_rows = lambda c: jnp.transpose(c, (0, 2, 3, 1)).reshape(bs, W_B, lb)
        ob = _attn_b_sample(qb, kbb, vbb, gb, to_rows(cache_b_k[l]), to_rows(cache_b_v[l]), bias_bs)
        xs = _merge(xs, gate[bp:c_rows], oa, ob, mg, w_oa_bf, w_ob_bf, w_out_bf, bs, t)
        outs[4].append(ka32.reshape(bs, t, H_A, 2 * HD_A))
        outs[5].append(va32.reshape(bs, t, H_A, 2 * HD_A))
        outs[6].append(kb32.reshape(bs, t, H_B, HD_B))
        outs[7].append(vb32.reshape(bs, t, H_B, HD_B))

    return (xp, xs) + tuple(jnp.stack(o) for o in outs)
```

```python
import functools
import math

import jax
import jax.numpy as jnp
from jax import lax
from jax.experimental import pallas as pl
from jax.experimental.pallas import tpu as pltpu

CHUNK = 64
H_A = 4
HD_A = 64
W_A = H_A * 2 * HD_A
H_B = 8
HD_B = 64
W_B = H_B * HD_B
BAND_CHUNKS = 8
BAND_PAST = BAND_CHUNKS * CHUNK
REL_CLIP_B = 128
T5_BUCKETS = 32
T5_MAX_EXACT = 8
T5_MAX_DIST = 128
EPS = 1e-6
NEG = -1e30

LANES = 128
SUBLANES = 8
SEG = 512
NORM_GROUP = 256
MAX_DIRECT_LOGIT = 60.0
BF16_ROUND_MARGIN = 1.02
VMEM_LIMIT = 56 * 1024 * 1024

LOG2E = math.log2(math.e)

F32 = jnp.float32
BF16 = jnp.bfloat16


def _t5_bucket(rel):
    half = T5_BUCKETS // 2
    assert (T5_MAX_DIST // T5_MAX_EXACT) ** 2 == 2 ** (half - T5_MAX_EXACT)
    ret = jnp.where(rel > 0, half, 0)
    n = jnp.abs(rel)
    large = T5_MAX_EXACT + sum((n * n >= T5_MAX_EXACT ** 2 * 2 ** j).astype(jnp.int32)
                               for j in range(1, half - T5_MAX_EXACT))
    large = jnp.minimum(large, half - 1)
    return ret + jnp.where(n < T5_MAX_EXACT, n, large)


def _t5_run(t5_bias, first_rel, count):
    assert max(abs(first_rel), abs(first_rel + count)) < 2 ** 15
    rel = first_rel + jnp.arange(count, dtype=jnp.int32)
    hit = _t5_bucket(rel)[:, None, None] == jnp.arange(T5_BUCKETS, dtype=jnp.int32)[None, :, None]
    return jnp.sum(jnp.where(hit, t5_bias[None], 0.0), axis=1)


def _clipped_run(table, first_rel, count):
    n = table.shape[0]
    first = first_rel + (n - 1) // 2
    n_lo = min(max(-first, 0), count)
    n_hi = min(max(first + count - n, 0), count)
    mid = count - n_lo - n_hi
    parts = [jnp.broadcast_to(table[:1], (n_lo, table.shape[1])),
             table[first + n_lo:first + n_lo + mid] if mid > 0 else table[:0],
             jnp.broadcast_to(table[n - 1:], (n_hi, table.shape[1]))]
    return jnp.concatenate(parts, axis=0)


def _far_bucket(n):
    half = T5_BUCKETS // 2
    return min(T5_MAX_EXACT + int(math.log(n / T5_MAX_EXACT) / math.log(T5_MAX_DIST / T5_MAX_EXACT) * (half - T5_MAX_EXACT)), half - 1)


def _nt(a, b):
    return lax.dot_general(a, b, (((1,), (1,)), ((), ())), preferred_element_type=F32)


def _silu(x):
    return x * (1.0 / (1.0 + jnp.exp(-x)))


def _sigmoid(x):
    return 1.0 / (1.0 + jnp.exp(-x))


def _mod_kernel(c_ref, w_ref, b_ref, o_ref):
    c = c_ref[...]
    o_ref[...] = jnp.dot(_silu(c), w_ref[...], preferred_element_type=F32,
                         precision=lax.Precision.HIGHEST) + b_ref[...]


def _modulation(c_all, w_ada, b_ada):
    rows, d = c_all.shape
    n_out = w_ada.shape[1]
    tn = d
    return pl.pallas_call(
        _mod_kernel,
        grid=(n_out // tn,),
        in_specs=[pl.BlockSpec((rows, d), lambda j: (0, 0)),
                  pl.BlockSpec((d, tn), lambda j: (0, j)),
                  pl.BlockSpec((1, tn), lambda j: (0, j))],
        out_specs=pl.BlockSpec((rows, tn), lambda j: (0, j)),
        out_shape=jax.ShapeDtypeStruct((rows, n_out), F32),
        name="adaln_mod",
    )(c_all, w_ada, b_ada.reshape(1, n_out))


def _proj_kernel(x_ref, shift_ref, scale_ref, gn_ref, w_ref, gqa_ref, gka_ref, gqb_ref, gkb_ref, pm_ref,
                 qa_ref, ka32_ref, kab_ref, va32_ref, vab_ref, ga_ref,
                 qb_ref, kb32_ref, kbb_ref, vb32_ref, vbb_ref, gb_ref, mg_ref):
    nb, t, d = x_ref.shape
    rows = nb * t
    x = x_ref[...]
    ms = jnp.mean(x * x, axis=-1, keepdims=True)
    xn = x * lax.rsqrt(ms + EPS) * gn_ref[...]
    h = xn * (1.0 + scale_ref[...]) + shift_ref[...]
    hb = h.reshape(rows, d).astype(BF16)

    def seg(c, width=SEG):
        return jnp.dot(hb, w_ref[:, c * SEG:c * SEG + width], preferred_element_type=F32)

    def head_norm(y, g_ref):
        sq = y * y
        hi = sq.astype(BF16)
        pm = pm_ref[...]
        parts = []
        for c in range(SEG // NORM_GROUP):
            sl = slice(c * NORM_GROUP, (c + 1) * NORM_GROUP)
            gms = jnp.dot(hi[:, sl], pm, preferred_element_type=F32)
            parts.append(y[:, sl] * lax.rsqrt(gms + EPS))
        return jnp.concatenate(parts, axis=1) * g_ref[...]

    def put(ref, y):
        ref[...] = y.astype(ref.dtype).reshape(ref.shape)

    def put_by_head(ref, y):
        for hh in range(H_A):
            ref[:, pl.ds(hh, t, stride=H_A), :] = y[:, hh * LANES:(hh + 1) * LANES].reshape(nb, t, LANES)

    put(qa_ref, head_norm(seg(0), gqa_ref) * (HD_A ** -0.5 * LOG2E))
    ka = head_norm(seg(1), gka_ref)
    put_by_head(ka32_ref, ka)
    put(kab_ref, ka)
    va = seg(2)
    put_by_head(va32_ref, va)
    put(vab_ref, va)
    put(ga_ref, seg(3))
    put(qb_ref, head_norm(seg(4), gqb_ref) * (HD_B ** -0.5 * LOG2E))
    kb = head_norm(seg(5), gkb_ref)
    put(kb32_ref, kb)
    put(kbb_ref, kb)
    vb = seg(6)
    put(vb32_ref, vb)
    put(vbb_ref, vb)
    put(gb_ref, seg(7))
    for c in range(8, 12):
        mg_ref[:, :, (c - 8) * SEG:(c - 7) * SEG] = seg(c).astype(mg_ref.dtype).reshape(nb, t, SEG)


def _project(x, shift, scale, g_norm, w_in_bf, gqa, gka, gqb, gkb, pm, nb, t):
    bx, sx, d = x.shape
    n_cols = w_in_bf.shape[1]
    grid = (bx // nb, sx // t)
    tok = lambda w: pl.BlockSpec((nb, t, w), lambda b, i: (b, i, 0))
    per_b = pl.BlockSpec((nb, 1, d), lambda b, i: (b, 0, 0))
    const = lambda shp: pl.BlockSpec(shp, lambda b, i: (0,) * len(shp))
    sds = lambda w, dt: jax.ShapeDtypeStruct((bx, sx, w), dt)
    by_head = jax.ShapeDtypeStruct((bx, sx * H_A, LANES), F32)
    out_shape = (sds(SEG, BF16), by_head, sds(SEG, BF16), by_head, sds(SEG, BF16), sds(SEG, BF16),
                 sds(SEG, BF16), sds(SEG, F32), sds(SEG, BF16), sds(SEG, F32), sds(SEG, BF16), sds(SEG, BF16),
                 sds(4 * SEG, BF16))
    out_specs = tuple(pl.BlockSpec((nb, s.shape[1] // (sx // t), s.shape[2]), lambda b, i: (b, i, 0)) for s in out_shape)
    return pl.pallas_call(
        _proj_kernel,
        grid=grid,
        in_specs=[tok(d), per_b, per_b, const((1, d)),
                  pl.BlockSpec((d, n_cols), lambda b, i: (0, 0), pipeline_mode=pl.Buffered(1)),
                  const((1, SEG)), const((1, SEG)), const((1, SEG)), const((1, SEG)),
                  const((NORM_GROUP, NORM_GROUP))],
        out_specs=out_specs,
        out_shape=out_shape,
        compiler_params=pltpu.CompilerParams(dimension_semantics=("arbitrary", "arbitrary"),
                                             vmem_limit_bytes=VMEM_LIMIT),
        name="in_proj",
    )(x, shift, scale, g_norm.reshape(1, d), w_in_bf, gqa, gka, gqb, gkb, pm)


def _bias_tile_kernel(off_ref, gen_ref, o_ref, *, tq, nk, qbase, kbase0, kstep, band, nvalid):
    v = pl.program_id(1)
    g = gen_ref[0, 0]
    x = jnp.broadcast_to(g, (tq, g.shape[-1]))
    y = pltpu.roll(x, 0, 1, stride=1, stride_axis=0)
    y = (y[:, :nk] - off_ref[pl.program_id(0)]) * LOG2E
    row = lax.broadcasted_iota(jnp.int32, (tq, nk), 0)
    col = lax.broadcasted_iota(jnp.int32, (tq, nk), 1)
    qc = (qbase + row) // CHUNK
    kc = (kbase0 - v * kstep + col) // CHUNK
    y = jnp.where(kc <= qc, y, NEG)
    if band:
        y = jnp.where(kc >= qc - BAND_CHUNKS, y, NEG)
    y = jnp.where(col < nvalid, y, NEG)
    o_ref[0, 0] = y


def _bias_tiles(table_fn, offset, n_heads, tq, nk, n_var, qbase, kbase0, kstep, band, nvalid):
    length = 1 << (tq + nk - 1).bit_length()
    assert length >= tq + nk - 1 and min(kbase0 - (n_var - 1) * kstep, qbase) >= 0
    gen = jnp.stack([jnp.concatenate([table_fn(kbase0 - v * kstep - qbase, nk),
                                      table_fn(kbase0 - v * kstep - qbase + nk - length, length - nk)], axis=0)
                     for v in range(n_var)])
    gen = jnp.transpose(gen, (2, 0, 1)).astype(F32).reshape(n_heads, n_var, 1, length)
    return pl.pallas_call(
        functools.partial(_bias_tile_kernel, tq=tq, nk=nk, qbase=qbase, kbase0=kbase0, kstep=kstep,
                          band=band, nvalid=nvalid),
        grid=(n_heads, n_var),
        in_specs=[pl.BlockSpec(memory_space=pltpu.SMEM),
                  pl.BlockSpec((1, 1, 1, length), lambda h, v: (h, v, 0, 0))],
        out_specs=pl.BlockSpec((1, 1, tq, nk), lambda h, v: (h, v, 0, 0)),
        out_shape=jax.ShapeDtypeStruct((n_heads, n_var, tq, nk), F32),
        name="bias_tiles",
    )(offset.astype(F32), gen)


def _split_halves(q):
    lane = lax.broadcasted_iota(jnp.int32, q.shape, 1)
    zero = jnp.zeros_like(q)
    return jnp.where(lane < LANES // 2, q, zero), jnp.where(lane >= LANES // 2, q, zero)


def _online_update(carry, s, v):
    m, l, acc = carry
    m_new = jnp.maximum(m, jnp.max(s, axis=-1, keepdims=True))
    alpha = jnp.exp2(m - m_new)
    p = jnp.exp2(s - m_new)
    l = alpha * l + jnp.sum(p, axis=-1, keepdims=True)
    acc = alpha * acc + jnp.dot(p.astype(BF16), v, preferred_element_type=F32)
    return m_new, l, acc


def _diff_init(tq):
    return (jnp.full((tq, 1), -jnp.inf, F32), jnp.zeros((tq, 1), F32), jnp.zeros((tq, LANES), F32))


def _diff_finish(c0, c1, lamv_ref, gsub_ref, g, lam_init):
    lamv = lamv_ref[...]
    e1 = jnp.exp(jnp.sum(lamv[0:1] * lamv[1:2], axis=-1, keepdims=True))
    e2 = jnp.exp(jnp.sum(lamv[2:3] * lamv[3:4], axis=-1, keepdims=True))
    lam = e1 - e2 + lam_init
    o = c0[2] / c0[1] - lam * (c1[2] / c1[1])
    o = o * lax.rsqrt(jnp.mean(o * o, axis=-1, keepdims=True) + EPS) * gsub_ref[...]
    o = o * (1.0 - lam_init)
    return o * _silu(g.astype(F32))


def _lane_partial_sum(p):
    out = p[:, :LANES]
    for c in range(1, p.shape[1] // LANES):
        out = out + p[:, c * LANES:(c + 1) * LANES]
    return out


def _attn_a_prompt_kernel(par_ref, q_ref, k_ref, v_ref, g_ref, bias_ref, lamv_ref, gsub_ref, o_ref, l_ref, acc_ref,
                          *, tq, lam_init):
    i = pl.program_id(2)
    has_near = i > 0
    n_far = jnp.maximum(i - 1, 0)
    near = pl.multiple_of(n_far * tq, tq)
    diag = pl.multiple_of(i * tq, tq)
    half = tq // 2
    bounded = par_ref[0] > 0.5
    n_heads = q_ref.shape[2] // LANES
    hs = [slice(hh * LANES, (hh + 1) * LANES) for hh in range(n_heads)]

    def finish_head(hh, c0, c1):
        o = _diff_finish(c0, c1, lamv_ref, gsub_ref, g_ref[0, :, hs[hh]], lam_init)
        o_ref[0, :, hs[hh]] = o.astype(o_ref.dtype)

    def run_online():
        for hh in range(n_heads):
            q0, q1 = _split_halves(q_ref[0, :, hs[hh]])

            def tile(carry, start, bias, mask=None):
                c0, c1 = carry
                k = k_ref[0, pl.ds(start, tq), hs[hh]]
                v = v_ref[0, pl.ds(start, tq), hs[hh]]
                s0, s1 = _nt(q0, k), _nt(q1, k)
                if bias is not None:
                    b = bias_ref[hh, 0, :, bias] if mask is None else bias_ref[hh, 0, :, bias] + mask
                    s0, s1 = s0 + b, s1 + b
                return _online_update(c0, s0, v), _online_update(c1, s1, v)

            init = _diff_init(tq)
            carry = lax.fori_loop(0, n_far, lambda j, c: tile(c, pl.multiple_of(j * tq, tq), None), (init, init))
            carry = tile(carry, near, slice(0, tq), jnp.where(has_near, 0.0, NEG))
            c0, c1 = tile(carry, diag, slice(tq, 2 * tq))
            finish_head(hh, c0, c1)

    def run_bounded():
        qs = [_split_halves(q_ref[0, :, sl]) for sl in hs]
        l_ref[...] = jnp.zeros(l_ref.shape, F32)
        acc_ref[...] = jnp.zeros(acc_ref.shape, F32)

        def accumulate(hh, start, nk, rows=slice(None), bias_cols=None, mask=None):
            k = k_ref[0, pl.ds(start, nk), hs[hh]]
            v = v_ref[0, pl.ds(start, nk), hs[hh]]
            bias = None if bias_cols is None else bias_ref[hh, 0, rows, bias_cols]
            if mask is not None:
                bias = bias + mask
            for mp in range(2):
                s = _nt(qs[hh][mp][rows], k)
                p = jnp.exp2(s if bias is None else s + bias)
                l_ref[hh, mp, rows, :] += _lane_partial_sum(p)
                acc_ref[hh, mp, rows, :] += jnp.dot(p.astype(BF16), v, preferred_element_type=F32)

        def far(j, carry):
            for hh in range(n_heads):
                accumulate(hh, pl.multiple_of(j * tq, tq), tq)
            return carry

        lax.fori_loop(0, n_far, far, 0)
        near_mask = jnp.where(has_near, 0.0, NEG)
        for hh in range(n_heads):
            accumulate(hh, near, tq, bias_cols=slice(0, tq), mask=near_mask)
            accumulate(hh, diag, half, slice(0, half), slice(tq, tq + half))
            accumulate(hh, diag, tq, slice(half, tq), slice(tq, 2 * tq))
            c0 = (None, jnp.sum(l_ref[hh, 0], axis=-1, keepdims=True), acc_ref[hh, 0])
            c1 = (None, jnp.sum(l_ref[hh, 1], axis=-1, keepdims=True), acc_ref[hh, 1])
            finish_head(hh, c0, c1)

    pl.when(bounded)(run_bounded)
    pl.when(jnp.logical_not(bounded))(run_online)


def _logits_bounded(g_q, g_k, head_dim, table, offset):
    qk = math.sqrt(head_dim) * LOG2E * jnp.max(jnp.abs(g_q)) * jnp.max(jnp.abs(g_k)) * BF16_ROUND_MARGIN
    bias = jnp.max(jnp.abs(table - offset[None, :])) * LOG2E
    return (qk + bias <= MAX_DIRECT_LOGIT).astype(F32).reshape(1)


def _attn_a_prompt(qa, kab, vab, ga, bias, bounded, lamv, gsub, tq, lam_init):
    b, s, _ = qa.shape
    assert bias.shape[1] == 1 and bias.shape[3] == 2 * tq
    hps = HEADS_PER_STEP_A
    tile = pl.BlockSpec((1, tq, hps * LANES), lambda bi, h, i: (bi, i, h))
    single = pl.Buffered(1)
    full = pl.BlockSpec((1, s, hps * LANES), lambda bi, h, i: (bi, 0, h))
    return pl.pallas_call(
        functools.partial(_attn_a_prompt_kernel, tq=tq, lam_init=lam_init),
        grid=(b, H_A // hps, s // tq),
        in_specs=[pl.BlockSpec(memory_space=pltpu.SMEM), tile, full, full, tile,
                  pl.BlockSpec((hps, 1, tq, 2 * tq), lambda bi, h, i: (h, 0, 0, 0), pipeline_mode=single),
                  pl.BlockSpec((4, HD_A), lambda bi, h, i: (0, 0)),
                  pl.BlockSpec((1, LANES), lambda bi, h, i: (0, 0))],
        out_specs=tile,
        out_shape=jax.ShapeDtypeStruct((b, s, W_A), BF16),
        scratch_shapes=[pltpu.VMEM((hps, 2, tq, LANES), F32), pltpu.VMEM((hps, 2, tq, LANES), F32)],
        compiler_params=pltpu.CompilerParams(dimension_semantics=("arbitrary", "arbitrary", "arbitrary"),
                                             vmem_limit_bytes=VMEM_LIMIT),
        name="diff_attn_prompt",
    )(bounded, qa, kab, vab, ga, bias, lamv, gsub)


def _attn_a_sample_kernel(q_ref, kc_ref, vc_ref, kn_ref, vn_ref, g_ref, bias_ref, lamv_ref, gsub_ref, o_ref, *, lam_init):
    tq = q_ref.shape[1]
    nk = bias_ref.shape[3]
    past = kc_ref.shape[1] // H_A
    pad = jnp.zeros((nk - past - tq, LANES), BF16)
    for hh in range(H_A):
        sl = slice(hh * LANES, (hh + 1) * LANES)
        rows = pl.ds(hh, past, stride=H_A)
        qst = jnp.concatenate(_split_halves(q_ref[0, :, sl]), axis=0)
        k = jnp.concatenate([kc_ref[0, rows, :].astype(BF16), kn_ref[0, :, sl], pad], axis=0)
        v = jnp.concatenate([vc_ref[0, rows, :].astype(BF16), vn_ref[0, :, sl], pad], axis=0)
        bias = bias_ref[hh, 0]
        c = _online_update(_diff_init(2 * tq), _nt(qst, k) + jnp.concatenate([bias, bias], axis=0), v)
        c0, c1 = tuple(x[:tq] for x in c), tuple(x[tq:] for x in c)
        o_ref[0, :, sl] = _diff_finish(c0, c1, lamv_ref, gsub_ref, g_ref[0, :, sl], lam_init).astype(o_ref.dtype)


def _attn_a_sample(qa, kab, vab, ga, cache_k, cache_v, bias, lamv, gsub, lam_init):
    b, t, _ = qa.shape
    rows = cache_k.shape[1]
    nk = bias.shape[3]
    new = pl.BlockSpec((1, t, W_A), lambda bi: (bi, 0, 0))
    old = pl.BlockSpec((1, rows, LANES), lambda bi: (bi, 0, 0))
    return pl.pallas_call(
        functools.partial(_attn_a_sample_kernel, lam_init=lam_init),
        grid=(b,),
        in_specs=[new, old, old, new, new, new,
                  pl.BlockSpec((H_A, 1, t, nk), lambda bi: (0, 0, 0, 0)),
                  pl.BlockSpec((4, HD_A), lambda bi: (0, 0)),
                  pl.BlockSpec((1, LANES), lambda bi: (0, 0))],
        out_specs=new,
        out_shape=jax.ShapeDtypeStruct((b, t, W_A), BF16),
        name="diff_attn_sample",
    )(qa, cache_k, cache_v, kab, vab, ga, bias, lamv, gsub)


def _band_pair(q, k, v, g, bias_ref, bounded=False, first_head=0):
    t = q.shape[0]
    qst = jnp.concatenate(_split_halves(q), axis=0)
    s = _nt(qst, k) + jnp.concatenate([bias_ref[first_head, 0], bias_ref[first_head + 1, 0]], axis=0)
    if not bounded:
        s = s - jnp.max(s, axis=-1, keepdims=True)
    p = jnp.exp2(s)
    l = jnp.sum(p, axis=-1, keepdims=True)
    o2 = jnp.dot(p.astype(BF16), v, preferred_element_type=F32) / l
    lane = lax.broadcasted_iota(jnp.int32, (t, LANES), 1)
    o = jnp.where(lane < LANES // 2, o2[:t], o2[t:])
    return o * _silu(g.astype(F32))


def _attn_b_prompt_kernel(par_ref, q_ref, k_ref, v_ref, g_ref, bias_ref, o_ref, *, tq, n_var):
    i = pl.program_id(2)
    ws = pl.multiple_of(jnp.maximum(i - (n_var - 1), 0) * tq, tq)
    bounded = par_ref[0] > 0.5

    def run(direct):
        for pp in range(q_ref.shape[2] // LANES):
            sl = slice(pp * LANES, (pp + 1) * LANES)
            k = k_ref[0, pl.ds(ws, n_var * tq), sl]
            v = v_ref[0, pl.ds(ws, n_var * tq), sl]
            o = _band_pair(q_ref[0, :, sl], k, v, g_ref[0, :, sl], bias_ref, direct, 2 * pp)
            o_ref[0, :, sl] = o.astype(o_ref.dtype)

    pl.when(bounded)(lambda: run(True))
    pl.when(jnp.logical_not(bounded))(lambda: run(False))


def _attn_b_prompt(qb, kbb, vbb, gb, bias, bounded, tq):
    b, s, w = qb.shape
    n_var = bias.shape[1]
    assert bias.shape[3] == n_var * tq
    tile = pl.BlockSpec((1, tq, w), lambda bi, p, i: (bi, i, p))
    full = pl.BlockSpec((1, s, w), lambda bi, p, i: (bi, 0, p))
    return pl.pallas_call(
        functools.partial(_attn_b_prompt_kernel, tq=tq, n_var=n_var),
        grid=(b, 1, s // tq),
        in_specs=[pl.BlockSpec(memory_space=pltpu.SMEM), tile, full, full, tile,
                  pl.BlockSpec((H_B, 1, tq, n_var * tq), lambda bi, p, i: (p, jnp.minimum(i, n_var - 1), 0, 0))],
        out_specs=tile,
        out_shape=jax.ShapeDtypeStruct((b, s, W_B), BF16),
        compiler_params=pltpu.CompilerParams(dimension_semantics=("arbitrary", "arbitrary", "arbitrary"),
                                             vmem_limit_bytes=VMEM_LIMIT),
        name="band_attn_prompt",
    )(bounded, qb, kbb, vbb, gb, bias)


def _attn_b_sample_kernel(q_ref, kc_ref, vc_ref, kn_ref, vn_ref, g_ref, bias_ref, o_ref):
    nk = bias_ref.shape[3]
    past, t = kc_ref.shape[2], kn_ref.shape[1]
    pad = jnp.zeros((nk - past - t, LANES), BF16)
    lane = lax.broadcasted_iota(jnp.int32, (t, LANES), 1)
    for pp in range(q_ref.shape[2] // LANES):
        sl = slice(pp * LANES, (pp + 1) * LANES)
        qst = jnp.concatenate(_split_halves(q_ref[0, :, sl]), axis=0)
        k_new = jnp.concatenate([kn_ref[0, :, sl], pad], axis=0)
        v_new = jnp.concatenate([vn_ref[0, :, sl], pad], axis=0)
        s = jnp.concatenate([jnp.dot(qst, kc_ref[0, sl, :].astype(BF16), preferred_element_type=F32),
                             _nt(qst, k_new)], axis=1)
        s = s + jnp.concatenate([bias_ref[2 * pp, 0], bias_ref[2 * pp + 1, 0]], axis=0)
        p = jnp.exp2(s - jnp.max(s, axis=-1, keepdims=True))
        l = jnp.sum(p, axis=-1, keepdims=True)
        pb = p.astype(BF16)
        o2 = (_nt(pb[:, :past], vc_ref[0, sl, :].astype(BF16))
              + jnp.dot(pb[:, past:], v_new, preferred_element_type=F32)) / l
        o = jnp.where(lane < LANES // 2, o2[:t], o2[t:])
        o_ref[0, :, sl] = (o * _silu(g_ref[0, :, sl].astype(F32))).astype(o_ref.dtype)


def _attn_b_sample(qb, kbb, vbb, gb, cache_kt, cache_vt, bias):
    b, t, w = qb.shape
    past = cache_kt.shape[2]
    nk = bias.shape[3]
    assert past % LANES == 0
    new = pl.BlockSpec((1, t, w), lambda bi: (bi, 0, 0))
    old = pl.BlockSpec((1, w, past), lambda bi: (bi, 0, 0))
    return pl.pallas_call(
        _attn_b_sample_kernel,
        grid=(b,),
        in_specs=[new, old, old, new, new, new,
                  pl.BlockSpec((H_B, 1, t, nk), lambda bi: (0, 0, 0, 0))],
        out_specs=new,
        out_shape=jax.ShapeDtypeStruct((b, t, W_B), BF16),
        name="band_attn_sample",
    )(qb, cache_kt, cache_vt, kbb, vbb, gb, bias)


def _merge_kernel(x_ref, gate_ref, oa_ref, ob_ref, mg_ref, woa_ref, wob_ref, wout_ref, o_ref):
    nb, t, d = x_ref.shape
    rows = nb * t
    ya = jnp.dot(oa_ref[...].reshape(rows, W_A), woa_ref[...], preferred_element_type=F32)
    yb = jnp.dot(ob_ref[...].reshape(rows, W_B), wob_ref[...], preferred_element_type=F32)
    mg = mg_ref[...].reshape(rows, 2 * d).astype(F32)
    m = _sigmoid(mg[:, :d]) * ya + _sigmoid(mg[:, d:]) * yb
    y = jnp.dot(m.astype(BF16), wout_ref[...], preferred_element_type=F32)
    o_ref[...] = x_ref[...] + gate_ref[...] * y.reshape(nb, t, d)


def _merge(x, gate, oa, ob, mg, w_oa_bf, w_ob_bf, w_out_bf, nb, t):
    bx, sx, d = x.shape
    tok = lambda w: pl.BlockSpec((nb, t, w), lambda b, i: (b, i, 0))
    const = lambda shp: pl.BlockSpec(shp, lambda b, i: (0,) * len(shp))
    return pl.pallas_call(
        _merge_kernel,
        grid=(bx // nb, sx // t),
        in_specs=[tok(d), pl.BlockSpec((nb, 1, d), lambda b, i: (b, 0, 0)), tok(W_A), tok(W_B), tok(2 * d),
                  const((W_A, d)), const((W_B, d)), const((d, d))],
        out_specs=tok(d),
        out_shape=jax.ShapeDtypeStruct((bx, sx, d), F32),
        compiler_params=pltpu.CompilerParams(dimension_semantics=("arbitrary", "arbitrary"),
                                             vmem_limit_bytes=VMEM_LIMIT),
        name="merge_out",
    )(x, gate, oa, ob, mg, w_oa_bf, w_ob_bf, w_out_bf)


TQ_A = 512
HEADS_PER_STEP_A = 4
TQ_B = 256
TM_PROJ = 512
QBASE = 2048


def kernel(x_prompt, x_sample, cache_a_k, cache_a_v, cache_b_k, cache_b_v, c_prompt, c_sample, g_norm, w_ada, b_ada, w_in, g_qa, g_ka, lam_q1, lam_k1, lam_q2, lam_k2, g_subln, t5_bias, g_qb, g_kb, rel_bias_b, w_oa, w_ob, w_out):
    xp, xs = x_prompt, x_sample
    bp, s, d = xp.shape
    bs, t, _ = xs.shape
    depth = w_in.shape[0]
    past = cache_a_k.shape[2]
    lb = cache_b_k.shape[2]
    n_keep = min(BAND_PAST, s)
    assert s % TM_PROJ == 0 and s % TQ_A == 0 and s % TQ_B == 0 and (TQ_A // 2) % CHUNK == 0 and TQ_B % CHUNK == 0
    assert 2 * TQ_B >= BAND_PAST and past >= lb
    assert _far_bucket(TQ_A + 1) == T5_BUCKETS // 2 - 1

    r = jnp.arange(NORM_GROUP)
    pm = ((r[:, None] // HD_A) == (r[None, :] // HD_A)).astype(BF16) * (1.0 / HD_A)
    c_rows = bp + bs
    c_pad = -(-c_rows // SUBLANES) * SUBLANES
    c_all = jnp.concatenate([c_prompt, c_sample, jnp.zeros((c_pad - c_rows, d), F32)], axis=0)

    nk_as = -(-(past + t) // LANES) * LANES
    nk_bs = -(-(lb + t) // LANES) * LANES
    t5_fn = functools.partial(_t5_run, t5_bias)
    outs = [[] for _ in range(8)]
    for l in range(depth):
        lam_init = 0.8 - 0.6 * math.exp(-0.3 * l)
        rel_fn = functools.partial(_clipped_run, rel_bias_b[l])
        cfar = t5_bias[_t5_bucket(jnp.int32(-(TQ_A + 1)))]
        bias_ap = _bias_tiles(t5_fn, cfar, H_A, TQ_A, 2 * TQ_A, 1, QBASE, QBASE - TQ_A, 0, False, 2 * TQ_A)
        bias_bp = _bias_tiles(rel_fn, jnp.zeros((H_B,)), H_B, TQ_B, 3 * TQ_B, 3, QBASE, QBASE, TQ_B, True, 3 * TQ_B)
        bias_as = _bias_tiles(t5_fn, jnp.zeros((H_A,)), H_A, t, nk_as, 1, past, 0, 0, False, past + t)
        bias_bs = _bias_tiles(rel_fn, jnp.zeros((H_B,)), H_B, t, nk_bs, 1, past, past - lb, 0, True, lb + t)

        mod = _modulation(c_all, w_ada[l], b_ada[l])
        shift = mod[:, :d].reshape(c_pad, 1, d)
        scale = mod[:, d:2 * d].reshape(c_pad, 1, d)
        gate = mod[:, 2 * d:].reshape(c_pad, 1, d)
        w_in_bf = w_in[l].astype(BF16)
        w_oa_bf, w_ob_bf, w_out_bf = w_oa[l].astype(BF16), w_ob[l].astype(BF16), w_out[l].astype(BF16)
        tile8 = lambda g: jnp.tile(g, SEG // g.shape[0]).reshape(1, SEG)
        gains = (tile8(g_qa[l]), tile8(g_ka[l]), tile8(g_qb[l]), tile8(g_kb[l]))
        lamv = jnp.stack([lam_q1[l], lam_k1[l], lam_q2[l], lam_k2[l]])
        gsub = g_subln[l].reshape(1, LANES)

        (qa, ka32, kab, va32, vab, ga, qb, kb32, kbb, vb32, vbb, gb, mg) = _project(
            xp, shift[:bp], scale[:bp], g_norm[l], w_in_bf, *gains, pm, 1, TM_PROJ)
        bounded_a = _logits_bounded(g_qa[l], g_ka[l], HD_A, t5_bias, cfar)
        bounded_b = _logits_bounded(g_qb[l], g_kb[l], HD_B, rel_bias_b[l], jnp.zeros((H_B,)))
        oa = _attn_a_prompt(qa, kab, vab, ga, bias_ap, bounded_a, lamv, gsub, TQ_A, lam_init)
        ob = _attn_b_prompt(qb, kbb, vbb, gb, bias_bp, bounded_b, TQ_B)
        xp = _merge(xp, gate[:bp], oa, ob, mg, w_oa_bf, w_ob_bf, w_out_bf, 1, TM_PROJ)
        outs[0].append(ka32.reshape(bp, s, H_A, 2 * HD_A))
        outs[1].append(va32.reshape(bp, s, H_A, 2 * HD_A))
        outs[2].append(kb32[:, s - n_keep:].reshape(bp, n_keep, H_B, HD_B))
        outs[3].append(vb32[:, s - n_keep:].reshape(bp, n_keep, H_B, HD_B))

        (qa, ka32, kab, va32, vab, ga, qb, kb32, kbb, vb32, vbb, gb, mg) = _project(
            xs, shift[bp:c_rows], scale[bp:c_rows], g_norm[l], w_in_bf, *gains, pm, bs, t)
        oa = _attn_a_sample(qa, kab, vab, ga, cache_a_k[l].reshape(bs, past * H_A, LANES), cache_a_v[l].reshape(bs, past * H_A, LANES),
                            bias_as, lamv, gsub, lam_init)
        to_rows = lambda c: jnp.transpose(c, (0, 2, 3, 1)).reshape(bs, W_B, lb)
        ob = _attn_b_sample(qb, kbb, vbb, gb, to_rows(cache_b_k[l]), to_rows(cache_b_v[l]), bias_bs)
        xs = _merge(xs, gate[bp:c_rows], oa, ob, mg, w_oa_bf, w_ob_bf, w_out_bf, bs, t)
        outs[4].append(ka32.reshape(bs, t, H_A, 2 * HD_A))
        outs[5].append(va32.reshape(bs, t, H_A, 2 * HD_A))
        outs[6].append(kb32.reshape(bs, t, H_B, HD_B))
        outs[7].append(vb32.reshape(bs, t, H_B, HD_B))

    return (xp, xs) + tuple(jnp.stack(o) for o in outs)
```

```python
import functools
import math

import jax
import jax.numpy as jnp
from jax import lax
from jax.experimental import pallas as pl
from jax.experimental.pallas import tpu as pltpu

CHUNK = 64
H_A = 4
HD_A = 64
W_A = H_A * 2 * HD_A
H_B = 8
HD_B = 64
W_B = H_B * HD_B
BAND_CHUNKS = 8
BAND_PAST = BAND_CHUNKS * CHUNK
REL_CLIP_B = 128
T5_BUCKETS = 32
T5_MAX_EXACT = 8
T5_MAX_DIST = 128
EPS = 1e-6
NEG = -1e30

LANES = 128
SUBLANES = 8
SEG = 512
NORM_GROUP = 256
MAX_DIRECT_LOGIT = 60.0
BF16_ROUND_MARGIN = 1.02
VMEM_LIMIT = 56 * 1024 * 1024

LOG2E = math.log2(math.e)

F32 = jnp.float32
BF16 = jnp.bfloat16


def _t5_bucket(rel):
    half = T5_BUCKETS // 2
    assert (T5_MAX_DIST // T5_MAX_EXACT) ** 2 == 2 ** (half - T5_MAX_EXACT)
    ret = jnp.where(rel > 0, half, 0)
    n = jnp.abs(rel)
    large = T5_MAX_EXACT + sum((n * n >= T5_MAX_EXACT ** 2 * 2 ** j).astype(jnp.int32)
                               for j in range(1, half - T5_MAX_EXACT))
    large = jnp.minimum(large, half - 1)
    return ret + jnp.where(n < T5_MAX_EXACT, n, large)


def _t5_run(t5_bias, first_rel, count):
    assert max(abs(first_rel), abs(first_rel + count)) < 2 ** 15
    rel = first_rel + jnp.arange(count, dtype=jnp.int32)
    hit = _t5_bucket(rel)[:, None, None] == jnp.arange(T5_BUCKETS, dtype=jnp.int32)[None, :, None]
    return jnp.sum(jnp.where(hit, t5_bias[None], 0.0), axis=1)


def _clipped_run(table, first_rel, count):
    n = table.shape[0]
    first = first_rel + (n - 1) // 2
    n_lo = min(max(-first, 0), count)
    n_hi = min(max(first + count - n, 0), count)
    mid = count - n_lo - n_hi
    parts = [jnp.broadcast_to(table[:1], (n_lo, table.shape[1])),
             table[first + n_lo:first + n_lo + mid] if mid > 0 else table[:0],
             jnp.broadcast_to(table[n - 1:], (n_hi, table.shape[1]))]
    return jnp.concatenate(parts, axis=0)


def _far_bucket(n):
    half = T5_BUCKETS // 2
    return min(T5_MAX_EXACT + int(math.log(n / T5_MAX_EXACT) / math.log(T5_MAX_DIST / T5_MAX_EXACT) * (half - T5_MAX_EXACT)), half - 1)


def _nt(a, b):
    return lax.dot_general(a, b, (((1,), (1,)), ((), ())), preferred_element_type=F32)


def _silu(x):
    return x * (1.0 / (1.0 + jnp.exp(-x)))


def _sigmoid(x):
    return 1.0 / (1.0 + jnp.exp(-x))


def _mod_kernel(c_ref, w_ref, b_ref, o_ref):
    c = c_ref[...]
    o_ref[...] = jnp.dot(_silu(c), w_ref[...], preferred_element_type=F32,
                         precision=lax.Precision.HIGHEST) + b_ref[...]


def _modulation(c_all, w_ada, b_ada):
    rows, d = c_all.shape
    n_out = w_ada.shape[1]
    tn = d
    return pl.pallas_call(
        _mod_kernel,
        grid=(n_out // tn,),
        in_specs=[pl.BlockSpec((rows, d), lambda j: (0, 0)),
                  pl.BlockSpec((d, tn), lambda j: (0, j)),
                  pl.BlockSpec((1, tn), lambda j: (0, j))],
        out_specs=pl.BlockSpec((rows, tn), lambda j: (0, j)),
        out_shape=jax.ShapeDtypeStruct((rows, n_out), F32),
        name="adaln_mod",
    )(c_all, w_ada, b_ada.reshape(1, n_out))


def _proj_kernel(x_ref, shift_ref, scale_ref, gn_ref, w_ref, gqa_ref, gka_ref, gqb_ref, gkb_ref, pm_ref,
                 qa_ref, ka32_ref, kab_ref, va32_ref, vab_ref, ga_ref,
                 qb_ref, kb32_ref, kbb_ref, vb32_ref, vbb_ref, gb_ref, mg_ref):
    nb, t, d = x_ref.shape
    rows = nb * t
    x = x_ref[...]
    ms = jnp.mean(x * x, axis=-1, keepdims=True)
    xn = x * lax.rsqrt(ms + EPS) * gn_ref[...]
    h = xn * (1.0 + scale_ref[...]) + shift_ref[...]
    hb = h.reshape(rows, d).astype(BF16)

    def seg(c, width=SEG):
        return jnp.dot(hb, w_ref[:, c * SEG:c * SEG + width], preferred_element_type=F32)

    def head_norm(y, g_ref):
        sq = y * y
        hi = sq.astype(BF16)
        pm = pm_ref[...]
        parts = []
        for c in range(SEG // NORM_GROUP):
            sl = slice(c * NORM_GROUP, (c + 1) * NORM_GROUP)
            gms = jnp.dot(hi[:, sl], pm, preferred_element_type=F32)
            parts.append(y[:, sl] * lax.rsqrt(gms + EPS))
        return jnp.concatenate(parts, axis=1) * g_ref[...]

    def put(ref, y):
        ref[...] = y.astype(ref.dtype).reshape(ref.shape)

    def put_by_head(ref, y):
        for hh in range(H_A):
            ref[:, pl.ds(hh, t, stride=H_A), :] = y[:, hh * LANES:(hh + 1) * LANES].reshape(nb, t, LANES)

    put(qa_ref, head_norm(seg(0), gqa_ref) * (HD_A ** -0.5 * LOG2E))
    ka = head_norm(seg(1), gka_ref)
    put_by_head(ka32_ref, ka)
    put(kab_ref, ka)
    va = seg(2)
    put_by_head(va32_ref, va)
    put(vab_ref, va)
    put(ga_ref, seg(3))
    put(qb_ref, head_norm(seg(4), gqb_ref) * (HD_B ** -0.5 * LOG2E))
    kb = head_norm(seg(5), gkb_ref)
    put(kb32_ref, kb)
    put(kbb_ref, kb)
    vb = seg(6)
    put(vb32_ref, vb)
    put(vbb_ref, vb)
    put(gb_ref, seg(7))
    for c in range(8, 12):
        mg_ref[:, :, (c - 8) * SEG:(c - 7) * SEG] = seg(c).astype(mg_ref.dtype).reshape(nb, t, SEG)


def _project(x, shift, scale, g_norm, w_in_bf, gqa, gka, gqb, gkb, pm, nb, t):
    bx, sx, d = x.shape
    n_cols = w_in_bf.shape[1]
    grid = (bx // nb, sx // t)
    tok = lambda w: pl.BlockSpec((nb, t, w), lambda b, i: (b, i, 0))
    per_b = pl.BlockSpec((nb, 1, d), lambda b, i: (b, 0, 0))
    const = lambda shp: pl.BlockSpec(shp, lambda b, i: (0,) * len(shp))
    sds = lambda w, dt: jax.ShapeDtypeStruct((bx, sx, w), dt)
    by_head = jax.ShapeDtypeStruct((bx, sx * H_A, LANES), F32)
    out_shape = (sds(SEG, BF16), by_head, sds(SEG, BF16), by_head, sds(SEG, BF16), sds(SEG, BF16),
                 sds(SEG, BF16), sds(SEG, F32), sds(SEG, BF16), sds(SEG, F32), sds(SEG, BF16), sds(SEG, BF16),
                 sds(4 * SEG, BF16))
    out_specs = tuple(pl.BlockSpec((nb, s.shape[1] // (sx // t), s.shape[2]), lambda b, i: (b, i, 0)) for s in out_shape)
    return pl.pallas_call(
        _proj_kernel,
        grid=grid,
        in_specs=[tok(d), per_b, per_b, const((1, d)),
                  pl.BlockSpec((d, n_cols), lambda b, i: (0, 0), pipeline_mode=pl.Buffered(1)),
                  const((1, SEG)), const((1, SEG)), const((1, SEG)), const((1, SEG)),
                  const((NORM_GROUP, NORM_GROUP))],
        out_specs=out_specs,
        out_shape=out_shape,
        compiler_params=pltpu.CompilerParams(dimension_semantics=("arbitrary", "arbitrary"),
                                             vmem_limit_bytes=VMEM_LIMIT),
        name="in_proj",
    )(x, shift, scale, g_norm.reshape(1, d), w_in_bf, gqa, gka, gqb, gkb, pm)


def _bias_tile_kernel(off_ref, gen_ref, o_ref, *, tq, nk, qbase, kbase0, kstep, band, nvalid):
    v = pl.program_id(1)
    g = gen_ref[0, 0]
    x = jnp.broadcast_to(g, (tq, g.shape[-1]))
    y = pltpu.roll(x, 0, 1, stride=1, stride_axis=0)
    y = (y[:, :nk] - off_ref[pl.program_id(0)]) * LOG2E
    row = lax.broadcasted_iota(jnp.int32, (tq, nk), 0)
    col = lax.broadcasted_iota(jnp.int32, (tq, nk), 1)
    qc = (qbase + row) // CHUNK
    kc = (kbase0 - v * kstep + col) // CHUNK
    y = jnp.where(kc <= qc, y, NEG)
    if band:
        y = jnp.where(kc >= qc - BAND_CHUNKS, y, NEG)
    y = jnp.where(col < nvalid, y, NEG)
    o_ref[0, 0] = y


def _bias_tiles(table_fn, offset, n_heads, tq, nk, n_var, qbase, kbase0, kstep, band, nvalid):
    length = 1 << (tq + nk - 1).bit_length()
    assert length >= tq + nk - 1 and min(kbase0 - (n_var - 1) * kstep, qbase) >= 0
    gen = jnp.stack([jnp.concatenate([table_fn(kbase0 - v * kstep - qbase, nk),
                                      table_fn(kbase0 - v * kstep - qbase + nk - length, length - nk)], axis=0)
                     for v in range(n_var)])
    gen = jnp.transpose(gen, (2, 0, 1)).astype(F32).reshape(n_heads, n_var, 1, length)
    return pl.pallas_call(
        functools.partial(_bias_tile_kernel, tq=tq, nk=nk, qbase=qbase, kbase0=kbase0, kstep=kstep,
                          band=band, nvalid=nvalid),
        grid=(n_heads, n_var),
        in_specs=[pl.BlockSpec(memory_space=pltpu.SMEM),
                  pl.BlockSpec((1, 1, 1, length), lambda h, v: (h, v, 0, 0))],
        out_specs=pl.BlockSpec((1, 1, tq, nk), lambda h, v: (h, v, 0, 0)),
        out_shape=jax.ShapeDtypeStruct((n_heads, n_var, tq, nk), F32),
        name="bias_tiles",
    )(offset.astype(F32), gen)


def _split_halves(q):
    lane = lax.broadcasted_iota(jnp.int32, q.shape, 1)
    zero = jnp.zeros_like(q)
    return jnp.where(lane < LANES // 2, q, zero), jnp.where(lane >= LANES // 2, q, zero)


def _online_update(carry, s, v):
    m, l, acc = carry
    m_new = jnp.maximum(m, jnp.max(s, axis=-1, keepdims=True))
    alpha = jnp.exp2(m - m_new)
    p = jnp.exp2(s - m_new)
    l = alpha * l + jnp.sum(p, axis=-1, keepdims=True)
    acc = alpha * acc + jnp.dot(p.astype(BF16), v, preferred_element_type=F32)
    return m_new, l, acc


def _diff_init(tq):
    return (jnp.full((tq, 1), -jnp.inf, F32), jnp.zeros((tq, 1), F32), jnp.zeros((tq, LANES), F32))


def _diff_finish(c0, c1, lamv_ref, gsub_ref, g, lam_init):
    lamv = lamv_ref[...]
    e1 = jnp.exp(jnp.sum(lamv[0:1] * lamv[1:2], axis=-1, keepdims=True))
    e2 = jnp.exp(jnp.sum(lamv[2:3] * lamv[3:4], axis=-1, keepdims=True))
    lam = e1 - e2 + lam_init
    o = c0[2] / c0[1] - lam * (c1[2] / c1[1])
    o = o * lax.rsqrt(jnp.mean(o * o, axis=-1, keepdims=True) + EPS) * gsub_ref[...]
    o = o * (1.0 - lam_init)
    return o * _silu(g.astype(F32))


def _lane_partial_sum(p):
    out = p[:, :LANES]
    for c in range(1, p.shape[1] // LANES):
        out = out + p[:, c * LANES:(c + 1) * LANES]
    return out


def _attn_a_prompt_kernel(par_ref, q_ref, k_ref, v_ref, g_ref, bias_ref, lamv_ref, gsub_ref, o_ref, l_ref, acc_ref,
                          *, tq, lam_init):
    i = pl.program_id(2)
    has_near = i > 0
    n_far = jnp.maximum(i - 1, 0)
    near = pl.multiple_of(n_far * tq, tq)
    diag = pl.multiple_of(i * tq, tq)
    half = tq // 2
    bounded = par_ref[0] > 0.5
    n_heads = q_ref.shape[2] // LANES
    hs = [slice(hh * LANES, (hh + 1) * LANES) for hh in range(n_heads)]

    def finish_head(hh, c0, c1):
        o = _diff_finish(c0, c1, lamv_ref, gsub_ref, g_ref[0, :, hs[hh]], lam_init)
        o_ref[0, :, hs[hh]] = o.astype(o_ref.dtype)

    def run_online():
        for hh in range(n_heads):
            q0, q1 = _split_halves(q_ref[0, :, hs[hh]])

            def tile(carry, start, bias, mask=None):
                c0, c1 = carry
                k = k_ref[0, pl.ds(start, tq), hs[hh]]
                v = v_ref[0, pl.ds(start, tq), hs[hh]]
                s0, s1 = _nt(q0, k), _nt(q1, k)
                if bias is not None:
                    b = bias_ref[hh, 0, :, bias] if mask is None else bias_ref[hh, 0, :, bias] + mask
                    s0, s1 = s0 + b, s1 + b
                return _online_update(c0, s0, v), _online_update(c1, s1, v)

            init = _diff_init(tq)
            carry = lax.fori_loop(0, n_far, lambda j, c: tile(c, pl.multiple_of(j * tq, tq), None), (init, init))
            carry = tile(carry, near, slice(0, tq), jnp.where(has_near, 0.0, NEG))
            c0, c1 = tile(carry, diag, slice(tq, 2 * tq))
            finish_head(hh, c0, c1)

    def run_bounded():
        qs = [_split_halves(q_ref[0, :, sl]) for sl in hs]
        l_ref[...] = jnp.zeros(l_ref.shape, F32)
        acc_ref[...] = jnp.zeros(acc_ref.shape, F32)

        def accumulate(hh, start, nk, rows=slice(None), bias_cols=None, mask=None):
            k = k_ref[0, pl.ds(start, nk), hs[hh]]
            v = v_ref[0, pl.ds(start, nk), hs[hh]]
            bias = None if bias_cols is None else bias_ref[hh, 0, rows, bias_cols]
            if mask is not None:
                bias = bias + mask
            for mp in range(2):
                s = _nt(qs[hh][mp][rows], k)
                p = jnp.exp2(s if bias is None else s + bias)
                l_ref[hh, mp, rows, :] += _lane_partial_sum(p)
                acc_ref[hh, mp, rows, :] += jnp.dot(p.astype(BF16), v, preferred_element_type=F32)

        def far(j, carry):
            for hh in range(n_heads):
                accumulate(hh, pl.multiple_of(j * tq, tq), tq)
            return carry

        lax.fori_loop(0, n_far, far, 0)
        near_mask = jnp.where(has_near, 0.0, NEG)
        for hh in range(n_heads):
            accumulate(hh, near, tq, bias_cols=slice(0, tq), mask=near_mask)
            accumulate(hh, diag, half, slice(0, half), slice(tq, tq + half))
            accumulate(hh, diag, tq, slice(half, tq), slice(tq, 2 * tq))
            c0 = (None, jnp.sum(l_ref[hh, 0], axis=-1, keepdims=True), acc_ref[hh, 0])
            c1 = (None, jnp.sum(l_ref[hh, 1], axis=-1, keepdims=True), acc_ref[hh, 1])
            finish_head(hh, c0, c1)

    pl.when(bounded)(run_bounded)
    pl.when(jnp.logical_not(bounded))(run_online)


def _logits_bounded(g_q, g_k, head_dim, table, offset):
    qk = math.sqrt(head_dim) * LOG2E * jnp.max(jnp.abs(g_q)) * jnp.max(jnp.abs(g_k)) * BF16_ROUND_MARGIN
    bias = jnp.max(jnp.abs(table - offset[None, :])) * LOG2E
    return (qk + bias <= MAX_DIRECT_LOGIT).astype(F32).reshape(1)


def _attn_a_prompt(qa, kab, vab, ga, bias, bounded, lamv, gsub, tq, lam_init):
    b, s, _ = qa.shape
    assert bias.shape[1] == 1 and bias.shape[3] == 2 * tq
    hps = HEADS_PER_STEP_A
    tile = pl.BlockSpec((1, tq, hps * LANES), lambda bi, h, i: (bi, i, h))
    single = pl.Buffered(1)
    full = pl.BlockSpec((1, s, hps * LANES), lambda bi, h, i: (bi, 0, h))
    return pl.pallas_call(
        functools.partial(_attn_a_prompt_kernel, tq=tq, lam_init=lam_init),
        grid=(b, H_A // hps, s // tq),
        in_specs=[pl.BlockSpec(memory_space=pltpu.SMEM), tile, full, full, tile,
                  pl.BlockSpec((hps, 1, tq, 2 * tq), lambda bi, h, i: (h, 0, 0, 0), pipeline_mode=single),
                  pl.BlockSpec((4, HD_A), lambda bi, h, i: (0, 0)),
                  pl.BlockSpec((1, LANES), lambda bi, h, i: (0, 0))],
        out_specs=tile,
        out_shape=jax.ShapeDtypeStruct((b, s, W_A), BF16),
        scratch_shapes=[pltpu.VMEM((hps, 2, tq, LANES), F32), pltpu.VMEM((hps, 2, tq, LANES), F32)],
        compiler_params=pltpu.CompilerParams(dimension_semantics=("arbitrary", "arbitrary", "arbitrary"),
                                             vmem_limit_bytes=VMEM_LIMIT),
        name="diff_attn_prompt",
    )(bounded, qa, kab, vab, ga, bias, lamv, gsub)


def _attn_a_sample_kernel(q_ref, kc_ref, vc_ref, kn_ref, vn_ref, g_ref, bias_ref, lamv_ref, gsub_ref, o_ref, *, lam_init):
    tq = q_ref.shape[1]
    nk = bias_ref.shape[3]
    past = kc_ref.shape[1] // H_A
    pad = jnp.zeros((nk - past - tq, LANES), BF16)
    for hh in range(H_A):
        sl = slice(hh * LANES, (hh + 1) * LANES)
        rows = pl.ds(hh, past, stride=H_A)
        qst = jnp.concatenate(_split_halves(q_ref[0, :, sl]), axis=0)
        k = jnp.concatenate([kc_ref[0, rows, :].astype(BF16), kn_ref[0, :, sl], pad], axis=0)
        v = jnp.concatenate([vc_ref[0, rows, :].astype(BF16), vn_ref[0, :, sl], pad], axis=0)
        bias = bias_ref[hh, 0]
        c = _online_update(_diff_init(2 * tq), _nt(qst, k) + jnp.concatenate([bias, bias], axis=0), v)
        c0, c1 = tuple(x[:tq] for x in c), tuple(x[tq:] for x in c)
        o_ref[0, :, sl] = _diff_finish(c0, c1, lamv_ref, gsub_ref, g_ref[0, :, sl], lam_init).astype(o_ref.dtype)


def _attn_a_sample(qa, kab, vab, ga, cache_k, cache_v, bias, lamv, gsub, lam_init):
    b, t, _ = qa.shape
    rows = cache_k.shape[1]
    nk = bias.shape[3]
    new = pl.BlockSpec((1, t, W_A), lambda bi: (bi, 0, 0))
    old = pl.BlockSpec((1, rows, LANES), lambda bi: (bi, 0, 0))
    return pl.pallas_call(
        functools.partial(_attn_a_sample_kernel, lam_init=lam_init),
        grid=(b,),
        in_specs=[new, old, old, new, new, new,
                  pl.BlockSpec((H_A, 1, t, nk), lambda bi: (0, 0, 0, 0)),
                  pl.BlockSpec((4, HD_A), lambda bi: (0, 0)),
                  pl.BlockSpec((1, LANES), lambda bi: (0, 0))],
        out_specs=new,
        out_shape=jax.ShapeDtypeStruct((b, t, W_A), BF16),
        name="diff_attn_sample",
    )(qa, cache_k, cache_v, kab, vab, ga, bias, lamv, gsub)


def _band_pair(q, k, v, g, bias_ref, bounded=False, first_head=0):
    t = q.shape[0]
    qst = jnp.concatenate(_split_halves(q), axis=0)
    s = _nt(qst, k) + jnp.concatenate([bias_ref[first_head, 0], bias_ref[first_head + 1, 0]], axis=0)
    if not bounded:
        s = s - jnp.max(s, axis=-1, keepdims=True)
    p = jnp.exp2(s)
    l = jnp.sum(p, axis=-1, keepdims=True)
    o2 = jnp.dot(p.astype(BF16), v, preferred_element_type=F32) / l
    lane = lax.broadcasted_iota(jnp.int32, (t, LANES), 1)
    o = jnp.where(lane < LANES // 2, o2[:t], o2[t:])
    return o * _silu(g.astype(F32))


def _attn_b_prompt_kernel(par_ref, q_ref, k_ref, v_ref, g_ref, bias_ref, o_ref, *, tq, n_var):
    i = pl.program_id(2)
    ws = pl.multiple_of(jnp.maximum(i - (n_var - 1), 0) * tq, tq)
    bounded = par_ref[0] > 0.5

    def run(direct):
        for pp in range(q_ref.shape[2] // LANES):
            sl = slice(pp * LANES, (pp + 1) * LANES)
            k = k_ref[0, pl.ds(ws, n_var * tq), sl]
            v = v_ref[0, pl.ds(ws, n_var * tq), sl]
            o = _band_pair(q_ref[0, :, sl], k, v, g_ref[0, :, sl], bias_ref, direct, 2 * pp)
            o_ref[0, :, sl] = o.astype(o_ref.dtype)

    pl.when(bounded)(lambda: run(True))
    pl.when(jnp.logical_not(bounded))(lambda: run(False))


def _attn_b_prompt(qb, kbb, vbb, gb, bias, bounded, tq):
    b, s, w = qb.shape
    n_var = bias.shape[1]
    assert bias.shape[3] == n_var * tq
    tile = pl.BlockSpec((1, tq, w), lambda bi, p, i: (bi, i, p))
    full = pl.BlockSpec((1, s, w), lambda bi, p, i: (bi, 0, p))
    return pl.pallas_call(
        functools.partial(_attn_b_prompt_kernel, tq=tq, n_var=n_var),
        grid=(b, 1, s // tq),
        in_specs=[pl.BlockSpec(memory_space=pltpu.SMEM), tile, full, full, tile,
                  pl.BlockSpec((H_B, 1, tq, n_var * tq), lambda bi, p, i: (p, jnp.minimum(i, n_var - 1), 0, 0))],
        out_specs=tile,
        out_shape=jax.ShapeDtypeStruct((b, s, W_B), BF16),
        compiler_params=pltpu.CompilerParams(dimension_semantics=("arbitrary", "arbitrary", "arbitrary"),
                                             vmem_limit_bytes=VMEM_LIMIT),
        name="band_attn_prompt",
    )(bounded, qb, kbb, vbb, gb, bias)


def _attn_b_sample_kernel(q_ref, kc_ref, vc_ref, kn_ref, vn_ref, g_ref, bias_ref, o_ref):
    nk = bias_ref.shape[3]
    past, t = kc_ref.shape[2], kn_ref.shape[1]
    pad = jnp.zeros((nk - past - t, LANES), BF16)
    lane = lax.broadcasted_iota(jnp.int32, (t, LANES), 1)
    for pp in range(q_ref.shape[2] // LANES):
        sl = slice(pp * LANES, (pp + 1) * LANES)
        qst = jnp.concatenate(_split_halves(q_ref[0, :, sl]), axis=0)
        k_new = jnp.concatenate([kn_ref[0, :, sl], pad], axis=0)
        v_new = jnp.concatenate([vn_ref[0, :, sl], pad], axis=0)
        s = jnp.concatenate([jnp.dot(qst, kc_ref[0, sl, :].astype(BF16), preferred_element_type=F32),
                             _nt(qst, k_new)], axis=1)
        s = s + jnp.concatenate([bias_ref[2 * pp, 0], bias_ref[2 * pp + 1, 0]], axis=0)
        p = jnp.exp2(s - jnp.max(s, axis=-1, keepdims=True))
        l = jnp.sum(p, axis=-1, keepdims=True)
        pb = p.astype(BF16)
        o2 = (_nt(pb[:, :past], vc_ref[0, sl, :].astype(BF16))
              + jnp.dot(pb[:, past:], v_new, preferred_element_type=F32)) / l
        o = jnp.where(lane < LANES // 2, o2[:t], o2[t:])
        o_ref[0, :, sl] = (o * _silu(g_ref[0, :, sl].astype(F32))).astype(o_ref.dtype)


def _attn_b_sample(qb, kbb, vbb, gb, cache_kt, cache_vt, bias):
    b, t, w = qb.shape
    past = cache_kt.shape[2]
    nk = bias.shape[3]
    assert past % LANES == 0
    new = pl.BlockSpec((1, t, w), lambda bi: (bi, 0, 0))
    old = pl.BlockSpec((1, w, past), lambda bi: (bi, 0, 0))
    return pl.pallas_call(
        _attn_b_sample_kernel,
        grid=(b,),
        in_specs=[new, old, old, new, new, new,
                  pl.BlockSpec((H_B, 1, t, nk), lambda bi: (0, 0, 0, 0))],
        out_specs=new,
        out_shape=jax.ShapeDtypeStruct((b, t, W_B), BF16),
        name="band_attn_sample",
    )(qb, cache_kt, cache_vt, kbb, vbb, gb, bias)


def _merge_kernel(x_ref, gate_ref, oa_ref, ob_ref, mg_ref, woa_ref, wob_ref, wout_ref, o_ref):
    nb, t, d = x_ref.shape
    rows = nb * t
    ya = jnp.dot(oa_ref[...].reshape(rows, W_A), woa_ref[...], preferred_element_type=F32)
    yb = jnp.dot(ob_ref[...].reshape(rows, W_B), wob_ref[...], preferred_element_type=F32)
    mg = mg_ref[...].reshape(rows, 2 * d).astype(F32)
    m = _sigmoid(mg[:, :d]) * ya + _sigmoid(mg[:, d:]) * yb
    y = jnp.dot(m.astype(BF16), wout_ref[...], preferred_element_type=F32)
    o_ref[...] = x_ref[...] + gate_ref[...] * y.reshape(nb, t, d)


def _merge(x, gate, oa, ob, mg, w_oa_bf, w_ob_bf, w_out_bf, nb, t):
    bx, sx, d = x.shape
    tok = lambda w: pl.BlockSpec((nb, t, w), lambda b, i: (b, i, 0))
    const = lambda shp: pl.BlockSpec(shp, lambda b, i: (0,) * len(shp))
    return pl.pallas_call(
        _merge_kernel,
        grid=(bx // nb, sx // t),
        in_specs=[tok(d), pl.BlockSpec((nb, 1, d), lambda b, i: (b, 0, 0)), tok(W_A), tok(W_B), tok(2 * d),
                  const((W_A, d)), const((W_B, d)), const((d, d))],
        out_specs=tok(d),
        out_shape=jax.ShapeDtypeStruct((bx, sx, d), F32),
        compiler_params=pltpu.CompilerParams(dimension_semantics=("arbitrary", "arbitrary"),
                                             vmem_limit_bytes=VMEM_LIMIT),
        name="merge_out",
    )(x, gate, oa, ob, mg, w_oa_bf, w_ob_bf, w_out_bf)


TQ_A = 512
HEADS_PER_STEP_A = 4
TQ_B = 256
TM_PROJ = 512
TM_MERGE = 1024
QBASE = 2048


def kernel(x_prompt, x_sample, cache_a_k, cache_a_v, cache_b_k, cache_b_v, c_prompt, c_sample, g_norm, w_ada, b_ada, w_in, g_qa, g_ka, lam_q1, lam_k1, lam_q2, lam_k2, g_subln, t5_bias, g_qb, g_kb, rel_bias_b, w_oa, w_ob, w_out):
    xp, xs = x_prompt, x_sample
    bp, s, d = xp.shape
    bs, t, _ = xs.shape
    depth = w_in.shape[0]
    past = cache_a_k.shape[2]
    lb = cache_b_k.shape[2]
    n_keep = min(BAND_PAST, s)
    assert s % TM_PROJ == 0 and s % TM_MERGE == 0 and s % TQ_A == 0 and s % TQ_B == 0 and (TQ_A // 2) % CHUNK == 0 and TQ_B % CHUNK == 0
    assert 2 * TQ_B >= BAND_PAST and past >= lb
    assert _far_bucket(TQ_A + 1) == T5_BUCKETS // 2 - 1

    r = jnp.arange(NORM_GROUP)
    pm = ((r[:, None] // HD_A) == (r[None, :] // HD_A)).astype(BF16) * (1.0 / HD_A)
    c_rows = bp + bs
    c_pad = -(-c_rows // SUBLANES) * SUBLANES
    c_all = jnp.concatenate([c_prompt, c_sample, jnp.zeros((c_pad - c_rows, d), F32)], axis=0)

    nk_as = -(-(past + t) // LANES) * LANES
    nk_bs = -(-(lb + t) // LANES) * LANES
    t5_fn = functools.partial(_t5_run, t5_bias)
    outs = [[] for _ in range(8)]
    for l in range(depth):
        lam_init = 0.8 - 0.6 * math.exp(-0.3 * l)
        rel_fn = functools.partial(_clipped_run, rel_bias_b[l])
        cfar = t5_bias[_t5_bucket(jnp.int32(-(TQ_A + 1)))]
        bias_ap = _bias_tiles(t5_fn, cfar, H_A, TQ_A, 2 * TQ_A, 1, QBASE, QBASE - TQ_A, 0, False, 2 * TQ_A)
        bias_bp = _bias_tiles(rel_fn, jnp.zeros((H_B,)), H_B, TQ_B, 3 * TQ_B, 3, QBASE, QBASE, TQ_B, True, 3 * TQ_B)
        bias_as = _bias_tiles(t5_fn, jnp.zeros((H_A,)), H_A, t, nk_as, 1, past, 0, 0, False, past + t)
        bias_bs = _bias_tiles(rel_fn, jnp.zeros((H_B,)), H_B, t, nk_bs, 1, past, past - lb, 0, True, lb + t)

        mod = _modulation(c_all, w_ada[l], b_ada[l])
        shift = mod[:, :d].reshape(c_pad, 1, d)
        scale = mod[:, d:2 * d].reshape(c_pad, 1, d)
        gate = mod[:, 2 * d:].reshape(c_pad, 1, d)
        w_in_bf = w_in[l].astype(BF16)
        w_oa_bf, w_ob_bf, w_out_bf = w_oa[l].astype(BF16), w_ob[l].astype(BF16), w_out[l].astype(BF16)
        tile8 = lambda g: jnp.tile(g, SEG // g.shape[0]).reshape(1, SEG)
        gains = (tile8(g_qa[l]), tile8(g_ka[l]), tile8(g_qb[l]), tile8(g_kb[l]))
        lamv = jnp.stack([lam_q1[l], lam_k1[l], lam_q2[l], lam_k2[l]])
        gsub = g_subln[l].reshape(1, LANES)

        (qa, ka32, kab, va32, vab, ga, qb, kb32, kbb, vb32, vbb, gb, mg) = _project(
            xp, shift[:bp], scale[:bp], g_norm[l], w_in_bf, *gains, pm, 1, TM_PROJ)
        bounded_a = _logits_bounded(g_qa[l], g_ka[l], HD_A, t5_bias, cfar)
        bounded_b = _logits_bounded(g_qb[l], g_kb[l], HD_B, rel_bias_b[l], jnp.zeros((H_B,)))
        oa = _attn_a_prompt(qa, kab, vab, ga, bias_ap, bounded_a, lamv, gsub, TQ_A, lam_init)
        ob = _attn_b_prompt(qb, kbb, vbb, gb, bias_bp, bounded_b, TQ_B)
        xp = _merge(xp, gate[:bp], oa, ob, mg, w_oa_bf, w_ob_bf, w_out_bf, 1, TM_MERGE)
        outs[0].append(ka32.reshape(bp, s, H_A, 2 * HD_A))
        outs[1].append(va32.reshape(bp, s, H_A, 2 * HD_A))
        outs[2].append(kb32[:, s - n_keep:].reshape(bp, n_keep, H_B, HD_B))
        outs[3].append(vb32[:, s - n_keep:].reshape(bp, n_keep, H_B, HD_B))

        (qa, ka32, kab, va32, vab, ga, qb, kb32, kbb, vb32, vbb, gb, mg) = _project(
            xs, shift[bp:c_rows], scale[bp:c_rows], g_norm[l], w_in_bf, *gains, pm, bs, t)
        oa = _attn_a_sample(qa, kab, vab, ga, cache_a_k[l].reshape(bs, past * H_A, LANES), cache_a_v[l].reshape(bs, past * H_A, LANES),
                            bias_as, lamv, gsub, lam_init)
        to_rows = lambda c: jnp.transpose(c, (0, 2, 3, 1)).reshape(bs, W_B, lb)
        ob = _attn_b_sample(qb, kbb, vbb, gb, to_rows(cache_b_k[l]), to_rows(cache_b_v[l]), bias_bs)
        xs = _merge(xs, gate[bp:c_rows], oa, ob, mg, w_oa_bf, w_ob_bf, w_out_bf, bs, t)
        outs[4].append(ka32.reshape(bs, t, H_A, 2 * HD_A))
        outs[5].append(va32.reshape(bs, t, H_A, 2 * HD_A))
        outs[6].append(kb32.reshape(bs, t, H_B, HD_B))
        outs[7].append(vb32.reshape(bs, t, H_B, HD_B))

    return (xp, xs) + tuple(jnp.stack(o) for o in outs)
```

```python
import functools
import math

import jax
import jax.numpy as jnp
from jax import lax
from jax.experimental import pallas as pl
from jax.experimental.pallas import tpu as pltpu

CHUNK = 64
H_A = 4
HD_A = 64
W_A = H_A * 2 * HD_A
H_B = 8
HD_B = 64
W_B = H_B * HD_B
BAND_CHUNKS = 8
BAND_PAST = BAND_CHUNKS * CHUNK
REL_CLIP_B = 128
T5_BUCKETS = 32
T5_MAX_EXACT = 8
T5_MAX_DIST = 128
EPS = 1e-6
NEG = -1e30

LANES = 128
SUBLANES = 8
SEG = 512
NORM_GROUP = 256
MAX_DIRECT_LOGIT = 60.0
BF16_ROUND_MARGIN = 1.02
VMEM_LIMIT = 56 * 1024 * 1024

LOG2E = math.log2(math.e)

F32 = jnp.float32
BF16 = jnp.bfloat16


def _t5_bucket(rel):
    half = T5_BUCKETS // 2
    assert (T5_MAX_DIST // T5_MAX_EXACT) ** 2 == 2 ** (half - T5_MAX_EXACT)
    ret = jnp.where(rel > 0, half, 0)
    n = jnp.abs(rel)
    large = T5_MAX_EXACT + sum((n * n >= T5_MAX_EXACT ** 2 * 2 ** j).astype(jnp.int32)
                               for j in range(1, half - T5_MAX_EXACT))
    large = jnp.minimum(large, half - 1)
    return ret + jnp.where(n < T5_MAX_EXACT, n, large)


def _t5_run(t5_bias, first_rel, count):
    assert max(abs(first_rel), abs(first_rel + count)) < 2 ** 15
    rel = first_rel + jnp.arange(count, dtype=jnp.int32)
    hit = _t5_bucket(rel)[:, None, None] == jnp.arange(T5_BUCKETS, dtype=jnp.int32)[None, :, None]
    return jnp.sum(jnp.where(hit, t5_bias[None], 0.0), axis=1)


def _clipped_run(table, first_rel, count):
    n = table.shape[0]
    first = first_rel + (n - 1) // 2
    n_lo = min(max(-first, 0), count)
    n_hi = min(max(first + count - n, 0), count)
    mid = count - n_lo - n_hi
    parts = [jnp.broadcast_to(table[:1], (n_lo, table.shape[1])),
             table[first + n_lo:first + n_lo + mid] if mid > 0 else table[:0],
             jnp.broadcast_to(table[n - 1:], (n_hi, table.shape[1]))]
    return jnp.concatenate(parts, axis=0)


def _far_bucket(n):
    half = T5_BUCKETS // 2
    return min(T5_MAX_EXACT + int(math.log(n / T5_MAX_EXACT) / math.log(T5_MAX_DIST / T5_MAX_EXACT) * (half - T5_MAX_EXACT)), half - 1)


def _nt(a, b):
    return lax.dot_general(a, b, (((1,), (1,)), ((), ())), preferred_element_type=F32)


def _silu(x):
    return x * (1.0 / (1.0 + jnp.exp(-x)))


def _sigmoid(x):
    return 1.0 / (1.0 + jnp.exp(-x))


def _mod_kernel(c_ref, w_ref, b_ref, o_ref):
    c = c_ref[...]
    o_ref[...] = jnp.dot(_silu(c), w_ref[...], preferred_element_type=F32,
                         precision=lax.Precision.HIGHEST) + b_ref[...]


def _modulation(c_all, w_ada, b_ada):
    rows, d = c_all.shape
    n_out = w_ada.shape[1]
    tn = d
    return pl.pallas_call(
        _mod_kernel,
        grid=(n_out // tn,),
        in_specs=[pl.BlockSpec((rows, d), lambda j: (0, 0)),
                  pl.BlockSpec((d, tn), lambda j: (0, j)),
                  pl.BlockSpec((1, tn), lambda j: (0, j))],
        out_specs=pl.BlockSpec((rows, tn), lambda j: (0, j)),
        out_shape=jax.ShapeDtypeStruct((rows, n_out), F32),
        name="adaln_mod",
    )(c_all, w_ada, b_ada.reshape(1, n_out))


def _proj_kernel(x_ref, shift_ref, scale_ref, gn_ref, w_ref, gqa_ref, gka_ref, gqb_ref, gkb_ref, pm_ref,
                 qa_ref, ka32_ref, kab_ref, va32_ref, vab_ref, ga_ref,
                 qb_ref, kb32_ref, kbb_ref, vb32_ref, vbb_ref, gb_ref, mg_ref):
    nb, t, d = x_ref.shape
    rows = nb * t
    x = x_ref[...]
    ms = jnp.mean(x * x, axis=-1, keepdims=True)
    xn = x * lax.rsqrt(ms + EPS) * gn_ref[...]
    h = xn * (1.0 + scale_ref[...]) + shift_ref[...]
    hb = h.reshape(rows, d).astype(BF16)

    def seg(c, width=SEG):
        return jnp.dot(hb, w_ref[:, c * SEG:c * SEG + width], preferred_element_type=F32)

    def head_norm(y, g_ref):
        sq = y * y
        hi = sq.astype(BF16)
        pm = pm_ref[...]
        parts = []
        for c in range(SEG // NORM_GROUP):
            sl = slice(c * NORM_GROUP, (c + 1) * NORM_GROUP)
            gms = jnp.dot(hi[:, sl], pm, preferred_element_type=F32)
            parts.append(y[:, sl] * lax.rsqrt(gms + EPS))
        return jnp.concatenate(parts, axis=1) * g_ref[...]

    def put(ref, y):
        ref[...] = y.astype(ref.dtype).reshape(ref.shape)

    def put_by_head(ref, y):
        for hh in range(H_A):
            ref[:, pl.ds(hh, t, stride=H_A), :] = y[:, hh * LANES:(hh + 1) * LANES].reshape(nb, t, LANES)

    put(qa_ref, head_norm(seg(0), gqa_ref) * (HD_A ** -0.5 * LOG2E))
    ka = head_norm(seg(1), gka_ref)
    put_by_head(ka32_ref, ka)
    put(kab_ref, ka)
    va = seg(2)
    put_by_head(va32_ref, va)
    put(vab_ref, va)
    put(ga_ref, seg(3))
    put(qb_ref, head_norm(seg(4), gqb_ref) * (HD_B ** -0.5 * LOG2E))
    kb = head_norm(seg(5), gkb_ref)
    put(kb32_ref, kb)
    put(kbb_ref, kb)
    vb = seg(6)
    put(vb32_ref, vb)
    put(vbb_ref, vb)
    put(gb_ref, seg(7))
    for c in range(8, 12):
        mg_ref[:, :, (c - 8) * SEG:(c - 7) * SEG] = seg(c).astype(mg_ref.dtype).reshape(nb, t, SEG)


def _project(x, shift, scale, g_norm, w_in_bf, gqa, gka, gqb, gkb, pm, nb, t):
    bx, sx, d = x.shape
    n_cols = w_in_bf.shape[1]
    grid = (bx // nb, sx // t)
    tok = lambda w: pl.BlockSpec((nb, t, w), lambda b, i: (b, i, 0))
    per_b = pl.BlockSpec((nb, 1, d), lambda b, i: (b, 0, 0))
    const = lambda shp: pl.BlockSpec(shp, lambda b, i: (0,) * len(shp))
    sds = lambda w, dt: jax.ShapeDtypeStruct((bx, sx, w), dt)
    by_head = jax.ShapeDtypeStruct((bx, sx * H_A, LANES), F32)
    out_shape = (sds(SEG, BF16), by_head, sds(SEG, BF16), by_head, sds(SEG, BF16), sds(SEG, BF16),
                 sds(SEG, BF16), sds(SEG, F32), sds(SEG, BF16), sds(SEG, F32), sds(SEG, BF16), sds(SEG, BF16),
                 sds(4 * SEG, BF16))
    out_specs = tuple(pl.BlockSpec((nb, s.shape[1] // (sx // t), s.shape[2]), lambda b, i: (b, i, 0)) for s in out_shape)
    return pl.pallas_call(
        _proj_kernel,
        grid=grid,
        in_specs=[tok(d), per_b, per_b, const((1, d)),
                  pl.BlockSpec((d, n_cols), lambda b, i: (0, 0), pipeline_mode=pl.Buffered(1)),
                  const((1, SEG)), const((1, SEG)), const((1, SEG)), const((1, SEG)),
                  const((NORM_GROUP, NORM_GROUP))],
        out_specs=out_specs,
        out_shape=out_shape,
        compiler_params=pltpu.CompilerParams(dimension_semantics=("arbitrary", "arbitrary"),
                                             vmem_limit_bytes=VMEM_LIMIT),
        name="in_proj",
    )(x, shift, scale, g_norm.reshape(1, d), w_in_bf, gqa, gka, gqb, gkb, pm)


def _bias_tile_kernel(off_ref, gen_ref, o_ref, *, tq, nk, qbase, kbase0, kstep, band, nvalid):
    v = pl.program_id(1)
    g = gen_ref[0, 0]
    x = jnp.broadcast_to(g, (tq, g.shape[-1]))
    y = pltpu.roll(x, 0, 1, stride=1, stride_axis=0)
    y = (y[:, :nk] - off_ref[pl.program_id(0)]) * LOG2E
    row = lax.broadcasted_iota(jnp.int32, (tq, nk), 0)
    col = lax.broadcasted_iota(jnp.int32, (tq, nk), 1)
    qc = (qbase + row) // CHUNK
    kc = (kbase0 - v * kstep + col) // CHUNK
    y = jnp.where(kc <= qc, y, NEG)
    if band:
        y = jnp.where(kc >= qc - BAND_CHUNKS, y, NEG)
    y = jnp.where(col < nvalid, y, NEG)
    o_ref[0, 0] = y


def _bias_tiles(table_fn, offset, n_heads, tq, nk, n_var, qbase, kbase0, kstep, band, nvalid):
    length = 1 << (tq + nk - 1).bit_length()
    assert length >= tq + nk - 1 and min(kbase0 - (n_var - 1) * kstep, qbase) >= 0
    gen = jnp.stack([jnp.concatenate([table_fn(kbase0 - v * kstep - qbase, nk),
                                      table_fn(kbase0 - v * kstep - qbase + nk - length, length - nk)], axis=0)
                     for v in range(n_var)])
    gen = jnp.transpose(gen, (2, 0, 1)).astype(F32).reshape(n_heads, n_var, 1, length)
    return pl.pallas_call(
        functools.partial(_bias_tile_kernel, tq=tq, nk=nk, qbase=qbase, kbase0=kbase0, kstep=kstep,
                          band=band, nvalid=nvalid),
        grid=(n_heads, n_var),
        in_specs=[pl.BlockSpec(memory_space=pltpu.SMEM),
                  pl.BlockSpec((1, 1, 1, length), lambda h, v: (h, v, 0, 0))],
        out_specs=pl.BlockSpec((1, 1, tq, nk), lambda h, v: (h, v, 0, 0)),
        out_shape=jax.ShapeDtypeStruct((n_heads, n_var, tq, nk), F32),
        name="bias_tiles",
    )(offset.astype(F32), gen)


def _split_halves(q):
    lane = lax.broadcasted_iota(jnp.int32, q.shape, 1)
    zero = jnp.zeros_like(q)
    return jnp.where(lane < LANES // 2, q, zero), jnp.where(lane >= LANES // 2, q, zero)


def _online_update(carry, s, v):
    m, l, acc = carry
    m_new = jnp.maximum(m, jnp.max(s, axis=-1, keepdims=True))
    alpha = jnp.exp2(m - m_new)
    p = jnp.exp2(s - m_new)
    l = alpha * l + jnp.sum(p, axis=-1, keepdims=True)
    acc = alpha * acc + jnp.dot(p.astype(BF16), v, preferred_element_type=F32)
    return m_new, l, acc


def _diff_init(tq):
    return (jnp.full((tq, 1), -jnp.inf, F32), jnp.zeros((tq, 1), F32), jnp.zeros((tq, LANES), F32))


def _diff_finish(c0, c1, lamv_ref, gsub_ref, g, lam_init):
    lamv = lamv_ref[...]
    e1 = jnp.exp(jnp.sum(lamv[0:1] * lamv[1:2], axis=-1, keepdims=True))
    e2 = jnp.exp(jnp.sum(lamv[2:3] * lamv[3:4], axis=-1, keepdims=True))
    lam = e1 - e2 + lam_init
    o = c0[2] / c0[1] - lam * (c1[2] / c1[1])
    o = o * lax.rsqrt(jnp.mean(o * o, axis=-1, keepdims=True) + EPS) * gsub_ref[...]
    o = o * (1.0 - lam_init)
    return o * _silu(g.astype(F32))


def _lane_partial_sum(p):
    out = p[:, :LANES]
    for c in range(1, p.shape[1] // LANES):
        out = out + p[:, c * LANES:(c + 1) * LANES]
    return out


def _attn_a_prompt_kernel(par_ref, q_ref, k_ref, v_ref, g_ref, bias_ref, lamv_ref, gsub_ref, o_ref, l_ref, acc_ref,
                          *, tq, lam_init):
    i = pl.program_id(2)
    has_near = i > 0
    n_far = jnp.maximum(i - 1, 0)
    near = pl.multiple_of(n_far * tq, tq)
    diag = pl.multiple_of(i * tq, tq)
    half = tq // 2
    bounded = par_ref[0] > 0.5
    n_heads = q_ref.shape[2] // LANES
    hs = [slice(hh * LANES, (hh + 1) * LANES) for hh in range(n_heads)]

    def finish_head(hh, c0, c1):
        o = _diff_finish(c0, c1, lamv_ref, gsub_ref, g_ref[0, :, hs[hh]], lam_init)
        o_ref[0, :, hs[hh]] = o.astype(o_ref.dtype)

    def run_online():
        for hh in range(n_heads):
            q0, q1 = _split_halves(q_ref[0, :, hs[hh]])

            def tile(carry, start, bias, mask=None):
                c0, c1 = carry
                k = k_ref[0, pl.ds(start, tq), hs[hh]]
                v = v_ref[0, pl.ds(start, tq), hs[hh]]
                s0, s1 = _nt(q0, k), _nt(q1, k)
                if bias is not None:
                    b = bias_ref[hh, 0, :, bias] if mask is None else bias_ref[hh, 0, :, bias] + mask
                    s0, s1 = s0 + b, s1 + b
                return _online_update(c0, s0, v), _online_update(c1, s1, v)

            init = _diff_init(tq)
            carry = lax.fori_loop(0, n_far, lambda j, c: tile(c, pl.multiple_of(j * tq, tq), None), (init, init))
            carry = tile(carry, near, slice(0, tq), jnp.where(has_near, 0.0, NEG))
            c0, c1 = tile(carry, diag, slice(tq, 2 * tq))
            finish_head(hh, c0, c1)

    def run_bounded():
        qs = [_split_halves(q_ref[0, :, sl]) for sl in hs]
        l_ref[...] = jnp.zeros(l_ref.shape, F32)
        acc_ref[...] = jnp.zeros(acc_ref.shape, F32)

        def accumulate(hh, start, nk, rows=slice(None), bias_cols=None, mask=None):
            k = k_ref[0, pl.ds(start, nk), hs[hh]]
            v = v_ref[0, pl.ds(start, nk), hs[hh]]
            bias = None if bias_cols is None else bias_ref[hh, 0, rows, bias_cols]
            if mask is not None:
                bias = bias + mask
            for mp in range(2):
                s = _nt(qs[hh][mp][rows], k)
                p = jnp.exp2(s if bias is None else s + bias)
                l_ref[hh, mp, rows, :] += _lane_partial_sum(p)
                acc_ref[hh, mp, rows, :] += jnp.dot(p.astype(BF16), v, preferred_element_type=F32)

        def far(j, carry):
            for hh in range(n_heads):
                accumulate(hh, pl.multiple_of(j * tq, tq), tq)
            return carry

        lax.fori_loop(0, n_far, far, 0)
        near_mask = jnp.where(has_near, 0.0, NEG)
        for hh in range(n_heads):
            accumulate(hh, near, tq, bias_cols=slice(0, tq), mask=near_mask)
            accumulate(hh, diag, half, slice(0, half), slice(tq, tq + half))
            accumulate(hh, diag, tq, slice(half, tq), slice(tq, 2 * tq))
            c0 = (None, jnp.sum(l_ref[hh, 0], axis=-1, keepdims=True), acc_ref[hh, 0])
            c1 = (None, jnp.sum(l_ref[hh, 1], axis=-1, keepdims=True), acc_ref[hh, 1])
            finish_head(hh, c0, c1)

    pl.when(bounded)(run_bounded)
    pl.when(jnp.logical_not(bounded))(run_online)


def _logits_bounded(g_q, g_k, head_dim, table, offset):
    qk = math.sqrt(head_dim) * LOG2E * jnp.max(jnp.abs(g_q)) * jnp.max(jnp.abs(g_k)) * BF16_ROUND_MARGIN
    bias = jnp.max(jnp.abs(table - offset[None, :])) * LOG2E
    return (qk + bias <= MAX_DIRECT_LOGIT).astype(F32).reshape(1)


def _attn_a_prompt(qa, kab, vab, ga, bias, bounded, lamv, gsub, tq, lam_init):
    b, s, _ = qa.shape
    assert bias.shape[1] == 1 and bias.shape[3] == 2 * tq
    hps = HEADS_PER_STEP_A
    tile = pl.BlockSpec((1, tq, hps * LANES), lambda bi, h, i: (bi, i, h))
    single = pl.Buffered(1)
    full = pl.BlockSpec((1, s, hps * LANES), lambda bi, h, i: (bi, 0, h))
    return pl.pallas_call(
        functools.partial(_attn_a_prompt_kernel, tq=tq, lam_init=lam_init),
        grid=(b, H_A // hps, s // tq),
        in_specs=[pl.BlockSpec(memory_space=pltpu.SMEM), tile, full, full, tile,
                  pl.BlockSpec((hps, 1, tq, 2 * tq), lambda bi, h, i: (h, 0, 0, 0), pipeline_mode=single),
                  pl.BlockSpec((4, HD_A), lambda bi, h, i: (0, 0)),
                  pl.BlockSpec((1, LANES), lambda bi, h, i: (0, 0))],
        out_specs=tile,
        out_shape=jax.ShapeDtypeStruct((b, s, W_A), BF16),
        scratch_shapes=[pltpu.VMEM((hps, 2, tq, LANES), F32), pltpu.VMEM((hps, 2, tq, LANES), F32)],
        compiler_params=pltpu.CompilerParams(dimension_semantics=("arbitrary", "arbitrary", "arbitrary"),
                                             vmem_limit_bytes=VMEM_LIMIT),
        name="diff_attn_prompt",
    )(bounded, qa, kab, vab, ga, bias, lamv, gsub)


def _attn_a_sample_kernel(q_ref, kc_ref, vc_ref, kn_ref, vn_ref, g_ref, bias_ref, lamv_ref, gsub_ref, o_ref, *, lam_init):
    tq = q_ref.shape[1]
    nk = bias_ref.shape[3]
    past = kc_ref.shape[1] // H_A
    pad = jnp.zeros((nk - past - tq, LANES), BF16)
    for hh in range(H_A):
        sl = slice(hh * LANES, (hh + 1) * LANES)
        rows = pl.ds(hh, past, stride=H_A)
        qst = jnp.concatenate(_split_halves(q_ref[0, :, sl]), axis=0)
        k = jnp.concatenate([kc_ref[0, rows, :].astype(BF16), kn_ref[0, :, sl], pad], axis=0)
        v = jnp.concatenate([vc_ref[0, rows, :].astype(BF16), vn_ref[0, :, sl], pad], axis=0)
        bias = bias_ref[hh, 0]
        c = _online_update(_diff_init(2 * tq), _nt(qst, k) + jnp.concatenate([bias, bias], axis=0), v)
        c0, c1 = tuple(x[:tq] for x in c), tuple(x[tq:] for x in c)
        o_ref[0, :, sl] = _diff_finish(c0, c1, lamv_ref, gsub_ref, g_ref[0, :, sl], lam_init).astype(o_ref.dtype)


def _attn_a_sample(qa, kab, vab, ga, cache_k, cache_v, bias, lamv, gsub, lam_init):
    b, t, _ = qa.shape
    rows = cache_k.shape[1]
    nk = bias.shape[3]
    new = pl.BlockSpec((1, t, W_A), lambda bi: (bi, 0, 0))
    old = pl.BlockSpec((1, rows, LANES), lambda bi: (bi, 0, 0))
    return pl.pallas_call(
        functools.partial(_attn_a_sample_kernel, lam_init=lam_init),
        grid=(b,),
        in_specs=[new, old, old, new, new, new,
                  pl.BlockSpec((H_A, 1, t, nk), lambda bi: (0, 0, 0, 0)),
                  pl.BlockSpec((4, HD_A), lambda bi: (0, 0)),
                  pl.BlockSpec((1, LANES), lambda bi: (0, 0))],
        out_specs=new,
        out_shape=jax.ShapeDtypeStruct((b, t, W_A), BF16),
        name="diff_attn_sample",
    )(qa, cache_k, cache_v, kab, vab, ga, bias, lamv, gsub)


def _band_pair(q, k, v, g, bias_ref, bounded=False, first_head=0):
    t = q.shape[0]
    qst = jnp.concatenate(_split_halves(q), axis=0)
    s = _nt(qst, k) + jnp.concatenate([bias_ref[first_head, 0], bias_ref[first_head + 1, 0]], axis=0)
    if not bounded:
        s = s - jnp.max(s, axis=-1, keepdims=True)
    p = jnp.exp2(s)
    l = jnp.sum(p, axis=-1, keepdims=True)
    o2 = jnp.dot(p.astype(BF16), v, preferred_element_type=F32) / l
    lane = lax.broadcasted_iota(jnp.int32, (t, LANES), 1)
    o = jnp.where(lane < LANES // 2, o2[:t], o2[t:])
    return o * _silu(g.astype(F32))


def _attn_b_prompt_kernel(par_ref, q_ref, k_ref, v_ref, g_ref, *rest, tq, n_var):
    bias_refs, o_ref = rest[:-1], rest[-1]
    i = pl.program_id(2)
    bounded = par_ref[0] > 0.5

    def run(direct):
        for sub, bias_ref in enumerate(bias_refs):
            tile_idx = len(bias_refs) * i + sub
            ws = pl.multiple_of(jnp.maximum(tile_idx - (n_var - 1), 0) * tq, tq)
            rows = slice(sub * tq, (sub + 1) * tq)
            for pp in range(q_ref.shape[2] // LANES):
                sl = slice(pp * LANES, (pp + 1) * LANES)
                k = k_ref[0, pl.ds(ws, n_var * tq), sl]
                v = v_ref[0, pl.ds(ws, n_var * tq), sl]
                o = _band_pair(q_ref[0, rows, sl], k, v, g_ref[0, rows, sl], bias_ref, direct, 2 * pp)
                o_ref[0, rows, sl] = o.astype(o_ref.dtype)

    pl.when(bounded)(lambda: run(True))
    pl.when(jnp.logical_not(bounded))(lambda: run(False))


def _attn_b_prompt(qb, kbb, vbb, gb, bias, bounded, tq):
    b, s, w = qb.shape
    n_var = bias.shape[1]
    assert bias.shape[3] == n_var * tq
    tps = TILES_PER_STEP_B
    tile = pl.BlockSpec((1, tps * tq, w), lambda bi, p, i: (bi, i, p))
    full = pl.BlockSpec((1, s, w), lambda bi, p, i: (bi, 0, p), pipeline_mode=pl.Buffered(1))
    bias_spec = lambda sub: pl.BlockSpec((H_B, 1, tq, n_var * tq),
                                         lambda bi, p, i: (p, jnp.minimum(tps * i + sub, n_var - 1), 0, 0),
                                         pipeline_mode=pl.Buffered(1))
    return pl.pallas_call(
        functools.partial(_attn_b_prompt_kernel, tq=tq, n_var=n_var),
        grid=(b, 1, s // (tps * tq)),
        in_specs=[pl.BlockSpec(memory_space=pltpu.SMEM), tile, full, full, tile] + [bias_spec(sub) for sub in range(tps)],
        out_specs=tile,
        out_shape=jax.ShapeDtypeStruct((b, s, W_B), BF16),
        compiler_params=pltpu.CompilerParams(dimension_semantics=("arbitrary", "arbitrary", "arbitrary"),
                                             vmem_limit_bytes=VMEM_LIMIT),
        name="band_attn_prompt",
    )(bounded, qb, kbb, vbb, gb, *([bias] * tps))


def _attn_b_sample_kernel(q_ref, kc_ref, vc_ref, kn_ref, vn_ref, g_ref, bias_ref, o_ref):
    nk = bias_ref.shape[3]
    past, t = kc_ref.shape[2], kn_ref.shape[1]
    pad = jnp.zeros((nk - past - t, LANES), BF16)
    lane = lax.broadcasted_iota(jnp.int32, (t, LANES), 1)
    for pp in range(q_ref.shape[2] // LANES):
        sl = slice(pp * LANES, (pp + 1) * LANES)
        qst = jnp.concatenate(_split_halves(q_ref[0, :, sl]), axis=0)
        k_new = jnp.concatenate([kn_ref[0, :, sl], pad], axis=0)
        v_new = jnp.concatenate([vn_ref[0, :, sl], pad], axis=0)
        s = jnp.concatenate([jnp.dot(qst, kc_ref[0, sl, :].astype(BF16), preferred_element_type=F32),
                             _nt(qst, k_new)], axis=1)
        s = s + jnp.concatenate([bias_ref[2 * pp, 0], bias_ref[2 * pp + 1, 0]], axis=0)
        p = jnp.exp2(s - jnp.max(s, axis=-1, keepdims=True))
        l = jnp.sum(p, axis=-1, keepdims=True)
        pb = p.astype(BF16)
        o2 = (_nt(pb[:, :past], vc_ref[0, sl, :].astype(BF16))
              + jnp.dot(pb[:, past:], v_new, preferred_element_type=F32)) / l
        o = jnp.where(lane < LANES // 2, o2[:t], o2[t:])
        o_ref[0, :, sl] = (o * _silu(g_ref[0, :, sl].astype(F32))).astype(o_ref.dtype)


def _attn_b_sample(qb, kbb, vbb, gb, cache_kt, cache_vt, bias):
    b, t, w = qb.shape
    past = cache_kt.shape[2]
    nk = bias.shape[3]
    assert past % LANES == 0
    new = pl.BlockSpec((1, t, w), lambda bi: (bi, 0, 0))
    old = pl.BlockSpec((1, w, past), lambda bi: (bi, 0, 0))
    return pl.pallas_call(
        _attn_b_sample_kernel,
        grid=(b,),
        in_specs=[new, old, old, new, new, new,
                  pl.BlockSpec((H_B, 1, t, nk), lambda bi: (0, 0, 0, 0))],
        out_specs=new,
        out_shape=jax.ShapeDtypeStruct((b, t, W_B), BF16),
        name="band_attn_sample",
    )(qb, cache_kt, cache_vt, kbb, vbb, gb, bias)


def _merge_kernel(x_ref, gate_ref, oa_ref, ob_ref, mg_ref, woa_ref, wob_ref, wout_ref, o_ref):
    nb, t, d = x_ref.shape
    rows = nb * t
    ya = jnp.dot(oa_ref[...].reshape(rows, W_A), woa_ref[...], preferred_element_type=F32)
    yb = jnp.dot(ob_ref[...].reshape(rows, W_B), wob_ref[...], preferred_element_type=F32)
    mg = mg_ref[...].reshape(rows, 2 * d).astype(F32)
    m = _sigmoid(mg[:, :d]) * ya + _sigmoid(mg[:, d:]) * yb
    y = jnp.dot(m.astype(BF16), wout_ref[...], preferred_element_type=F32)
    o_ref[...] = x_ref[...] + gate_ref[...] * y.reshape(nb, t, d)


def _merge(x, gate, oa, ob, mg, w_oa_bf, w_ob_bf, w_out_bf, nb, t):
    bx, sx, d = x.shape
    tok = lambda w: pl.BlockSpec((nb, t, w), lambda b, i: (b, i, 0))
    const = lambda shp: pl.BlockSpec(shp, lambda b, i: (0,) * len(shp))
    return pl.pallas_call(
        _merge_kernel,
        grid=(bx // nb, sx // t),
        in_specs=[tok(d), pl.BlockSpec((nb, 1, d), lambda b, i: (b, 0, 0)), tok(W_A), tok(W_B), tok(2 * d),
                  const((W_A, d)), const((W_B, d)), const((d, d))],
        out_specs=tok(d),
        out_shape=jax.ShapeDtypeStruct((bx, sx, d), F32),
        compiler_params=pltpu.CompilerParams(dimension_semantics=("arbitrary", "arbitrary"),
                                             vmem_limit_bytes=VMEM_LIMIT),
        name="merge_out",
    )(x, gate, oa, ob, mg, w_oa_bf, w_ob_bf, w_out_bf)


TQ_A = 512
HEADS_PER_STEP_A = 4
TQ_B = 256
TILES_PER_STEP_B = 2
TM_PROJ = 512
TM_MERGE = 1024
QBASE = 2048


def kernel(x_prompt, x_sample, cache_a_k, cache_a_v, cache_b_k, cache_b_v, c_prompt, c_sample, g_norm, w_ada, b_ada, w_in, g_qa, g_ka, lam_q1, lam_k1, lam_q2, lam_k2, g_subln, t5_bias, g_qb, g_kb, rel_bias_b, w_oa, w_ob, w_out):
    xp, xs = x_prompt, x_sample
    bp, s, d = xp.shape
    bs, t, _ = xs.shape
    depth = w_in.shape[0]
    past = cache_a_k.shape[2]
    lb = cache_b_k.shape[2]
    n_keep = min(BAND_PAST, s)
    assert s % TM_PROJ == 0 and s % TM_MERGE == 0 and s % TQ_A == 0 and s % TQ_B == 0 and (TQ_A // 2) % CHUNK == 0 and TQ_B % CHUNK == 0
    assert 2 * TQ_B >= BAND_PAST and past >= lb
    assert _far_bucket(TQ_A + 1) == T5_BUCKETS // 2 - 1

    r = jnp.arange(NORM_GROUP)
    pm = ((r[:, None] // HD_A) == (r[None, :] // HD_A)).astype(BF16) * (1.0 / HD_A)
    c_rows = bp + bs
    c_pad = -(-c_rows // SUBLANES) * SUBLANES
    c_all = jnp.concatenate([c_prompt, c_sample, jnp.zeros((c_pad - c_rows, d), F32)], axis=0)

    nk_as = -(-(past + t) // LANES) * LANES
    nk_bs = -(-(lb + t) // LANES) * LANES
    t5_fn = functools.partial(_t5_run, t5_bias)
    outs = [[] for _ in range(8)]
    for l in range(depth):
        lam_init = 0.8 - 0.6 * math.exp(-0.3 * l)
        rel_fn = functools.partial(_clipped_run, rel_bias_b[l])
        cfar = t5_bias[_t5_bucket(jnp.int32(-(TQ_A + 1)))]
        bias_ap = _bias_tiles(t5_fn, cfar, H_A, TQ_A, 2 * TQ_A, 1, QBASE, QBASE - TQ_A, 0, False, 2 * TQ_A)
        bias_bp = _bias_tiles(rel_fn, jnp.zeros((H_B,)), H_B, TQ_B, 3 * TQ_B, 3, QBASE, QBASE, TQ_B, True, 3 * TQ_B)
        bias_as = _bias_tiles(t5_fn, jnp.zeros((H_A,)), H_A, t, nk_as, 1, past, 0, 0, False, past + t)
        bias_bs = _bias_tiles(rel_fn, jnp.zeros((H_B,)), H_B, t, nk_bs, 1, past, past - lb, 0, True, lb + t)

        mod = _modulation(c_all, w_ada[l], b_ada[l])
        shift = mod[:, :d].reshape(c_pad, 1, d)
        scale = mod[:, d:2 * d].reshape(c_pad, 1, d)
        gate = mod[:, 2 * d:].reshape(c_pad, 1, d)
        w_in_bf = w_in[l].astype(BF16)
        w_oa_bf, w_ob_bf, w_out_bf = w_oa[l].astype(BF16), w_ob[l].astype(BF16), w_out[l].astype(BF16)
        tile8 = lambda g: jnp.tile(g, SEG // g.shape[0]).reshape(1, SEG)
        gains = (tile8(g_qa[l]), tile8(g_ka[l]), tile8(g_qb[l]), tile8(g_kb[l]))
        lamv = jnp.stack([lam_q1[l], lam_k1[l], lam_q2[l], lam_k2[l]])
        gsub = g_subln[l].reshape(1, LANES)

        (qa, ka32, kab, va32, vab, ga, qb, kb32, kbb, vb32, vbb, gb, mg) = _project(
            xp, shift[:bp], scale[:bp], g_norm[l], w_in_bf, *gains, pm, 1, TM_PROJ)
        bounded_a = _logits_bounded(g_qa[l], g_ka[l], HD_A, t5_bias, cfar)
        bounded_b = _logits_bounded(g_qb[l], g_kb[l], HD_B, rel_bias_b[l], jnp.zeros((H_B,)))
        oa = _attn_a_prompt(qa, kab, vab, ga, bias_ap, bounded_a, lamv, gsub, TQ_A, lam_init)
        ob = _attn_b_prompt(qb, kbb, vbb, gb, bias_bp, bounded_b, TQ_B)
        xp = _merge(xp, gate[:bp], oa, ob, mg, w_oa_bf, w_ob_bf, w_out_bf, 1, TM_MERGE)
        outs[0].append(ka32.reshape(bp, s, H_A, 2 * HD_A))
        outs[1].append(va32.reshape(bp, s, H_A, 2 * HD_A))
        outs[2].append(kb32[:, s - n_keep:].reshape(bp, n_keep, H_B, HD_B))
        outs[3].append(vb32[:, s - n_keep:].reshape(bp, n_keep, H_B, HD_B))

        (qa, ka32, kab, va32, vab, ga, qb, kb32, kbb, vb32, vbb, gb, mg) = _project(
            xs, shift[bp:c_rows], scale[bp:c_rows], g_norm[l], w_in_bf, *gains, pm, bs, t)
        oa = _attn_a_sample(qa, kab, vab, ga, cache_a_k[l].reshape(bs, past * H_A, LANES), cache_a_v[l].reshape(bs, past * H_A, LANES),
                            bias_as, lamv, gsub, lam_init)
        to_rows = lambda c: jnp.transpose(c, (0, 2, 3, 1)).reshape(bs, W_B, lb)
        ob = _attn_b_sample(qb, kbb, vbb, gb, to_rows(cache_b_k[l]), to_rows(cache_b_v[l]), bias_bs)
        xs = _merge(xs, gate[bp:c_rows], oa, ob, mg, w_oa_bf, w_ob_bf, w_out_bf, bs, t)
        outs[4].append(ka32.reshape(bs, t, H_A, 2 * HD_A))
        outs[5].append(va32.reshape(bs, t, H_A, 2 * HD_A))
        outs[6].append(kb32.reshape(bs, t, H_B, HD_B))
        outs[7].append(vb32.reshape(bs, t, H_B, HD_B))

    return (xp, xs) + tuple(jnp.stack(o) for o in outs)
```

```python
import functools
import math

import jax
import jax.numpy as jnp
from jax import lax
from jax.experimental import pallas as pl
from jax.experimental.pallas import tpu as pltpu

CHUNK = 64
H_A = 4
HD_A = 64
W_A = H_A * 2 * HD_A
H_B = 8
HD_B = 64
W_B = H_B * HD_B
BAND_CHUNKS = 8
BAND_PAST = BAND_CHUNKS * CHUNK
REL_CLIP_B = 128
T5_BUCKETS = 32
T5_MAX_EXACT = 8
T5_MAX_DIST = 128
EPS = 1e-6
NEG = -1e30

LANES = 128
SUBLANES = 8
SEG = 512
NORM_GROUP = 256
MAX_DIRECT_LOGIT = 60.0
BF16_ROUND_MARGIN = 1.02
VMEM_LIMIT = 56 * 1024 * 1024

LOG2E = math.log2(math.e)

F32 = jnp.float32
BF16 = jnp.bfloat16


def _t5_bucket(rel):
    half = T5_BUCKETS // 2
    assert (T5_MAX_DIST // T5_MAX_EXACT) ** 2 == 2 ** (half - T5_MAX_EXACT)
    ret = jnp.where(rel > 0, half, 0)
    n = jnp.abs(rel)
    large = T5_MAX_EXACT + sum((n * n >= T5_MAX_EXACT ** 2 * 2 ** j).astype(jnp.int32)
                               for j in range(1, half - T5_MAX_EXACT))
    large = jnp.minimum(large, half - 1)
    return ret + jnp.where(n < T5_MAX_EXACT, n, large)


def _t5_run(t5_bias, first_rel, count):
    assert max(abs(first_rel), abs(first_rel + count)) < 2 ** 15
    rel = first_rel + jnp.arange(count, dtype=jnp.int32)
    hit = _t5_bucket(rel)[:, None, None] == jnp.arange(T5_BUCKETS, dtype=jnp.int32)[None, :, None]
    return jnp.sum(jnp.where(hit, t5_bias[None], 0.0), axis=1)


def _clipped_run(table, first_rel, count):
    n = table.shape[0]
    first = first_rel + (n - 1) // 2
    n_lo = min(max(-first, 0), count)
    n_hi = min(max(first + count - n, 0), count)
    mid = count - n_lo - n_hi
    parts = [jnp.broadcast_to(table[:1], (n_lo, table.shape[1])),
             table[first + n_lo:first + n_lo + mid] if mid > 0 else table[:0],
             jnp.broadcast_to(table[n - 1:], (n_hi, table.shape[1]))]
    return jnp.concatenate(parts, axis=0)


def _far_bucket(n):
    half = T5_BUCKETS // 2
    return min(T5_MAX_EXACT + int(math.log(n / T5_MAX_EXACT) / math.log(T5_MAX_DIST / T5_MAX_EXACT) * (half - T5_MAX_EXACT)), half - 1)


def _nt(a, b):
    return lax.dot_general(a, b, (((1,), (1,)), ((), ())), preferred_element_type=F32)


def _silu(x):
    return x * (1.0 / (1.0 + jnp.exp(-x)))


def _sigmoid(x):
    return 1.0 / (1.0 + jnp.exp(-x))


def _mod_kernel(c_ref, w_ref, b_ref, o_ref):
    c = c_ref[...]
    o_ref[...] = jnp.dot(_silu(c), w_ref[...], preferred_element_type=F32,
                         precision=lax.Precision.HIGHEST) + b_ref[...]


def _modulation(c_all, w_ada, b_ada):
    rows, d = c_all.shape
    n_out = w_ada.shape[1]
    tn = d
    return pl.pallas_call(
        _mod_kernel,
        grid=(n_out // tn,),
        in_specs=[pl.BlockSpec((rows, d), lambda j: (0, 0)),
                  pl.BlockSpec((d, tn), lambda j: (0, j)),
                  pl.BlockSpec((1, tn), lambda j: (0, j))],
        out_specs=pl.BlockSpec((rows, tn), lambda j: (0, j)),
        out_shape=jax.ShapeDtypeStruct((rows, n_out), F32),
        name="adaln_mod",
    )(c_all, w_ada, b_ada.reshape(1, n_out))


def _proj_kernel(x_ref, shift_ref, scale_ref, gn_ref, w_ref, gqa_ref, gka_ref, gqb_ref, gkb_ref, pm_ref,
                 qa_ref, ka32_ref, kab_ref, va32_ref, vab_ref, ga_ref,
                 qb_ref, kb32_ref, kbb_ref, vb32_ref, vbb_ref, gb_ref, mg_ref):
    nb, t, d = x_ref.shape
    rows = nb * t
    x = x_ref[...]
    ms = jnp.mean(x * x, axis=-1, keepdims=True)
    xn = x * lax.rsqrt(ms + EPS) * gn_ref[...]
    h = xn * (1.0 + scale_ref[...]) + shift_ref[...]
    hb = h.reshape(rows, d).astype(BF16)

    def seg(c, width=SEG):
        return jnp.dot(hb, w_ref[:, c * SEG:c * SEG + width], preferred_element_type=F32)

    def head_norm(y, g_ref):
        sq = y * y
        hi = sq.astype(BF16)
        pm = pm_ref[...]
        parts = []
        for c in range(SEG // NORM_GROUP):
            sl = slice(c * NORM_GROUP, (c + 1) * NORM_GROUP)
            gms = jnp.dot(hi[:, sl], pm, preferred_element_type=F32)
            parts.append(y[:, sl] * lax.rsqrt(gms + EPS))
        return jnp.concatenate(parts, axis=1) * g_ref[...]

    def put(ref, y):
        ref[...] = y.astype(ref.dtype).reshape(ref.shape)

    def put_by_head(ref, y):
        for hh in range(H_A):
            ref[:, pl.ds(hh, t, stride=H_A), :] = y[:, hh * LANES:(hh + 1) * LANES].reshape(nb, t, LANES)

    put(qa_ref, head_norm(seg(0), gqa_ref) * (HD_A ** -0.5 * LOG2E))
    ka = head_norm(seg(1), gka_ref)
    put_by_head(ka32_ref, ka)
    put(kab_ref, ka)
    va = seg(2)
    put_by_head(va32_ref, va)
    put(vab_ref, va)
    put(ga_ref, seg(3))
    put(qb_ref, head_norm(seg(4), gqb_ref) * (HD_B ** -0.5 * LOG2E))
    kb = head_norm(seg(5), gkb_ref)
    put(kb32_ref, kb)
    put(kbb_ref, kb)
    vb = seg(6)
    put(vb32_ref, vb)
    put(vbb_ref, vb)
    put(gb_ref, seg(7))
    for c in range(8, 12):
        mg_ref[:, :, (c - 8) * SEG:(c - 7) * SEG] = seg(c).astype(mg_ref.dtype).reshape(nb, t, SEG)


def _project(x, shift, scale, g_norm, w_in_bf, gqa, gka, gqb, gkb, pm, nb, t):
    bx, sx, d = x.shape
    n_cols = w_in_bf.shape[1]
    grid = (bx // nb, sx // t)
    tok = lambda w: pl.BlockSpec((nb, t, w), lambda b, i: (b, i, 0))
    per_b = pl.BlockSpec((nb, 1, d), lambda b, i: (b, 0, 0))
    const = lambda shp: pl.BlockSpec(shp, lambda b, i: (0,) * len(shp))
    sds = lambda w, dt: jax.ShapeDtypeStruct((bx, sx, w), dt)
    by_head = jax.ShapeDtypeStruct((bx, sx * H_A, LANES), F32)
    out_shape = (sds(SEG, BF16), by_head, sds(SEG, BF16), by_head, sds(SEG, BF16), sds(SEG, BF16),
                 sds(SEG, BF16), sds(SEG, F32), sds(SEG, BF16), sds(SEG, F32), sds(SEG, BF16), sds(SEG, BF16),
                 sds(4 * SEG, BF16))
    out_specs = tuple(pl.BlockSpec((nb, s.shape[1] // (sx // t), s.shape[2]), lambda b, i: (b, i, 0)) for s in out_shape)
    return pl.pallas_call(
        _proj_kernel,
        grid=grid,
        in_specs=[tok(d), per_b, per_b, const((1, d)),
                  pl.BlockSpec((d, n_cols), lambda b, i: (0, 0), pipeline_mode=pl.Buffered(1)),
                  const((1, SEG)), const((1, SEG)), const((1, SEG)), const((1, SEG)),
                  const((NORM_GROUP, NORM_GROUP))],
        out_specs=out_specs,
        out_shape=out_shape,
        compiler_params=pltpu.CompilerParams(dimension_semantics=("arbitrary", "arbitrary"),
                                             vmem_limit_bytes=VMEM_LIMIT),
        name="in_proj",
    )(x, shift, scale, g_norm.reshape(1, d), w_in_bf, gqa, gka, gqb, gkb, pm)


def _bias_tile_kernel(off_ref, gen_ref, o_ref, *, tq, nk, qbase, kbase0, kstep, band, nvalid):
    v = pl.program_id(1)
    g = gen_ref[0, 0]
    x = jnp.broadcast_to(g, (tq, g.shape[-1]))
    y = pltpu.roll(x, 0, 1, stride=1, stride_axis=0)
    y = (y[:, :nk] - off_ref[pl.program_id(0)]) * LOG2E
    row = lax.broadcasted_iota(jnp.int32, (tq, nk), 0)
    col = lax.broadcasted_iota(jnp.int32, (tq, nk), 1)
    qc = (qbase + row) // CHUNK
    kc = (kbase0 - v * kstep + col) // CHUNK
    y = jnp.where(kc <= qc, y, NEG)
    if band:
        y = jnp.where(kc >= qc - BAND_CHUNKS, y, NEG)
    y = jnp.where(col < nvalid, y, NEG)
    o_ref[0, 0] = y


def _bias_tiles(table_fn, offset, n_heads, tq, nk, n_var, qbase, kbase0, kstep, band, nvalid):
    length = 1 << (tq + nk - 1).bit_length()
    assert length >= tq + nk - 1 and min(kbase0 - (n_var - 1) * kstep, qbase) >= 0
    gen = jnp.stack([jnp.concatenate([table_fn(kbase0 - v * kstep - qbase, nk),
                                      table_fn(kbase0 - v * kstep - qbase + nk - length, length - nk)], axis=0)
                     for v in range(n_var)])
    gen = jnp.transpose(gen, (2, 0, 1)).astype(F32).reshape(n_heads, n_var, 1, length)
    return pl.pallas_call(
        functools.partial(_bias_tile_kernel, tq=tq, nk=nk, qbase=qbase, kbase0=kbase0, kstep=kstep,
                          band=band, nvalid=nvalid),
        grid=(n_heads, n_var),
        in_specs=[pl.BlockSpec(memory_space=pltpu.SMEM),
                  pl.BlockSpec((1, 1, 1, length), lambda h, v: (h, v, 0, 0))],
        out_specs=pl.BlockSpec((1, 1, tq, nk), lambda h, v: (h, v, 0, 0)),
        out_shape=jax.ShapeDtypeStruct((n_heads, n_var, tq, nk), F32),
        name="bias_tiles",
    )(offset.astype(F32), gen)


def _split_halves(q):
    lane = lax.broadcasted_iota(jnp.int32, q.shape, 1)
    zero = jnp.zeros_like(q)
    return jnp.where(lane < LANES // 2, q, zero), jnp.where(lane >= LANES // 2, q, zero)


def _online_update(carry, s, v):
    m, l, acc = carry
    m_new = jnp.maximum(m, jnp.max(s, axis=-1, keepdims=True))
    alpha = jnp.exp2(m - m_new)
    p = jnp.exp2(s - m_new)
    l = alpha * l + jnp.sum(p, axis=-1, keepdims=True)
    acc = alpha * acc + jnp.dot(p.astype(BF16), v, preferred_element_type=F32)
    return m_new, l, acc


def _diff_init(tq):
    return (jnp.full((tq, 1), -jnp.inf, F32), jnp.zeros((tq, 1), F32), jnp.zeros((tq, LANES), F32))


def _diff_finish(c0, c1, lamv_ref, gsub_ref, g, lam_init):
    lamv = lamv_ref[...]
    e1 = jnp.exp(jnp.sum(lamv[0:1] * lamv[1:2], axis=-1, keepdims=True))
    e2 = jnp.exp(jnp.sum(lamv[2:3] * lamv[3:4], axis=-1, keepdims=True))
    lam = e1 - e2 + lam_init
    o = c0[2] / c0[1] - lam * (c1[2] / c1[1])
    o = o * lax.rsqrt(jnp.mean(o * o, axis=-1, keepdims=True) + EPS) * gsub_ref[...]
    o = o * (1.0 - lam_init)
    return o * _silu(g.astype(F32))


def _lane_partial_sum(p):
    out = p[:, :LANES]
    for c in range(1, p.shape[1] // LANES):
        out = out + p[:, c * LANES:(c + 1) * LANES]
    return out


def _attn_a_prompt_kernel(par_ref, q_ref, k_ref, v_ref, g_ref, bias_ref, lamv_ref, gsub_ref, o_ref, l_ref, acc_ref,
                          *, tq, lam_init):
    i = pl.program_id(2)
    has_near = i > 0
    n_far = jnp.maximum(i - 1, 0)
    near = pl.multiple_of(n_far * tq, tq)
    diag = pl.multiple_of(i * tq, tq)
    half = tq // 2
    bounded = par_ref[0] > 0.5
    n_heads = q_ref.shape[2] // LANES
    hs = [slice(hh * LANES, (hh + 1) * LANES) for hh in range(n_heads)]

    def finish_head(hh, c0, c1):
        o = _diff_finish(c0, c1, lamv_ref, gsub_ref, g_ref[0, :, hs[hh]], lam_init)
        o_ref[0, :, hs[hh]] = o.astype(o_ref.dtype)

    def run_online():
        for hh in range(n_heads):
            q0, q1 = _split_halves(q_ref[0, :, hs[hh]])

            def tile(carry, start, bias, mask=None):
                c0, c1 = carry
                k = k_ref[0, pl.ds(start, tq), hs[hh]]
                v = v_ref[0, pl.ds(start, tq), hs[hh]]
                s0, s1 = _nt(q0, k), _nt(q1, k)
                if bias is not None:
                    b = bias_ref[hh, 0, :, bias] if mask is None else bias_ref[hh, 0, :, bias] + mask
                    s0, s1 = s0 + b, s1 + b
                return _online_update(c0, s0, v), _online_update(c1, s1, v)

            init = _diff_init(tq)
            carry = lax.fori_loop(0, n_far, lambda j, c: tile(c, pl.multiple_of(j * tq, tq), None), (init, init))
            carry = tile(carry, near, slice(0, tq), jnp.where(has_near, 0.0, NEG))
            c0, c1 = tile(carry, diag, slice(tq, 2 * tq))
            finish_head(hh, c0, c1)

    def run_bounded():
        qs = [_split_halves(q_ref[0, :, sl]) for sl in hs]
        l_ref[...] = jnp.zeros(l_ref.shape, F32)
        acc_ref[...] = jnp.zeros(acc_ref.shape, F32)

        def accumulate(hh, start, nk, rows=slice(None), bias_cols=None, mask=None):
            k = k_ref[0, pl.ds(start, nk), hs[hh]]
            v = v_ref[0, pl.ds(start, nk), hs[hh]]
            bias = None if bias_cols is None else bias_ref[hh, 0, rows, bias_cols]
            if mask is not None:
                bias = bias + mask
            for mp in range(2):
                s = _nt(qs[hh][mp][rows], k)
                p = jnp.exp2(s if bias is None else s + bias)
                l_ref[hh, mp, rows, :] += _lane_partial_sum(p)
                acc_ref[hh, mp, rows, :] += jnp.dot(p.astype(BF16), v, preferred_element_type=F32)

        def far(j, carry):
            for hh in range(n_heads):
                accumulate(hh, pl.multiple_of(j * tq, tq), tq)
            return carry

        lax.fori_loop(0, n_far, far, 0)
        near_mask = jnp.where(has_near, 0.0, NEG)
        for hh in range(n_heads):
            accumulate(hh, near, tq, bias_cols=slice(0, tq), mask=near_mask)
            accumulate(hh, diag, half, slice(0, half), slice(tq, tq + half))
            accumulate(hh, diag, tq, slice(half, tq), slice(tq, 2 * tq))
            c0 = (None, jnp.sum(l_ref[hh, 0], axis=-1, keepdims=True), acc_ref[hh, 0])
            c1 = (None, jnp.sum(l_ref[hh, 1], axis=-1, keepdims=True), acc_ref[hh, 1])
            finish_head(hh, c0, c1)

    pl.when(bounded)(run_bounded)
    pl.when(jnp.logical_not(bounded))(run_online)


def _logits_bounded(g_q, g_k, head_dim, table, offset):
    qk = math.sqrt(head_dim) * LOG2E * jnp.max(jnp.abs(g_q)) * jnp.max(jnp.abs(g_k)) * BF16_ROUND_MARGIN
    bias = jnp.max(jnp.abs(table - offset[None, :])) * LOG2E
    return (qk + bias <= MAX_DIRECT_LOGIT).astype(F32).reshape(1)


def _attn_a_prompt(qa, kab, vab, ga, bias, bounded, lamv, gsub, tq, lam_init):
    b, s, _ = qa.shape
    assert bias.shape[1] == 1 and bias.shape[3] == 2 * tq
    hps = HEADS_PER_STEP_A
    tile = pl.BlockSpec((1, tq, hps * LANES), lambda bi, h, i: (bi, i, h))
    single = pl.Buffered(1)
    full = pl.BlockSpec((1, s, hps * LANES), lambda bi, h, i: (bi, 0, h))
    return pl.pallas_call(
        functools.partial(_attn_a_prompt_kernel, tq=tq, lam_init=lam_init),
        grid=(b, H_A // hps, s // tq),
        in_specs=[pl.BlockSpec(memory_space=pltpu.SMEM), tile, full, full, tile,
                  pl.BlockSpec((hps, 1, tq, 2 * tq), lambda bi, h, i: (h, 0, 0, 0), pipeline_mode=single),
                  pl.BlockSpec((4, HD_A), lambda bi, h, i: (0, 0)),
                  pl.BlockSpec((1, LANES), lambda bi, h, i: (0, 0))],
        out_specs=tile,
        out_shape=jax.ShapeDtypeStruct((b, s, W_A), BF16),
        scratch_shapes=[pltpu.VMEM((hps, 2, tq, LANES), F32), pltpu.VMEM((hps, 2, tq, LANES), F32)],
        compiler_params=pltpu.CompilerParams(dimension_semantics=("arbitrary", "arbitrary", "arbitrary"),
                                             vmem_limit_bytes=VMEM_LIMIT),
        name="diff_attn_prompt",
    )(bounded, qa, kab, vab, ga, bias, lamv, gsub)


def _attn_a_sample_kernel(q_ref, kc_ref, vc_ref, kn_ref, vn_ref, g_ref, bias_ref, lamv_ref, gsub_ref, o_ref, *, lam_init):
    tq = q_ref.shape[1]
    nk = bias_ref.shape[3]
    past = kc_ref.shape[1] // H_A
    pad = jnp.zeros((nk - past - tq, LANES), BF16)
    for hh in range(H_A):
        sl = slice(hh * LANES, (hh + 1) * LANES)
        rows = pl.ds(hh, past, stride=H_A)
        qst = jnp.concatenate(_split_halves(q_ref[0, :, sl]), axis=0)
        k = jnp.concatenate([kc_ref[0, rows, :].astype(BF16), kn_ref[0, :, sl], pad], axis=0)
        v = jnp.concatenate([vc_ref[0, rows, :].astype(BF16), vn_ref[0, :, sl], pad], axis=0)
        bias = bias_ref[hh, 0]
        c = _online_update(_diff_init(2 * tq), _nt(qst, k) + jnp.concatenate([bias, bias], axis=0), v)
        c0, c1 = tuple(x[:tq] for x in c), tuple(x[tq:] for x in c)
        o_ref[0, :, sl] = _diff_finish(c0, c1, lamv_ref, gsub_ref, g_ref[0, :, sl], lam_init).astype(o_ref.dtype)


def _attn_a_sample(qa, kab, vab, ga, cache_k, cache_v, bias, lamv, gsub, lam_init):
    b, t, _ = qa.shape
    rows = cache_k.shape[1]
    nk = bias.shape[3]
    new = pl.BlockSpec((1, t, W_A), lambda bi: (bi, 0, 0))
    old = pl.BlockSpec((1, rows, LANES), lambda bi: (bi, 0, 0))
    return pl.pallas_call(
        functools.partial(_attn_a_sample_kernel, lam_init=lam_init),
        grid=(b,),
        in_specs=[new, old, old, new, new, new,
                  pl.BlockSpec((H_A, 1, t, nk), lambda bi: (0, 0, 0, 0)),
                  pl.BlockSpec((4, HD_A), lambda bi: (0, 0)),
                  pl.BlockSpec((1, LANES), lambda bi: (0, 0))],
        out_specs=new,
        out_shape=jax.ShapeDtypeStruct((b, t, W_A), BF16),
        name="diff_attn_sample",
    )(qa, cache_k, cache_v, kab, vab, ga, bias, lamv, gsub)


def _band_pair(q, k, v, g, bias_ref, bounded, first_head, piece_masks):
    t = q.shape[0]
    qst = jnp.concatenate(_split_halves(q), axis=0)
    bias = jnp.concatenate([bias_ref[first_head, 0], bias_ref[first_head + 1, 0]], axis=0)
    w = bias.shape[1] // (len(piece_masks) + 1)
    bias = jnp.concatenate([bias[:, c * w:(c + 1) * w] + m for c, m in enumerate(piece_masks)]
                           + [bias[:, len(piece_masks) * w:]], axis=1)
    s = _nt(qst, k) + bias
    if not bounded:
        s = s - jnp.max(s, axis=-1, keepdims=True)
    p = jnp.exp2(s)
    l = jnp.sum(p, axis=-1, keepdims=True)
    o2 = jnp.dot(p.astype(BF16), v, preferred_element_type=F32) / l
    lane = lax.broadcasted_iota(jnp.int32, (t, LANES), 1)
    o = jnp.where(lane < LANES // 2, o2[:t], o2[t:])
    return o * _silu(g.astype(F32))


def _attn_b_prompt_kernel(par_ref, q_ref, k_ref, v_ref, g_ref, bias_ref, o_ref, *, tq, n_prev):
    i = pl.program_id(2)
    tiles = q_ref.shape[1] // tq
    bounded = par_ref[0] > 0.5

    def run(direct):
        for sub in range(tiles):
            first = tiles * i + sub - n_prev
            rows = slice(sub * tq, (sub + 1) * tq)
            starts = [pl.multiple_of(jnp.maximum(first + c, 0) * tq, tq) for c in range(n_prev + 1)]
            masks = [jnp.where(first + c >= 0, 0.0, NEG) for c in range(n_prev)]
            for pp in range(q_ref.shape[2] // LANES):
                sl = slice(pp * LANES, (pp + 1) * LANES)
                k = jnp.concatenate([k_ref[0, pl.ds(st, tq), sl] for st in starts], axis=0)
                v = jnp.concatenate([v_ref[0, pl.ds(st, tq), sl] for st in starts], axis=0)
                o = _band_pair(q_ref[0, rows, sl], k, v, g_ref[0, rows, sl], bias_ref, direct, 2 * pp, masks)
                o_ref[0, rows, sl] = o.astype(o_ref.dtype)

    pl.when(bounded)(lambda: run(True))
    pl.when(jnp.logical_not(bounded))(lambda: run(False))


def _attn_b_prompt(qb, kbb, vbb, gb, bias, bounded, tq):
    b, s, w = qb.shape
    n_prev = bias.shape[3] // tq - 1
    assert bias.shape[1] == 1 and bias.shape[3] == (n_prev + 1) * tq
    tps = TILES_PER_STEP_B
    tile = pl.BlockSpec((1, tps * tq, w), lambda bi, p, i: (bi, i, p))
    full = pl.BlockSpec((1, s, w), lambda bi, p, i: (bi, 0, p))
    return pl.pallas_call(
        functools.partial(_attn_b_prompt_kernel, tq=tq, n_prev=n_prev),
        grid=(b, 1, s // (tps * tq)),
        in_specs=[pl.BlockSpec(memory_space=pltpu.SMEM), tile, full, full, tile,
                  pl.BlockSpec((H_B, 1, tq, (n_prev + 1) * tq), lambda bi, p, i: (0, 0, 0, 0),
                               pipeline_mode=pl.Buffered(1))],
        out_specs=tile,
        out_shape=jax.ShapeDtypeStruct((b, s, W_B), BF16),
        compiler_params=pltpu.CompilerParams(dimension_semantics=("arbitrary", "arbitrary", "arbitrary"),
                                             vmem_limit_bytes=VMEM_LIMIT),
        name="band_attn_prompt",
    )(bounded, qb, kbb, vbb, gb, bias)


def _attn_b_sample_kernel(q_ref, kc_ref, vc_ref, kn_ref, vn_ref, g_ref, bias_ref, o_ref):
    nk = bias_ref.shape[3]
    past, t = kc_ref.shape[2], kn_ref.shape[1]
    pad = jnp.zeros((nk - past - t, LANES), BF16)
    lane = lax.broadcasted_iota(jnp.int32, (t, LANES), 1)
    for pp in range(q_ref.shape[2] // LANES):
        sl = slice(pp * LANES, (pp + 1) * LANES)
        qst = jnp.concatenate(_split_halves(q_ref[0, :, sl]), axis=0)
        k_new = jnp.concatenate([kn_ref[0, :, sl], pad], axis=0)
        v_new = jnp.concatenate([vn_ref[0, :, sl], pad], axis=0)
        s = jnp.concatenate([jnp.dot(qst, kc_ref[0, sl, :].astype(BF16), preferred_element_type=F32),
                             _nt(qst, k_new)], axis=1)
        s = s + jnp.concatenate([bias_ref[2 * pp, 0], bias_ref[2 * pp + 1, 0]], axis=0)
        p = jnp.exp2(s - jnp.max(s, axis=-1, keepdims=True))
        l = jnp.sum(p, axis=-1, keepdims=True)
        pb = p.astype(BF16)
        o2 = (_nt(pb[:, :past], vc_ref[0, sl, :].astype(BF16))
              + jnp.dot(pb[:, past:], v_new, preferred_element_type=F32)) / l
        o = jnp.where(lane < LANES // 2, o2[:t], o2[t:])
        o_ref[0, :, sl] = (o * _silu(g_ref[0, :, sl].astype(F32))).astype(o_ref.dtype)


def _attn_b_sample(qb, kbb, vbb, gb, cache_kt, cache_vt, bias):
    b, t, w = qb.shape
    past = cache_kt.shape[2]
    nk = bias.shape[3]
    assert past % LANES == 0
    new = pl.BlockSpec((1, t, w), lambda bi: (bi, 0, 0))
    old = pl.BlockSpec((1, w, past), lambda bi: (bi, 0, 0))
    return pl.pallas_call(
        _attn_b_sample_kernel,
        grid=(b,),
        in_specs=[new, old, old, new, new, new,
                  pl.BlockSpec((H_B, 1, t, nk), lambda bi: (0, 0, 0, 0))],
        out_specs=new,
        out_shape=jax.ShapeDtypeStruct((b, t, W_B), BF16),
        name="band_attn_sample",
    )(qb, cache_kt, cache_vt, kbb, vbb, gb, bias)


def _merge_kernel(x_ref, gate_ref, oa_ref, ob_ref, mg_ref, woa_ref, wob_ref, wout_ref, o_ref):
    nb, t, d = x_ref.shape
    rows = nb * t
    ya = jnp.dot(oa_ref[...].reshape(rows, W_A), woa_ref[...], preferred_element_type=F32)
    yb = jnp.dot(ob_ref[...].reshape(rows, W_B), wob_ref[...], preferred_element_type=F32)
    mg = mg_ref[...].reshape(rows, 2 * d).astype(F32)
    m = _sigmoid(mg[:, :d]) * ya + _sigmoid(mg[:, d:]) * yb
    y = jnp.dot(m.astype(BF16), wout_ref[...], preferred_element_type=F32)
    o_ref[...] = x_ref[...] + gate_ref[...] * y.reshape(nb, t, d)


def _merge(x, gate, oa, ob, mg, w_oa_bf, w_ob_bf, w_out_bf, nb, t):
    bx, sx, d = x.shape
    tok = lambda w: pl.BlockSpec((nb, t, w), lambda b, i: (b, i, 0))
    const = lambda shp: pl.BlockSpec(shp, lambda b, i: (0,) * len(shp))
    return pl.pallas_call(
        _merge_kernel,
        grid=(bx // nb, sx // t),
        in_specs=[tok(d), pl.BlockSpec((nb, 1, d), lambda b, i: (b, 0, 0)), tok(W_A), tok(W_B), tok(2 * d),
                  const((W_A, d)), const((W_B, d)), const((d, d))],
        out_specs=tok(d),
        out_shape=jax.ShapeDtypeStruct((bx, sx, d), F32),
        compiler_params=pltpu.CompilerParams(dimension_semantics=("arbitrary", "arbitrary"),
                                             vmem_limit_bytes=VMEM_LIMIT),
        name="merge_out",
    )(x, gate, oa, ob, mg, w_oa_bf, w_ob_bf, w_out_bf)


TQ_A = 512
HEADS_PER_STEP_A = 4
TQ_B = 256
PREV_TILES_B = BAND_PAST // TQ_B
TILES_PER_STEP_B = 2
TM_PROJ = 512
TM_MERGE = 1024
QBASE = 2048


def kernel(x_prompt, x_sample, cache_a_k, cache_a_v, cache_b_k, cache_b_v, c_prompt, c_sample, g_norm, w_ada, b_ada, w_in, g_qa, g_ka, lam_q1, lam_k1, lam_q2, lam_k2, g_subln, t5_bias, g_qb, g_kb, rel_bias_b, w_oa, w_ob, w_out):
    xp, xs = x_prompt, x_sample
    bp, s, d = xp.shape
    bs, t, _ = xs.shape
    depth = w_in.shape[0]
    past = cache_a_k.shape[2]
    lb = cache_b_k.shape[2]
    n_keep = min(BAND_PAST, s)
    assert s % TM_PROJ == 0 and s % TM_MERGE == 0 and s % TQ_A == 0 and s % TQ_B == 0 and (TQ_A // 2) % CHUNK == 0 and TQ_B % CHUNK == 0
    assert PREV_TILES_B * TQ_B >= BAND_PAST and s % (TILES_PER_STEP_B * TQ_B) == 0 and past >= lb
    assert _far_bucket(TQ_A + 1) == T5_BUCKETS // 2 - 1

    r = jnp.arange(NORM_GROUP)
    pm = ((r[:, None] // HD_A) == (r[None, :] // HD_A)).astype(BF16) * (1.0 / HD_A)
    c_rows = bp + bs
    c_pad = -(-c_rows // SUBLANES) * SUBLANES
    c_all = jnp.concatenate([c_prompt, c_sample, jnp.zeros((c_pad - c_rows, d), F32)], axis=0)

    nk_as = -(-(past + t) // LANES) * LANES
    nk_bs = -(-(lb + t) // LANES) * LANES
    t5_fn = functools.partial(_t5_run, t5_bias)
    outs = [[] for _ in range(8)]
    for l in range(depth):
        lam_init = 0.8 - 0.6 * math.exp(-0.3 * l)
        rel_fn = functools.partial(_clipped_run, rel_bias_b[l])
        cfar = t5_bias[_t5_bucket(jnp.int32(-(TQ_A + 1)))]
        bias_ap = _bias_tiles(t5_fn, cfar, H_A, TQ_A, 2 * TQ_A, 1, QBASE, QBASE - TQ_A, 0, False, 2 * TQ_A)
        bias_bp = _bias_tiles(rel_fn, jnp.zeros((H_B,)), H_B, TQ_B, (PREV_TILES_B + 1) * TQ_B, 1, QBASE,
                              QBASE - PREV_TILES_B * TQ_B, 0, True, (PREV_TILES_B + 1) * TQ_B)
        bias_as = _bias_tiles(t5_fn, jnp.zeros((H_A,)), H_A, t, nk_as, 1, past, 0, 0, False, past + t)
        bias_bs = _bias_tiles(rel_fn, jnp.zeros((H_B,)), H_B, t, nk_bs, 1, past, past - lb, 0, True, lb + t)

        mod = _modulation(c_all, w_ada[l], b_ada[l])
        shift = mod[:, :d].reshape(c_pad, 1, d)
        scale = mod[:, d:2 * d].reshape(c_pad, 1, d)
        gate = mod[:, 2 * d:].reshape(c_pad, 1, d)
        w_in_bf = w_in[l].astype(BF16)
        w_oa_bf, w_ob_bf, w_out_bf = w_oa[l].astype(BF16), w_ob[l].astype(BF16), w_out[l].astype(BF16)
        tile8 = lambda g: jnp.tile(g, SEG // g.shape[0]).reshape(1, SEG)
        gains = (tile8(g_qa[l]), tile8(g_ka[l]), tile8(g_qb[l]), tile8(g_kb[l]))
        lamv = jnp.stack([lam_q1[l], lam_k1[l], lam_q2[l], lam_k2[l]])
        gsub = g_subln[l].reshape(1, LANES)

        (qa, ka32, kab, va32, vab, ga, qb, kb32, kbb, vb32, vbb, gb, mg) = _project(
            xp, shift[:bp], scale[:bp], g_norm[l], w_in_bf, *gains, pm, 1, TM_PROJ)
        bounded_a = _logits_bounded(g_qa[l], g_ka[l], HD_A, t5_bias, cfar)
        bounded_b = _logits_bounded(g_qb[l], g_kb[l], HD_B, rel_bias_b[l], jnp.zeros((H_B,)))
        oa = _attn_a_prompt(qa, kab, vab, ga, bias_ap, bounded_a, lamv, gsub, TQ_A, lam_init)
        ob = _attn_b_prompt(qb, kbb, vbb, gb, bias_bp, bounded_b, TQ_B)
        xp = _merge(xp, gate[:bp], oa, ob, mg, w_oa_bf, w_ob_bf, w_out_bf, 1, TM_MERGE)
        outs[0].append(ka32.reshape(bp, s, H_A, 2 * HD_A))
        outs[1].append(va32.reshape(bp, s, H_A, 2 * HD_A))
        outs[2].append(kb32[:, s - n_keep:].reshape(bp, n_keep, H_B, HD_B))
        outs[3].append(vb32[:, s - n_keep:].reshape(bp, n_keep, H_B, HD_B))

        (qa, ka32, kab, va32, vab, ga, qb, kb32, kbb, vb32, vbb, gb, mg) = _project(
            xs, shift[bp:c_rows], scale[bp:c_rows], g_norm[l], w_in_bf, *gains, pm, bs, t)
        oa = _attn_a_sample(qa, kab, vab, ga, cache_a_k[l].reshape(bs, past * H_A, LANES), cache_a_v[l].reshape(bs, past * H_A, LANES),
                            bias_as, lamv, gsub, lam_init)
        to_rows = lambda c: jnp.transpose(c, (0, 2, 3, 1)).reshape(bs, W_B, lb)
        ob = _attn_b_sample(qb, kbb, vbb, gb, to_rows(cache_b_k[l]), to_rows(cache_b_v[l]), bias_bs)
        xs = _merge(xs, gate[bp:c_rows], oa, ob, mg, w_oa_bf, w_ob_bf, w_out_bf, bs, t)
        outs[4].append(ka32.reshape(bs, t, H_A, 2 * HD_A))
        outs[5].append(va32.reshape(bs, t, H_A, 2 * HD_A))
        outs[6].append(kb32.reshape(bs, t, H_B, HD_B))
        outs[7].append(vb32.reshape(bs, t, H_B, HD_B))

    return (xp, xs) + tuple(jnp.stack(o) for o in outs)
```

```python
import functools
import math

import jax
import jax.numpy as jnp
from jax import lax
from jax.experimental import pallas as pl
from jax.experimental.pallas import tpu as pltpu

CHUNK = 64
H_A = 4
HD_A = 64
W_A = H_A * 2 * HD_A
H_B = 8
HD_B = 64
W_B = H_B * HD_B
BAND_CHUNKS = 8
BAND_PAST = BAND_CHUNKS * CHUNK
REL_CLIP_B = 128
T5_BUCKETS = 32
T5_MAX_EXACT = 8
T5_MAX_DIST = 128
EPS = 1e-6
NEG = -1e30

LANES = 128
SUBLANES = 8
SEG = 512
NORM_GROUP = 256
MAX_DIRECT_LOGIT = 60.0
BF16_ROUND_MARGIN = 1.02
VMEM_LIMIT = 56 * 1024 * 1024

LOG2E = math.log2(math.e)

F32 = jnp.float32
BF16 = jnp.bfloat16


def _t5_bucket(rel):
    half = T5_BUCKETS // 2
    assert (T5_MAX_DIST // T5_MAX_EXACT) ** 2 == 2 ** (half - T5_MAX_EXACT)
    ret = jnp.where(rel > 0, half, 0)
    n = jnp.abs(rel)
    large = T5_MAX_EXACT + sum((n * n >= T5_MAX_EXACT ** 2 * 2 ** j).astype(jnp.int32)
                               for j in range(1, half - T5_MAX_EXACT))
    large = jnp.minimum(large, half - 1)
    return ret + jnp.where(n < T5_MAX_EXACT, n, large)


def _t5_run(t5_bias, first_rel, count):
    assert max(abs(first_rel), abs(first_rel + count)) < 2 ** 15
    rel = first_rel + jnp.arange(count, dtype=jnp.int32)
    hit = _t5_bucket(rel)[:, None, None] == jnp.arange(T5_BUCKETS, dtype=jnp.int32)[None, :, None]
    return jnp.sum(jnp.where(hit, t5_bias[None], 0.0), axis=1)


def _clipped_run(table, first_rel, count):
    n = table.shape[0]
    first = first_rel + (n - 1) // 2
    n_lo = min(max(-first, 0), count)
    n_hi = min(max(first + count - n, 0), count)
    mid = count - n_lo - n_hi
    parts = [jnp.broadcast_to(table[:1], (n_lo, table.shape[1])),
             table[first + n_lo:first + n_lo + mid] if mid > 0 else table[:0],
             jnp.broadcast_to(table[n - 1:], (n_hi, table.shape[1]))]
    return jnp.concatenate(parts, axis=0)


def _far_bucket(n):
    half = T5_BUCKETS // 2
    return min(T5_MAX_EXACT + int(math.log(n / T5_MAX_EXACT) / math.log(T5_MAX_DIST / T5_MAX_EXACT) * (half - T5_MAX_EXACT)), half - 1)


def _nt(a, b):
    return lax.dot_general(a, b, (((1,), (1,)), ((), ())), preferred_element_type=F32)


def _silu(x):
    return x * (1.0 / (1.0 + jnp.exp(-x)))


def _sigmoid(x):
    return 1.0 / (1.0 + jnp.exp(-x))


def _mod_kernel(c_ref, w_ref, b_ref, o_ref):
    c = c_ref[...]
    o_ref[...] = jnp.dot(_silu(c), w_ref[...], preferred_element_type=F32,
                         precision=lax.Precision.HIGHEST) + b_ref[...]


def _modulation(c_all, w_ada, b_ada):
    rows, d = c_all.shape
    n_out = w_ada.shape[1]
    tn = d
    return pl.pallas_call(
        _mod_kernel,
        grid=(n_out // tn,),
        in_specs=[pl.BlockSpec((rows, d), lambda j: (0, 0)),
                  pl.BlockSpec((d, tn), lambda j: (0, j)),
                  pl.BlockSpec((1, tn), lambda j: (0, j))],
        out_specs=pl.BlockSpec((rows, tn), lambda j: (0, j)),
        out_shape=jax.ShapeDtypeStruct((rows, n_out), F32),
        name="adaln_mod",
    )(c_all, w_ada, b_ada.reshape(1, n_out))


def _proj_kernel(x_ref, shift_ref, scale_ref, gn_ref, w_ref, gqa_ref, gka_ref, gqb_ref, gkb_ref, pm_ref,
                 qa_ref, ka32_ref, kab_ref, va32_ref, vab_ref, ga_ref,
                 qb_ref, kb32_ref, kbb_ref, vb32_ref, vbb_ref, gb_ref, mg_ref):
    nb, t, d = x_ref.shape
    rows = nb * t
    x = x_ref[...]
    ms = jnp.mean(x * x, axis=-1, keepdims=True)
    xn = x * lax.rsqrt(ms + EPS) * gn_ref[...]
    h = xn * (1.0 + scale_ref[...]) + shift_ref[...]
    hb = h.reshape(rows, d).astype(BF16)

    def seg(c, width=SEG):
        return jnp.dot(hb, w_ref[:, c * SEG:c * SEG + width], preferred_element_type=F32)

    def head_norm(y, g_ref):
        sq = y * y
        hi = sq.astype(BF16)
        pm = pm_ref[...]
        parts = []
        for c in range(SEG // NORM_GROUP):
            sl = slice(c * NORM_GROUP, (c + 1) * NORM_GROUP)
            gms = jnp.dot(hi[:, sl], pm, preferred_element_type=F32)
            parts.append(y[:, sl] * lax.rsqrt(gms + EPS))
        return jnp.concatenate(parts, axis=1) * g_ref[...]

    def put(ref, y):
        ref[...] = y.astype(ref.dtype).reshape(ref.shape)

    def put_by_head(ref, y):
        for hh in range(H_A):
            ref[:, pl.ds(hh, t, stride=H_A), :] = y[:, hh * LANES:(hh + 1) * LANES].reshape(nb, t, LANES)

    put(qa_ref, head_norm(seg(0), gqa_ref) * (HD_A ** -0.5 * LOG2E))
    ka = head_norm(seg(1), gka_ref)
    put_by_head(ka32_ref, ka)
    put(kab_ref, ka)
    va = seg(2)
    put_by_head(va32_ref, va)
    put(vab_ref, va)
    put(ga_ref, seg(3))
    put(qb_ref, head_norm(seg(4), gqb_ref) * (HD_B ** -0.5 * LOG2E))
    kb = head_norm(seg(5), gkb_ref)
    put(kb32_ref, kb)
    put(kbb_ref, kb)
    vb = seg(6)
    put(vb32_ref, vb)
    put(vbb_ref, vb)
    put(gb_ref, seg(7))
    for c in range(8, 12):
        mg_ref[:, :, (c - 8) * SEG:(c - 7) * SEG] = seg(c).astype(mg_ref.dtype).reshape(nb, t, SEG)


def _project(x, shift, scale, g_norm, w_in_bf, gqa, gka, gqb, gkb, pm, nb, t):
    bx, sx, d = x.shape
    n_cols = w_in_bf.shape[1]
    grid = (bx // nb, sx // t)
    tok = lambda w: pl.BlockSpec((nb, t, w), lambda b, i: (b, i, 0))
    per_b = pl.BlockSpec((nb, 1, d), lambda b, i: (b, 0, 0))
    const = lambda shp: pl.BlockSpec(shp, lambda b, i: (0,) * len(shp))
    sds = lambda w, dt: jax.ShapeDtypeStruct((bx, sx, w), dt)
    by_head = jax.ShapeDtypeStruct((bx, sx * H_A, LANES), F32)
    out_shape = (sds(SEG, BF16), by_head, sds(SEG, BF16), by_head, sds(SEG, BF16), sds(SEG, BF16),
                 sds(SEG, BF16), sds(SEG, F32), sds(SEG, BF16), sds(SEG, F32), sds(SEG, BF16), sds(SEG, BF16),
                 sds(4 * SEG, BF16))
    out_specs = tuple(pl.BlockSpec((nb, s.shape[1] // (sx // t), s.shape[2]), lambda b, i: (b, i, 0)) for s in out_shape)
    return pl.pallas_call(
        _proj_kernel,
        grid=grid,
        in_specs=[tok(d), per_b, per_b, const((1, d)),
                  pl.BlockSpec((d, n_cols), lambda b, i: (0, 0), pipeline_mode=pl.Buffered(1)),
                  const((1, SEG)), const((1, SEG)), const((1, SEG)), const((1, SEG)),
                  const((NORM_GROUP, NORM_GROUP))],
        out_specs=out_specs,
        out_shape=out_shape,
        compiler_params=pltpu.CompilerParams(dimension_semantics=("arbitrary", "arbitrary"),
                                             vmem_limit_bytes=VMEM_LIMIT),
        name="in_proj",
    )(x, shift, scale, g_norm.reshape(1, d), w_in_bf, gqa, gka, gqb, gkb, pm)


def _bias_tile_kernel(off_ref, gen_ref, o_ref, *, tq, nk, qbase, kbase0, kstep, band, nvalid):
    v = pl.program_id(1)
    g = gen_ref[0, 0]
    x = jnp.broadcast_to(g, (tq, g.shape[-1]))
    y = pltpu.roll(x, 0, 1, stride=1, stride_axis=0)
    y = (y[:, :nk] - off_ref[pl.program_id(0)]) * LOG2E
    row = lax.broadcasted_iota(jnp.int32, (tq, nk), 0)
    col = lax.broadcasted_iota(jnp.int32, (tq, nk), 1)
    qc = (qbase + row) // CHUNK
    kc = (kbase0 - v * kstep + col) // CHUNK
    y = jnp.where(kc <= qc, y, NEG)
    if band:
        y = jnp.where(kc >= qc - BAND_CHUNKS, y, NEG)
    y = jnp.where(col < nvalid, y, NEG)
    o_ref[0, 0] = y


def _bias_tiles(table_fn, offset, n_heads, tq, nk, n_var, qbase, kbase0, kstep, band, nvalid):
    length = 1 << (tq + nk - 1).bit_length()
    assert length >= tq + nk - 1 and min(kbase0 - (n_var - 1) * kstep, qbase) >= 0
    gen = jnp.stack([jnp.concatenate([table_fn(kbase0 - v * kstep - qbase, nk),
                                      table_fn(kbase0 - v * kstep - qbase + nk - length, length - nk)], axis=0)
                     for v in range(n_var)])
    gen = jnp.transpose(gen, (2, 0, 1)).astype(F32).reshape(n_heads, n_var, 1, length)
    return pl.pallas_call(
        functools.partial(_bias_tile_kernel, tq=tq, nk=nk, qbase=qbase, kbase0=kbase0, kstep=kstep,
                          band=band, nvalid=nvalid),
        grid=(n_heads, n_var),
        in_specs=[pl.BlockSpec(memory_space=pltpu.SMEM),
                  pl.BlockSpec((1, 1, 1, length), lambda h, v: (h, v, 0, 0))],
        out_specs=pl.BlockSpec((1, 1, tq, nk), lambda h, v: (h, v, 0, 0)),
        out_shape=jax.ShapeDtypeStruct((n_heads, n_var, tq, nk), F32),
        name="bias_tiles",
    )(offset.astype(F32), gen)


def _split_halves(q):
    lane = lax.broadcasted_iota(jnp.int32, q.shape, 1)
    zero = jnp.zeros_like(q)
    return jnp.where(lane < LANES // 2, q, zero), jnp.where(lane >= LANES // 2, q, zero)


def _online_update(carry, s, v):
    m, l, acc = carry
    m_new = jnp.maximum(m, jnp.max(s, axis=-1, keepdims=True))
    alpha = jnp.exp2(m - m_new)
    p = jnp.exp2(s - m_new)
    l = alpha * l + jnp.sum(p, axis=-1, keepdims=True)
    acc = alpha * acc + jnp.dot(p.astype(BF16), v, preferred_element_type=F32)
    return m_new, l, acc


def _diff_init(tq):
    return (jnp.full((tq, 1), -jnp.inf, F32), jnp.zeros((tq, 1), F32), jnp.zeros((tq, LANES), F32))


def _diff_finish(c0, c1, lamv_ref, gsub_ref, g, lam_init):
    lamv = lamv_ref[...]
    e1 = jnp.exp(jnp.sum(lamv[0:1] * lamv[1:2], axis=-1, keepdims=True))
    e2 = jnp.exp(jnp.sum(lamv[2:3] * lamv[3:4], axis=-1, keepdims=True))
    lam = e1 - e2 + lam_init
    o = c0[2] / c0[1] - lam * (c1[2] / c1[1])
    o = o * lax.rsqrt(jnp.mean(o * o, axis=-1, keepdims=True) + EPS) * gsub_ref[...]
    o = o * (1.0 - lam_init)
    return o * _silu(g.astype(F32))


def _lane_partial_sum(p):
    out = p[:, :LANES]
    for c in range(1, p.shape[1] // LANES):
        out = out + p[:, c * LANES:(c + 1) * LANES]
    return out


def _attn_a_prompt_kernel(par_ref, q_ref, k_ref, v_ref, g_ref, bias_ref, lamv_ref, gsub_ref, o_ref, l_ref, acc_ref,
                          *, tq, lam_init):
    i = pl.program_id(2)
    has_near = i > 0
    n_far = jnp.maximum(i - 1, 0)
    near = pl.multiple_of(n_far * tq, tq)
    diag = pl.multiple_of(i * tq, tq)
    half = tq // 2
    bounded = par_ref[0] > 0.5
    n_heads = q_ref.shape[2] // LANES
    hs = [slice(hh * LANES, (hh + 1) * LANES) for hh in range(n_heads)]

    def finish_head(hh, c0, c1):
        o = _diff_finish(c0, c1, lamv_ref, gsub_ref, g_ref[0, :, hs[hh]], lam_init)
        o_ref[0, :, hs[hh]] = o.astype(o_ref.dtype)

    def run_online():
        for hh in range(n_heads):
            q0, q1 = _split_halves(q_ref[0, :, hs[hh]])

            def tile(carry, start, bias, mask=None):
                c0, c1 = carry
                k = k_ref[0, pl.ds(start, tq), hs[hh]]
                v = v_ref[0, pl.ds(start, tq), hs[hh]]
                s0, s1 = _nt(q0, k), _nt(q1, k)
                if bias is not None:
                    b = bias_ref[hh, 0, :, bias] if mask is None else bias_ref[hh, 0, :, bias] + mask
                    s0, s1 = s0 + b, s1 + b
                return _online_update(c0, s0, v), _online_update(c1, s1, v)

            init = _diff_init(tq)
            carry = lax.fori_loop(0, n_far, lambda j, c: tile(c, pl.multiple_of(j * tq, tq), None), (init, init))
            carry = tile(carry, near, slice(0, tq), jnp.where(has_near, 0.0, NEG))
            c0, c1 = tile(carry, diag, slice(tq, 2 * tq))
            finish_head(hh, c0, c1)

    def run_bounded():
        qs = [_split_halves(q_ref[0, :, sl]) for sl in hs]
        l_ref[...] = jnp.zeros(l_ref.shape, F32)
        acc_ref[...] = jnp.zeros(acc_ref.shape, F32)

        def accumulate(hh, start, nk, rows=slice(None), bias_cols=None, mask=None):
            k = k_ref[0, pl.ds(start, nk), hs[hh]]
            v = v_ref[0, pl.ds(start, nk), hs[hh]]
            bias = None if bias_cols is None else bias_ref[hh, 0, rows, bias_cols]
            if mask is not None:
                bias = bias + mask
            for mp in range(2):
                s = _nt(qs[hh][mp][rows], k)
                p = jnp.exp2(s if bias is None else s + bias)
                l_ref[hh, mp, rows, :] += _lane_partial_sum(p)
                acc_ref[hh, mp, rows, :] += jnp.dot(p.astype(BF16), v, preferred_element_type=F32)

        def far(j, carry):
            for hh in range(n_heads):
                accumulate(hh, pl.multiple_of(j * tq, tq), tq)
            return carry

        lax.fori_loop(0, n_far, far, 0)
        near_mask = jnp.where(has_near, 0.0, NEG)
        for hh in range(n_heads):
            accumulate(hh, near, tq, bias_cols=slice(0, tq), mask=near_mask)
            accumulate(hh, diag, half, slice(0, half), slice(tq, tq + half))
            accumulate(hh, diag, tq, slice(half, tq), slice(tq, 2 * tq))
            c0 = (None, jnp.sum(l_ref[hh, 0], axis=-1, keepdims=True), acc_ref[hh, 0])
            c1 = (None, jnp.sum(l_ref[hh, 1], axis=-1, keepdims=True), acc_ref[hh, 1])
            finish_head(hh, c0, c1)

    pl.when(bounded)(run_bounded)
    pl.when(jnp.logical_not(bounded))(run_online)


def _logits_bounded(g_q, g_k, head_dim, table, offset):
    qk = math.sqrt(head_dim) * LOG2E * jnp.max(jnp.abs(g_q)) * jnp.max(jnp.abs(g_k)) * BF16_ROUND_MARGIN
    bias = jnp.max(jnp.abs(table - offset[None, :])) * LOG2E
    return (qk + bias <= MAX_DIRECT_LOGIT).astype(F32).reshape(1)


def _attn_a_prompt(qa, kab, vab, ga, bias, bounded, lamv, gsub, tq, lam_init):
    b, s, _ = qa.shape
    assert bias.shape[1] == 1 and bias.shape[3] == 2 * tq
    hps = HEADS_PER_STEP_A
    tile = pl.BlockSpec((1, tq, hps * LANES), lambda bi, h, i: (bi, i, h))
    single = pl.Buffered(1)
    full = pl.BlockSpec((1, s, hps * LANES), lambda bi, h, i: (bi, 0, h))
    return pl.pallas_call(
        functools.partial(_attn_a_prompt_kernel, tq=tq, lam_init=lam_init),
        grid=(b, H_A // hps, s // tq),
        in_specs=[pl.BlockSpec(memory_space=pltpu.SMEM), tile, full, full, tile,
                  pl.BlockSpec((hps, 1, tq, 2 * tq), lambda bi, h, i: (h, 0, 0, 0), pipeline_mode=single),
                  pl.BlockSpec((4, HD_A), lambda bi, h, i: (0, 0)),
                  pl.BlockSpec((1, LANES), lambda bi, h, i: (0, 0))],
        out_specs=tile,
        out_shape=jax.ShapeDtypeStruct((b, s, W_A), BF16),
        scratch_shapes=[pltpu.VMEM((hps, 2, tq, LANES), F32), pltpu.VMEM((hps, 2, tq, LANES), F32)],
        compiler_params=pltpu.CompilerParams(dimension_semantics=("arbitrary", "arbitrary", "arbitrary"),
                                             vmem_limit_bytes=VMEM_LIMIT),
        name="diff_attn_prompt",
    )(bounded, qa, kab, vab, ga, bias, lamv, gsub)


def _attn_a_sample_kernel(q_ref, kc_ref, vc_ref, kn_ref, vn_ref, g_ref, bias_ref, lamv_ref, gsub_ref, o_ref, *, lam_init):
    tq = q_ref.shape[1]
    nk = bias_ref.shape[3]
    past = kc_ref.shape[1] // H_A
    pad = jnp.zeros((nk - past - tq, LANES), BF16)
    for hh in range(H_A):
        sl = slice(hh * LANES, (hh + 1) * LANES)
        rows = pl.ds(hh, past, stride=H_A)
        qst = jnp.concatenate(_split_halves(q_ref[0, :, sl]), axis=0)
        k = jnp.concatenate([kc_ref[0, rows, :].astype(BF16), kn_ref[0, :, sl], pad], axis=0)
        v = jnp.concatenate([vc_ref[0, rows, :].astype(BF16), vn_ref[0, :, sl], pad], axis=0)
        bias = bias_ref[hh, 0]
        c = _online_update(_diff_init(2 * tq), _nt(qst, k) + jnp.concatenate([bias, bias], axis=0), v)
        c0, c1 = tuple(x[:tq] for x in c), tuple(x[tq:] for x in c)
        o_ref[0, :, sl] = _diff_finish(c0, c1, lamv_ref, gsub_ref, g_ref[0, :, sl], lam_init).astype(o_ref.dtype)


def _band_pair(q, k, v, g, bias_ref, bounded, first_head, piece_masks):
    t = q.shape[0]
    qst = jnp.concatenate(_split_halves(q), axis=0)
    bias = jnp.concatenate([bias_ref[first_head, 0], bias_ref[first_head + 1, 0]], axis=0)
    w = bias.shape[1] // (len(piece_masks) + 1)
    bias = jnp.concatenate([bias[:, c * w:(c + 1) * w] + m for c, m in enumerate(piece_masks)]
                           + [bias[:, len(piece_masks) * w:]], axis=1)
    s = _nt(qst, k) + bias
    if not bounded:
        s = s - jnp.max(s, axis=-1, keepdims=True)
    p = jnp.exp2(s)
    l = jnp.sum(p, axis=-1, keepdims=True)
    o2 = jnp.dot(p.astype(BF16), v, preferred_element_type=F32) / l
    lane = lax.broadcasted_iota(jnp.int32, (t, LANES), 1)
    o = jnp.where(lane < LANES // 2, o2[:t], o2[t:])
    return o * _silu(g.astype(F32))


def _attn_b_prompt_kernel(par_ref, q_ref, k_ref, v_ref, g_ref, bias_ref, o_ref, *, tq, n_prev):
    i = pl.program_id(2)
    tiles = q_ref.shape[1] // tq
    bounded = par_ref[0] > 0.5

    def run(direct):
        for sub in range(tiles):
            first = tiles * i + sub - n_prev
            rows = slice(sub * tq, (sub + 1) * tq)
            starts = [pl.multiple_of(jnp.maximum(first + c, 0) * tq, tq) for c in range(n_prev + 1)]
            masks = [jnp.where(first + c >= 0, 0.0, NEG) for c in range(n_prev)]
            for pp in range(q_ref.shape[2] // LANES):
                sl = slice(pp * LANES, (pp + 1) * LANES)
                k = jnp.concatenate([k_ref[0, pl.ds(st, tq), sl] for st in starts], axis=0)
                v = jnp.concatenate([v_ref[0, pl.ds(st, tq), sl] for st in starts], axis=0)
                o = _band_pair(q_ref[0, rows, sl], k, v, g_ref[0, rows, sl], bias_ref, direct, 2 * pp, masks)
                o_ref[0, rows, sl] = o.astype(o_ref.dtype)

    pl.when(bounded)(lambda: run(True))
    pl.when(jnp.logical_not(bounded))(lambda: run(False))


def _attn_b_prompt(qb, kbb, vbb, gb, bias, bounded, tq):
    b, s, w = qb.shape
    n_prev = bias.shape[3] // tq - 1
    assert bias.shape[1] == 1 and bias.shape[3] == (n_prev + 1) * tq
    tps = TILES_PER_STEP_B
    tile = pl.BlockSpec((1, tps * tq, w), lambda bi, p, i: (bi, i, p))
    full = pl.BlockSpec((1, s, w), lambda bi, p, i: (bi, 0, p))
    return pl.pallas_call(
        functools.partial(_attn_b_prompt_kernel, tq=tq, n_prev=n_prev),
        grid=(b, 1, s // (tps * tq)),
        in_specs=[pl.BlockSpec(memory_space=pltpu.SMEM), tile, full, full, tile,
                  pl.BlockSpec((H_B, 1, tq, (n_prev + 1) * tq), lambda bi, p, i: (0, 0, 0, 0),
                               pipeline_mode=pl.Buffered(1))],
        out_specs=tile,
        out_shape=jax.ShapeDtypeStruct((b, s, W_B), BF16),
        compiler_params=pltpu.CompilerParams(dimension_semantics=("arbitrary", "arbitrary", "arbitrary"),
                                             vmem_limit_bytes=VMEM_LIMIT),
        name="band_attn_prompt",
    )(bounded, qb, kbb, vbb, gb, bias)


def _attn_b_sample_kernel(q_ref, kc_ref, vc_ref, kn_ref, vn_ref, g_ref, bias_ref, o_ref):
    nk = bias_ref.shape[3]
    past, t = kc_ref.shape[2], kn_ref.shape[1]
    pad = jnp.zeros((nk - past - t, LANES), BF16)
    lane = lax.broadcasted_iota(jnp.int32, (t, LANES), 1)
    for pp in range(q_ref.shape[2] // LANES):
        sl = slice(pp * LANES, (pp + 1) * LANES)
        qst = jnp.concatenate(_split_halves(q_ref[0, :, sl]), axis=0)
        k_new = jnp.concatenate([kn_ref[0, :, sl], pad], axis=0)
        v_new = jnp.concatenate([vn_ref[0, :, sl], pad], axis=0)
        s = jnp.concatenate([jnp.dot(qst, kc_ref[0, sl, :].astype(BF16), preferred_element_type=F32),
                             _nt(qst, k_new)], axis=1)
        s = s + jnp.concatenate([bias_ref[2 * pp, 0], bias_ref[2 * pp + 1, 0]], axis=0)
        p = jnp.exp2(s - jnp.max(s, axis=-1, keepdims=True))
        l = jnp.sum(p, axis=-1, keepdims=True)
        pb = p.astype(BF16)
        o2 = (_nt(pb[:, :past], vc_ref[0, sl, :].astype(BF16))
              + jnp.dot(pb[:, past:], v_new, preferred_element_type=F32)) / l
        o = jnp.where(lane < LANES // 2, o2[:t], o2[t:])
        o_ref[0, :, sl] = (o * _silu(g_ref[0, :, sl].astype(F32))).astype(o_ref.dtype)


N_A_SAMPLE_INPUTS = 9
N_B_SAMPLE_INPUTS = 7


def _attn_sample_kernel(*refs, lam_init):
    a_in = refs[:N_A_SAMPLE_INPUTS]
    b_in = refs[N_A_SAMPLE_INPUTS:N_A_SAMPLE_INPUTS + N_B_SAMPLE_INPUTS]
    oa_ref, ob_ref = refs[N_A_SAMPLE_INPUTS + N_B_SAMPLE_INPUTS:]
    _attn_a_sample_kernel(*a_in, oa_ref, lam_init=lam_init)
    _attn_b_sample_kernel(*b_in, ob_ref)


def _attn_sample(qa, kab, vab, ga, cache_a_k, cache_a_v, bias_a, lamv, gsub,
                 qb, kbb, vbb, gb, cache_b_kt, cache_b_vt, bias_b, lam_init):
    b, t, _ = qa.shape
    assert cache_b_kt.shape[2] % LANES == 0
    new = pl.BlockSpec((1, t, SEG), lambda bi: (bi, 0, 0))
    whole = lambda x: pl.BlockSpec((1,) + x.shape[1:], lambda bi: (bi,) + (0,) * (x.ndim - 1))
    const = lambda x: pl.BlockSpec(x.shape, lambda bi: (0,) * x.ndim)
    return pl.pallas_call(
        functools.partial(_attn_sample_kernel, lam_init=lam_init),
        grid=(b,),
        in_specs=[new, whole(cache_a_k), whole(cache_a_v), new, new, new, const(bias_a), const(lamv), const(gsub),
                  new, whole(cache_b_kt), whole(cache_b_vt), new, new, new, const(bias_b)],
        out_specs=(new, new),
        out_shape=(jax.ShapeDtypeStruct((b, t, W_A), BF16), jax.ShapeDtypeStruct((b, t, W_B), BF16)),
        name="attn_sample",
    )(qa, cache_a_k, cache_a_v, kab, vab, ga, bias_a, lamv, gsub, qb, cache_b_kt, cache_b_vt, kbb, vbb, gb, bias_b)


def _merge_kernel(x_ref, gate_ref, oa_ref, ob_ref, mg_ref, woa_ref, wob_ref, wout_ref, o_ref):
    nb, t, d = x_ref.shape
    rows = nb * t
    ya = jnp.dot(oa_ref[...].reshape(rows, W_A), woa_ref[...], preferred_element_type=F32)
    yb = jnp.dot(ob_ref[...].reshape(rows, W_B), wob_ref[...], preferred_element_type=F32)
    mg = mg_ref[...].reshape(rows, 2 * d).astype(F32)
    m = _sigmoid(mg[:, :d]) * ya + _sigmoid(mg[:, d:]) * yb
    y = jnp.dot(m.astype(BF16), wout_ref[...], preferred_element_type=F32)
    o_ref[...] = x_ref[...] + gate_ref[...] * y.reshape(nb, t, d)


def _merge(x, gate, oa, ob, mg, w_oa_bf, w_ob_bf, w_out_bf, nb, t):
    bx, sx, d = x.shape
    tok = lambda w: pl.BlockSpec((nb, t, w), lambda b, i: (b, i, 0))
    const = lambda shp: pl.BlockSpec(shp, lambda b, i: (0,) * len(shp))
    return pl.pallas_call(
        _merge_kernel,
        grid=(bx // nb, sx // t),
        in_specs=[tok(d), pl.BlockSpec((nb, 1, d), lambda b, i: (b, 0, 0)), tok(W_A), tok(W_B), tok(2 * d),
                  const((W_A, d)), const((W_B, d)), const((d, d))],
        out_specs=tok(d),
        out_shape=jax.ShapeDtypeStruct((bx, sx, d), F32),
        compiler_params=pltpu.CompilerParams(dimension_semantics=("arbitrary", "arbitrary"),
                                             vmem_limit_bytes=VMEM_LIMIT),
        name="merge_out",
    )(x, gate, oa, ob, mg, w_oa_bf, w_ob_bf, w_out_bf)


TQ_A = 512
HEADS_PER_STEP_A = 4
TQ_B = 256
PREV_TILES_B = BAND_PAST // TQ_B
TILES_PER_STEP_B = 2
TM_PROJ = 512
TM_MERGE = 1024
QBASE = 2048


def kernel(x_prompt, x_sample, cache_a_k, cache_a_v, cache_b_k, cache_b_v, c_prompt, c_sample, g_norm, w_ada, b_ada, w_in, g_qa, g_ka, lam_q1, lam_k1, lam_q2, lam_k2, g_subln, t5_bias, g_qb, g_kb, rel_bias_b, w_oa, w_ob, w_out):
    xp, xs = x_prompt, x_sample
    bp, s, d = xp.shape
    bs, t, _ = xs.shape
    depth = w_in.shape[0]
    past = cache_a_k.shape[2]
    lb = cache_b_k.shape[2]
    n_keep = min(BAND_PAST, s)
    assert s % TM_PROJ == 0 and s % TM_MERGE == 0 and s % TQ_A == 0 and s % TQ_B == 0 and (TQ_A // 2) % CHUNK == 0 and TQ_B % CHUNK == 0
    assert PREV_TILES_B * TQ_B >= BAND_PAST and s % (TILES_PER_STEP_B * TQ_B) == 0 and past >= lb
    assert _far_bucket(TQ_A + 1) == T5_BUCKETS // 2 - 1

    r = jnp.arange(NORM_GROUP)
    pm = ((r[:, None] // HD_A) == (r[None, :] // HD_A)).astype(BF16) * (1.0 / HD_A)
    c_rows = bp + bs
    c_pad = -(-c_rows // SUBLANES) * SUBLANES
    c_all = jnp.concatenate([c_prompt, c_sample, jnp.zeros((c_pad - c_rows, d), F32)], axis=0)

    nk_as = -(-(past + t) // LANES) * LANES
    nk_bs = -(-(lb + t) // LANES) * LANES
    t5_fn = functools.partial(_t5_run, t5_bias)
    outs = [[] for _ in range(8)]
    for l in range(depth):
        lam_init = 0.8 - 0.6 * math.exp(-0.3 * l)
        rel_fn = functools.partial(_clipped_run, rel_bias_b[l])
        cfar = t5_bias[_t5_bucket(jnp.int32(-(TQ_A + 1)))]
        bias_ap = _bias_tiles(t5_fn, cfar, H_A, TQ_A, 2 * TQ_A, 1, QBASE, QBASE - TQ_A, 0, False, 2 * TQ_A)
        bias_bp = _bias_tiles(rel_fn, jnp.zeros((H_B,)), H_B, TQ_B, (PREV_TILES_B + 1) * TQ_B, 1, QBASE,
                              QBASE - PREV_TILES_B * TQ_B, 0, True, (PREV_TILES_B + 1) * TQ_B)
        bias_as = _bias_tiles(t5_fn, jnp.zeros((H_A,)), H_A, t, nk_as, 1, past, 0, 0, False, past + t)
        bias_bs = _bias_tiles(rel_fn, jnp.zeros((H_B,)), H_B, t, nk_bs, 1, past, past - lb, 0, True, lb + t)

        mod = _modulation(c_all, w_ada[l], b_ada[l])
        shift = mod[:, :d].reshape(c_pad, 1, d)
        scale = mod[:, d:2 * d].reshape(c_pad, 1, d)
        gate = mod[:, 2 * d:].reshape(c_pad, 1, d)
        w_in_bf = w_in[l].astype(BF16)
        w_oa_bf, w_ob_bf, w_out_bf = w_oa[l].astype(BF16), w_ob[l].astype(BF16), w_out[l].astype(BF16)
        tile8 = lambda g: jnp.tile(g, SEG // g.shape[0]).reshape(1, SEG)
        gains = (tile8(g_qa[l]), tile8(g_ka[l]), tile8(g_qb[l]), tile8(g_kb[l]))
        lamv = jnp.stack([lam_q1[l], lam_k1[l], lam_q2[l], lam_k2[l]])
        gsub = g_subln[l].reshape(1, LANES)

        (qa, ka32, kab, va32, vab, ga, qb, kb32, kbb, vb32, vbb, gb, mg) = _project(
            xp, shift[:bp], scale[:bp], g_norm[l], w_in_bf, *gains, pm, 1, TM_PROJ)
        bounded_a = _logits_bounded(g_qa[l], g_ka[l], HD_A, t5_bias, cfar)
        bounded_b = _logits_bounded(g_qb[l], g_kb[l], HD_B, rel_bias_b[l], jnp.zeros((H_B,)))
        oa = _attn_a_prompt(qa, kab, vab, ga, bias_ap, bounded_a, lamv, gsub, TQ_A, lam_init)
        ob = _attn_b_prompt(qb, kbb, vbb, gb, bias_bp, bounded_b, TQ_B)
        xp = _merge(xp, gate[:bp], oa, ob, mg, w_oa_bf, w_ob_bf, w_out_bf, 1, TM_MERGE)
        outs[0].append(ka32.reshape(bp, s, H_A, 2 * HD_A))
        outs[1].append(va32.reshape(bp, s, H_A, 2 * HD_A))
        outs[2].append(kb32[:, s - n_keep:].reshape(bp, n_keep, H_B, HD_B))
        outs[3].append(vb32[:, s - n_keep:].reshape(bp, n_keep, H_B, HD_B))

        (qa, ka32, kab, va32, vab, ga, qb, kb32, kbb, vb32, vbb, gb, mg) = _project(
            xs, shift[bp:c_rows], scale[bp:c_rows], g_norm[l], w_in_bf, *gains, pm, bs, t)
        by_head = lambda c: c.reshape(bs, past * H_A, LANES)
        to_rows = lambda c: jnp.transpose(c, (0, 2, 3, 1)).reshape(bs, W_B, lb)
        oa, ob = _attn_sample(qa, kab, vab, ga, by_head(cache_a_k[l]), by_head(cache_a_v[l]), bias_as, lamv, gsub,
                              qb, kbb, vbb, gb, to_rows(cache_b_k[l]), to_rows(cache_b_v[l]), bias_bs, lam_init)
        xs = _merge(xs, gate[bp:c_rows], oa, ob, mg, w_oa_bf, w_ob_bf, w_out_bf, bs, t)
        outs[4].append(ka32.reshape(bs, t, H_A, 2 * HD_A))
        outs[5].append(va32.reshape(bs, t, H_A, 2 * HD_A))
        outs[6].append(kb32.reshape(bs, t, H_B, HD_B))
        outs[7].append(vb32.reshape(bs, t, H_B, HD_B))

    return (xp, xs) + tuple(jnp.stack(o) for o in outs)
```

```python
import functools
import math

import jax
import jax.numpy as jnp
from jax import lax
from jax.experimental import pallas as pl
from jax.experimental.pallas import tpu as pltpu

CHUNK = 64
H_A = 4
HD_A = 64
W_A = H_A * 2 * HD_A
H_B = 8
HD_B = 64
W_B = H_B * HD_B
BAND_CHUNKS = 8
BAND_PAST = BAND_CHUNKS * CHUNK
REL_CLIP_B = 128
T5_BUCKETS = 32
T5_MAX_EXACT = 8
T5_MAX_DIST = 128
EPS = 1e-6
NEG = -1e30

LANES = 128
SUBLANES = 8
SEG = 512
NORM_GROUP = 256
MAX_DIRECT_LOGIT = 60.0
BF16_ROUND_MARGIN = 1.02
VMEM_LIMIT = 56 * 1024 * 1024

LOG2E = math.log2(math.e)

F32 = jnp.float32
BF16 = jnp.bfloat16


def _t5_bucket(rel):
    half = T5_BUCKETS // 2
    assert (T5_MAX_DIST // T5_MAX_EXACT) ** 2 == 2 ** (half - T5_MAX_EXACT)
    ret = jnp.where(rel > 0, half, 0)
    n = jnp.abs(rel)
    large = T5_MAX_EXACT + sum((n * n >= T5_MAX_EXACT ** 2 * 2 ** j).astype(jnp.int32)
                               for j in range(1, half - T5_MAX_EXACT))
    large = jnp.minimum(large, half - 1)
    return ret + jnp.where(n < T5_MAX_EXACT, n, large)


def _t5_run(t5_bias, first_rel, count):
    assert max(abs(first_rel), abs(first_rel + count)) < 2 ** 15
    rel = first_rel + jnp.arange(count, dtype=jnp.int32)
    hit = _t5_bucket(rel)[:, None, None] == jnp.arange(T5_BUCKETS, dtype=jnp.int32)[None, :, None]
    return jnp.sum(jnp.where(hit, t5_bias[None], 0.0), axis=1)


def _clipped_run(table, first_rel, count):
    n = table.shape[0]
    first = first_rel + (n - 1) // 2
    n_lo = min(max(-first, 0), count)
    n_hi = min(max(first + count - n, 0), count)
    mid = count - n_lo - n_hi
    parts = [jnp.broadcast_to(table[:1], (n_lo, table.shape[1])),
             table[first + n_lo:first + n_lo + mid] if mid > 0 else table[:0],
             jnp.broadcast_to(table[n - 1:], (n_hi, table.shape[1]))]
    return jnp.concatenate(parts, axis=0)


def _far_bucket(n):
    half = T5_BUCKETS // 2
    return min(T5_MAX_EXACT + int(math.log(n / T5_MAX_EXACT) / math.log(T5_MAX_DIST / T5_MAX_EXACT) * (half - T5_MAX_EXACT)), half - 1)


def _nt(a, b):
    return lax.dot_general(a, b, (((1,), (1,)), ((), ())), preferred_element_type=F32)


def _silu(x):
    return x * (1.0 / (1.0 + jnp.exp(-x)))


def _sigmoid(x):
    return 1.0 / (1.0 + jnp.exp(-x))


def _mod_kernel(c_ref, w_ref, b_ref, o_ref):
    a, w = _silu(c_ref[...]), w_ref[...]
    a_hi, w_hi = a.astype(BF16), w.astype(BF16)
    a_lo, w_lo = (a - a_hi.astype(F32)).astype(BF16), (w - w_hi.astype(F32)).astype(BF16)
    dot = functools.partial(jnp.dot, preferred_element_type=F32)
    o_ref[...] = dot(a_hi, w_hi) + (dot(a_hi, w_lo) + dot(a_lo, w_hi)) + b_ref[...]


def _modulation(c_all, w_ada, b_ada):
    rows, d = c_all.shape
    n_out = w_ada.shape[1]
    tn = d
    return pl.pallas_call(
        _mod_kernel,
        grid=(n_out // tn,),
        in_specs=[pl.BlockSpec((rows, d), lambda j: (0, 0)),
                  pl.BlockSpec((d, tn), lambda j: (0, j)),
                  pl.BlockSpec((1, tn), lambda j: (0, j))],
        out_specs=pl.BlockSpec((rows, tn), lambda j: (0, j)),
        out_shape=jax.ShapeDtypeStruct((rows, n_out), F32),
        name="adaln_mod",
    )(c_all, w_ada, b_ada.reshape(1, n_out))


def _proj_kernel(x_ref, shift_ref, scale_ref, gn_ref, w_ref, gqa_ref, gka_ref, gqb_ref, gkb_ref, pm_ref,
                 qa_ref, ka32_ref, kab_ref, va32_ref, vab_ref, ga_ref,
                 qb_ref, kb32_ref, kbb_ref, vb32_ref, vbb_ref, gb_ref, mg_ref):
    nb, t, d = x_ref.shape
    rows = nb * t
    x = x_ref[...]
    ms = jnp.mean(x * x, axis=-1, keepdims=True)
    xn = x * lax.rsqrt(ms + EPS) * gn_ref[...]
    h = xn * (1.0 + scale_ref[...]) + shift_ref[...]
    hb = h.reshape(rows, d).astype(BF16)

    def seg(c, width=SEG):
        return jnp.dot(hb, w_ref[:, c * SEG:c * SEG + width], preferred_element_type=F32)

    def head_norm(y, g_ref):
        sq = y * y
        hi = sq.astype(BF16)
        pm = pm_ref[...]
        parts = []
        for c in range(SEG // NORM_GROUP):
            sl = slice(c * NORM_GROUP, (c + 1) * NORM_GROUP)
            gms = jnp.dot(hi[:, sl], pm, preferred_element_type=F32)
            parts.append(y[:, sl] * lax.rsqrt(gms + EPS))
        return jnp.concatenate(parts, axis=1) * g_ref[...]

    def put(ref, y):
        ref[...] = y.astype(ref.dtype).reshape(ref.shape)

    def put_by_head(ref, y):
        for hh in range(H_A):
            ref[:, pl.ds(hh, t, stride=H_A), :] = y[:, hh * LANES:(hh + 1) * LANES].reshape(nb, t, LANES)

    put(qa_ref, head_norm(seg(0), gqa_ref) * (HD_A ** -0.5 * LOG2E))
    ka = head_norm(seg(1), gka_ref)
    put_by_head(ka32_ref, ka)
    put(kab_ref, ka)
    va = seg(2)
    put_by_head(va32_ref, va)
    put(vab_ref, va)
    put(ga_ref, seg(3))
    put(qb_ref, head_norm(seg(4), gqb_ref) * (HD_B ** -0.5 * LOG2E))
    kb = head_norm(seg(5), gkb_ref)
    put(kb32_ref, kb)
    put(kbb_ref, kb)
    vb = seg(6)
    put(vb32_ref, vb)
    put(vbb_ref, vb)
    put(gb_ref, seg(7))
    for c in range(8, 12):
        mg_ref[:, :, (c - 8) * SEG:(c - 7) * SEG] = seg(c).astype(mg_ref.dtype).reshape(nb, t, SEG)


def _project(x, shift, scale, g_norm, w_in_bf, gqa, gka, gqb, gkb, pm, nb, t):
    bx, sx, d = x.shape
    n_cols = w_in_bf.shape[1]
    grid = (bx // nb, sx // t)
    tok = lambda w: pl.BlockSpec((nb, t, w), lambda b, i: (b, i, 0))
    per_b = pl.BlockSpec((nb, 1, d), lambda b, i: (b, 0, 0))
    const = lambda shp: pl.BlockSpec(shp, lambda b, i: (0,) * len(shp))
    sds = lambda w, dt: jax.ShapeDtypeStruct((bx, sx, w), dt)
    by_head = jax.ShapeDtypeStruct((bx, sx * H_A, LANES), F32)
    out_shape = (sds(SEG, BF16), by_head, sds(SEG, BF16), by_head, sds(SEG, BF16), sds(SEG, BF16),
                 sds(SEG, BF16), sds(SEG, F32), sds(SEG, BF16), sds(SEG, F32), sds(SEG, BF16), sds(SEG, BF16),
                 sds(4 * SEG, BF16))
    out_specs = tuple(pl.BlockSpec((nb, s.shape[1] // (sx // t), s.shape[2]), lambda b, i: (b, i, 0)) for s in out_shape)
    return pl.pallas_call(
        _proj_kernel,
        grid=grid,
        in_specs=[tok(d), per_b, per_b, const((1, d)),
                  pl.BlockSpec((d, n_cols), lambda b, i: (0, 0), pipeline_mode=pl.Buffered(1)),
                  const((1, SEG)), const((1, SEG)), const((1, SEG)), const((1, SEG)),
                  const((NORM_GROUP, NORM_GROUP))],
        out_specs=out_specs,
        out_shape=out_shape,
        compiler_params=pltpu.CompilerParams(dimension_semantics=("arbitrary", "arbitrary"),
                                             vmem_limit_bytes=VMEM_LIMIT),
        name="in_proj",
    )(x, shift, scale, g_norm.reshape(1, d), w_in_bf, gqa, gka, gqb, gkb, pm)


def _bias_tile_kernel(off_ref, gen_ref, o_ref, *, tq, nk, qbase, kbase0, kstep, band, nvalid):
    v = pl.program_id(1)
    g = gen_ref[0, 0]
    x = jnp.broadcast_to(g, (tq, g.shape[-1]))
    y = pltpu.roll(x, 0, 1, stride=1, stride_axis=0)
    y = (y[:, :nk] - off_ref[pl.program_id(0)]) * LOG2E
    row = lax.broadcasted_iota(jnp.int32, (tq, nk), 0)
    col = lax.broadcasted_iota(jnp.int32, (tq, nk), 1)
    qc = (qbase + row) // CHUNK
    kc = (kbase0 - v * kstep + col) // CHUNK
    y = jnp.where(kc <= qc, y, NEG)
    if band:
        y = jnp.where(kc >= qc - BAND_CHUNKS, y, NEG)
    y = jnp.where(col < nvalid, y, NEG)
    o_ref[0, 0] = y


def _bias_tiles(table_fn, offset, n_heads, tq, nk, n_var, qbase, kbase0, kstep, band, nvalid):
    length = 1 << (tq + nk - 1).bit_length()
    assert length >= tq + nk - 1 and min(kbase0 - (n_var - 1) * kstep, qbase) >= 0
    gen = jnp.stack([jnp.concatenate([table_fn(kbase0 - v * kstep - qbase, nk),
                                      table_fn(kbase0 - v * kstep - qbase + nk - length, length - nk)], axis=0)
                     for v in range(n_var)])
    gen = jnp.transpose(gen, (2, 0, 1)).astype(F32).reshape(n_heads, n_var, 1, length)
    return pl.pallas_call(
        functools.partial(_bias_tile_kernel, tq=tq, nk=nk, qbase=qbase, kbase0=kbase0, kstep=kstep,
                          band=band, nvalid=nvalid),
        grid=(n_heads, n_var),
        in_specs=[pl.BlockSpec(memory_space=pltpu.SMEM),
                  pl.BlockSpec((1, 1, 1, length), lambda h, v: (h, v, 0, 0))],
        out_specs=pl.BlockSpec((1, 1, tq, nk), lambda h, v: (h, v, 0, 0)),
        out_shape=jax.ShapeDtypeStruct((n_heads, n_var, tq, nk), F32),
        name="bias_tiles",
    )(offset.astype(F32), gen)


def _split_halves(q):
    lane = lax.broadcasted_iota(jnp.int32, q.shape, 1)
    zero = jnp.zeros_like(q)
    return jnp.where(lane < LANES // 2, q, zero), jnp.where(lane >= LANES // 2, q, zero)


def _online_update(carry, s, v):
    m, l, acc = carry
    m_new = jnp.maximum(m, jnp.max(s, axis=-1, keepdims=True))
    alpha = jnp.exp2(m - m_new)
    p = jnp.exp2(s - m_new)
    l = alpha * l + jnp.sum(p, axis=-1, keepdims=True)
    acc = alpha * acc + jnp.dot(p.astype(BF16), v, preferred_element_type=F32)
    return m_new, l, acc


def _diff_init(tq):
    return (jnp.full((tq, 1), -jnp.inf, F32), jnp.zeros((tq, 1), F32), jnp.zeros((tq, LANES), F32))


def _diff_finish(c0, c1, lamv_ref, gsub_ref, g, lam_init):
    lamv = lamv_ref[...]
    e1 = jnp.exp(jnp.sum(lamv[0:1] * lamv[1:2], axis=-1, keepdims=True))
    e2 = jnp.exp(jnp.sum(lamv[2:3] * lamv[3:4], axis=-1, keepdims=True))
    lam = e1 - e2 + lam_init
    o = c0[2] / c0[1] - lam * (c1[2] / c1[1])
    o = o * lax.rsqrt(jnp.mean(o * o, axis=-1, keepdims=True) + EPS) * gsub_ref[...]
    o = o * (1.0 - lam_init)
    return o * _silu(g.astype(F32))


def _lane_partial_sum(p):
    out = p[:, :LANES]
    for c in range(1, p.shape[1] // LANES):
        out = out + p[:, c * LANES:(c + 1) * LANES]
    return out


def _attn_a_prompt_kernel(par_ref, q_ref, k_ref, v_ref, g_ref, bias_ref, lamv_ref, gsub_ref, o_ref, l_ref, acc_ref,
                          *, tq, lam_init):
    i = pl.program_id(2)
    has_near = i > 0
    n_far = jnp.maximum(i - 1, 0)
    near = pl.multiple_of(n_far * tq, tq)
    diag = pl.multiple_of(i * tq, tq)
    half = tq // 2
    bounded = par_ref[0] > 0.5
    n_heads = q_ref.shape[2] // LANES
    hs = [slice(hh * LANES, (hh + 1) * LANES) for hh in range(n_heads)]

    def finish_head(hh, c0, c1):
        o = _diff_finish(c0, c1, lamv_ref, gsub_ref, g_ref[0, :, hs[hh]], lam_init)
        o_ref[0, :, hs[hh]] = o.astype(o_ref.dtype)

    def run_online():
        for hh in range(n_heads):
            q0, q1 = _split_halves(q_ref[0, :, hs[hh]])

            def tile(carry, start, bias, mask=None):
                c0, c1 = carry
                k = k_ref[0, pl.ds(start, tq), hs[hh]]
                v = v_ref[0, pl.ds(start, tq), hs[hh]]
                s0, s1 = _nt(q0, k), _nt(q1, k)
                if bias is not None:
                    b = bias_ref[hh, 0, :, bias] if mask is None else bias_ref[hh, 0, :, bias] + mask
                    s0, s1 = s0 + b, s1 + b
                return _online_update(c0, s0, v), _online_update(c1, s1, v)

            init = _diff_init(tq)
            carry = lax.fori_loop(0, n_far, lambda j, c: tile(c, pl.multiple_of(j * tq, tq), None), (init, init))
            carry = tile(carry, near, slice(0, tq), jnp.where(has_near, 0.0, NEG))
            c0, c1 = tile(carry, diag, slice(tq, 2 * tq))
            finish_head(hh, c0, c1)

    def run_bounded():
        qs = [_split_halves(q_ref[0, :, sl]) for sl in hs]

        def accumulate(hh, start, nk, rows=slice(None), bias_cols=None, mask=None, init=False):
            k = k_ref[0, pl.ds(start, nk), hs[hh]]
            v = v_ref[0, pl.ds(start, nk), hs[hh]]
            bias = None if bias_cols is None else bias_ref[hh, 0, rows, bias_cols]
            if mask is not None:
                bias = bias + mask
            for mp in range(2):
                s = _nt(qs[hh][mp][rows], k)
                p = jnp.exp2(s if bias is None else s + bias)
                dl = _lane_partial_sum(p)
                da = jnp.dot(p.astype(BF16), v, preferred_element_type=F32)
                if init:
                    l_ref[hh, mp, rows, :] = dl
                    acc_ref[hh, mp, rows, :] = da
                else:
                    l_ref[hh, mp, rows, :] += dl
                    acc_ref[hh, mp, rows, :] += da

        near_mask = jnp.where(has_near, 0.0, NEG)
        for hh in range(n_heads):
            accumulate(hh, near, tq, bias_cols=slice(0, tq), mask=near_mask, init=True)

        def far(j, carry):
            for hh in range(n_heads):
                accumulate(hh, pl.multiple_of(j * tq, tq), tq)
            return carry

        lax.fori_loop(0, n_far, far, 0)
        for hh in range(n_heads):
            accumulate(hh, diag, half, slice(0, half), slice(tq, tq + half))
            accumulate(hh, diag, tq, slice(half, tq), slice(tq, 2 * tq))
            c0 = (None, jnp.sum(l_ref[hh, 0], axis=-1, keepdims=True), acc_ref[hh, 0])
            c1 = (None, jnp.sum(l_ref[hh, 1], axis=-1, keepdims=True), acc_ref[hh, 1])
            finish_head(hh, c0, c1)

    pl.when(bounded)(run_bounded)
    pl.when(jnp.logical_not(bounded))(run_online)


def _logits_bounded(g_q, g_k, head_dim, table, offset):
    qk = math.sqrt(head_dim) * LOG2E * jnp.max(jnp.abs(g_q)) * jnp.max(jnp.abs(g_k)) * BF16_ROUND_MARGIN
    bias = jnp.max(jnp.abs(table - offset[None, :])) * LOG2E
    return (qk + bias <= MAX_DIRECT_LOGIT).astype(F32).reshape(1)


def _attn_a_prompt(qa, kab, vab, ga, bias, bounded, lamv, gsub, tq, lam_init):
    b, s, _ = qa.shape
    assert bias.shape[1] == 1 and bias.shape[3] == 2 * tq
    hps = HEADS_PER_STEP_A
    tile = pl.BlockSpec((1, tq, hps * LANES), lambda bi, h, i: (bi, i, h))
    single = pl.Buffered(1)
    full = pl.BlockSpec((1, s, hps * LANES), lambda bi, h, i: (bi, 0, h))
    return pl.pallas_call(
        functools.partial(_attn_a_prompt_kernel, tq=tq, lam_init=lam_init),
        grid=(b, H_A // hps, s // tq),
        in_specs=[pl.BlockSpec(memory_space=pltpu.SMEM), tile, full, full, tile,
                  pl.BlockSpec((hps, 1, tq, 2 * tq), lambda bi, h, i: (h, 0, 0, 0), pipeline_mode=single),
                  pl.BlockSpec((4, HD_A), lambda bi, h, i: (0, 0)),
                  pl.BlockSpec((1, LANES), lambda bi, h, i: (0, 0))],
        out_specs=tile,
        out_shape=jax.ShapeDtypeStruct((b, s, W_A), BF16),
        scratch_shapes=[pltpu.VMEM((hps, 2, tq, LANES), F32), pltpu.VMEM((hps, 2, tq, LANES), F32)],
        compiler_params=pltpu.CompilerParams(dimension_semantics=("arbitrary", "arbitrary", "arbitrary"),
                                             vmem_limit_bytes=VMEM_LIMIT),
        name="diff_attn_prompt",
    )(bounded, qa, kab, vab, ga, bias, lamv, gsub)


def _attn_a_sample_kernel(q_ref, kc_ref, vc_ref, kn_ref, vn_ref, g_ref, bias_ref, lamv_ref, gsub_ref, o_ref, *, lam_init):
    tq = q_ref.shape[1]
    nk = bias_ref.shape[3]
    past = kc_ref.shape[1] // H_A
    pad = jnp.zeros((nk - past - tq, LANES), BF16)
    for hh in range(H_A):
        sl = slice(hh * LANES, (hh + 1) * LANES)
        rows = pl.ds(hh, past, stride=H_A)
        qst = jnp.concatenate(_split_halves(q_ref[0, :, sl]), axis=0)
        k = jnp.concatenate([kc_ref[0, rows, :].astype(BF16), kn_ref[0, :, sl], pad], axis=0)
        v = jnp.concatenate([vc_ref[0, rows, :].astype(BF16), vn_ref[0, :, sl], pad], axis=0)
        bias = bias_ref[hh, 0]
        c = _online_update(_diff_init(2 * tq), _nt(qst, k) + jnp.concatenate([bias, bias], axis=0), v)
        c0, c1 = tuple(x[:tq] for x in c), tuple(x[tq:] for x in c)
        o_ref[0, :, sl] = _diff_finish(c0, c1, lamv_ref, gsub_ref, g_ref[0, :, sl], lam_init).astype(o_ref.dtype)


def _band_pair(q, k, v, g, bias_ref, bounded, first_head, piece_masks):
    t = q.shape[0]
    qst = jnp.concatenate(_split_halves(q), axis=0)
    bias = jnp.concatenate([bias_ref[first_head, 0], bias_ref[first_head + 1, 0]], axis=0)
    w = bias.shape[1] // (len(piece_masks) + 1)
    bias = jnp.concatenate([bias[:, c * w:(c + 1) * w] + m for c, m in enumerate(piece_masks)]
                           + [bias[:, len(piece_masks) * w:]], axis=1)
    s = _nt(qst, k) + bias
    if not bounded:
        s = s - jnp.max(s, axis=-1, keepdims=True)
    p = jnp.exp2(s)
    l = jnp.sum(p, axis=-1, keepdims=True)
    o2 = jnp.dot(p.astype(BF16), v, preferred_element_type=F32) / l
    lane = lax.broadcasted_iota(jnp.int32, (t, LANES), 1)
    o = jnp.where(lane < LANES // 2, o2[:t], o2[t:])
    return o * _silu(g.astype(F32))


def _attn_b_prompt_kernel(par_ref, q_ref, k_ref, v_ref, g_ref, bias_ref, o_ref, *, tq, n_prev):
    i = pl.program_id(2)
    tiles = q_ref.shape[1] // tq
    bounded = par_ref[0] > 0.5

    def run(direct):
        for sub in range(tiles):
            first = tiles * i + sub - n_prev
            rows = slice(sub * tq, (sub + 1) * tq)
            starts = [pl.multiple_of(jnp.maximum(first + c, 0) * tq, tq) for c in range(n_prev + 1)]
            masks = [jnp.where(first + c >= 0, 0.0, NEG) for c in range(n_prev)]
            for pp in range(q_ref.shape[2] // LANES):
                sl = slice(pp * LANES, (pp + 1) * LANES)
                k = jnp.concatenate([k_ref[0, pl.ds(st, tq), sl] for st in starts], axis=0)
                v = jnp.concatenate([v_ref[0, pl.ds(st, tq), sl] for st in starts], axis=0)
                o = _band_pair(q_ref[0, rows, sl], k, v, g_ref[0, rows, sl], bias_ref, direct, 2 * pp, masks)
                o_ref[0, rows, sl] = o.astype(o_ref.dtype)

    pl.when(bounded)(lambda: run(True))
    pl.when(jnp.logical_not(bounded))(lambda: run(False))


def _attn_b_prompt(qb, kbb, vbb, gb, bias, bounded, tq):
    b, s, w = qb.shape
    n_prev = bias.shape[3] // tq - 1
    assert bias.shape[1] == 1 and bias.shape[3] == (n_prev + 1) * tq
    tps = TILES_PER_STEP_B
    tile = pl.BlockSpec((1, tps * tq, w), lambda bi, p, i: (bi, i, p))
    full = pl.BlockSpec((1, s, w), lambda bi, p, i: (bi, 0, p))
    return pl.pallas_call(
        functools.partial(_attn_b_prompt_kernel, tq=tq, n_prev=n_prev),
        grid=(b, 1, s // (tps * tq)),
        in_specs=[pl.BlockSpec(memory_space=pltpu.SMEM), tile, full, full, tile,
                  pl.BlockSpec((H_B, 1, tq, (n_prev + 1) * tq), lambda bi, p, i: (0, 0, 0, 0),
                               pipeline_mode=pl.Buffered(1))],
        out_specs=tile,
        out_shape=jax.ShapeDtypeStruct((b, s, W_B), BF16),
        compiler_params=pltpu.CompilerParams(dimension_semantics=("arbitrary", "arbitrary", "arbitrary"),
                                             vmem_limit_bytes=VMEM_LIMIT),
        name="band_attn_prompt",
    )(bounded, qb, kbb, vbb, gb, bias)


def _attn_b_sample_kernel(q_ref, kc_ref, vc_ref, kn_ref, vn_ref, g_ref, bias_ref, o_ref):
    nk = bias_ref.shape[3]
    past, t = kc_ref.shape[2], kn_ref.shape[1]
    pad = jnp.zeros((nk - past - t, LANES), BF16)
    lane = lax.broadcasted_iota(jnp.int32, (t, LANES), 1)
    for pp in range(q_ref.shape[2] // LANES):
        sl = slice(pp * LANES, (pp + 1) * LANES)
        qst = jnp.concatenate(_split_halves(q_ref[0, :, sl]), axis=0)
        k_new = jnp.concatenate([kn_ref[0, :, sl], pad], axis=0)
        v_new = jnp.concatenate([vn_ref[0, :, sl], pad], axis=0)
        s = jnp.concatenate([jnp.dot(qst, kc_ref[0, sl, :].astype(BF16), preferred_element_type=F32),
                             _nt(qst, k_new)], axis=1)
        s = s + jnp.concatenate([bias_ref[2 * pp, 0], bias_ref[2 * pp + 1, 0]], axis=0)
        p = jnp.exp2(s - jnp.max(s, axis=-1, keepdims=True))
        l = jnp.sum(p, axis=-1, keepdims=True)
        pb = p.astype(BF16)
        o2 = (_nt(pb[:, :past], vc_ref[0, sl, :].astype(BF16))
              + jnp.dot(pb[:, past:], v_new, preferred_element_type=F32)) / l
        o = jnp.where(lane < LANES // 2, o2[:t], o2[t:])
        o_ref[0, :, sl] = (o * _silu(g_ref[0, :, sl].astype(F32))).astype(o_ref.dtype)


N_A_SAMPLE_INPUTS = 9
N_B_SAMPLE_INPUTS = 7


def _attn_sample_kernel(*refs, lam_init):
    a_in = refs[:N_A_SAMPLE_INPUTS]
    b_in = refs[N_A_SAMPLE_INPUTS:N_A_SAMPLE_INPUTS + N_B_SAMPLE_INPUTS]
    oa_ref, ob_ref = refs[N_A_SAMPLE_INPUTS + N_B_SAMPLE_INPUTS:]
    _attn_a_sample_kernel(*a_in, oa_ref, lam_init=lam_init)
    _attn_b_sample_kernel(*b_in, ob_ref)


def _attn_sample(qa, kab, vab, ga, cache_a_k, cache_a_v, bias_a, lamv, gsub,
                 qb, kbb, vbb, gb, cache_b_kt, cache_b_vt, bias_b, lam_init):
    b, t, _ = qa.shape
    assert cache_b_kt.shape[2] % LANES == 0
    new = pl.BlockSpec((1, t, SEG), lambda bi: (bi, 0, 0))
    whole = lambda x: pl.BlockSpec((1,) + x.shape[1:], lambda bi: (bi,) + (0,) * (x.ndim - 1))
    const = lambda x: pl.BlockSpec(x.shape, lambda bi: (0,) * x.ndim)
    return pl.pallas_call(
        functools.partial(_attn_sample_kernel, lam_init=lam_init),
        grid=(b,),
        in_specs=[new, whole(cache_a_k), whole(cache_a_v), new, new, new, const(bias_a), const(lamv), const(gsub),
                  new, whole(cache_b_kt), whole(cache_b_vt), new, new, new, const(bias_b)],
        out_specs=(new, new),
        out_shape=(jax.ShapeDtypeStruct((b, t, W_A), BF16), jax.ShapeDtypeStruct((b, t, W_B), BF16)),
        name="attn_sample",
    )(qa, cache_a_k, cache_a_v, kab, vab, ga, bias_a, lamv, gsub, qb, cache_b_kt, cache_b_vt, kbb, vbb, gb, bias_b)


def _merge_kernel(x_ref, gate_ref, oa_ref, ob_ref, mg_ref, woa_ref, wob_ref, wout_ref, o_ref):
    nb, t, d = x_ref.shape
    rows = nb * t
    ya = jnp.dot(oa_ref[...].reshape(rows, W_A), woa_ref[...], preferred_element_type=F32)
    yb = jnp.dot(ob_ref[...].reshape(rows, W_B), wob_ref[...], preferred_element_type=F32)
    mg = mg_ref[...].reshape(rows, 2 * d).astype(F32)
    m = _sigmoid(mg[:, :d]) * ya + _sigmoid(mg[:, d:]) * yb
    y = jnp.dot(m.astype(BF16), wout_ref[...], preferred_element_type=F32)
    o_ref[...] = x_ref[...] + gate_ref[...] * y.reshape(nb, t, d)


def _merge(x, gate, oa, ob, mg, w_oa_bf, w_ob_bf, w_out_bf, nb, t):
    bx, sx, d = x.shape
    tok = lambda w: pl.BlockSpec((nb, t, w), lambda b, i: (b, i, 0))
    const = lambda shp: pl.BlockSpec(shp, lambda b, i: (0,) * len(shp))
    return pl.pallas_call(
        _merge_kernel,
        grid=(bx // nb, sx // t),
        in_specs=[tok(d), pl.BlockSpec((nb, 1, d), lambda b, i: (b, 0, 0)), tok(W_A), tok(W_B), tok(2 * d),
                  const((W_A, d)), const((W_B, d)), const((d, d))],
        out_specs=tok(d),
        out_shape=jax.ShapeDtypeStruct((bx, sx, d), F32),
        compiler_params=pltpu.CompilerParams(dimension_semantics=("arbitrary", "arbitrary"),
                                             vmem_limit_bytes=VMEM_LIMIT),
        name="merge_out",
    )(x, gate, oa, ob, mg, w_oa_bf, w_ob_bf, w_out_bf)


TQ_A = 512
HEADS_PER_STEP_A = 4
TQ_B = 256
PREV_TILES_B = BAND_PAST // TQ_B
TILES_PER_STEP_B = 2
TM_PROJ = 512
TM_MERGE = 1024
QBASE = 2048


def kernel(x_prompt, x_sample, cache_a_k, cache_a_v, cache_b_k, cache_b_v, c_prompt, c_sample, g_norm, w_ada, b_ada, w_in, g_qa, g_ka, lam_q1, lam_k1, lam_q2, lam_k2, g_subln, t5_bias, g_qb, g_kb, rel_bias_b, w_oa, w_ob, w_out):
    xp, xs = x_prompt, x_sample
    bp, s, d = xp.shape
    bs, t, _ = xs.shape
    depth = w_in.shape[0]
    past = cache_a_k.shape[2]
    lb = cache_b_k.shape[2]
    n_keep = min(BAND_PAST, s)
    assert s % TM_PROJ == 0 and s % TM_MERGE == 0 and s % TQ_A == 0 and s % TQ_B == 0 and (TQ_A // 2) % CHUNK == 0 and TQ_B % CHUNK == 0
    assert PREV_TILES_B * TQ_B >= BAND_PAST and s % (TILES_PER_STEP_B * TQ_B) == 0 and past >= lb
    assert _far_bucket(TQ_A + 1) == T5_BUCKETS // 2 - 1

    r = jnp.arange(NORM_GROUP)
    pm = ((r[:, None] // HD_A) == (r[None, :] // HD_A)).astype(BF16) * (1.0 / HD_A)
    c_rows = bp + bs
    c_pad = -(-c_rows // SUBLANES) * SUBLANES
    c_all = jnp.concatenate([c_prompt, c_sample, jnp.zeros((c_pad - c_rows, d), F32)], axis=0)

    nk_as = -(-(past + t) // LANES) * LANES
    nk_bs = -(-(lb + t) // LANES) * LANES
    t5_fn = functools.partial(_t5_run, t5_bias)
    outs = [[] for _ in range(8)]
    for l in range(depth):
        lam_init = 0.8 - 0.6 * math.exp(-0.3 * l)
        rel_fn = functools.partial(_clipped_run, rel_bias_b[l])
        cfar = t5_bias[_t5_bucket(jnp.int32(-(TQ_A + 1)))]
        bias_ap = _bias_tiles(t5_fn, cfar, H_A, TQ_A, 2 * TQ_A, 1, QBASE, QBASE - TQ_A, 0, False, 2 * TQ_A)
        bias_bp = _bias_tiles(rel_fn, jnp.zeros((H_B,)), H_B, TQ_B, (PREV_TILES_B + 1) * TQ_B, 1, QBASE,
                              QBASE - PREV_TILES_B * TQ_B, 0, True, (PREV_TILES_B + 1) * TQ_B)
        bias_as = _bias_tiles(t5_fn, jnp.zeros((H_A,)), H_A, t, nk_as, 1, past, 0, 0, False, past + t)
        bias_bs = _bias_tiles(rel_fn, jnp.zeros((H_B,)), H_B, t, nk_bs, 1, past, past - lb, 0, True, lb + t)

        mod = _modulation(c_all, w_ada[l], b_ada[l])
        shift = mod[:, :d].reshape(c_pad, 1, d)
        scale = mod[:, d:2 * d].reshape(c_pad, 1, d)
        gate = mod[:, 2 * d:].reshape(c_pad, 1, d)
        w_in_bf = w_in[l].astype(BF16)
        w_oa_bf, w_ob_bf, w_out_bf = w_oa[l].astype(BF16), w_ob[l].astype(BF16), w_out[l].astype(BF16)
        tile8 = lambda g: jnp.tile(g, SEG // g.shape[0]).reshape(1, SEG)
        gains = (tile8(g_qa[l]), tile8(g_ka[l]), tile8(g_qb[l]), tile8(g_kb[l]))
        lamv = jnp.stack([lam_q1[l], lam_k1[l], lam_q2[l], lam_k2[l]])
        gsub = g_subln[l].reshape(1, LANES)

        (qa, ka32, kab, va32, vab, ga, qb, kb32, kbb, vb32, vbb, gb, mg) = _project(
            xp, shift[:bp], scale[:bp], g_norm[l], w_in_bf, *gains, pm, 1, TM_PROJ)
        bounded_a = _logits_bounded(g_qa[l], g_ka[l], HD_A, t5_bias, cfar)
        bounded_b = _logits_bounded(g_qb[l], g_kb[l], HD_B, rel_bias_b[l], jnp.zeros((H_B,)))
        oa = _attn_a_prompt(qa, kab, vab, ga, bias_ap, bounded_a, lamv, gsub, TQ_A, lam_init)
        ob = _attn_b_prompt(qb, kbb, vbb, gb, bias_bp, bounded_b, TQ_B)
        xp = _merge(xp, gate[:bp], oa, ob, mg, w_oa_bf, w_ob_bf, w_out_bf, 1, TM_MERGE)
        outs[0].append(ka32.reshape(bp, s, H_A, 2 * HD_A))
        outs[1].append(va32.reshape(bp, s, H_A, 2 * HD_A))
        outs[2].append(kb32[:, s - n_keep:].reshape(bp, n_keep, H_B, HD_B))
        outs[3].append(vb32[:, s - n_keep:].reshape(bp, n_keep, H_B, HD_B))

        (qa, ka32, kab, va32, vab, ga, qb, kb32, kbb, vb32, vbb, gb, mg) = _project(
            xs, shift[bp:c_rows], scale[bp:c_rows], g_norm[l], w_in_bf, *gains, pm, bs, t)
        by_head = lambda c: c.reshape(bs, past * H_A, LANES)
        to_rows = lambda c: jnp.transpose(c, (0, 2, 3, 1)).reshape(bs, W_B, lb)
        oa, ob = _attn_sample(qa, kab, vab, ga, by_head(cache_a_k[l]), by_head(cache_a_v[l]), bias_as, lamv, gsub,
                              qb, kbb, vbb, gb, to_rows(cache_b_k[l]), to_rows(cache_b_v[l]), bias_bs, lam_init)
        xs = _merge(xs, gate[bp:c_rows], oa, ob, mg, w_oa_bf, w_ob_bf, w_out_bf, bs, t)
        outs[4].append(ka32.reshape(bs, t, H_A, 2 * HD_A))
        outs[5].append(va32.reshape(bs, t, H_A, 2 * HD_A))
        outs[6].append(kb32.reshape(bs, t, H_B, HD_B))
        outs[7].append(vb32.reshape(bs, t, H_B, HD_B))

    return (xp, xs) + tuple(jnp.stack(o) for o in outs)
```

```python
import functools
import math

import jax
import jax.numpy as jnp
from jax import lax
from jax.experimental import pallas as pl
from jax.experimental.pallas import tpu as pltpu

CHUNK = 64
H_A = 4
HD_A = 64
W_A = H_A * 2 * HD_A
H_B = 8
HD_B = 64
W_B = H_B * HD_B
BAND_CHUNKS = 8
BAND_PAST = BAND_CHUNKS * CHUNK
REL_CLIP_B = 128
T5_BUCKETS = 32
T5_MAX_EXACT = 8
T5_MAX_DIST = 128
EPS = 1e-6
NEG = -1e30

LANES = 128
SUBLANES = 8
SEG = 512
NORM_GROUP = 256
MAX_DIRECT_LOGIT = 60.0
BF16_ROUND_MARGIN = 1.02
VMEM_LIMIT = 56 * 1024 * 1024

LOG2E = math.log2(math.e)

F32 = jnp.float32
BF16 = jnp.bfloat16


def _t5_bucket(rel):
    half = T5_BUCKETS // 2
    assert (T5_MAX_DIST // T5_MAX_EXACT) ** 2 == 2 ** (half - T5_MAX_EXACT)
    ret = jnp.where(rel > 0, half, 0)
    n = jnp.abs(rel)
    large = T5_MAX_EXACT + sum((n * n >= T5_MAX_EXACT ** 2 * 2 ** j).astype(jnp.int32)
                               for j in range(1, half - T5_MAX_EXACT))
    large = jnp.minimum(large, half - 1)
    return ret + jnp.where(n < T5_MAX_EXACT, n, large)


def _t5_run(t5_bias, first_rel, count):
    assert max(abs(first_rel), abs(first_rel + count)) < 2 ** 15
    rel = first_rel + jnp.arange(count, dtype=jnp.int32)
    hit = _t5_bucket(rel)[:, None, None] == jnp.arange(T5_BUCKETS, dtype=jnp.int32)[None, :, None]
    return jnp.sum(jnp.where(hit, t5_bias[None], 0.0), axis=1)


def _clipped_run(table, first_rel, count):
    n = table.shape[0]
    first = first_rel + (n - 1) // 2
    n_lo = min(max(-first, 0), count)
    n_hi = min(max(first + count - n, 0), count)
    mid = count - n_lo - n_hi
    parts = [jnp.broadcast_to(table[:1], (n_lo, table.shape[1])),
             table[first + n_lo:first + n_lo + mid] if mid > 0 else table[:0],
             jnp.broadcast_to(table[n - 1:], (n_hi, table.shape[1]))]
    return jnp.concatenate(parts, axis=0)


def _far_bucket(n):
    half = T5_BUCKETS // 2
    return min(T5_MAX_EXACT + int(math.log(n / T5_MAX_EXACT) / math.log(T5_MAX_DIST / T5_MAX_EXACT) * (half - T5_MAX_EXACT)), half - 1)


def _nt(a, b):
    return lax.dot_general(a, b, (((1,), (1,)), ((), ())), preferred_element_type=F32)


def _silu(x):
    return x * (1.0 / (1.0 + jnp.exp(-x)))


def _sigmoid(x):
    return 1.0 / (1.0 + jnp.exp(-x))


def _mod_kernel(c_ref, w_ref, b_ref, o_ref):
    a, w = _silu(c_ref[...]), w_ref[...]
    a_hi, w_hi = a.astype(BF16), w.astype(BF16)
    a_lo, w_lo = (a - a_hi.astype(F32)).astype(BF16), (w - w_hi.astype(F32)).astype(BF16)
    dot = functools.partial(jnp.dot, preferred_element_type=F32)
    o_ref[...] = dot(a_hi, w_hi) + (dot(a_hi, w_lo) + dot(a_lo, w_hi)) + b_ref[...]


def _modulation(c_all, w_ada, b_ada):
    rows, d = c_all.shape
    n_out = w_ada.shape[1]
    tn = d
    return pl.pallas_call(
        _mod_kernel,
        grid=(n_out // tn,),
        in_specs=[pl.BlockSpec((rows, d), lambda j: (0, 0)),
                  pl.BlockSpec((d, tn), lambda j: (0, j)),
                  pl.BlockSpec((1, tn), lambda j: (0, j))],
        out_specs=pl.BlockSpec((rows, tn), lambda j: (0, j)),
        out_shape=jax.ShapeDtypeStruct((rows, n_out), F32),
        name="adaln_mod",
    )(c_all, w_ada, b_ada.reshape(1, n_out))


def _proj_kernel(x_ref, shift_ref, scale_ref, gn_ref, w_ref, gqa_ref, gka_ref, gqb_ref, gkb_ref, pm_ref,
                 qa_ref, ka32_ref, kab_ref, va32_ref, vab_ref, ga_ref,
                 qb_ref, kb32_ref, kbb_ref, vb32_ref, vbb_ref, gb_ref, mg_ref):
    nb, t, d = x_ref.shape
    rows = nb * t
    x = x_ref[...]
    ms = jnp.mean(x * x, axis=-1, keepdims=True)
    xn = x * lax.rsqrt(ms + EPS) * gn_ref[...]
    h = xn * (1.0 + scale_ref[...]) + shift_ref[...]
    hb = h.reshape(rows, d).astype(BF16)

    def seg(c, width=SEG):
        return jnp.dot(hb, w_ref[:, c * SEG:c * SEG + width], preferred_element_type=F32)

    def head_norm(y, g_ref):
        sq = y * y
        hi = sq.astype(BF16)
        pm = pm_ref[...]
        parts = []
        for c in range(SEG // NORM_GROUP):
            sl = slice(c * NORM_GROUP, (c + 1) * NORM_GROUP)
            gms = jnp.dot(hi[:, sl], pm, preferred_element_type=F32)
            parts.append(y[:, sl] * lax.rsqrt(gms + EPS))
        return jnp.concatenate(parts, axis=1) * g_ref[...]

    def put(ref, y):
        ref[...] = y.astype(ref.dtype).reshape(ref.shape)

    def put_by_head(ref, y):
        for hh in range(H_A):
            ref[:, pl.ds(hh, t, stride=H_A), :] = y[:, hh * LANES:(hh + 1) * LANES].reshape(nb, t, LANES)

    put(qa_ref, head_norm(seg(0), gqa_ref) * (HD_A ** -0.5 * LOG2E))
    ka = head_norm(seg(1), gka_ref)
    put_by_head(ka32_ref, ka)
    put(kab_ref, ka)
    va = seg(2)
    put_by_head(va32_ref, va)
    put(vab_ref, va)
    put(ga_ref, seg(3))
    put(qb_ref, head_norm(seg(4), gqb_ref) * (HD_B ** -0.5 * LOG2E))
    kb = head_norm(seg(5), gkb_ref)
    put(kb32_ref, kb)
    put(kbb_ref, kb)
    vb = seg(6)
    put(vb32_ref, vb)
    put(vbb_ref, vb)
    put(gb_ref, seg(7))
    for c in range(8, 12):
        mg_ref[:, :, (c - 8) * SEG:(c - 7) * SEG] = seg(c).astype(mg_ref.dtype).reshape(nb, t, SEG)


def _project(x, shift, scale, g_norm, w_in_bf, gqa, gka, gqb, gkb, pm, nb, t):
    bx, sx, d = x.shape
    n_cols = w_in_bf.shape[1]
    grid = (bx // nb, sx // t)
    tok = lambda w: pl.BlockSpec((nb, t, w), lambda b, i: (b, i, 0))
    per_b = pl.BlockSpec((nb, 1, d), lambda b, i: (b, 0, 0))
    const = lambda shp: pl.BlockSpec(shp, lambda b, i: (0,) * len(shp))
    sds = lambda w, dt: jax.ShapeDtypeStruct((bx, sx, w), dt)
    by_head = jax.ShapeDtypeStruct((bx, sx * H_A, LANES), F32)
    out_shape = (sds(SEG, BF16), by_head, sds(SEG, BF16), by_head, sds(SEG, BF16), sds(SEG, BF16),
                 sds(SEG, BF16), sds(SEG, F32), sds(SEG, BF16), sds(SEG, F32), sds(SEG, BF16), sds(SEG, BF16),
                 sds(4 * SEG, BF16))
    out_specs = tuple(pl.BlockSpec((nb, s.shape[1] // (sx // t), s.shape[2]), lambda b, i: (b, i, 0)) for s in out_shape)
    return pl.pallas_call(
        _proj_kernel,
        grid=grid,
        in_specs=[tok(d), per_b, per_b, const((1, d)),
                  pl.BlockSpec((d, n_cols), lambda b, i: (0, 0), pipeline_mode=pl.Buffered(1)),
                  const((1, SEG)), const((1, SEG)), const((1, SEG)), const((1, SEG)),
                  const((NORM_GROUP, NORM_GROUP))],
        out_specs=out_specs,
        out_shape=out_shape,
        compiler_params=pltpu.CompilerParams(dimension_semantics=("arbitrary", "arbitrary"),
                                             vmem_limit_bytes=VMEM_LIMIT),
        name="in_proj",
    )(x, shift, scale, g_norm.reshape(1, d), w_in_bf, gqa, gka, gqb, gkb, pm)


def _bias_tile_kernel(*refs, specs):
    h = pl.program_id(0)
    n = len(specs)
    for (n_heads, tq, nk, qbase, kbase, band, nvalid), off_ref, gen_ref, o_ref in zip(
            specs, refs[0:2 * n:2], refs[1:2 * n:2], refs[2 * n:]):
        def expand(off_ref=off_ref, gen_ref=gen_ref, o_ref=o_ref, tq=tq, nk=nk, qbase=qbase, kbase=kbase,
                   band=band, nvalid=nvalid):
            g = gen_ref[0, 0]
            x = jnp.broadcast_to(g, (tq, g.shape[-1]))
            y = pltpu.roll(x, 0, 1, stride=1, stride_axis=0)
            y = (y[:, :nk] - off_ref[h]) * LOG2E
            row = lax.broadcasted_iota(jnp.int32, (tq, nk), 0)
            col = lax.broadcasted_iota(jnp.int32, (tq, nk), 1)
            qc = (qbase + row) // CHUNK
            kc = (kbase + col) // CHUNK
            y = jnp.where(kc <= qc, y, NEG)
            if band:
                y = jnp.where(kc >= qc - BAND_CHUNKS, y, NEG)
            o_ref[0, 0] = jnp.where(col < nvalid, y, NEG)

        pl.when(h < n_heads)(expand)


def _bias_tiles(requests):
    max_heads = max(r[2] for r in requests)
    operands, in_specs, out_specs, out_shape, specs = [], [], [], [], []
    for table_fn, offset, n_heads, tq, nk, qbase, kbase, band, nvalid in requests:
        length = 1 << (tq + nk - 1).bit_length()
        assert length >= tq + nk - 1 and min(kbase, qbase) >= 0
        gen = jnp.concatenate([table_fn(kbase - qbase, nk), table_fn(kbase - qbase + nk - length, length - nk)], axis=0)
        gen = jnp.transpose(gen, (1, 0)).astype(F32).reshape(n_heads, 1, 1, length)
        head = lambda h, n_heads=n_heads: (jnp.minimum(h, n_heads - 1), 0, 0, 0)
        operands += [jnp.pad(offset.astype(F32), (0, max_heads - n_heads)), gen]
        in_specs += [pl.BlockSpec(memory_space=pltpu.SMEM), pl.BlockSpec((1, 1, 1, length), head)]
        out_specs.append(pl.BlockSpec((1, 1, tq, nk), head))
        out_shape.append(jax.ShapeDtypeStruct((n_heads, 1, tq, nk), F32))
        specs.append((n_heads, tq, nk, qbase, kbase, band, nvalid))
    return pl.pallas_call(
        functools.partial(_bias_tile_kernel, specs=tuple(specs)),
        grid=(max_heads,),
        in_specs=in_specs,
        out_specs=tuple(out_specs),
        out_shape=tuple(out_shape),
        name="bias_tiles",
    )(*operands)


def _split_halves(q):
    lane = lax.broadcasted_iota(jnp.int32, q.shape, 1)
    zero = jnp.zeros_like(q)
    return jnp.where(lane < LANES // 2, q, zero), jnp.where(lane >= LANES // 2, q, zero)


def _online_update(carry, s, v):
    m, l, acc = carry
    m_new = jnp.maximum(m, jnp.max(s, axis=-1, keepdims=True))
    alpha = jnp.exp2(m - m_new)
    p = jnp.exp2(s - m_new)
    l = alpha * l + jnp.sum(p, axis=-1, keepdims=True)
    acc = alpha * acc + jnp.dot(p.astype(BF16), v, preferred_element_type=F32)
    return m_new, l, acc


def _diff_init(tq):
    return (jnp.full((tq, 1), -jnp.inf, F32), jnp.zeros((tq, 1), F32), jnp.zeros((tq, LANES), F32))


def _diff_finish(c0, c1, lamv_ref, gsub_ref, g, lam_init):
    lamv = lamv_ref[...]
    e1 = jnp.exp(jnp.sum(lamv[0:1] * lamv[1:2], axis=-1, keepdims=True))
    e2 = jnp.exp(jnp.sum(lamv[2:3] * lamv[3:4], axis=-1, keepdims=True))
    lam = e1 - e2 + lam_init
    o = c0[2] / c0[1] - lam * (c1[2] / c1[1])
    o = o * lax.rsqrt(jnp.mean(o * o, axis=-1, keepdims=True) + EPS) * gsub_ref[...]
    o = o * (1.0 - lam_init)
    return o * _silu(g.astype(F32))


def _lane_partial_sum(p):
    out = p[:, :LANES]
    for c in range(1, p.shape[1] // LANES):
        out = out + p[:, c * LANES:(c + 1) * LANES]
    return out


def _attn_a_prompt_kernel(par_ref, q_ref, k_ref, v_ref, g_ref, bias_ref, lamv_ref, gsub_ref, o_ref, l_ref, acc_ref,
                          *, tq, lam_init):
    i = pl.program_id(2)
    has_near = i > 0
    n_far = jnp.maximum(i - 1, 0)
    near = pl.multiple_of(n_far * tq, tq)
    diag = pl.multiple_of(i * tq, tq)
    half = tq // 2
    bounded = par_ref[0] > 0.5
    n_heads = q_ref.shape[2] // LANES
    hs = [slice(hh * LANES, (hh + 1) * LANES) for hh in range(n_heads)]

    def finish_head(hh, c0, c1):
        o = _diff_finish(c0, c1, lamv_ref, gsub_ref, g_ref[0, :, hs[hh]], lam_init)
        o_ref[0, :, hs[hh]] = o.astype(o_ref.dtype)

    def run_online():
        for hh in range(n_heads):
            q0, q1 = _split_halves(q_ref[0, :, hs[hh]])

            def tile(carry, start, bias, mask=None):
                c0, c1 = carry
                k = k_ref[0, pl.ds(start, tq), hs[hh]]
                v = v_ref[0, pl.ds(start, tq), hs[hh]]
                s0, s1 = _nt(q0, k), _nt(q1, k)
                if bias is not None:
                    b = bias_ref[hh, 0, :, bias] if mask is None else bias_ref[hh, 0, :, bias] + mask
                    s0, s1 = s0 + b, s1 + b
                return _online_update(c0, s0, v), _online_update(c1, s1, v)

            init = _diff_init(tq)
            carry = lax.fori_loop(0, n_far, lambda j, c: tile(c, pl.multiple_of(j * tq, tq), None), (init, init))
            carry = tile(carry, near, slice(0, tq), jnp.where(has_near, 0.0, NEG))
            c0, c1 = tile(carry, diag, slice(tq, 2 * tq))
            finish_head(hh, c0, c1)

    def run_bounded():
        qs = [_split_halves(q_ref[0, :, sl]) for sl in hs]

        def accumulate(hh, start, nk, rows=slice(None), bias_cols=None, mask=None, init=False):
            k = k_ref[0, pl.ds(start, nk), hs[hh]]
            v = v_ref[0, pl.ds(start, nk), hs[hh]]
            bias = None if bias_cols is None else bias_ref[hh, 0, rows, bias_cols]
            if mask is not None:
                bias = bias + mask
            for mp in range(2):
                s = _nt(qs[hh][mp][rows], k)
                p = jnp.exp2(s if bias is None else s + bias)
                dl = _lane_partial_sum(p)
                da = jnp.dot(p.astype(BF16), v, preferred_element_type=F32)
                if init:
                    l_ref[hh, mp, rows, :] = dl
                    acc_ref[hh, mp, rows, :] = da
                else:
                    l_ref[hh, mp, rows, :] += dl
                    acc_ref[hh, mp, rows, :] += da

        near_mask = jnp.where(has_near, 0.0, NEG)
        for hh in range(n_heads):
            accumulate(hh, near, tq, bias_cols=slice(0, tq), mask=near_mask, init=True)

        def far(j, carry):
            for hh in range(n_heads):
                accumulate(hh, pl.multiple_of(j * tq, tq), tq)
            return carry

        lax.fori_loop(0, n_far, far, 0)
        for hh in range(n_heads):
            accumulate(hh, diag, half, slice(0, half), slice(tq, tq + half))
            accumulate(hh, diag, tq, slice(half, tq), slice(tq, 2 * tq))
            c0 = (None, jnp.sum(l_ref[hh, 0], axis=-1, keepdims=True), acc_ref[hh, 0])
            c1 = (None, jnp.sum(l_ref[hh, 1], axis=-1, keepdims=True), acc_ref[hh, 1])
            finish_head(hh, c0, c1)

    pl.when(bounded)(run_bounded)
    pl.when(jnp.logical_not(bounded))(run_online)


def _logits_bounded(g_q, g_k, head_dim, table, offset):
    qk = math.sqrt(head_dim) * LOG2E * jnp.max(jnp.abs(g_q)) * jnp.max(jnp.abs(g_k)) * BF16_ROUND_MARGIN
    bias = jnp.max(jnp.abs(table - offset[None, :])) * LOG2E
    return (qk + bias <= MAX_DIRECT_LOGIT).astype(F32).reshape(1)


def _attn_a_prompt(qa, kab, vab, ga, bias, bounded, lamv, gsub, tq, lam_init):
    b, s, _ = qa.shape
    assert bias.shape[1] == 1 and bias.shape[3] == 2 * tq
    hps = HEADS_PER_STEP_A
    tile = pl.BlockSpec((1, tq, hps * LANES), lambda bi, h, i: (bi, i, h))
    single = pl.Buffered(1)
    full = pl.BlockSpec((1, s, hps * LANES), lambda bi, h, i: (bi, 0, h))
    return pl.pallas_call(
        functools.partial(_attn_a_prompt_kernel, tq=tq, lam_init=lam_init),
        grid=(b, H_A // hps, s // tq),
        in_specs=[pl.BlockSpec(memory_space=pltpu.SMEM), tile, full, full, tile,
                  pl.BlockSpec((hps, 1, tq, 2 * tq), lambda bi, h, i: (h, 0, 0, 0), pipeline_mode=single),
                  pl.BlockSpec((4, HD_A), lambda bi, h, i: (0, 0)),
                  pl.BlockSpec((1, LANES), lambda bi, h, i: (0, 0))],
        out_specs=tile,
        out_shape=jax.ShapeDtypeStruct((b, s, W_A), BF16),
        scratch_shapes=[pltpu.VMEM((hps, 2, tq, LANES), F32), pltpu.VMEM((hps, 2, tq, LANES), F32)],
        compiler_params=pltpu.CompilerParams(dimension_semantics=("arbitrary", "arbitrary", "arbitrary"),
                                             vmem_limit_bytes=VMEM_LIMIT),
        name="diff_attn_prompt",
    )(bounded, qa, kab, vab, ga, bias, lamv, gsub)


def _attn_a_sample_kernel(q_ref, kc_ref, vc_ref, kn_ref, vn_ref, g_ref, bias_ref, lamv_ref, gsub_ref, o_ref, *, lam_init):
    tq = q_ref.shape[1]
    nk = bias_ref.shape[3]
    past = kc_ref.shape[1] // H_A
    pad = jnp.zeros((nk - past - tq, LANES), BF16)
    for hh in range(H_A):
        sl = slice(hh * LANES, (hh + 1) * LANES)
        rows = pl.ds(hh, past, stride=H_A)
        qst = jnp.concatenate(_split_halves(q_ref[0, :, sl]), axis=0)
        k = jnp.concatenate([kc_ref[0, rows, :].astype(BF16), kn_ref[0, :, sl], pad], axis=0)
        v = jnp.concatenate([vc_ref[0, rows, :].astype(BF16), vn_ref[0, :, sl], pad], axis=0)
        bias = bias_ref[hh, 0]
        c = _online_update(_diff_init(2 * tq), _nt(qst, k) + jnp.concatenate([bias, bias], axis=0), v)
        c0, c1 = tuple(x[:tq] for x in c), tuple(x[tq:] for x in c)
        o_ref[0, :, sl] = _diff_finish(c0, c1, lamv_ref, gsub_ref, g_ref[0, :, sl], lam_init).astype(o_ref.dtype)


def _band_pair(q, k, v, g, bias_ref, bounded, first_head, piece_masks):
    t = q.shape[0]
    qst = jnp.concatenate(_split_halves(q), axis=0)
    bias = jnp.concatenate([bias_ref[first_head, 0], bias_ref[first_head + 1, 0]], axis=0)
    w = bias.shape[1] // (len(piece_masks) + 1)
    bias = jnp.concatenate([bias[:, c * w:(c + 1) * w] + m for c, m in enumerate(piece_masks)]
                           + [bias[:, len(piece_masks) * w:]], axis=1)
    s = _nt(qst, k) + bias
    if not bounded:
        s = s - jnp.max(s, axis=-1, keepdims=True)
    p = jnp.exp2(s)
    l = jnp.sum(p, axis=-1, keepdims=True)
    o2 = jnp.dot(p.astype(BF16), v, preferred_element_type=F32) / l
    lane = lax.broadcasted_iota(jnp.int32, (t, LANES), 1)
    o = jnp.where(lane < LANES // 2, o2[:t], o2[t:])
    return o * _silu(g.astype(F32))


def _attn_b_prompt_kernel(par_ref, q_ref, k_ref, v_ref, g_ref, bias_ref, o_ref, *, tq, n_prev):
    i = pl.program_id(2)
    tiles = q_ref.shape[1] // tq
    bounded = par_ref[0] > 0.5

    def run(direct):
        for sub in range(tiles):
            first = tiles * i + sub - n_prev
            rows = slice(sub * tq, (sub + 1) * tq)
            starts = [pl.multiple_of(jnp.maximum(first + c, 0) * tq, tq) for c in range(n_prev + 1)]
            masks = [jnp.where(first + c >= 0, 0.0, NEG) for c in range(n_prev)]
            for pp in range(q_ref.shape[2] // LANES):
                sl = slice(pp * LANES, (pp + 1) * LANES)
                k = jnp.concatenate([k_ref[0, pl.ds(st, tq), sl] for st in starts], axis=0)
                v = jnp.concatenate([v_ref[0, pl.ds(st, tq), sl] for st in starts], axis=0)
                o = _band_pair(q_ref[0, rows, sl], k, v, g_ref[0, rows, sl], bias_ref, direct, 2 * pp, masks)
                o_ref[0, rows, sl] = o.astype(o_ref.dtype)

    pl.when(bounded)(lambda: run(True))
    pl.when(jnp.logical_not(bounded))(lambda: run(False))


def _attn_b_prompt(qb, kbb, vbb, gb, bias, bounded, tq):
    b, s, w = qb.shape
    n_prev = bias.shape[3] // tq - 1
    assert bias.shape[1] == 1 and bias.shape[3] == (n_prev + 1) * tq
    tps = TILES_PER_STEP_B
    tile = pl.BlockSpec((1, tps * tq, w), lambda bi, p, i: (bi, i, p))
    full = pl.BlockSpec((1, s, w), lambda bi, p, i: (bi, 0, p))
    return pl.pallas_call(
        functools.partial(_attn_b_prompt_kernel, tq=tq, n_prev=n_prev),
        grid=(b, 1, s // (tps * tq)),
        in_specs=[pl.BlockSpec(memory_space=pltpu.SMEM), tile, full, full, tile,
                  pl.BlockSpec((H_B, 1, tq, (n_prev + 1) * tq), lambda bi, p, i: (0, 0, 0, 0),
                               pipeline_mode=pl.Buffered(1))],
        out_specs=tile,
        out_shape=jax.ShapeDtypeStruct((b, s, W_B), BF16),
        compiler_params=pltpu.CompilerParams(dimension_semantics=("arbitrary", "arbitrary", "arbitrary"),
                                             vmem_limit_bytes=VMEM_LIMIT),
        name="band_attn_prompt",
    )(bounded, qb, kbb, vbb, gb, bias)


def _attn_b_sample_kernel(q_ref, kc_ref, vc_ref, kn_ref, vn_ref, g_ref, bias_ref, o_ref):
    nk = bias_ref.shape[3]
    past, t = kc_ref.shape[2], kn_ref.shape[1]
    pad = jnp.zeros((nk - past - t, LANES), BF16)
    lane = lax.broadcasted_iota(jnp.int32, (t, LANES), 1)
    for pp in range(q_ref.shape[2] // LANES):
        sl = slice(pp * LANES, (pp + 1) * LANES)
        qst = jnp.concatenate(_split_halves(q_ref[0, :, sl]), axis=0)
        k_new = jnp.concatenate([kn_ref[0, :, sl], pad], axis=0)
        v_new = jnp.concatenate([vn_ref[0, :, sl], pad], axis=0)
        s = jnp.concatenate([jnp.dot(qst, kc_ref[0, sl, :].astype(BF16), preferred_element_type=F32),
                             _nt(qst, k_new)], axis=1)
        s = s + jnp.concatenate([bias_ref[2 * pp, 0], bias_ref[2 * pp + 1, 0]], axis=0)
        p = jnp.exp2(s - jnp.max(s, axis=-1, keepdims=True))
        l = jnp.sum(p, axis=-1, keepdims=True)
        pb = p.astype(BF16)
        o2 = (_nt(pb[:, :past], vc_ref[0, sl, :].astype(BF16))
              + jnp.dot(pb[:, past:], v_new, preferred_element_type=F32)) / l
        o = jnp.where(lane < LANES // 2, o2[:t], o2[t:])
        o_ref[0, :, sl] = (o * _silu(g_ref[0, :, sl].astype(F32))).astype(o_ref.dtype)


N_A_SAMPLE_INPUTS = 9
N_B_SAMPLE_INPUTS = 7


def _attn_sample_kernel(*refs, lam_init):
    a_in = refs[:N_A_SAMPLE_INPUTS]
    b_in = refs[N_A_SAMPLE_INPUTS:N_A_SAMPLE_INPUTS + N_B_SAMPLE_INPUTS]
    oa_ref, ob_ref = refs[N_A_SAMPLE_INPUTS + N_B_SAMPLE_INPUTS:]
    _attn_a_sample_kernel(*a_in, oa_ref, lam_init=lam_init)
    _attn_b_sample_kernel(*b_in, ob_ref)


def _attn_sample(qa, kab, vab, ga, cache_a_k, cache_a_v, bias_a, lamv, gsub,
                 qb, kbb, vbb, gb, cache_b_kt, cache_b_vt, bias_b, lam_init):
    b, t, _ = qa.shape
    assert cache_b_kt.shape[2] % LANES == 0
    new = pl.BlockSpec((1, t, SEG), lambda bi: (bi, 0, 0))
    whole = lambda x: pl.BlockSpec((1,) + x.shape[1:], lambda bi: (bi,) + (0,) * (x.ndim - 1))
    const = lambda x: pl.BlockSpec(x.shape, lambda bi: (0,) * x.ndim)
    return pl.pallas_call(
        functools.partial(_attn_sample_kernel, lam_init=lam_init),
        grid=(b,),
        in_specs=[new, whole(cache_a_k), whole(cache_a_v), new, new, new, const(bias_a), const(lamv), const(gsub),
                  new, whole(cache_b_kt), whole(cache_b_vt), new, new, new, const(bias_b)],
        out_specs=(new, new),
        out_shape=(jax.ShapeDtypeStruct((b, t, W_A), BF16), jax.ShapeDtypeStruct((b, t, W_B), BF16)),
        name="attn_sample",
    )(qa, cache_a_k, cache_a_v, kab, vab, ga, bias_a, lamv, gsub, qb, cache_b_kt, cache_b_vt, kbb, vbb, gb, bias_b)


def _merge_kernel(x_ref, gate_ref, oa_ref, ob_ref, mg_ref, woa_ref, wob_ref, wout_ref, o_ref):
    nb, t, d = x_ref.shape
    rows = nb * t
    ya = jnp.dot(oa_ref[...].reshape(rows, W_A), woa_ref[...], preferred_element_type=F32)
    yb = jnp.dot(ob_ref[...].reshape(rows, W_B), wob_ref[...], preferred_element_type=F32)
    mg = mg_ref[...].reshape(rows, 2 * d).astype(F32)
    m = _sigmoid(mg[:, :d]) * ya + _sigmoid(mg[:, d:]) * yb
    y = jnp.dot(m.astype(BF16), wout_ref[...], preferred_element_type=F32)
    o_ref[...] = x_ref[...] + gate_ref[...] * y.reshape(nb, t, d)


def _merge(x, gate, oa, ob, mg, w_oa_bf, w_ob_bf, w_out_bf, nb, t):
    bx, sx, d = x.shape
    tok = lambda w: pl.BlockSpec((nb, t, w), lambda b, i: (b, i, 0))
    const = lambda shp: pl.BlockSpec(shp, lambda b, i: (0,) * len(shp))
    return pl.pallas_call(
        _merge_kernel,
        grid=(bx // nb, sx // t),
        in_specs=[tok(d), pl.BlockSpec((nb, 1, d), lambda b, i: (b, 0, 0)), tok(W_A), tok(W_B), tok(2 * d),
                  const((W_A, d)), const((W_B, d)), const((d, d))],
        out_specs=tok(d),
        out_shape=jax.ShapeDtypeStruct((bx, sx, d), F32),
        compiler_params=pltpu.CompilerParams(dimension_semantics=("arbitrary", "arbitrary"),
                                             vmem_limit_bytes=VMEM_LIMIT),
        name="merge_out",
    )(x, gate, oa, ob, mg, w_oa_bf, w_ob_bf, w_out_bf)


TQ_A = 512
HEADS_PER_STEP_A = 4
TQ_B = 256
PREV_TILES_B = BAND_PAST // TQ_B
TILES_PER_STEP_B = 2
TM_PROJ = 512
TM_MERGE = 1024
QBASE = 2048


def kernel(x_prompt, x_sample, cache_a_k, cache_a_v, cache_b_k, cache_b_v, c_prompt, c_sample, g_norm, w_ada, b_ada, w_in, g_qa, g_ka, lam_q1, lam_k1, lam_q2, lam_k2, g_subln, t5_bias, g_qb, g_kb, rel_bias_b, w_oa, w_ob, w_out):
    xp, xs = x_prompt, x_sample
    bp, s, d = xp.shape
    bs, t, _ = xs.shape
    depth = w_in.shape[0]
    past = cache_a_k.shape[2]
    lb = cache_b_k.shape[2]
    n_keep = min(BAND_PAST, s)
    assert s % TM_PROJ == 0 and s % TM_MERGE == 0 and s % TQ_A == 0 and s % TQ_B == 0 and (TQ_A // 2) % CHUNK == 0 and TQ_B % CHUNK == 0
    assert PREV_TILES_B * TQ_B >= BAND_PAST and s % (TILES_PER_STEP_B * TQ_B) == 0 and past >= lb
    assert _far_bucket(TQ_A + 1) == T5_BUCKETS // 2 - 1

    r = jnp.arange(NORM_GROUP)
    pm = ((r[:, None] // HD_A) == (r[None, :] // HD_A)).astype(BF16) * (1.0 / HD_A)
    c_rows = bp + bs
    c_pad = -(-c_rows // SUBLANES) * SUBLANES
    c_all = jnp.concatenate([c_prompt, c_sample, jnp.zeros((c_pad - c_rows, d), F32)], axis=0)

    nk_as = -(-(past + t) // LANES) * LANES
    nk_bs = -(-(lb + t) // LANES) * LANES
    t5_fn = functools.partial(_t5_run, t5_bias)
    outs = [[] for _ in range(8)]
    for l in range(depth):
        lam_init = 0.8 - 0.6 * math.exp(-0.3 * l)
        rel_fn = functools.partial(_clipped_run, rel_bias_b[l])
        cfar = t5_bias[_t5_bucket(jnp.int32(-(TQ_A + 1)))]
        nk_bp = (PREV_TILES_B + 1) * TQ_B
        bias_ap, bias_bp, bias_as, bias_bs = _bias_tiles([
            (t5_fn, cfar, H_A, TQ_A, 2 * TQ_A, QBASE, QBASE - TQ_A, False, 2 * TQ_A),
            (rel_fn, jnp.zeros((H_B,)), H_B, TQ_B, nk_bp, QBASE, QBASE - PREV_TILES_B * TQ_B, True, nk_bp),
            (t5_fn, jnp.zeros((H_A,)), H_A, t, nk_as, past, 0, False, past + t),
            (rel_fn, jnp.zeros((H_B,)), H_B, t, nk_bs, past, past - lb, True, lb + t)])

        mod = _modulation(c_all, w_ada[l], b_ada[l])
        shift = mod[:, :d].reshape(c_pad, 1, d)
        scale = mod[:, d:2 * d].reshape(c_pad, 1, d)
        gate = mod[:, 2 * d:].reshape(c_pad, 1, d)
        w_in_bf = w_in[l].astype(BF16)
        w_oa_bf, w_ob_bf, w_out_bf = w_oa[l].astype(BF16), w_ob[l].astype(BF16), w_out[l].astype(BF16)
        tile8 = lambda g: jnp.tile(g, SEG // g.shape[0]).reshape(1, SEG)
        gains = (tile8(g_qa[l]), tile8(g_ka[l]), tile8(g_qb[l]), tile8(g_kb[l]))
        lamv = jnp.stack([lam_q1[l], lam_k1[l], lam_q2[l], lam_k2[l]])
        gsub = g_subln[l].reshape(1, LANES)

        (qa, ka32, kab, va32, vab, ga, qb, kb32, kbb, vb32, vbb, gb, mg) = _project(
            xp, shift[:bp], scale[:bp], g_norm[l], w_in_bf, *gains, pm, 1, TM_PROJ)
        bounded_a = _logits_bounded(g_qa[l], g_ka[l], HD_A, t5_bias, cfar)
        bounded_b = _logits_bounded(g_qb[l], g_kb[l], HD_B, rel_bias_b[l], jnp.zeros((H_B,)))
        oa = _attn_a_prompt(qa, kab, vab, ga, bias_ap, bounded_a, lamv, gsub, TQ_A, lam_init)
        ob = _attn_b_prompt(qb, kbb, vbb, gb, bias_bp, bounded_b, TQ_B)
        xp = _merge(xp, gate[:bp], oa, ob, mg, w_oa_bf, w_ob_bf, w_out_bf, 1, TM_MERGE)
        outs[0].append(ka32.reshape(bp, s, H_A, 2 * HD_A))
        outs[1].append(va32.reshape(bp, s, H_A, 2 * HD_A))
        outs[2].append(kb32[:, s - n_keep:].reshape(bp, n_keep, H_B, HD_B))
        outs[3].append(vb32[:, s - n_keep:].reshape(bp, n_keep, H_B, HD_B))

        (qa, ka32, kab, va32, vab, ga, qb, kb32, kbb, vb32, vbb, gb, mg) = _project(
            xs, shift[bp:c_rows], scale[bp:c_rows], g_norm[l], w_in_bf, *gains, pm, bs, t)
        by_head = lambda c: c.reshape(bs, past * H_A, LANES)
        to_rows = lambda c: jnp.transpose(c, (0, 2, 3, 1)).reshape(bs, W_B, lb)
        oa, ob = _attn_sample(qa, kab, vab, ga, by_head(cache_a_k[l]), by_head(cache_a_v[l]), bias_as, lamv, gsub,
                              qb, kbb, vbb, gb, to_rows(cache_b_k[l]), to_rows(cache_b_v[l]), bias_bs, lam_init)
        xs = _merge(xs, gate[bp:c_rows], oa, ob, mg, w_oa_bf, w_ob_bf, w_out_bf, bs, t)
        outs[4].append(ka32.reshape(bs, t, H_A, 2 * HD_A))
        outs[5].append(va32.reshape(bs, t, H_A, 2 * HD_A))
        outs[6].append(kb32.reshape(bs, t, H_B, HD_B))
        outs[7].append(vb32.reshape(bs, t, H_B, HD_B))

    return (xp, xs) + tuple(jnp.stack(o) for o in outs)
```

```python
import functools
import math

import jax
import jax.numpy as jnp
from jax import lax
from jax.experimental import pallas as pl
from jax.experimental.pallas import tpu as pltpu

CHUNK = 64
H_A = 4
HD_A = 64
W_A = H_A * 2 * HD_A
H_B = 8
HD_B = 64
W_B = H_B * HD_B
BAND_CHUNKS = 8
BAND_PAST = BAND_CHUNKS * CHUNK
REL_CLIP_B = 128
T5_BUCKETS = 32
T5_MAX_EXACT = 8
T5_MAX_DIST = 128
EPS = 1e-6
NEG = -1e30

LANES = 128
SUBLANES = 8
SEG = 512
NORM_GROUP = 256
MAX_DIRECT_LOGIT = 60.0
BF16_ROUND_MARGIN = 1.02
VMEM_LIMIT = 56 * 1024 * 1024

LOG2E = math.log2(math.e)

F32 = jnp.float32
BF16 = jnp.bfloat16


def _t5_bucket(rel):
    half = T5_BUCKETS // 2
    assert (T5_MAX_DIST // T5_MAX_EXACT) ** 2 == 2 ** (half - T5_MAX_EXACT)
    ret = jnp.where(rel > 0, half, 0)
    n = jnp.abs(rel)
    large = T5_MAX_EXACT + sum((n * n >= T5_MAX_EXACT ** 2 * 2 ** j).astype(jnp.int32)
                               for j in range(1, half - T5_MAX_EXACT))
    large = jnp.minimum(large, half - 1)
    return ret + jnp.where(n < T5_MAX_EXACT, n, large)


def _t5_run(t5_bias, first_rel, count):
    assert max(abs(first_rel), abs(first_rel + count)) < 2 ** 15
    rel = first_rel + jnp.arange(count, dtype=jnp.int32)
    hit = _t5_bucket(rel)[:, None, None] == jnp.arange(T5_BUCKETS, dtype=jnp.int32)[None, :, None]
    return jnp.sum(jnp.where(hit, t5_bias[None], 0.0), axis=1)


def _clipped_run(table, first_rel, count):
    n = table.shape[0]
    first = first_rel + (n - 1) // 2
    n_lo = min(max(-first, 0), count)
    n_hi = min(max(first + count - n, 0), count)
    mid = count - n_lo - n_hi
    parts = [jnp.broadcast_to(table[:1], (n_lo, table.shape[1])),
             table[first + n_lo:first + n_lo + mid] if mid > 0 else table[:0],
             jnp.broadcast_to(table[n - 1:], (n_hi, table.shape[1]))]
    return jnp.concatenate(parts, axis=0)


def _far_bucket(n):
    half = T5_BUCKETS // 2
    return min(T5_MAX_EXACT + int(math.log(n / T5_MAX_EXACT) / math.log(T5_MAX_DIST / T5_MAX_EXACT) * (half - T5_MAX_EXACT)), half - 1)


def _nt(a, b):
    return lax.dot_general(a, b, (((1,), (1,)), ((), ())), preferred_element_type=F32)


def _silu(x):
    return x * (1.0 / (1.0 + jnp.exp(-x)))


def _sigmoid(x):
    return 1.0 / (1.0 + jnp.exp(-x))


def _mod_kernel(c_ref, w_ref, b_ref, o_ref):
    a, w = _silu(c_ref[...]), w_ref[...]
    a_hi, w_hi = a.astype(BF16), w.astype(BF16)
    a_lo, w_lo = (a - a_hi.astype(F32)).astype(BF16), (w - w_hi.astype(F32)).astype(BF16)
    dot = functools.partial(jnp.dot, preferred_element_type=F32)
    o_ref[...] = dot(a_hi, w_hi) + (dot(a_hi, w_lo) + dot(a_lo, w_hi)) + b_ref[...]


def _modulation(c_all, w_ada, b_ada):
    rows, d = c_all.shape
    n_out = w_ada.shape[1]
    tn = d
    return pl.pallas_call(
        _mod_kernel,
        grid=(n_out // tn,),
        in_specs=[pl.BlockSpec((rows, d), lambda j: (0, 0)),
                  pl.BlockSpec((d, tn), lambda j: (0, j)),
                  pl.BlockSpec((1, tn), lambda j: (0, j))],
        out_specs=pl.BlockSpec((rows, tn), lambda j: (0, j)),
        out_shape=jax.ShapeDtypeStruct((rows, n_out), F32),
        name="adaln_mod",
    )(c_all, w_ada, b_ada.reshape(1, n_out))


def _proj_kernel(x_ref, shift_ref, scale_ref, gn_ref, w_ref, gqa_ref, gka_ref, gqb_ref, gkb_ref, pm_ref,
                 qa_ref, ka32_ref, kab_ref, va32_ref, vab_ref, ga_ref,
                 qb_ref, kb32_ref, kbb_ref, vb32_ref, vbb_ref, gb_ref, mg_ref):
    nb, t, d = x_ref.shape
    rows = nb * t
    x = x_ref[...]
    ms = jnp.mean(x * x, axis=-1, keepdims=True)
    xn = x * lax.rsqrt(ms + EPS) * gn_ref[...]
    h = xn * (1.0 + scale_ref[...]) + shift_ref[...]
    hb = h.reshape(rows, d).astype(BF16)

    def seg(c, width=SEG):
        return jnp.dot(hb, w_ref[:, c * SEG:c * SEG + width], preferred_element_type=F32)

    def head_norm(y, g_ref):
        sq = y * y
        hi = sq.astype(BF16)
        pm = pm_ref[...]
        parts = []
        for c in range(SEG // NORM_GROUP):
            sl = slice(c * NORM_GROUP, (c + 1) * NORM_GROUP)
            gms = jnp.dot(hi[:, sl], pm, preferred_element_type=F32)
            parts.append(y[:, sl] * lax.rsqrt(gms + EPS))
        return jnp.concatenate(parts, axis=1) * g_ref[...]

    def put(ref, y):
        ref[...] = y.astype(ref.dtype).reshape(ref.shape)

    def put_by_head(ref, y):
        for hh in range(H_A):
            ref[:, pl.ds(hh, t, stride=H_A), :] = y[:, hh * LANES:(hh + 1) * LANES].reshape(nb, t, LANES)

    put(qa_ref, head_norm(seg(0), gqa_ref) * (HD_A ** -0.5 * LOG2E))
    ka = head_norm(seg(1), gka_ref)
    put_by_head(ka32_ref, ka)
    put(kab_ref, ka)
    va = seg(2)
    put_by_head(va32_ref, va)
    put(vab_ref, va)
    put(ga_ref, seg(3))
    put(qb_ref, head_norm(seg(4), gqb_ref) * (HD_B ** -0.5 * LOG2E))
    kb = head_norm(seg(5), gkb_ref)
    put(kb32_ref, kb)
    put(kbb_ref, kb)
    vb = seg(6)
    put(vb32_ref, vb)
    put(vbb_ref, vb)
    put(gb_ref, seg(7))
    for c in range(8, 12):
        mg_ref[:, :, (c - 8) * SEG:(c - 7) * SEG] = seg(c).astype(mg_ref.dtype).reshape(nb, t, SEG)


def _project(x, shift, scale, g_norm, w_in_bf, gqa, gka, gqb, gkb, pm, nb, t):
    bx, sx, d = x.shape
    n_cols = w_in_bf.shape[1]
    grid = (bx // nb, sx // t)
    tok = lambda w: pl.BlockSpec((nb, t, w), lambda b, i: (b, i, 0))
    per_b = pl.BlockSpec((nb, 1, d), lambda b, i: (b, 0, 0))
    const = lambda shp: pl.BlockSpec(shp, lambda b, i: (0,) * len(shp))
    sds = lambda w, dt: jax.ShapeDtypeStruct((bx, sx, w), dt)
    by_head = jax.ShapeDtypeStruct((bx, sx * H_A, LANES), F32)
    out_shape = (sds(SEG, BF16), by_head, sds(SEG, BF16), by_head, sds(SEG, BF16), sds(SEG, BF16),
                 sds(SEG, BF16), sds(SEG, F32), sds(SEG, BF16), sds(SEG, F32), sds(SEG, BF16), sds(SEG, BF16),
                 sds(4 * SEG, BF16))
    out_specs = tuple(pl.BlockSpec((nb, s.shape[1] // (sx // t), s.shape[2]), lambda b, i: (b, i, 0)) for s in out_shape)
    return pl.pallas_call(
        _proj_kernel,
        grid=grid,
        in_specs=[tok(d), per_b, per_b, const((1, d)),
                  pl.BlockSpec((d, n_cols), lambda b, i: (0, 0), pipeline_mode=pl.Buffered(1)),
                  const((1, SEG)), const((1, SEG)), const((1, SEG)), const((1, SEG)),
                  const((NORM_GROUP, NORM_GROUP))],
        out_specs=out_specs,
        out_shape=out_shape,
        compiler_params=pltpu.CompilerParams(dimension_semantics=("arbitrary", "arbitrary"),
                                             vmem_limit_bytes=VMEM_LIMIT),
        name="in_proj",
    )(x, shift, scale, g_norm.reshape(1, d), w_in_bf, gqa, gka, gqb, gkb, pm)


def _bias_tile_kernel(off_ref, gen_ref, o_ref, *, tq, nk, qbase, kbase0, kstep, band, nvalid):
    v = pl.program_id(1)
    g = gen_ref[0, 0]
    x = jnp.broadcast_to(g, (tq, g.shape[-1]))
    y = pltpu.roll(x, 0, 1, stride=1, stride_axis=0)
    y = (y[:, :nk] - off_ref[pl.program_id(0)]) * LOG2E
    row = lax.broadcasted_iota(jnp.int32, (tq, nk), 0)
    col = lax.broadcasted_iota(jnp.int32, (tq, nk), 1)
    qc = (qbase + row) // CHUNK
    kc = (kbase0 - v * kstep + col) // CHUNK
    y = jnp.where(kc <= qc, y, NEG)
    if band:
        y = jnp.where(kc >= qc - BAND_CHUNKS, y, NEG)
    y = jnp.where(col < nvalid, y, NEG)
    o_ref[0, 0] = y


def _bias_tiles(table_fn, offset, n_heads, tq, nk, n_var, qbase, kbase0, kstep, band, nvalid):
    length = 1 << (tq + nk - 1).bit_length()
    assert length >= tq + nk - 1 and min(kbase0 - (n_var - 1) * kstep, qbase) >= 0
    gen = jnp.stack([jnp.concatenate([table_fn(kbase0 - v * kstep - qbase, nk),
                                      table_fn(kbase0 - v * kstep - qbase + nk - length, length - nk)], axis=0)
                     for v in range(n_var)])
    gen = jnp.transpose(gen, (2, 0, 1)).astype(F32).reshape(n_heads, n_var, 1, length)
    return pl.pallas_call(
        functools.partial(_bias_tile_kernel, tq=tq, nk=nk, qbase=qbase, kbase0=kbase0, kstep=kstep,
                          band=band, nvalid=nvalid),
        grid=(n_heads, n_var),
        in_specs=[pl.BlockSpec(memory_space=pltpu.SMEM),
                  pl.BlockSpec((1, 1, 1, length), lambda h, v: (h, v, 0, 0))],
        out_specs=pl.BlockSpec((1, 1, tq, nk), lambda h, v: (h, v, 0, 0)),
        out_shape=jax.ShapeDtypeStruct((n_heads, n_var, tq, nk), F32),
        name="bias_tiles",
    )(offset.astype(F32), gen)


def _split_halves(q):
    lane = lax.broadcasted_iota(jnp.int32, q.shape, 1)
    zero = jnp.zeros_like(q)
    return jnp.where(lane < LANES // 2, q, zero), jnp.where(lane >= LANES // 2, q, zero)


def _online_update(carry, s, v):
    m, l, acc = carry
    m_new = jnp.maximum(m, jnp.max(s, axis=-1, keepdims=True))
    alpha = jnp.exp2(m - m_new)
    p = jnp.exp2(s - m_new)
    l = alpha * l + jnp.sum(p, axis=-1, keepdims=True)
    acc = alpha * acc + jnp.dot(p.astype(BF16), v, preferred_element_type=F32)
    return m_new, l, acc


def _diff_init(tq):
    return (jnp.full((tq, 1), -jnp.inf, F32), jnp.zeros((tq, 1), F32), jnp.zeros((tq, LANES), F32))


def _diff_finish(c0, c1, lamv_ref, gsub_ref, g, lam_init):
    lamv = lamv_ref[...]
    e1 = jnp.exp(jnp.sum(lamv[0:1] * lamv[1:2], axis=-1, keepdims=True))
    e2 = jnp.exp(jnp.sum(lamv[2:3] * lamv[3:4], axis=-1, keepdims=True))
    lam = e1 - e2 + lam_init
    o = c0[2] / c0[1] - lam * (c1[2] / c1[1])
    o = o * lax.rsqrt(jnp.mean(o * o, axis=-1, keepdims=True) + EPS) * gsub_ref[...]
    o = o * (1.0 - lam_init)
    return o * _silu(g.astype(F32))


def _lane_partial_sum(p):
    out = p[:, :LANES]
    for c in range(1, p.shape[1] // LANES):
        out = out + p[:, c * LANES:(c + 1) * LANES]
    return out


def _attn_a_prompt_kernel(par_ref, q_ref, k_ref, v_ref, g_ref, bias_ref, lamv_ref, gsub_ref, o_ref, l_ref, acc_ref,
                          *, tq, lam_init):
    i = pl.program_id(2)
    has_near = i > 0
    n_far = jnp.maximum(i - 1, 0)
    near = pl.multiple_of(n_far * tq, tq)
    diag = pl.multiple_of(i * tq, tq)
    half = tq // 2
    bounded = par_ref[0] > 0.5
    n_heads = q_ref.shape[2] // LANES
    hs = [slice(hh * LANES, (hh + 1) * LANES) for hh in range(n_heads)]

    def finish_head(hh, c0, c1):
        o = _diff_finish(c0, c1, lamv_ref, gsub_ref, g_ref[0, :, hs[hh]], lam_init)
        o_ref[0, :, hs[hh]] = o.astype(o_ref.dtype)

    def run_online():
        for hh in range(n_heads):
            q0, q1 = _split_halves(q_ref[0, :, hs[hh]])

            def tile(carry, start, bias, mask=None):
                c0, c1 = carry
                k = k_ref[0, pl.ds(start, tq), hs[hh]]
                v = v_ref[0, pl.ds(start, tq), hs[hh]]
                s0, s1 = _nt(q0, k), _nt(q1, k)
                if bias is not None:
                    b = bias_ref[hh, 0, :, bias] if mask is None else bias_ref[hh, 0, :, bias] + mask
                    s0, s1 = s0 + b, s1 + b
                return _online_update(c0, s0, v), _online_update(c1, s1, v)

            init = _diff_init(tq)
            carry = lax.fori_loop(0, n_far, lambda j, c: tile(c, pl.multiple_of(j * tq, tq), None), (init, init))
            carry = tile(carry, near, slice(0, tq), jnp.where(has_near, 0.0, NEG))
            c0, c1 = tile(carry, diag, slice(tq, 2 * tq))
            finish_head(hh, c0, c1)

    def run_bounded():
        qs = [_split_halves(q_ref[0, :, sl]) for sl in hs]
        q_all = [jnp.concatenate(pair, axis=0) for pair in qs]

        def accumulate(hh, start, nk, rows=None, bias_cols=None, mask=None, init=False):
            k = k_ref[0, pl.ds(start, nk), hs[hh]]
            v = v_ref[0, pl.ds(start, nk), hs[hh]]
            if rows is None:
                rows, q2 = slice(0, tq), q_all[hh]
            else:
                q2 = jnp.concatenate([qs[hh][0][rows], qs[hh][1][rows]], axis=0)
            s = _nt(q2, k)
            if bias_cols is not None:
                bias = bias_ref[hh, 0, rows, bias_cols]
                if mask is not None:
                    bias = bias + mask
                s = s + jnp.concatenate([bias, bias], axis=0)
            p = jnp.exp2(s)
            dl = _lane_partial_sum(p)
            da = jnp.dot(p.astype(BF16), v, preferred_element_type=F32)
            r = q2.shape[0] // 2
            for mp in range(2):
                if init:
                    l_ref[hh, mp, rows, :] = dl[mp * r:(mp + 1) * r]
                    acc_ref[hh, mp, rows, :] = da[mp * r:(mp + 1) * r]
                else:
                    l_ref[hh, mp, rows, :] += dl[mp * r:(mp + 1) * r]
                    acc_ref[hh, mp, rows, :] += da[mp * r:(mp + 1) * r]

        near_mask = jnp.where(has_near, 0.0, NEG)
        for hh in range(n_heads):
            accumulate(hh, near, tq, bias_cols=slice(0, tq), mask=near_mask, init=True)

        def far(j, carry):
            for hh in range(n_heads):
                accumulate(hh, pl.multiple_of(j * tq, tq), tq)
            return carry

        lax.fori_loop(0, n_far, far, 0)
        for hh in range(n_heads):
            accumulate(hh, diag, half, slice(0, half), slice(tq, tq + half))
            accumulate(hh, diag, tq, slice(half, tq), slice(tq, 2 * tq))
            c0 = (None, jnp.sum(l_ref[hh, 0], axis=-1, keepdims=True), acc_ref[hh, 0])
            c1 = (None, jnp.sum(l_ref[hh, 1], axis=-1, keepdims=True), acc_ref[hh, 1])
            finish_head(hh, c0, c1)

    pl.when(bounded)(run_bounded)
    pl.when(jnp.logical_not(bounded))(run_online)


def _logits_bounded(g_q, g_k, head_dim, table, offset):
    qk = math.sqrt(head_dim) * LOG2E * jnp.max(jnp.abs(g_q)) * jnp.max(jnp.abs(g_k)) * BF16_ROUND_MARGIN
    bias = jnp.max(jnp.abs(table - offset[None, :])) * LOG2E
    return (qk + bias <= MAX_DIRECT_LOGIT).astype(F32).reshape(1)


def _attn_a_prompt(qa, kab, vab, ga, bias, bounded, lamv, gsub, tq, lam_init):
    b, s, _ = qa.shape
    assert bias.shape[1] == 1 and bias.shape[3] == 2 * tq
    hps = HEADS_PER_STEP_A
    tile = pl.BlockSpec((1, tq, hps * LANES), lambda bi, h, i: (bi, i, h))
    single = pl.Buffered(1)
    full = pl.BlockSpec((1, s, hps * LANES), lambda bi, h, i: (bi, 0, h))
    return pl.pallas_call(
        functools.partial(_attn_a_prompt_kernel, tq=tq, lam_init=lam_init),
        grid=(b, H_A // hps, s // tq),
        in_specs=[pl.BlockSpec(memory_space=pltpu.SMEM), tile, full, full, tile,
                  pl.BlockSpec((hps, 1, tq, 2 * tq), lambda bi, h, i: (h, 0, 0, 0), pipeline_mode=single),
                  pl.BlockSpec((4, HD_A), lambda bi, h, i: (0, 0)),
                  pl.BlockSpec((1, LANES), lambda bi, h, i: (0, 0))],
        out_specs=tile,
        out_shape=jax.ShapeDtypeStruct((b, s, W_A), BF16),
        scratch_shapes=[pltpu.VMEM((hps, 2, tq, LANES), F32), pltpu.VMEM((hps, 2, tq, LANES), F32)],
        compiler_params=pltpu.CompilerParams(dimension_semantics=("arbitrary", "arbitrary", "arbitrary"),
                                             vmem_limit_bytes=VMEM_LIMIT),
        name="diff_attn_prompt",
    )(bounded, qa, kab, vab, ga, bias, lamv, gsub)


def _attn_a_sample_kernel(q_ref, kc_ref, vc_ref, kn_ref, vn_ref, g_ref, bias_ref, lamv_ref, gsub_ref, o_ref, *, lam_init):
    tq = q_ref.shape[1]
    nk = bias_ref.shape[3]
    past = kc_ref.shape[1] // H_A
    pad = jnp.zeros((nk - past - tq, LANES), BF16)
    for hh in range(H_A):
        sl = slice(hh * LANES, (hh + 1) * LANES)
        rows = pl.ds(hh, past, stride=H_A)
        qst = jnp.concatenate(_split_halves(q_ref[0, :, sl]), axis=0)
        k = jnp.concatenate([kc_ref[0, rows, :].astype(BF16), kn_ref[0, :, sl], pad], axis=0)
        v = jnp.concatenate([vc_ref[0, rows, :].astype(BF16), vn_ref[0, :, sl], pad], axis=0)
        bias = bias_ref[hh, 0]
        c = _online_update(_diff_init(2 * tq), _nt(qst, k) + jnp.concatenate([bias, bias], axis=0), v)
        c0, c1 = tuple(x[:tq] for x in c), tuple(x[tq:] for x in c)
        o_ref[0, :, sl] = _diff_finish(c0, c1, lamv_ref, gsub_ref, g_ref[0, :, sl], lam_init).astype(o_ref.dtype)


def _band_pair(q, k, v, g, bias_ref, bounded, first_head, piece_masks):
    t = q.shape[0]
    qst = jnp.concatenate(_split_halves(q), axis=0)
    bias = jnp.concatenate([bias_ref[first_head, 0], bias_ref[first_head + 1, 0]], axis=0)
    w = bias.shape[1] // (len(piece_masks) + 1)
    bias = jnp.concatenate([bias[:, c * w:(c + 1) * w] + m for c, m in enumerate(piece_masks)]
                           + [bias[:, len(piece_masks) * w:]], axis=1)
    s = _nt(qst, k) + bias
    if not bounded:
        s = s - jnp.max(s, axis=-1, keepdims=True)
    p = jnp.exp2(s)
    l = jnp.sum(p, axis=-1, keepdims=True)
    o2 = jnp.dot(p.astype(BF16), v, preferred_element_type=F32) / l
    lane = lax.broadcasted_iota(jnp.int32, (t, LANES), 1)
    o = jnp.where(lane < LANES // 2, o2[:t], o2[t:])
    return o * _silu(g.astype(F32))


def _attn_b_prompt_kernel(par_ref, q_ref, k_ref, v_ref, g_ref, bias_ref, o_ref, *, tq, n_prev):
    i = pl.program_id(2)
    tiles = q_ref.shape[1] // tq
    bounded = par_ref[0] > 0.5

    def run(direct):
        for sub in range(tiles):
            first = tiles * i + sub - n_prev
            rows = slice(sub * tq, (sub + 1) * tq)
            starts = [pl.multiple_of(jnp.maximum(first + c, 0) * tq, tq) for c in range(n_prev + 1)]
            masks = [jnp.where(first + c >= 0, 0.0, NEG) for c in range(n_prev)]
            for pp in range(q_ref.shape[2] // LANES):
                sl = slice(pp * LANES, (pp + 1) * LANES)
                k = jnp.concatenate([k_ref[0, pl.ds(st, tq), sl] for st in starts], axis=0)
                v = jnp.concatenate([v_ref[0, pl.ds(st, tq), sl] for st in starts], axis=0)
                o = _band_pair(q_ref[0, rows, sl], k, v, g_ref[0, rows, sl], bias_ref, direct, 2 * pp, masks)
                o_ref[0, rows, sl] = o.astype(o_ref.dtype)

    pl.when(bounded)(lambda: run(True))
    pl.when(jnp.logical_not(bounded))(lambda: run(False))


def _attn_b_prompt(qb, kbb, vbb, gb, bias, bounded, tq):
    b, s, w = qb.shape
    n_prev = bias.shape[3] // tq - 1
    assert bias.shape[1] == 1 and bias.shape[3] == (n_prev + 1) * tq
    tps = TILES_PER_STEP_B
    tile = pl.BlockSpec((1, tps * tq, w), lambda bi, p, i: (bi, i, p))
    full = pl.BlockSpec((1, s, w), lambda bi, p, i: (bi, 0, p))
    return pl.pallas_call(
        functools.partial(_attn_b_prompt_kernel, tq=tq, n_prev=n_prev),
        grid=(b, 1, s // (tps * tq)),
        in_specs=[pl.BlockSpec(memory_space=pltpu.SMEM), tile, full, full, tile,
                  pl.BlockSpec((H_B, 1, tq, (n_prev + 1) * tq), lambda bi, p, i: (0, 0, 0, 0),
                               pipeline_mode=pl.Buffered(1))],
        out_specs=tile,
        out_shape=jax.ShapeDtypeStruct((b, s, W_B), BF16),
        compiler_params=pltpu.CompilerParams(dimension_semantics=("arbitrary", "arbitrary", "arbitrary"),
                                             vmem_limit_bytes=VMEM_LIMIT),
        name="band_attn_prompt",
    )(bounded, qb, kbb, vbb, gb, bias)


def _attn_b_sample_kernel(q_ref, kc_ref, vc_ref, kn_ref, vn_ref, g_ref, bias_ref, o_ref):
    nk = bias_ref.shape[3]
    past, t = kc_ref.shape[2], kn_ref.shape[1]
    pad = jnp.zeros((nk - past - t, LANES), BF16)
    lane = lax.broadcasted_iota(jnp.int32, (t, LANES), 1)
    for pp in range(q_ref.shape[2] // LANES):
        sl = slice(pp * LANES, (pp + 1) * LANES)
        qst = jnp.concatenate(_split_halves(q_ref[0, :, sl]), axis=0)
        k_new = jnp.concatenate([kn_ref[0, :, sl], pad], axis=0)
        v_new = jnp.concatenate([vn_ref[0, :, sl], pad], axis=0)
        s = jnp.concatenate([jnp.dot(qst, kc_ref[0, sl, :].astype(BF16), preferred_element_type=F32),
                             _nt(qst, k_new)], axis=1)
        s = s + jnp.concatenate([bias_ref[2 * pp, 0], bias_ref[2 * pp + 1, 0]], axis=0)
        p = jnp.exp2(s - jnp.max(s, axis=-1, keepdims=True))
        l = jnp.sum(p, axis=-1, keepdims=True)
        pb = p.astype(BF16)
        o2 = (_nt(pb[:, :past], vc_ref[0, sl, :].astype(BF16))
              + jnp.dot(pb[:, past:], v_new, preferred_element_type=F32)) / l
        o = jnp.where(lane < LANES // 2, o2[:t], o2[t:])
        o_ref[0, :, sl] = (o * _silu(g_ref[0, :, sl].astype(F32))).astype(o_ref.dtype)


N_A_SAMPLE_INPUTS = 9
N_B_SAMPLE_INPUTS = 7


def _attn_sample_kernel(*refs, lam_init):
    a_in = refs[:N_A_SAMPLE_INPUTS]
    b_in = refs[N_A_SAMPLE_INPUTS:N_A_SAMPLE_INPUTS + N_B_SAMPLE_INPUTS]
    oa_ref, ob_ref = refs[N_A_SAMPLE_INPUTS + N_B_SAMPLE_INPUTS:]
    _attn_a_sample_kernel(*a_in, oa_ref, lam_init=lam_init)
    _attn_b_sample_kernel(*b_in, ob_ref)


def _attn_sample(qa, kab, vab, ga, cache_a_k, cache_a_v, bias_a, lamv, gsub,
                 qb, kbb, vbb, gb, cache_b_kt, cache_b_vt, bias_b, lam_init):
    b, t, _ = qa.shape
    assert cache_b_kt.shape[2] % LANES == 0
    new = pl.BlockSpec((1, t, SEG), lambda bi: (bi, 0, 0))
    whole = lambda x: pl.BlockSpec((1,) + x.shape[1:], lambda bi: (bi,) + (0,) * (x.ndim - 1))
    const = lambda x: pl.BlockSpec(x.shape, lambda bi: (0,) * x.ndim)
    return pl.pallas_call(
        functools.partial(_attn_sample_kernel, lam_init=lam_init),
        grid=(b,),
        in_specs=[new, whole(cache_a_k), whole(cache_a_v), new, new, new, const(bias_a), const(lamv), const(gsub),
                  new, whole(cache_b_kt), whole(cache_b_vt), new, new, new, const(bias_b)],
        out_specs=(new, new),
        out_shape=(jax.ShapeDtypeStruct((b, t, W_A), BF16), jax.ShapeDtypeStruct((b, t, W_B), BF16)),
        name="attn_sample",
    )(qa, cache_a_k, cache_a_v, kab, vab, ga, bias_a, lamv, gsub, qb, cache_b_kt, cache_b_vt, kbb, vbb, gb, bias_b)


def _merge_kernel(x_ref, gate_ref, oa_ref, ob_ref, mg_ref, woa_ref, wob_ref, wout_ref, o_ref):
    nb, t, d = x_ref.shape
    rows = nb * t
    ya = jnp.dot(oa_ref[...].reshape(rows, W_A), woa_ref[...], preferred_element_type=F32)
    yb = jnp.dot(ob_ref[...].reshape(rows, W_B), wob_ref[...], preferred_element_type=F32)
    mg = mg_ref[...].reshape(rows, 2 * d).astype(F32)
    m = _sigmoid(mg[:, :d]) * ya + _sigmoid(mg[:, d:]) * yb
    y = jnp.dot(m.astype(BF16), wout_ref[...], preferred_element_type=F32)
    o_ref[...] = x_ref[...] + gate_ref[...] * y.reshape(nb, t, d)


def _merge(x, gate, oa, ob, mg, w_oa_bf, w_ob_bf, w_out_bf, nb, t):
    bx, sx, d = x.shape
    tok = lambda w: pl.BlockSpec((nb, t, w), lambda b, i: (b, i, 0))
    const = lambda shp: pl.BlockSpec(shp, lambda b, i: (0,) * len(shp))
    return pl.pallas_call(
        _merge_kernel,
        grid=(bx // nb, sx // t),
        in_specs=[tok(d), pl.BlockSpec((nb, 1, d), lambda b, i: (b, 0, 0)), tok(W_A), tok(W_B), tok(2 * d),
                  const((W_A, d)), const((W_B, d)), const((d, d))],
        out_specs=tok(d),
        out_shape=jax.ShapeDtypeStruct((bx, sx, d), F32),
        compiler_params=pltpu.CompilerParams(dimension_semantics=("arbitrary", "arbitrary"),
                                             vmem_limit_bytes=VMEM_LIMIT),
        name="merge_out",
    )(x, gate, oa, ob, mg, w_oa_bf, w_ob_bf, w_out_bf)


TQ_A = 512
HEADS_PER_STEP_A = 4
TQ_B = 256
PREV_TILES_B = BAND_PAST // TQ_B
TILES_PER_STEP_B = 2
TM_PROJ = 512
TM_MERGE = 1024
QBASE = 2048


def kernel(x_prompt, x_sample, cache_a_k, cache_a_v, cache_b_k, cache_b_v, c_prompt, c_sample, g_norm, w_ada, b_ada, w_in, g_qa, g_ka, lam_q1, lam_k1, lam_q2, lam_k2, g_subln, t5_bias, g_qb, g_kb, rel_bias_b, w_oa, w_ob, w_out):
    xp, xs = x_prompt, x_sample
    bp, s, d = xp.shape
    bs, t, _ = xs.shape
    depth = w_in.shape[0]
    past = cache_a_k.shape[2]
    lb = cache_b_k.shape[2]
    n_keep = min(BAND_PAST, s)
    assert s % TM_PROJ == 0 and s % TM_MERGE == 0 and s % TQ_A == 0 and s % TQ_B == 0 and (TQ_A // 2) % CHUNK == 0 and TQ_B % CHUNK == 0
    assert PREV_TILES_B * TQ_B >= BAND_PAST and s % (TILES_PER_STEP_B * TQ_B) == 0 and past >= lb
    assert _far_bucket(TQ_A + 1) == T5_BUCKETS // 2 - 1

    r = jnp.arange(NORM_GROUP)
    pm = ((r[:, None] // HD_A) == (r[None, :] // HD_A)).astype(BF16) * (1.0 / HD_A)
    c_rows = bp + bs
    c_pad = -(-c_rows // SUBLANES) * SUBLANES
    c_all = jnp.concatenate([c_prompt, c_sample, jnp.zeros((c_pad - c_rows, d), F32)], axis=0)

    nk_as = -(-(past + t) // LANES) * LANES
    nk_bs = -(-(lb + t) // LANES) * LANES
    t5_fn = functools.partial(_t5_run, t5_bias)
    outs = [[] for _ in range(8)]
    for l in range(depth):
        lam_init = 0.8 - 0.6 * math.exp(-0.3 * l)
        rel_fn = functools.partial(_clipped_run, rel_bias_b[l])
        cfar = t5_bias[_t5_bucket(jnp.int32(-(TQ_A + 1)))]
        bias_ap = _bias_tiles(t5_fn, cfar, H_A, TQ_A, 2 * TQ_A, 1, QBASE, QBASE - TQ_A, 0, False, 2 * TQ_A)
        bias_bp = _bias_tiles(rel_fn, jnp.zeros((H_B,)), H_B, TQ_B, (PREV_TILES_B + 1) * TQ_B, 1, QBASE,
                              QBASE - PREV_TILES_B * TQ_B, 0, True, (PREV_TILES_B + 1) * TQ_B)
        bias_as = _bias_tiles(t5_fn, jnp.zeros((H_A,)), H_A, t, nk_as, 1, past, 0, 0, False, past + t)
        bias_bs = _bias_tiles(rel_fn, jnp.zeros((H_B,)), H_B, t, nk_bs, 1, past, past - lb, 0, True, lb + t)

        mod = _modulation(c_all, w_ada[l], b_ada[l])
        shift = mod[:, :d].reshape(c_pad, 1, d)
        scale = mod[:, d:2 * d].reshape(c_pad, 1, d)
        gate = mod[:, 2 * d:].reshape(c_pad, 1, d)
        w_in_bf = w_in[l].astype(BF16)
        w_oa_bf, w_ob_bf, w_out_bf = w_oa[l].astype(BF16), w_ob[l].astype(BF16), w_out[l].astype(BF16)
        tile8 = lambda g: jnp.tile(g, SEG // g.shape[0]).reshape(1, SEG)
        gains = (tile8(g_qa[l]), tile8(g_ka[l]), tile8(g_qb[l]), tile8(g_kb[l]))
        lamv = jnp.stack([lam_q1[l], lam_k1[l], lam_q2[l], lam_k2[l]])
        gsub = g_subln[l].reshape(1, LANES)

        (qa, ka32, kab, va32, vab, ga, qb, kb32, kbb, vb32, vbb, gb, mg) = _project(
            xp, shift[:bp], scale[:bp], g_norm[l], w_in_bf, *gains, pm, 1, TM_PROJ)
        bounded_a = _logits_bounded(g_qa[l], g_ka[l], HD_A, t5_bias, cfar)
        bounded_b = _logits_bounded(g_qb[l], g_kb[l], HD_B, rel_bias_b[l], jnp.zeros((H_B,)))
        oa = _attn_a_prompt(qa, kab, vab, ga, bias_ap, bounded_a, lamv, gsub, TQ_A, lam_init)
        ob = _attn_b_prompt(qb, kbb, vbb, gb, bias_bp, bounded_b, TQ_B)
        xp = _merge(xp, gate[:bp], oa, ob, mg, w_oa_bf, w_ob_bf, w_out_bf, 1, TM_MERGE)
        outs[0].append(ka32.reshape(bp, s, H_A, 2 * HD_A))
        outs[1].append(va32.reshape(bp, s, H_A, 2 * HD_A))
        outs[2].append(kb32[:, s - n_keep:].reshape(bp, n_keep, H_B, HD_B))
        outs[3].append(vb32[:, s - n_keep:].reshape(bp, n_keep, H_B, HD_B))

        (qa, ka32, kab, va32, vab, ga, qb, kb32, kbb, vb32, vbb, gb, mg) = _project(
            xs, shift[bp:c_rows], scale[bp:c_rows], g_norm[l], w_in_bf, *gains, pm, bs, t)
        by_head = lambda c: c.reshape(bs, past * H_A, LANES)
        to_rows = lambda c: jnp.transpose(c, (0, 2, 3, 1)).reshape(bs, W_B, lb)
        oa, ob = _attn_sample(qa, kab, vab, ga, by_head(cache_a_k[l]), by_head(cache_a_v[l]), bias_as, lamv, gsub,
                              qb, kbb, vbb, gb, to_rows(cache_b_k[l]), to_rows(cache_b_v[l]), bias_bs, lam_init)
        xs = _merge(xs, gate[bp:c_rows], oa, ob, mg, w_oa_bf, w_ob_bf, w_out_bf, bs, t)
        outs[4].append(ka32.reshape(bs, t, H_A, 2 * HD_A))
        outs[5].append(va32.reshape(bs, t, H_A, 2 * HD_A))
        outs[6].append(kb32.reshape(bs, t, H_B, HD_B))
        outs[7].append(vb32.reshape(bs, t, H_B, HD_B))

    return (xp, xs) + tuple(jnp.stack(o) for o in outs)
```

```python
import functools
import math

import jax
import jax.numpy as jnp
from jax import lax
from jax.experimental import pallas as pl
from jax.experimental.pallas import tpu as pltpu

CHUNK = 64
H_A = 4
HD_A = 64
W_A = H_A * 2 * HD_A
H_B = 8
HD_B = 64
W_B = H_B * HD_B
BAND_CHUNKS = 8
BAND_PAST = BAND_CHUNKS * CHUNK
REL_CLIP_B = 128
T5_BUCKETS = 32
T5_MAX_EXACT = 8
T5_MAX_DIST = 128
EPS = 1e-6
NEG = -1e30

LANES = 128
SUBLANES = 8
SEG = 512
NORM_GROUP = 256
MAX_DIRECT_LOGIT = 60.0
BF16_ROUND_MARGIN = 1.02
VMEM_LIMIT = 56 * 1024 * 1024

LOG2E = math.log2(math.e)

F32 = jnp.float32
BF16 = jnp.bfloat16


def _t5_bucket(rel):
    half = T5_BUCKETS // 2
    assert (T5_MAX_DIST // T5_MAX_EXACT) ** 2 == 2 ** (half - T5_MAX_EXACT)
    ret = jnp.where(rel > 0, half, 0)
    n = jnp.abs(rel)
    large = T5_MAX_EXACT + sum((n * n >= T5_MAX_EXACT ** 2 * 2 ** j).astype(jnp.int32)
                               for j in range(1, half - T5_MAX_EXACT))
    large = jnp.minimum(large, half - 1)
    return ret + jnp.where(n < T5_MAX_EXACT, n, large)


def _t5_run(t5_bias, first_rel, count):
    assert max(abs(first_rel), abs(first_rel + count)) < 2 ** 15
    rel = first_rel + jnp.arange(count, dtype=jnp.int32)
    hit = _t5_bucket(rel)[:, None, None] == jnp.arange(T5_BUCKETS, dtype=jnp.int32)[None, :, None]
    return jnp.sum(jnp.where(hit, t5_bias[None], 0.0), axis=1)


def _clipped_run(table, first_rel, count):
    n = table.shape[0]
    first = first_rel + (n - 1) // 2
    n_lo = min(max(-first, 0), count)
    n_hi = min(max(first + count - n, 0), count)
    mid = count - n_lo - n_hi
    parts = [jnp.broadcast_to(table[:1], (n_lo, table.shape[1])),
             table[first + n_lo:first + n_lo + mid] if mid > 0 else table[:0],
             jnp.broadcast_to(table[n - 1:], (n_hi, table.shape[1]))]
    return jnp.concatenate(parts, axis=0)


def _far_bucket(n):
    half = T5_BUCKETS // 2
    return min(T5_MAX_EXACT + int(math.log(n / T5_MAX_EXACT) / math.log(T5_MAX_DIST / T5_MAX_EXACT) * (half - T5_MAX_EXACT)), half - 1)


def _nt(a, b):
    return lax.dot_general(a, b, (((1,), (1,)), ((), ())), preferred_element_type=F32)


def _silu(x):
    return x * (1.0 / (1.0 + jnp.exp(-x)))


def _sigmoid(x):
    return 1.0 / (1.0 + jnp.exp(-x))


def _mod_kernel(c_ref, w_ref, b_ref, o_ref):
    a, w = _silu(c_ref[...]), w_ref[...]
    a_hi, w_hi = a.astype(BF16), w.astype(BF16)
    a_lo, w_lo = (a - a_hi.astype(F32)).astype(BF16), (w - w_hi.astype(F32)).astype(BF16)
    dot = functools.partial(jnp.dot, preferred_element_type=F32)
    o_ref[...] = dot(a_hi, w_hi) + (dot(a_hi, w_lo) + dot(a_lo, w_hi)) + b_ref[...]


def _modulation(c_all, w_ada, b_ada):
    rows, d = c_all.shape
    n_out = w_ada.shape[1]
    tn = d
    return pl.pallas_call(
        _mod_kernel,
        grid=(n_out // tn,),
        in_specs=[pl.BlockSpec((rows, d), lambda j: (0, 0)),
                  pl.BlockSpec((d, tn), lambda j: (0, j)),
                  pl.BlockSpec((1, tn), lambda j: (0, j))],
        out_specs=pl.BlockSpec((rows, tn), lambda j: (0, j)),
        out_shape=jax.ShapeDtypeStruct((rows, n_out), F32),
        name="adaln_mod",
    )(c_all, w_ada, b_ada.reshape(1, n_out))


def _proj_kernel(x_ref, shift_ref, scale_ref, gn_ref, w_ref, gqa_ref, gka_ref, gqb_ref, gkb_ref, pm_ref,
                 qa_ref, ka32_ref, kab_ref, va32_ref, vab_ref, ga_ref,
                 qb_ref, kb32_ref, kbb_ref, vb32_ref, vbb_ref, gb_ref, mg_ref):
    nb, t, d = x_ref.shape
    rows = nb * t
    x = x_ref[...]
    ms = jnp.mean(x * x, axis=-1, keepdims=True)
    xn = x * lax.rsqrt(ms + EPS) * gn_ref[...]
    h = xn * (1.0 + scale_ref[...]) + shift_ref[...]
    hb = h.reshape(rows, d).astype(BF16)

    def seg(c, width=SEG):
        return jnp.dot(hb, w_ref[:, c * SEG:c * SEG + width], preferred_element_type=F32)

    def head_norm(y, g_ref):
        sq = y * y
        hi = sq.astype(BF16)
        pm = pm_ref[...]
        parts = []
        for c in range(SEG // NORM_GROUP):
            sl = slice(c * NORM_GROUP, (c + 1) * NORM_GROUP)
            gms = jnp.dot(hi[:, sl], pm, preferred_element_type=F32)
            parts.append(y[:, sl] * lax.rsqrt(gms + EPS))
        return jnp.concatenate(parts, axis=1) * g_ref[...]

    def put(ref, y):
        ref[...] = y.astype(ref.dtype).reshape(ref.shape)

    def put_by_head(ref, y):
        for hh in range(H_A):
            ref[:, pl.ds(hh, t, stride=H_A), :] = y[:, hh * LANES:(hh + 1) * LANES].reshape(nb, t, LANES)

    put(qa_ref, head_norm(seg(0), gqa_ref) * (HD_A ** -0.5 * LOG2E))
    ka = head_norm(seg(1), gka_ref)
    put_by_head(ka32_ref, ka)
    put(kab_ref, ka)
    va = seg(2)
    put_by_head(va32_ref, va)
    put(vab_ref, va)
    put(ga_ref, seg(3))
    put(qb_ref, head_norm(seg(4), gqb_ref) * (HD_B ** -0.5 * LOG2E))
    kb = head_norm(seg(5), gkb_ref)
    put(kb32_ref, kb)
    put(kbb_ref, kb)
    vb = seg(6)
    put(vb32_ref, vb)
    put(vbb_ref, vb)
    put(gb_ref, seg(7))
    for c in range(8, 12):
        mg_ref[:, :, (c - 8) * SEG:(c - 7) * SEG] = seg(c).astype(mg_ref.dtype).reshape(nb, t, SEG)


def _project(x, shift, scale, g_norm, w_in_bf, gqa, gka, gqb, gkb, pm, nb, t):
    bx, sx, d = x.shape
    n_cols = w_in_bf.shape[1]
    grid = (bx // nb, sx // t)
    tok = lambda w: pl.BlockSpec((nb, t, w), lambda b, i: (b, i, 0))
    per_b = pl.BlockSpec((nb, 1, d), lambda b, i: (b, 0, 0))
    const = lambda shp: pl.BlockSpec(shp, lambda b, i: (0,) * len(shp))
    sds = lambda w, dt: jax.ShapeDtypeStruct((bx, sx, w), dt)
    by_head = jax.ShapeDtypeStruct((bx, sx * H_A, LANES), F32)
    out_shape = (sds(SEG, BF16), by_head, sds(SEG, BF16), by_head, sds(SEG, BF16), sds(SEG, BF16),
                 sds(SEG, BF16), sds(SEG, F32), sds(SEG, BF16), sds(SEG, F32), sds(SEG, BF16), sds(SEG, BF16),
                 sds(4 * SEG, BF16))
    out_specs = tuple(pl.BlockSpec((nb, s.shape[1] // (sx // t), s.shape[2]), lambda b, i: (b, i, 0)) for s in out_shape)
    return pl.pallas_call(
        _proj_kernel,
        grid=grid,
        in_specs=[tok(d), per_b, per_b, const((1, d)),
                  pl.BlockSpec((d, n_cols), lambda b, i: (0, 0), pipeline_mode=pl.Buffered(1)),
                  const((1, SEG)), const((1, SEG)), const((1, SEG)), const((1, SEG)),
                  const((NORM_GROUP, NORM_GROUP))],
        out_specs=out_specs,
        out_shape=out_shape,
        compiler_params=pltpu.CompilerParams(dimension_semantics=("arbitrary", "arbitrary"),
                                             vmem_limit_bytes=VMEM_LIMIT),
        name="in_proj",
    )(x, shift, scale, g_norm.reshape(1, d), w_in_bf, gqa, gka, gqb, gkb, pm)


def _bias_tile_kernel(off_ref, gen_ref, o_ref, *, tq, nk, qbase, kbase0, kstep, band, nvalid):
    v = pl.program_id(1)
    g = gen_ref[0, 0]
    x = jnp.broadcast_to(g, (tq, g.shape[-1]))
    y = pltpu.roll(x, 0, 1, stride=1, stride_axis=0)
    y = (y[:, :nk] - off_ref[pl.program_id(0)]) * LOG2E
    row = lax.broadcasted_iota(jnp.int32, (tq, nk), 0)
    col = lax.broadcasted_iota(jnp.int32, (tq, nk), 1)
    qc = (qbase + row) // CHUNK
    kc = (kbase0 - v * kstep + col) // CHUNK
    y = jnp.where(kc <= qc, y, NEG)
    if band:
        y = jnp.where(kc >= qc - BAND_CHUNKS, y, NEG)
    y = jnp.where(col < nvalid, y, NEG)
    o_ref[0, 0] = y


def _bias_tiles(table_fn, offset, n_heads, tq, nk, n_var, qbase, kbase0, kstep, band, nvalid):
    length = 1 << (tq + nk - 1).bit_length()
    assert length >= tq + nk - 1 and min(kbase0 - (n_var - 1) * kstep, qbase) >= 0
    gen = jnp.stack([jnp.concatenate([table_fn(kbase0 - v * kstep - qbase, nk),
                                      table_fn(kbase0 - v * kstep - qbase + nk - length, length - nk)], axis=0)
                     for v in range(n_var)])
    gen = jnp.transpose(gen, (2, 0, 1)).astype(F32).reshape(n_heads, n_var, 1, length)
    return pl.pallas_call(
        functools.partial(_bias_tile_kernel, tq=tq, nk=nk, qbase=qbase, kbase0=kbase0, kstep=kstep,
                          band=band, nvalid=nvalid),
        grid=(n_heads, n_var),
        in_specs=[pl.BlockSpec(memory_space=pltpu.SMEM),
                  pl.BlockSpec((1, 1, 1, length), lambda h, v: (h, v, 0, 0))],
        out_specs=pl.BlockSpec((1, 1, tq, nk), lambda h, v: (h, v, 0, 0)),
        out_shape=jax.ShapeDtypeStruct((n_heads, n_var, tq, nk), F32),
        name="bias_tiles",
    )(offset.astype(F32), gen)


def _split_halves(q):
    lane = lax.broadcasted_iota(jnp.int32, q.shape, 1)
    zero = jnp.zeros_like(q)
    return jnp.where(lane < LANES // 2, q, zero), jnp.where(lane >= LANES // 2, q, zero)


def _online_update(carry, s, v):
    m, l, acc = carry
    m_new = jnp.maximum(m, jnp.max(s, axis=-1, keepdims=True))
    alpha = jnp.exp2(m - m_new)
    p = jnp.exp2(s - m_new)
    l = alpha * l + jnp.sum(p, axis=-1, keepdims=True)
    acc = alpha * acc + jnp.dot(p.astype(BF16), v, preferred_element_type=F32)
    return m_new, l, acc


def _diff_init(tq):
    return (jnp.full((tq, 1), -jnp.inf, F32), jnp.zeros((tq, 1), F32), jnp.zeros((tq, LANES), F32))


def _diff_finish(c0, c1, lamv_ref, gsub_ref, g, lam_init):
    lamv = lamv_ref[...]
    e1 = jnp.exp(jnp.sum(lamv[0:1] * lamv[1:2], axis=-1, keepdims=True))
    e2 = jnp.exp(jnp.sum(lamv[2:3] * lamv[3:4], axis=-1, keepdims=True))
    lam = e1 - e2 + lam_init
    o = c0[2] / c0[1] - lam * (c1[2] / c1[1])
    o = o * lax.rsqrt(jnp.mean(o * o, axis=-1, keepdims=True) + EPS) * gsub_ref[...]
    o = o * (1.0 - lam_init)
    return o * _silu(g.astype(F32))


def _lane_partial_sum(p):
    out = p[:, :LANES]
    for c in range(1, p.shape[1] // LANES):
        out = out + p[:, c * LANES:(c + 1) * LANES]
    return out


def _attn_a_prompt_kernel(par_ref, q_ref, k_ref, v_ref, g_ref, bias_ref, lamv_ref, gsub_ref, o_ref, l_ref, acc_ref,
                          *, tq, lam_init):
    i = pl.program_id(2)
    has_near = i > 0
    n_far = jnp.maximum(i - 1, 0)
    near = pl.multiple_of(n_far * tq, tq)
    diag = pl.multiple_of(i * tq, tq)
    half = tq // 2
    bounded = par_ref[0] > 0.5
    n_heads = q_ref.shape[2] // LANES
    hs = [slice(hh * LANES, (hh + 1) * LANES) for hh in range(n_heads)]

    def finish_head(hh, c0, c1):
        o = _diff_finish(c0, c1, lamv_ref, gsub_ref, g_ref[0, :, hs[hh]], lam_init)
        o_ref[0, :, hs[hh]] = o.astype(o_ref.dtype)

    def run_online():
        for hh in range(n_heads):
            q0, q1 = _split_halves(q_ref[0, :, hs[hh]])

            def tile(carry, start, bias, mask=None):
                c0, c1 = carry
                k = k_ref[0, pl.ds(start, tq), hs[hh]]
                v = v_ref[0, pl.ds(start, tq), hs[hh]]
                s0, s1 = _nt(q0, k), _nt(q1, k)
                if bias is not None:
                    b = bias_ref[hh, 0, :, bias] if mask is None else bias_ref[hh, 0, :, bias] + mask
                    s0, s1 = s0 + b, s1 + b
                return _online_update(c0, s0, v), _online_update(c1, s1, v)

            init = _diff_init(tq)
            carry = lax.fori_loop(0, n_far, lambda j, c: tile(c, pl.multiple_of(j * tq, tq), None), (init, init))
            carry = tile(carry, near, slice(0, tq), jnp.where(has_near, 0.0, NEG))
            c0, c1 = tile(carry, diag, slice(tq, 2 * tq))
            finish_head(hh, c0, c1)

    def run_bounded():
        qs = [_split_halves(q_ref[0, :, sl]) for sl in hs]
        q_all = [jnp.concatenate(pair, axis=0) for pair in qs]

        def accumulate(hh, start, nk, rows=None, bias_cols=None, mask=None, init=False):
            k = k_ref[0, pl.ds(start, nk), hs[hh]]
            v = v_ref[0, pl.ds(start, nk), hs[hh]]
            if rows is None:
                rows, q2 = slice(0, tq), q_all[hh]
            else:
                q2 = jnp.concatenate([qs[hh][0][rows], qs[hh][1][rows]], axis=0)
            s = _nt(q2, k)
            if bias_cols is not None:
                bias = bias_ref[hh, 0, rows, bias_cols]
                if mask is not None:
                    bias = bias + mask
                s = s + jnp.concatenate([bias, bias], axis=0)
            p = jnp.exp2(s)
            dl = _lane_partial_sum(p)
            da = jnp.dot(p.astype(BF16), v, preferred_element_type=F32)
            r = q2.shape[0] // 2
            for mp in range(2):
                if init:
                    l_ref[hh, mp, rows, :] = dl[mp * r:(mp + 1) * r]
                    acc_ref[hh, mp, rows, :] = da[mp * r:(mp + 1) * r]
                else:
                    l_ref[hh, mp, rows, :] += dl[mp * r:(mp + 1) * r]
                    acc_ref[hh, mp, rows, :] += da[mp * r:(mp + 1) * r]

        near_mask = jnp.where(has_near, 0.0, NEG)
        for hh in range(n_heads):
            accumulate(hh, near, tq, bias_cols=slice(0, tq), mask=near_mask, init=True)

        def far(j, carry):
            for hh in range(n_heads):
                accumulate(hh, pl.multiple_of(j * tq, tq), tq)
            return carry

        lax.fori_loop(0, n_far, far, 0)
        for hh in range(n_heads):
            accumulate(hh, diag, half, slice(0, half), slice(tq, tq + half))
            accumulate(hh, diag, tq, slice(half, tq), slice(tq, 2 * tq))
            c0 = (None, jnp.sum(l_ref[hh, 0], axis=-1, keepdims=True), acc_ref[hh, 0])
            c1 = (None, jnp.sum(l_ref[hh, 1], axis=-1, keepdims=True), acc_ref[hh, 1])
            finish_head(hh, c0, c1)

    pl.when(bounded)(run_bounded)
    pl.when(jnp.logical_not(bounded))(run_online)


def _logits_bounded(g_q, g_k, head_dim, table, offset):
    qk = math.sqrt(head_dim) * LOG2E * jnp.max(jnp.abs(g_q)) * jnp.max(jnp.abs(g_k)) * BF16_ROUND_MARGIN
    bias = jnp.max(jnp.abs(table - offset[None, :])) * LOG2E
    return (qk + bias <= MAX_DIRECT_LOGIT).astype(F32).reshape(1)


def _attn_a_prompt(qa, kab, vab, ga, bias, bounded, lamv, gsub, tq, lam_init):
    b, s, _ = qa.shape
    assert bias.shape[1] == 1 and bias.shape[3] == 2 * tq
    hps = HEADS_PER_STEP_A
    tile = pl.BlockSpec((1, tq, hps * LANES), lambda bi, h, i: (bi, i, h))
    single = pl.Buffered(1)
    full = pl.BlockSpec((1, s, hps * LANES), lambda bi, h, i: (bi, 0, h))
    return pl.pallas_call(
        functools.partial(_attn_a_prompt_kernel, tq=tq, lam_init=lam_init),
        grid=(b, H_A // hps, s // tq),
        in_specs=[pl.BlockSpec(memory_space=pltpu.SMEM), tile, full, full, tile,
                  pl.BlockSpec((hps, 1, tq, 2 * tq), lambda bi, h, i: (h, 0, 0, 0), pipeline_mode=single),
                  pl.BlockSpec((4, HD_A), lambda bi, h, i: (0, 0)),
                  pl.BlockSpec((1, LANES), lambda bi, h, i: (0, 0))],
        out_specs=tile,
        out_shape=jax.ShapeDtypeStruct((b, s, W_A), BF16),
        scratch_shapes=[pltpu.VMEM((hps, 2, tq, LANES), F32), pltpu.VMEM((hps, 2, tq, LANES), F32)],
        compiler_params=pltpu.CompilerParams(dimension_semantics=("arbitrary", "arbitrary", "arbitrary"),
                                             vmem_limit_bytes=VMEM_LIMIT),
        name="diff_attn_prompt",
    )(bounded, qa, kab, vab, ga, bias, lamv, gsub)


def _attn_a_sample_kernel(q_ref, kc_ref, vc_ref, kn_ref, vn_ref, g_ref, bias_ref, lamv_ref, gsub_ref, o_ref, *, lam_init):
    tq = q_ref.shape[1]
    nk = bias_ref.shape[3]
    past = kc_ref.shape[1] // H_A
    pad = jnp.zeros((nk - past - tq, LANES), BF16)
    for hh in range(H_A):
        sl = slice(hh * LANES, (hh + 1) * LANES)
        rows = pl.ds(hh, past, stride=H_A)
        qst = jnp.concatenate(_split_halves(q_ref[0, :, sl]), axis=0)
        k = jnp.concatenate([kc_ref[0, rows, :].astype(BF16), kn_ref[0, :, sl], pad], axis=0)
        v = jnp.concatenate([vc_ref[0, rows, :].astype(BF16), vn_ref[0, :, sl], pad], axis=0)
        bias = bias_ref[hh, 0]
        c = _online_update(_diff_init(2 * tq), _nt(qst, k) + jnp.concatenate([bias, bias], axis=0), v)
        c0, c1 = tuple(x[:tq] for x in c), tuple(x[tq:] for x in c)
        o_ref[0, :, sl] = _diff_finish(c0, c1, lamv_ref, gsub_ref, g_ref[0, :, sl], lam_init).astype(o_ref.dtype)


def _band_pair(q, k, v, g, bias_ref, bounded, first_head, piece_masks):
    t = q.shape[0]
    qst = jnp.concatenate(_split_halves(q), axis=0)
    bias = jnp.concatenate([bias_ref[first_head, 0], bias_ref[first_head + 1, 0]], axis=0)
    w = bias.shape[1] // (len(piece_masks) + 1)
    bias = jnp.concatenate([bias[:, c * w:(c + 1) * w] + m for c, m in enumerate(piece_masks)]
                           + [bias[:, len(piece_masks) * w:]], axis=1)
    s = _nt(qst, k) + bias
    if not bounded:
        s = s - jnp.max(s, axis=-1, keepdims=True)
    p = jnp.exp2(s)
    l = jnp.sum(p, axis=-1, keepdims=True)
    o2 = jnp.dot(p.astype(BF16), v, preferred_element_type=F32) / l
    lane = lax.broadcasted_iota(jnp.int32, (t, LANES), 1)
    o = jnp.where(lane < LANES // 2, o2[:t], o2[t:])
    return o * _silu(g.astype(F32))


def _attn_b_prompt_kernel(par_ref, q_ref, k_ref, v_ref, g_ref, bias_ref, o_ref, *, tq, n_prev):
    i = pl.program_id(2)
    tiles = q_ref.shape[1] // tq
    bounded = par_ref[0] > 0.5

    def run(direct):
        for sub in range(tiles):
            first = tiles * i + sub - n_prev
            rows = slice(sub * tq, (sub + 1) * tq)
            starts = [pl.multiple_of(jnp.maximum(first + c, 0) * tq, tq) for c in range(n_prev + 1)]
            masks = [jnp.where(first + c >= 0, 0.0, NEG) for c in range(n_prev)]
            for pp in range(q_ref.shape[2] // LANES):
                sl = slice(pp * LANES, (pp + 1) * LANES)
                k = jnp.concatenate([k_ref[0, pl.ds(st, tq), sl] for st in starts], axis=0)
                v = jnp.concatenate([v_ref[0, pl.ds(st, tq), sl] for st in starts], axis=0)
                o = _band_pair(q_ref[0, rows, sl], k, v, g_ref[0, rows, sl], bias_ref, direct, 2 * pp, masks)
                o_ref[0, rows, sl] = o.astype(o_ref.dtype)

    pl.when(bounded)(lambda: run(True))
    pl.when(jnp.logical_not(bounded))(lambda: run(False))


def _attn_b_prompt(qb, kbb, vbb, gb, bias, bounded, tq):
    b, s, w = qb.shape
    n_prev = bias.shape[3] // tq - 1
    assert bias.shape[1] == 1 and bias.shape[3] == (n_prev + 1) * tq
    tps = TILES_PER_STEP_B
    tile = pl.BlockSpec((1, tps * tq, w), lambda bi, p, i: (bi, i, p))
    full = pl.BlockSpec((1, s, w), lambda bi, p, i: (bi, 0, p))
    return pl.pallas_call(
        functools.partial(_attn_b_prompt_kernel, tq=tq, n_prev=n_prev),
        grid=(b, 1, s // (tps * tq)),
        in_specs=[pl.BlockSpec(memory_space=pltpu.SMEM), tile, full, full, tile,
                  pl.BlockSpec((H_B, 1, tq, (n_prev + 1) * tq), lambda bi, p, i: (0, 0, 0, 0),
                               pipeline_mode=pl.Buffered(1))],
        out_specs=tile,
        out_shape=jax.ShapeDtypeStruct((b, s, W_B), BF16),
        compiler_params=pltpu.CompilerParams(dimension_semantics=("arbitrary", "arbitrary", "arbitrary"),
                                             vmem_limit_bytes=VMEM_LIMIT),
        name="band_attn_prompt",
    )(bounded, qb, kbb, vbb, gb, bias)


def _attn_b_sample_kernel(q_ref, kc_ref, vc_ref, kn_ref, vn_ref, g_ref, bias_ref, o_ref):
    nk = bias_ref.shape[3]
    past, t = kc_ref.shape[2], kn_ref.shape[1]
    pad = jnp.zeros((nk - past - t, LANES), BF16)
    lane = lax.broadcasted_iota(jnp.int32, (t, LANES), 1)
    for pp in range(q_ref.shape[2] // LANES):
        sl = slice(pp * LANES, (pp + 1) * LANES)
        qst = jnp.concatenate(_split_halves(q_ref[0, :, sl]), axis=0)
        k_new = jnp.concatenate([kn_ref[0, :, sl], pad], axis=0)
        v_new = jnp.concatenate([vn_ref[0, :, sl], pad], axis=0)
        s = jnp.concatenate([jnp.dot(qst, kc_ref[0, sl, :].astype(BF16), preferred_element_type=F32),
                             _nt(qst, k_new)], axis=1)
        s = s + jnp.concatenate([bias_ref[2 * pp, 0], bias_ref[2 * pp + 1, 0]], axis=0)
        p = jnp.exp2(s - jnp.max(s, axis=-1, keepdims=True))
        l = jnp.sum(p, axis=-1, keepdims=True)
        pb = p.astype(BF16)
        o2 = (_nt(pb[:, :past], vc_ref[0, sl, :].astype(BF16))
              + jnp.dot(pb[:, past:], v_new, preferred_element_type=F32)) / l
        o = jnp.where(lane < LANES // 2, o2[:t], o2[t:])
        o_ref[0, :, sl] = (o * _silu(g_ref[0, :, sl].astype(F32))).astype(o_ref.dtype)


N_A_SAMPLE_INPUTS = 9
N_B_SAMPLE_INPUTS = 7


def _attn_sample_kernel(*refs, lam_init):
    a_in = refs[:N_A_SAMPLE_INPUTS]
    b_in = refs[N_A_SAMPLE_INPUTS:N_A_SAMPLE_INPUTS + N_B_SAMPLE_INPUTS]
    oa_ref, ob_ref = refs[N_A_SAMPLE_INPUTS + N_B_SAMPLE_INPUTS:]
    _attn_a_sample_kernel(*a_in, oa_ref, lam_init=lam_init)
    _attn_b_sample_kernel(*b_in, ob_ref)


def _attn_sample(qa, kab, vab, ga, cache_a_k, cache_a_v, bias_a, lamv, gsub,
                 qb, kbb, vbb, gb, cache_b_kt, cache_b_vt, bias_b, lam_init):
    b, t, _ = qa.shape
    assert cache_b_kt.shape[2] % LANES == 0
    new = pl.BlockSpec((1, t, SEG), lambda bi: (bi, 0, 0))
    whole = lambda x: pl.BlockSpec((1,) + x.shape[1:], lambda bi: (bi,) + (0,) * (x.ndim - 1))
    const = lambda x: pl.BlockSpec(x.shape, lambda bi: (0,) * x.ndim)
    return pl.pallas_call(
        functools.partial(_attn_sample_kernel, lam_init=lam_init),
        grid=(b,),
        in_specs=[new, whole(cache_a_k), whole(cache_a_v), new, new, new, const(bias_a), const(lamv), const(gsub),
                  new, whole(cache_b_kt), whole(cache_b_vt), new, new, new, const(bias_b)],
        out_specs=(new, new),
        out_shape=(jax.ShapeDtypeStruct((b, t, W_A), BF16), jax.ShapeDtypeStruct((b, t, W_B), BF16)),
        name="attn_sample",
    )(qa, cache_a_k, cache_a_v, kab, vab, ga, bias_a, lamv, gsub, qb, cache_b_kt, cache_b_vt, kbb, vbb, gb, bias_b)


def _merge_kernel(x_ref, gate_ref, oa_ref, ob_ref, mg_ref, woa_ref, wob_ref, wout_ref, o_ref):
    nb, t, d = x_ref.shape
    rows = nb * t
    ya = jnp.dot(oa_ref[...].reshape(rows, W_A), woa_ref[...], preferred_element_type=F32)
    yb = jnp.dot(ob_ref[...].reshape(rows, W_B), wob_ref[...], preferred_element_type=F32)
    mg = mg_ref[...].reshape(rows, 2 * d).astype(F32)
    m = _sigmoid(mg[:, :d]) * ya + _sigmoid(mg[:, d:]) * yb
    y = jnp.dot(m.astype(BF16), wout_ref[...], preferred_element_type=F32)
    o_ref[...] = x_ref[...] + gate_ref[...] * y.reshape(nb, t, d)


def _merge(x, gate, oa, ob, mg, w_oa_bf, w_ob_bf, w_out_bf, nb, t):
    bx, sx, d = x.shape
    grid = (bx // nb, sx // t)
    tok = lambda w, mode=None: pl.BlockSpec((nb, t, w), lambda b, i: (b, i, 0), pipeline_mode=mode)
    const = lambda shp: pl.BlockSpec(shp, lambda b, i: (0,) * len(shp))
    if grid[0] * grid[1] > 2:
        deep = pl.Buffered(3)
        inner_in = [tok(d, deep), pl.BlockSpec((nb, 1, d), lambda b, i: (b, 0, 0)), tok(W_A), tok(W_B), tok(2 * d, deep)]

        def streamed(x_hbm, gate_hbm, oa_hbm, ob_hbm, mg_hbm, woa_ref, wob_ref, wout_ref, o_hbm):
            def step(x_ref, gate_ref, oa_ref, ob_ref, mg_ref, o_ref):
                _merge_kernel(x_ref, gate_ref, oa_ref, ob_ref, mg_ref, woa_ref, wob_ref, wout_ref, o_ref)

            pltpu.emit_pipeline(step, grid=grid, in_specs=inner_in, out_specs=[tok(d)])(
                x_hbm, gate_hbm, oa_hbm, ob_hbm, mg_hbm, o_hbm)

        hbm = pl.BlockSpec(memory_space=pl.ANY)
        vmem = pl.BlockSpec(memory_space=pltpu.VMEM)
        return pl.pallas_call(
            streamed,
            in_specs=[hbm] * 5 + [vmem] * 3,
            out_specs=hbm,
            out_shape=jax.ShapeDtypeStruct((bx, sx, d), F32),
            compiler_params=pltpu.CompilerParams(vmem_limit_bytes=VMEM_LIMIT),
            name="merge_out",
        )(x, gate, oa, ob, mg, w_oa_bf, w_ob_bf, w_out_bf)
    return pl.pallas_call(
        _merge_kernel,
        grid=grid,
        in_specs=[tok(d), pl.BlockSpec((nb, 1, d), lambda b, i: (b, 0, 0)), tok(W_A), tok(W_B), tok(2 * d),
                  const((W_A, d)), const((W_B, d)), const((d, d))],
        out_specs=tok(d),
        out_shape=jax.ShapeDtypeStruct((bx, sx, d), F32),
        compiler_params=pltpu.CompilerParams(dimension_semantics=("arbitrary", "arbitrary"),
                                             vmem_limit_bytes=VMEM_LIMIT),
        name="merge_out",
    )(x, gate, oa, ob, mg, w_oa_bf, w_ob_bf, w_out_bf)


TQ_A = 512
HEADS_PER_STEP_A = 4
TQ_B = 256
PREV_TILES_B = BAND_PAST // TQ_B
TILES_PER_STEP_B = 2
TM_PROJ = 512
TM_MERGE = 1024
QBASE = 2048


def kernel(x_prompt, x_sample, cache_a_k, cache_a_v, cache_b_k, cache_b_v, c_prompt, c_sample, g_norm, w_ada, b_ada, w_in, g_qa, g_ka, lam_q1, lam_k1, lam_q2, lam_k2, g_subln, t5_bias, g_qb, g_kb, rel_bias_b, w_oa, w_ob, w_out):
    xp, xs = x_prompt, x_sample
    bp, s, d = xp.shape
    bs, t, _ = xs.shape
    depth = w_in.shape[0]
    past = cache_a_k.shape[2]
    lb = cache_b_k.shape[2]
    n_keep = min(BAND_PAST, s)
    assert s % TM_PROJ == 0 and s % TM_MERGE == 0 and s % TQ_A == 0 and s % TQ_B == 0 and (TQ_A // 2) % CHUNK == 0 and TQ_B % CHUNK == 0
    assert PREV_TILES_B * TQ_B >= BAND_PAST and s % (TILES_PER_STEP_B * TQ_B) == 0 and past >= lb
    assert _far_bucket(TQ_A + 1) == T5_BUCKETS // 2 - 1

    r = jnp.arange(NORM_GROUP)
    pm = ((r[:, None] // HD_A) == (r[None, :] // HD_A)).astype(BF16) * (1.0 / HD_A)
    c_rows = bp + bs
    c_pad = -(-c_rows // SUBLANES) * SUBLANES
    c_all = jnp.concatenate([c_prompt, c_sample, jnp.zeros((c_pad - c_rows, d), F32)], axis=0)

    nk_as = -(-(past + t) // LANES) * LANES
    nk_bs = -(-(lb + t) // LANES) * LANES
    t5_fn = functools.partial(_t5_run, t5_bias)
    outs = [[] for _ in range(8)]
    for l in range(depth):
        lam_init = 0.8 - 0.6 * math.exp(-0.3 * l)
        rel_fn = functools.partial(_clipped_run, rel_bias_b[l])
        cfar = t5_bias[_t5_bucket(jnp.int32(-(TQ_A + 1)))]
        bias_ap = _bias_tiles(t5_fn, cfar, H_A, TQ_A, 2 * TQ_A, 1, QBASE, QBASE - TQ_A, 0, False, 2 * TQ_A)
        bias_bp = _bias_tiles(rel_fn, jnp.zeros((H_B,)), H_B, TQ_B, (PREV_TILES_B + 1) * TQ_B, 1, QBASE,
                              QBASE - PREV_TILES_B * TQ_B, 0, True, (PREV_TILES_B + 1) * TQ_B)
        bias_as = _bias_tiles(t5_fn, jnp.zeros((H_A,)), H_A, t, nk_as, 1, past, 0, 0, False, past + t)
        bias_bs = _bias_tiles(rel_fn, jnp.zeros((H_B,)), H_B, t, nk_bs, 1, past, past - lb, 0, True, lb + t)

        mod = _modulation(c_all, w_ada[l], b_ada[l])
        shift = mod[:, :d].reshape(c_pad, 1, d)
        scale = mod[:, d:2 * d].reshape(c_pad, 1, d)
        gate = mod[:, 2 * d:].reshape(c_pad, 1, d)
        w_in_bf = w_in[l].astype(BF16)
        w_oa_bf, w_ob_bf, w_out_bf = w_oa[l].astype(BF16), w_ob[l].astype(BF16), w_out[l].astype(BF16)
        tile8 = lambda g: jnp.tile(g, SEG // g.shape[0]).reshape(1, SEG)
        gains = (tile8(g_qa[l]), tile8(g_ka[l]), tile8(g_qb[l]), tile8(g_kb[l]))
        lamv = jnp.stack([lam_q1[l], lam_k1[l], lam_q2[l], lam_k2[l]])
        gsub = g_subln[l].reshape(1, LANES)

        (qa, ka32, kab, va32, vab, ga, qb, kb32, kbb, vb32, vbb, gb, mg) = _project(
            xp, shift[:bp], scale[:bp], g_norm[l], w_in_bf, *gains, pm, 1, TM_PROJ)
        bounded_a = _logits_bounded(g_qa[l], g_ka[l], HD_A, t5_bias, cfar)
        bounded_b = _logits_bounded(g_qb[l], g_kb[l], HD_B, rel_bias_b[l], jnp.zeros((H_B,)))
        oa = _attn_a_prompt(qa, kab, vab, ga, bias_ap, bounded_a, lamv, gsub, TQ_A, lam_init)
        ob = _attn_b_prompt(qb, kbb, vbb, gb, bias_bp, bounded_b, TQ_B)
        xp = _merge(xp, gate[:bp], oa, ob, mg, w_oa_bf, w_ob_bf, w_out_bf, 1, TM_MERGE)
        outs[0].append(ka32.reshape(bp, s, H_A, 2 * HD_A))
        outs[1].append(va32.reshape(bp, s, H_A, 2 * HD_A))
        outs[2].append(kb32[:, s - n_keep:].reshape(bp, n_keep, H_B, HD_B))
        outs[3].append(vb32[:, s - n_keep:].reshape(bp, n_keep, H_B, HD_B))

        (qa, ka32, kab, va32, vab, ga, qb, kb32, kbb, vb32, vbb, gb, mg) = _project(
            xs, shift[bp:c_rows], scale[bp:c_rows], g_norm[l], w_in_bf, *gains, pm, bs, t)
        by_head = lambda c: c.reshape(bs, past * H_A, LANES)
        to_rows = lambda c: jnp.transpose(c, (0, 2, 3, 1)).reshape(bs, W_B, lb)
        oa, ob = _attn_sample(qa, kab, vab, ga, by_head(cache_a_k[l]), by_head(cache_a_v[l]), bias_as, lamv, gsub,
                              qb, kbb, vbb, gb, to_rows(cache_b_k[l]), to_rows(cache_b_v[l]), bias_bs, lam_init)
        xs = _merge(xs, gate[bp:c_rows], oa, ob, mg, w_oa_bf, w_ob_bf, w_out_bf, bs, t)
        outs[4].append(ka32.reshape(bs, t, H_A, 2 * HD_A))
        outs[5].append(va32.reshape(bs, t, H_A, 2 * HD_A))
        outs[6].append(kb32.reshape(bs, t, H_B, HD_B))
        outs[7].append(vb32.reshape(bs, t, H_B, HD_B))

    return (xp, xs) + tuple(jnp.stack(o) for o in outs)
```

```python
import functools
import math

import jax
import jax.numpy as jnp
from jax import lax
from jax.experimental import pallas as pl
from jax.experimental.pallas import tpu as pltpu

CHUNK = 64
H_A = 4
HD_A = 64
W_A = H_A * 2 * HD_A
H_B = 8
HD_B = 64
W_B = H_B * HD_B
BAND_CHUNKS = 8
BAND_PAST = BAND_CHUNKS * CHUNK
REL_CLIP_B = 128
T5_BUCKETS = 32
T5_MAX_EXACT = 8
T5_MAX_DIST = 128
EPS = 1e-6
NEG = -1e30

LANES = 128
SUBLANES = 8
SEG = 512
NORM_GROUP = 256
MAX_DIRECT_LOGIT = 60.0
BF16_ROUND_MARGIN = 1.02
VMEM_LIMIT = 56 * 1024 * 1024

LOG2E = math.log2(math.e)

F32 = jnp.float32
BF16 = jnp.bfloat16


def _t5_bucket(rel):
    half = T5_BUCKETS // 2
    assert (T5_MAX_DIST // T5_MAX_EXACT) ** 2 == 2 ** (half - T5_MAX_EXACT)
    ret = jnp.where(rel > 0, half, 0)
    n = jnp.abs(rel)
    large = T5_MAX_EXACT + sum((n * n >= T5_MAX_EXACT ** 2 * 2 ** j).astype(jnp.int32)
                               for j in range(1, half - T5_MAX_EXACT))
    large = jnp.minimum(large, half - 1)
    return ret + jnp.where(n < T5_MAX_EXACT, n, large)


def _t5_run(t5_bias, first_rel, count):
    assert max(abs(first_rel), abs(first_rel + count)) < 2 ** 15
    rel = first_rel + jnp.arange(count, dtype=jnp.int32)
    hit = _t5_bucket(rel)[:, None, None] == jnp.arange(T5_BUCKETS, dtype=jnp.int32)[None, :, None]
    return jnp.sum(jnp.where(hit, t5_bias[None], 0.0), axis=1)


def _clipped_run(table, first_rel, count):
    n = table.shape[0]
    first = first_rel + (n - 1) // 2
    n_lo = min(max(-first, 0), count)
    n_hi = min(max(first + count - n, 0), count)
    mid = count - n_lo - n_hi
    parts = [jnp.broadcast_to(table[:1], (n_lo, table.shape[1])),
             table[first + n_lo:first + n_lo + mid] if mid > 0 else table[:0],
             jnp.broadcast_to(table[n - 1:], (n_hi, table.shape[1]))]
    return jnp.concatenate(parts, axis=0)


def _far_bucket(n):
    half = T5_BUCKETS // 2
    return min(T5_MAX_EXACT + int(math.log(n / T5_MAX_EXACT) / math.log(T5_MAX_DIST / T5_MAX_EXACT) * (half - T5_MAX_EXACT)), half - 1)


def _nt(a, b):
    return lax.dot_general(a, b, (((1,), (1,)), ((), ())), preferred_element_type=F32)


def _silu(x):
    return x * (1.0 / (1.0 + jnp.exp(-x)))


def _sigmoid(x):
    return 1.0 / (1.0 + jnp.exp(-x))


def _mod_kernel(c_ref, w_ref, b_ref, o_ref):
    a, w = _silu(c_ref[...]), w_ref[...]
    a_hi, w_hi = a.astype(BF16), w.astype(BF16)
    a_lo, w_lo = (a - a_hi.astype(F32)).astype(BF16), (w - w_hi.astype(F32)).astype(BF16)
    dot = functools.partial(jnp.dot, preferred_element_type=F32)
    o_ref[...] = dot(a_hi, w_hi) + (dot(a_hi, w_lo) + dot(a_lo, w_hi)) + b_ref[...]


def _modulation(c_all, w_ada, b_ada):
    rows, d = c_all.shape
    n_out = w_ada.shape[1]
    tn = d
    return pl.pallas_call(
        _mod_kernel,
        grid=(n_out // tn,),
        in_specs=[pl.BlockSpec((rows, d), lambda j: (0, 0)),
                  pl.BlockSpec((d, tn), lambda j: (0, j)),
                  pl.BlockSpec((1, tn), lambda j: (0, j))],
        out_specs=pl.BlockSpec((rows, tn), lambda j: (0, j)),
        out_shape=jax.ShapeDtypeStruct((rows, n_out), F32),
        name="adaln_mod",
    )(c_all, w_ada, b_ada.reshape(1, n_out))


def _proj_kernel(x_ref, shift_ref, scale_ref, gn_ref, w_ref, gqa_ref, gka_ref, gqb_ref, gkb_ref, pm_ref,
                 qa_ref, ka32_ref, kab_ref, va32_ref, vab_ref, ga_ref,
                 qb_ref, kb32_ref, kbb_ref, vb32_ref, vbb_ref, gb_ref, mg_ref):
    nb, t, d = x_ref.shape
    rows = nb * t
    x = x_ref[...]
    ms = jnp.mean(x * x, axis=-1, keepdims=True)
    xn = x * lax.rsqrt(ms + EPS) * gn_ref[...]
    h = xn * (1.0 + scale_ref[...]) + shift_ref[...]
    hb = h.reshape(rows, d).astype(BF16)

    def seg(c, width=SEG):
        return jnp.dot(hb, w_ref[:, c * SEG:c * SEG + width], preferred_element_type=F32)

    def head_norm(y, g_ref):
        sq = y * y
        hi = sq.astype(BF16)
        pm = pm_ref[...]
        parts = []
        for c in range(SEG // NORM_GROUP):
            sl = slice(c * NORM_GROUP, (c + 1) * NORM_GROUP)
            gms = jnp.dot(hi[:, sl], pm, preferred_element_type=F32)
            parts.append(y[:, sl] * lax.rsqrt(gms + EPS))
        return jnp.concatenate(parts, axis=1) * g_ref[...]

    def put(ref, y):
        ref[...] = y.astype(ref.dtype).reshape(ref.shape)

    def put_by_head(ref, y):
        for hh in range(H_A):
            ref[:, pl.ds(hh, t, stride=H_A), :] = y[:, hh * LANES:(hh + 1) * LANES].reshape(nb, t, LANES)

    put(qa_ref, head_norm(seg(0), gqa_ref) * (HD_A ** -0.5 * LOG2E))
    ka = head_norm(seg(1), gka_ref)
    put_by_head(ka32_ref, ka)
    put(kab_ref, ka)
    va = seg(2)
    put_by_head(va32_ref, va)
    put(vab_ref, va)
    put(ga_ref, seg(3))
    put(qb_ref, head_norm(seg(4), gqb_ref) * (HD_B ** -0.5 * LOG2E))
    kb = head_norm(seg(5), gkb_ref)
    put(kb32_ref, kb)
    put(kbb_ref, kb)
    vb = seg(6)
    put(vb32_ref, vb)
    put(vbb_ref, vb)
    put(gb_ref, seg(7))
    for c in range(8, 12):
        mg_ref[:, :, (c - 8) * SEG:(c - 7) * SEG] = seg(c).astype(mg_ref.dtype).reshape(nb, t, SEG)


N_PROJ_INPUTS = 10
N_PROJ_OUTPUTS = 13
W_REF_INDEX = 4


def _proj_staged_kernel(*refs):
    ins = refs[:N_PROJ_INPUTS]
    outs = refs[N_PROJ_INPUTS:N_PROJ_INPUTS + N_PROJ_OUTPUTS]
    w_bf_hbm, w_vmem, stage, chunk_sems, out_sem = refs[N_PROJ_INPUTS + N_PROJ_OUTPUTS:]
    w_hbm = ins[W_REF_INDEX]
    n = pl.program_id(0) * pl.num_programs(1) + pl.program_id(1)
    last = pl.num_programs(0) * pl.num_programs(1) - 1
    n_chunks = w_vmem.shape[1] // SEG

    def chunk_copy(c):
        return pltpu.make_async_copy(w_hbm.at[:, pl.ds(c * SEG, SEG)], stage.at[c % 2], chunk_sems.at[c % 2])

    write_back = pltpu.make_async_copy(w_vmem, w_bf_hbm, out_sem)

    @pl.when(n == 0)
    def _():
        chunk_copy(0).start()
        for c in range(n_chunks):
            if c + 1 < n_chunks:
                chunk_copy(c + 1).start()
            chunk_copy(c).wait()
            w_vmem[:, c * SEG:(c + 1) * SEG] = stage[c % 2].astype(BF16)
        write_back.start()

    _proj_kernel(*ins[:W_REF_INDEX], w_vmem, *ins[W_REF_INDEX + 1:], *outs)

    @pl.when(n == last)
    def _():
        write_back.wait()


def _project(x, shift, scale, g_norm, w_in, gqa, gka, gqb, gkb, pm, nb, t):
    bx, sx, d = x.shape
    n_cols = w_in.shape[1]
    staged = w_in.dtype != BF16
    grid = (bx // nb, sx // t)
    tok = lambda w: pl.BlockSpec((nb, t, w), lambda b, i: (b, i, 0))
    per_b = pl.BlockSpec((nb, 1, d), lambda b, i: (b, 0, 0))
    const = lambda shp: pl.BlockSpec(shp, lambda b, i: (0,) * len(shp))
    sds = lambda w, dt: jax.ShapeDtypeStruct((bx, sx, w), dt)
    by_head = jax.ShapeDtypeStruct((bx, sx * H_A, LANES), F32)
    out_shape = (sds(SEG, BF16), by_head, sds(SEG, BF16), by_head, sds(SEG, BF16), sds(SEG, BF16),
                 sds(SEG, BF16), sds(SEG, F32), sds(SEG, BF16), sds(SEG, F32), sds(SEG, BF16), sds(SEG, BF16),
                 sds(4 * SEG, BF16))
    out_specs = tuple(pl.BlockSpec((nb, s.shape[1] // (sx // t), s.shape[2]), lambda b, i: (b, i, 0)) for s in out_shape)
    assert len(out_shape) == N_PROJ_OUTPUTS
    w_spec = pl.BlockSpec((d, n_cols), lambda b, i: (0, 0), pipeline_mode=pl.Buffered(1))
    scratch = []
    if staged:
        w_spec = pl.BlockSpec(memory_space=pl.ANY)
        out_shape += (jax.ShapeDtypeStruct((d, n_cols), BF16),)
        out_specs += (pl.BlockSpec(memory_space=pl.ANY),)
        scratch = [pltpu.VMEM((d, n_cols), BF16), pltpu.VMEM((2, d, SEG), F32),
                   pltpu.SemaphoreType.DMA((2,)), pltpu.SemaphoreType.DMA(())]
    return pl.pallas_call(
        _proj_staged_kernel if staged else _proj_kernel,
        grid=grid,
        in_specs=[tok(d), per_b, per_b, const((1, d)), w_spec,
                  const((1, SEG)), const((1, SEG)), const((1, SEG)), const((1, SEG)),
                  const((NORM_GROUP, NORM_GROUP))],
        out_specs=out_specs,
        out_shape=out_shape,
        scratch_shapes=scratch,
        compiler_params=pltpu.CompilerParams(dimension_semantics=("arbitrary", "arbitrary"),
                                             vmem_limit_bytes=VMEM_LIMIT),
        name="in_proj",
    )(x, shift, scale, g_norm.reshape(1, d), w_in, gqa, gka, gqb, gkb, pm)


def _bias_tile_kernel(off_ref, gen_ref, o_ref, *, tq, nk, qbase, kbase0, kstep, band, nvalid):
    v = pl.program_id(1)
    g = gen_ref[0, 0]
    x = jnp.broadcast_to(g, (tq, g.shape[-1]))
    y = pltpu.roll(x, 0, 1, stride=1, stride_axis=0)
    y = (y[:, :nk] - off_ref[pl.program_id(0)]) * LOG2E
    row = lax.broadcasted_iota(jnp.int32, (tq, nk), 0)
    col = lax.broadcasted_iota(jnp.int32, (tq, nk), 1)
    qc = (qbase + row) // CHUNK
    kc = (kbase0 - v * kstep + col) // CHUNK
    y = jnp.where(kc <= qc, y, NEG)
    if band:
        y = jnp.where(kc >= qc - BAND_CHUNKS, y, NEG)
    y = jnp.where(col < nvalid, y, NEG)
    o_ref[0, 0] = y


def _bias_tiles(table_fn, offset, n_heads, tq, nk, n_var, qbase, kbase0, kstep, band, nvalid):
    length = 1 << (tq + nk - 1).bit_length()
    assert length >= tq + nk - 1 and min(kbase0 - (n_var - 1) * kstep, qbase) >= 0
    gen = jnp.stack([jnp.concatenate([table_fn(kbase0 - v * kstep - qbase, nk),
                                      table_fn(kbase0 - v * kstep - qbase + nk - length, length - nk)], axis=0)
                     for v in range(n_var)])
    gen = jnp.transpose(gen, (2, 0, 1)).astype(F32).reshape(n_heads, n_var, 1, length)
    return pl.pallas_call(
        functools.partial(_bias_tile_kernel, tq=tq, nk=nk, qbase=qbase, kbase0=kbase0, kstep=kstep,
                          band=band, nvalid=nvalid),
        grid=(n_heads, n_var),
        in_specs=[pl.BlockSpec(memory_space=pltpu.SMEM),
                  pl.BlockSpec((1, 1, 1, length), lambda h, v: (h, v, 0, 0))],
        out_specs=pl.BlockSpec((1, 1, tq, nk), lambda h, v: (h, v, 0, 0)),
        out_shape=jax.ShapeDtypeStruct((n_heads, n_var, tq, nk), F32),
        name="bias_tiles",
    )(offset.astype(F32), gen)


def _split_halves(q):
    lane = lax.broadcasted_iota(jnp.int32, q.shape, 1)
    zero = jnp.zeros_like(q)
    return jnp.where(lane < LANES // 2, q, zero), jnp.where(lane >= LANES // 2, q, zero)


def _online_update(carry, s, v):
    m, l, acc = carry
    m_new = jnp.maximum(m, jnp.max(s, axis=-1, keepdims=True))
    alpha = jnp.exp2(m - m_new)
    p = jnp.exp2(s - m_new)
    l = alpha * l + jnp.sum(p, axis=-1, keepdims=True)
    acc = alpha * acc + jnp.dot(p.astype(BF16), v, preferred_element_type=F32)
    return m_new, l, acc


def _diff_init(tq):
    return (jnp.full((tq, 1), -jnp.inf, F32), jnp.zeros((tq, 1), F32), jnp.zeros((tq, LANES), F32))


def _diff_finish(c0, c1, lamv_ref, gsub_ref, g, lam_init):
    lamv = lamv_ref[...]
    e1 = jnp.exp(jnp.sum(lamv[0:1] * lamv[1:2], axis=-1, keepdims=True))
    e2 = jnp.exp(jnp.sum(lamv[2:3] * lamv[3:4], axis=-1, keepdims=True))
    lam = e1 - e2 + lam_init
    o = c0[2] / c0[1] - lam * (c1[2] / c1[1])
    o = o * lax.rsqrt(jnp.mean(o * o, axis=-1, keepdims=True) + EPS) * gsub_ref[...]
    o = o * (1.0 - lam_init)
    return o * _silu(g.astype(F32))


def _lane_partial_sum(p):
    out = p[:, :LANES]
    for c in range(1, p.shape[1] // LANES):
        out = out + p[:, c * LANES:(c + 1) * LANES]
    return out


def _attn_a_prompt_kernel(par_ref, q_ref, k_ref, v_ref, g_ref, bias_ref, lamv_ref, gsub_ref, o_ref, l_ref, acc_ref,
                          *, tq, lam_init):
    i = pl.program_id(2)
    has_near = i > 0
    n_far = jnp.maximum(i - 1, 0)
    near = pl.multiple_of(n_far * tq, tq)
    diag = pl.multiple_of(i * tq, tq)
    half = tq // 2
    bounded = par_ref[0] > 0.5
    n_heads = q_ref.shape[2] // LANES
    hs = [slice(hh * LANES, (hh + 1) * LANES) for hh in range(n_heads)]

    def finish_head(hh, c0, c1):
        o = _diff_finish(c0, c1, lamv_ref, gsub_ref, g_ref[0, :, hs[hh]], lam_init)
        o_ref[0, :, hs[hh]] = o.astype(o_ref.dtype)

    def run_online():
        for hh in range(n_heads):
            q0, q1 = _split_halves(q_ref[0, :, hs[hh]])

            def tile(carry, start, bias, mask=None):
                c0, c1 = carry
                k = k_ref[0, pl.ds(start, tq), hs[hh]]
                v = v_ref[0, pl.ds(start, tq), hs[hh]]
                s0, s1 = _nt(q0, k), _nt(q1, k)
                if bias is not None:
                    b = bias_ref[hh, 0, :, bias] if mask is None else bias_ref[hh, 0, :, bias] + mask
                    s0, s1 = s0 + b, s1 + b
                return _online_update(c0, s0, v), _online_update(c1, s1, v)

            init = _diff_init(tq)
            carry = lax.fori_loop(0, n_far, lambda j, c: tile(c, pl.multiple_of(j * tq, tq), None), (init, init))
            carry = tile(carry, near, slice(0, tq), jnp.where(has_near, 0.0, NEG))
            c0, c1 = tile(carry, diag, slice(tq, 2 * tq))
            finish_head(hh, c0, c1)

    def run_bounded():
        qs = [_split_halves(q_ref[0, :, sl]) for sl in hs]
        q_all = [jnp.concatenate(pair, axis=0) for pair in qs]

        def accumulate(hh, start, nk, rows=None, bias_cols=None, mask=None, init=False):
            k = k_ref[0, pl.ds(start, nk), hs[hh]]
            v = v_ref[0, pl.ds(start, nk), hs[hh]]
            if rows is None:
                rows, q2 = slice(0, tq), q_all[hh]
            else:
                q2 = jnp.concatenate([qs[hh][0][rows], qs[hh][1][rows]], axis=0)
            s = _nt(q2, k)
            if bias_cols is not None:
                bias = bias_ref[hh, 0, rows, bias_cols]
                if mask is not None:
                    bias = bias + mask
                s = s + jnp.concatenate([bias, bias], axis=0)
            p = jnp.exp2(s)
            dl = _lane_partial_sum(p)
            da = jnp.dot(p.astype(BF16), v, preferred_element_type=F32)
            r = q2.shape[0] // 2
            for mp in range(2):
                if init:
                    l_ref[hh, mp, rows, :] = dl[mp * r:(mp + 1) * r]
                    acc_ref[hh, mp, rows, :] = da[mp * r:(mp + 1) * r]
                else:
                    l_ref[hh, mp, rows, :] += dl[mp * r:(mp + 1) * r]
                    acc_ref[hh, mp, rows, :] += da[mp * r:(mp + 1) * r]

        near_mask = jnp.where(has_near, 0.0, NEG)
        for hh in range(n_heads):
            accumulate(hh, near, tq, bias_cols=slice(0, tq), mask=near_mask, init=True)

        def far(j, carry):
            for hh in range(n_heads):
                accumulate(hh, pl.multiple_of(j * tq, tq), tq)
            return carry

        lax.fori_loop(0, n_far, far, 0)
        for hh in range(n_heads):
            accumulate(hh, diag, half, slice(0, half), slice(tq, tq + half))
            accumulate(hh, diag, tq, slice(half, tq), slice(tq, 2 * tq))
            c0 = (None, jnp.sum(l_ref[hh, 0], axis=-1, keepdims=True), acc_ref[hh, 0])
            c1 = (None, jnp.sum(l_ref[hh, 1], axis=-1, keepdims=True), acc_ref[hh, 1])
            finish_head(hh, c0, c1)

    pl.when(bounded)(run_bounded)
    pl.when(jnp.logical_not(bounded))(run_online)


def _logits_bounded(g_q, g_k, head_dim, table, offset):
    qk = math.sqrt(head_dim) * LOG2E * jnp.max(jnp.abs(g_q)) * jnp.max(jnp.abs(g_k)) * BF16_ROUND_MARGIN
    bias = jnp.max(jnp.abs(table - offset[None, :])) * LOG2E
    return (qk + bias <= MAX_DIRECT_LOGIT).astype(F32).reshape(1)


def _attn_a_prompt(qa, kab, vab, ga, bias, bounded, lamv, gsub, tq, lam_init):
    b, s, _ = qa.shape
    assert bias.shape[1] == 1 and bias.shape[3] == 2 * tq
    hps = HEADS_PER_STEP_A
    tile = pl.BlockSpec((1, tq, hps * LANES), lambda bi, h, i: (bi, i, h))
    single = pl.Buffered(1)
    full = pl.BlockSpec((1, s, hps * LANES), lambda bi, h, i: (bi, 0, h))
    return pl.pallas_call(
        functools.partial(_attn_a_prompt_kernel, tq=tq, lam_init=lam_init),
        grid=(b, H_A // hps, s // tq),
        in_specs=[pl.BlockSpec(memory_space=pltpu.SMEM), tile, full, full, tile,
                  pl.BlockSpec((hps, 1, tq, 2 * tq), lambda bi, h, i: (h, 0, 0, 0), pipeline_mode=single),
                  pl.BlockSpec((4, HD_A), lambda bi, h, i: (0, 0)),
                  pl.BlockSpec((1, LANES), lambda bi, h, i: (0, 0))],
        out_specs=tile,
        out_shape=jax.ShapeDtypeStruct((b, s, W_A), BF16),
        scratch_shapes=[pltpu.VMEM((hps, 2, tq, LANES), F32), pltpu.VMEM((hps, 2, tq, LANES), F32)],
        compiler_params=pltpu.CompilerParams(dimension_semantics=("arbitrary", "arbitrary", "arbitrary"),
                                             vmem_limit_bytes=VMEM_LIMIT),
        name="diff_attn_prompt",
    )(bounded, qa, kab, vab, ga, bias, lamv, gsub)


def _attn_a_sample_kernel(q_ref, kc_ref, vc_ref, kn_ref, vn_ref, g_ref, bias_ref, lamv_ref, gsub_ref, o_ref, *, lam_init):
    tq = q_ref.shape[1]
    nk = bias_ref.shape[3]
    past = kc_ref.shape[1] // H_A
    pad = jnp.zeros((nk - past - tq, LANES), BF16)
    for hh in range(H_A):
        sl = slice(hh * LANES, (hh + 1) * LANES)
        rows = pl.ds(hh, past, stride=H_A)
        qst = jnp.concatenate(_split_halves(q_ref[0, :, sl]), axis=0)
        k = jnp.concatenate([kc_ref[0, rows, :].astype(BF16), kn_ref[0, :, sl], pad], axis=0)
        v = jnp.concatenate([vc_ref[0, rows, :].astype(BF16), vn_ref[0, :, sl], pad], axis=0)
        bias = bias_ref[hh, 0]
        c = _online_update(_diff_init(2 * tq), _nt(qst, k) + jnp.concatenate([bias, bias], axis=0), v)
        c0, c1 = tuple(x[:tq] for x in c), tuple(x[tq:] for x in c)
        o_ref[0, :, sl] = _diff_finish(c0, c1, lamv_ref, gsub_ref, g_ref[0, :, sl], lam_init).astype(o_ref.dtype)


def _band_pair(q, k, v, g, bias_ref, bounded, first_head, piece_masks):
    t = q.shape[0]
    qst = jnp.concatenate(_split_halves(q), axis=0)
    bias = jnp.concatenate([bias_ref[first_head, 0], bias_ref[first_head + 1, 0]], axis=0)
    w = bias.shape[1] // (len(piece_masks) + 1)
    bias = jnp.concatenate([bias[:, c * w:(c + 1) * w] + m for c, m in enumerate(piece_masks)]
                           + [bias[:, len(piece_masks) * w:]], axis=1)
    s = _nt(qst, k) + bias
    if not bounded:
        s = s - jnp.max(s, axis=-1, keepdims=True)
    p = jnp.exp2(s)
    l = jnp.sum(p, axis=-1, keepdims=True)
    o2 = jnp.dot(p.astype(BF16), v, preferred_element_type=F32) / l
    lane = lax.broadcasted_iota(jnp.int32, (t, LANES), 1)
    o = jnp.where(lane < LANES // 2, o2[:t], o2[t:])
    return o * _silu(g.astype(F32))


def _attn_b_prompt_kernel(par_ref, q_ref, k_ref, v_ref, g_ref, bias_ref, o_ref, *, tq, n_prev):
    i = pl.program_id(2)
    tiles = q_ref.shape[1] // tq
    bounded = par_ref[0] > 0.5

    def run(direct):
        for sub in range(tiles):
            first = tiles * i + sub - n_prev
            rows = slice(sub * tq, (sub + 1) * tq)
            starts = [pl.multiple_of(jnp.maximum(first + c, 0) * tq, tq) for c in range(n_prev + 1)]
            masks = [jnp.where(first + c >= 0, 0.0, NEG) for c in range(n_prev)]
            for pp in range(q_ref.shape[2] // LANES):
                sl = slice(pp * LANES, (pp + 1) * LANES)
                k = jnp.concatenate([k_ref[0, pl.ds(st, tq), sl] for st in starts], axis=0)
                v = jnp.concatenate([v_ref[0, pl.ds(st, tq), sl] for st in starts], axis=0)
                o = _band_pair(q_ref[0, rows, sl], k, v, g_ref[0, rows, sl], bias_ref, direct, 2 * pp, masks)
                o_ref[0, rows, sl] = o.astype(o_ref.dtype)

    pl.when(bounded)(lambda: run(True))
    pl.when(jnp.logical_not(bounded))(lambda: run(False))


def _attn_b_prompt(qb, kbb, vbb, gb, bias, bounded, tq):
    b, s, w = qb.shape
    n_prev = bias.shape[3] // tq - 1
    assert bias.shape[1] == 1 and bias.shape[3] == (n_prev + 1) * tq
    tps = TILES_PER_STEP_B
    tile = pl.BlockSpec((1, tps * tq, w), lambda bi, p, i: (bi, i, p))
    full = pl.BlockSpec((1, s, w), lambda bi, p, i: (bi, 0, p))
    return pl.pallas_call(
        functools.partial(_attn_b_prompt_kernel, tq=tq, n_prev=n_prev),
        grid=(b, 1, s // (tps * tq)),
        in_specs=[pl.BlockSpec(memory_space=pltpu.SMEM), tile, full, full, tile,
                  pl.BlockSpec((H_B, 1, tq, (n_prev + 1) * tq), lambda bi, p, i: (0, 0, 0, 0),
                               pipeline_mode=pl.Buffered(1))],
        out_specs=tile,
        out_shape=jax.ShapeDtypeStruct((b, s, W_B), BF16),
        compiler_params=pltpu.CompilerParams(dimension_semantics=("arbitrary", "arbitrary", "arbitrary"),
                                             vmem_limit_bytes=VMEM_LIMIT),
        name="band_attn_prompt",
    )(bounded, qb, kbb, vbb, gb, bias)


def _attn_b_sample_kernel(q_ref, kc_ref, vc_ref, kn_ref, vn_ref, g_ref, bias_ref, o_ref):
    nk = bias_ref.shape[3]
    past, t = kc_ref.shape[2], kn_ref.shape[1]
    pad = jnp.zeros((nk - past - t, LANES), BF16)
    lane = lax.broadcasted_iota(jnp.int32, (t, LANES), 1)
    for pp in range(q_ref.shape[2] // LANES):
        sl = slice(pp * LANES, (pp + 1) * LANES)
        qst = jnp.concatenate(_split_halves(q_ref[0, :, sl]), axis=0)
        k_new = jnp.concatenate([kn_ref[0, :, sl], pad], axis=0)
        v_new = jnp.concatenate([vn_ref[0, :, sl], pad], axis=0)
        s = jnp.concatenate([jnp.dot(qst, kc_ref[0, sl, :].astype(BF16), preferred_element_type=F32),
                             _nt(qst, k_new)], axis=1)
        s = s + jnp.concatenate([bias_ref[2 * pp, 0], bias_ref[2 * pp + 1, 0]], axis=0)
        p = jnp.exp2(s - jnp.max(s, axis=-1, keepdims=True))
        l = jnp.sum(p, axis=-1, keepdims=True)
        pb = p.astype(BF16)
        o2 = (_nt(pb[:, :past], vc_ref[0, sl, :].astype(BF16))
              + jnp.dot(pb[:, past:], v_new, preferred_element_type=F32)) / l
        o = jnp.where(lane < LANES // 2, o2[:t], o2[t:])
        o_ref[0, :, sl] = (o * _silu(g_ref[0, :, sl].astype(F32))).astype(o_ref.dtype)


N_A_SAMPLE_INPUTS = 9
N_B_SAMPLE_INPUTS = 7


def _attn_sample_kernel(*refs, lam_init):
    a_in = refs[:N_A_SAMPLE_INPUTS]
    b_in = refs[N_A_SAMPLE_INPUTS:N_A_SAMPLE_INPUTS + N_B_SAMPLE_INPUTS]
    oa_ref, ob_ref = refs[N_A_SAMPLE_INPUTS + N_B_SAMPLE_INPUTS:]
    _attn_a_sample_kernel(*a_in, oa_ref, lam_init=lam_init)
    _attn_b_sample_kernel(*b_in, ob_ref)


def _attn_sample(qa, kab, vab, ga, cache_a_k, cache_a_v, bias_a, lamv, gsub,
                 qb, kbb, vbb, gb, cache_b_kt, cache_b_vt, bias_b, lam_init):
    b, t, _ = qa.shape
    assert cache_b_kt.shape[2] % LANES == 0
    new = pl.BlockSpec((1, t, SEG), lambda bi: (bi, 0, 0))
    whole = lambda x: pl.BlockSpec((1,) + x.shape[1:], lambda bi: (bi,) + (0,) * (x.ndim - 1))
    const = lambda x: pl.BlockSpec(x.shape, lambda bi: (0,) * x.ndim)
    return pl.pallas_call(
        functools.partial(_attn_sample_kernel, lam_init=lam_init),
        grid=(b,),
        in_specs=[new, whole(cache_a_k), whole(cache_a_v), new, new, new, const(bias_a), const(lamv), const(gsub),
                  new, whole(cache_b_kt), whole(cache_b_vt), new, new, new, const(bias_b)],
        out_specs=(new, new),
        out_shape=(jax.ShapeDtypeStruct((b, t, W_A), BF16), jax.ShapeDtypeStruct((b, t, W_B), BF16)),
        name="attn_sample",
    )(qa, cache_a_k, cache_a_v, kab, vab, ga, bias_a, lamv, gsub, qb, cache_b_kt, cache_b_vt, kbb, vbb, gb, bias_b)


def _merge_kernel(x_ref, gate_ref, oa_ref, ob_ref, mg_ref, woa_ref, wob_ref, wout_ref, o_ref):
    nb, t, d = x_ref.shape
    rows = nb * t
    ya = jnp.dot(oa_ref[...].reshape(rows, W_A), woa_ref[...], preferred_element_type=F32)
    yb = jnp.dot(ob_ref[...].reshape(rows, W_B), wob_ref[...], preferred_element_type=F32)
    mg = mg_ref[...].reshape(rows, 2 * d).astype(F32)
    m = _sigmoid(mg[:, :d]) * ya + _sigmoid(mg[:, d:]) * yb
    y = jnp.dot(m.astype(BF16), wout_ref[...], preferred_element_type=F32)
    o_ref[...] = x_ref[...] + gate_ref[...] * y.reshape(nb, t, d)


def _merge(x, gate, oa, ob, mg, w_oa_bf, w_ob_bf, w_out_bf, nb, t):
    bx, sx, d = x.shape
    tok = lambda w: pl.BlockSpec((nb, t, w), lambda b, i: (b, i, 0))
    const = lambda shp: pl.BlockSpec(shp, lambda b, i: (0,) * len(shp))
    return pl.pallas_call(
        _merge_kernel,
        grid=(bx // nb, sx // t),
        in_specs=[tok(d), pl.BlockSpec((nb, 1, d), lambda b, i: (b, 0, 0)), tok(W_A), tok(W_B), tok(2 * d),
                  const((W_A, d)), const((W_B, d)), const((d, d))],
        out_specs=tok(d),
        out_shape=jax.ShapeDtypeStruct((bx, sx, d), F32),
        compiler_params=pltpu.CompilerParams(dimension_semantics=("arbitrary", "arbitrary"),
                                             vmem_limit_bytes=VMEM_LIMIT),
        name="merge_out",
    )(x, gate, oa, ob, mg, w_oa_bf, w_ob_bf, w_out_bf)


TQ_A = 512
HEADS_PER_STEP_A = 4
TQ_B = 256
PREV_TILES_B = BAND_PAST // TQ_B
TILES_PER_STEP_B = 2
TM_PROJ = 512
TM_MERGE = 1024
QBASE = 2048


def kernel(x_prompt, x_sample, cache_a_k, cache_a_v, cache_b_k, cache_b_v, c_prompt, c_sample, g_norm, w_ada, b_ada, w_in, g_qa, g_ka, lam_q1, lam_k1, lam_q2, lam_k2, g_subln, t5_bias, g_qb, g_kb, rel_bias_b, w_oa, w_ob, w_out):
    xp, xs = x_prompt, x_sample
    bp, s, d = xp.shape
    bs, t, _ = xs.shape
    depth = w_in.shape[0]
    past = cache_a_k.shape[2]
    lb = cache_b_k.shape[2]
    n_keep = min(BAND_PAST, s)
    assert s % TM_PROJ == 0 and s % TM_MERGE == 0 and s % TQ_A == 0 and s % TQ_B == 0 and (TQ_A // 2) % CHUNK == 0 and TQ_B % CHUNK == 0
    assert PREV_TILES_B * TQ_B >= BAND_PAST and s % (TILES_PER_STEP_B * TQ_B) == 0 and past >= lb
    assert _far_bucket(TQ_A + 1) == T5_BUCKETS // 2 - 1

    r = jnp.arange(NORM_GROUP)
    pm = ((r[:, None] // HD_A) == (r[None, :] // HD_A)).astype(BF16) * (1.0 / HD_A)
    c_rows = bp + bs
    c_pad = -(-c_rows // SUBLANES) * SUBLANES
    c_all = jnp.concatenate([c_prompt, c_sample, jnp.zeros((c_pad - c_rows, d), F32)], axis=0)

    nk_as = -(-(past + t) // LANES) * LANES
    nk_bs = -(-(lb + t) // LANES) * LANES
    t5_fn = functools.partial(_t5_run, t5_bias)
    outs = [[] for _ in range(8)]
    for l in range(depth):
        lam_init = 0.8 - 0.6 * math.exp(-0.3 * l)
        rel_fn = functools.partial(_clipped_run, rel_bias_b[l])
        cfar = t5_bias[_t5_bucket(jnp.int32(-(TQ_A + 1)))]
        bias_ap = _bias_tiles(t5_fn, cfar, H_A, TQ_A, 2 * TQ_A, 1, QBASE, QBASE - TQ_A, 0, False, 2 * TQ_A)
        bias_bp = _bias_tiles(rel_fn, jnp.zeros((H_B,)), H_B, TQ_B, (PREV_TILES_B + 1) * TQ_B, 1, QBASE,
                              QBASE - PREV_TILES_B * TQ_B, 0, True, (PREV_TILES_B + 1) * TQ_B)
        bias_as = _bias_tiles(t5_fn, jnp.zeros((H_A,)), H_A, t, nk_as, 1, past, 0, 0, False, past + t)
        bias_bs = _bias_tiles(rel_fn, jnp.zeros((H_B,)), H_B, t, nk_bs, 1, past, past - lb, 0, True, lb + t)

        mod = _modulation(c_all, w_ada[l], b_ada[l])
        shift = mod[:, :d].reshape(c_pad, 1, d)
        scale = mod[:, d:2 * d].reshape(c_pad, 1, d)
        gate = mod[:, 2 * d:].reshape(c_pad, 1, d)
        w_oa_bf, w_ob_bf, w_out_bf = w_oa[l].astype(BF16), w_ob[l].astype(BF16), w_out[l].astype(BF16)
        tile8 = lambda g: jnp.tile(g, SEG // g.shape[0]).reshape(1, SEG)
        gains = (tile8(g_qa[l]), tile8(g_ka[l]), tile8(g_qb[l]), tile8(g_kb[l]))
        lamv = jnp.stack([lam_q1[l], lam_k1[l], lam_q2[l], lam_k2[l]])
        gsub = g_subln[l].reshape(1, LANES)

        (qa, ka32, kab, va32, vab, ga, qb, kb32, kbb, vb32, vbb, gb, mg, w_in_bf) = _project(
            xp, shift[:bp], scale[:bp], g_norm[l], w_in[l], *gains, pm, 1, TM_PROJ)
        bounded_a = _logits_bounded(g_qa[l], g_ka[l], HD_A, t5_bias, cfar)
        bounded_b = _logits_bounded(g_qb[l], g_kb[l], HD_B, rel_bias_b[l], jnp.zeros((H_B,)))
        oa = _attn_a_prompt(qa, kab, vab, ga, bias_ap, bounded_a, lamv, gsub, TQ_A, lam_init)
        ob = _attn_b_prompt(qb, kbb, vbb, gb, bias_bp, bounded_b, TQ_B)
        xp = _merge(xp, gate[:bp], oa, ob, mg, w_oa_bf, w_ob_bf, w_out_bf, 1, TM_MERGE)
        outs[0].append(ka32.reshape(bp, s, H_A, 2 * HD_A))
        outs[1].append(va32.reshape(bp, s, H_A, 2 * HD_A))
        outs[2].append(kb32[:, s - n_keep:].reshape(bp, n_keep, H_B, HD_B))
        outs[3].append(vb32[:, s - n_keep:].reshape(bp, n_keep, H_B, HD_B))

        (qa, ka32, kab, va32, vab, ga, qb, kb32, kbb, vb32, vbb, gb, mg) = _project(
            xs, shift[bp:c_rows], scale[bp:c_rows], g_norm[l], w_in_bf, *gains, pm, bs, t)
        by_head = lambda c: c.reshape(bs, past * H_A, LANES)
        to_rows = lambda c: jnp.transpose(c, (0, 2, 3, 1)).reshape(bs, W_B, lb)
        oa, ob = _attn_sample(qa, kab, vab, ga, by_head(cache_a_k[l]), by_head(cache_a_v[l]), bias_as, lamv, gsub,
                              qb, kbb, vbb, gb, to_rows(cache_b_k[l]), to_rows(cache_b_v[l]), bias_bs, lam_init)
        xs = _merge(xs, gate[bp:c_rows], oa, ob, mg, w_oa_bf, w_ob_bf, w_out_bf, bs, t)
        outs[4].append(ka32.reshape(bs, t, H_A, 2 * HD_A))
        outs[5].append(va32.reshape(bs, t, H_A, 2 * HD_A))
        outs[6].append(kb32.reshape(bs, t, H_B, HD_B))
        outs[7].append(vb32.reshape(bs, t, H_B, HD_B))

    return (xp, xs) + tuple(jnp.stack(o) for o in outs)
```

```python
import functools
import math

import jax
import jax.numpy as jnp
from jax import lax
from jax.experimental import pallas as pl
from jax.experimental.pallas import tpu as pltpu

CHUNK = 64
H_A = 4
HD_A = 64
W_A = H_A * 2 * HD_A
H_B = 8
HD_B = 64
W_B = H_B * HD_B
BAND_CHUNKS = 8
BAND_PAST = BAND_CHUNKS * CHUNK
REL_CLIP_B = 128
T5_BUCKETS = 32
T5_MAX_EXACT = 8
T5_MAX_DIST = 128
EPS = 1e-6
NEG = -1e30

LANES = 128
SUBLANES = 8
SEG = 512
NORM_GROUP = 256
MAX_DIRECT_LOGIT = 60.0
BF16_ROUND_MARGIN = 1.02
VMEM_LIMIT = 56 * 1024 * 1024

LOG2E = math.log2(math.e)

F32 = jnp.float32
BF16 = jnp.bfloat16


def _t5_bucket(rel):
    half = T5_BUCKETS // 2
    assert (T5_MAX_DIST // T5_MAX_EXACT) ** 2 == 2 ** (half - T5_MAX_EXACT)
    ret = jnp.where(rel > 0, half, 0)
    n = jnp.abs(rel)
    large = T5_MAX_EXACT + sum((n * n >= T5_MAX_EXACT ** 2 * 2 ** j).astype(jnp.int32)
                               for j in range(1, half - T5_MAX_EXACT))
    large = jnp.minimum(large, half - 1)
    return ret + jnp.where(n < T5_MAX_EXACT, n, large)


def _t5_run(t5_bias, first_rel, count):
    assert max(abs(first_rel), abs(first_rel + count)) < 2 ** 15
    rel = first_rel + jnp.arange(count, dtype=jnp.int32)
    hit = _t5_bucket(rel)[:, None, None] == jnp.arange(T5_BUCKETS, dtype=jnp.int32)[None, :, None]
    return jnp.sum(jnp.where(hit, t5_bias[None], 0.0), axis=1)


def _clipped_run(table, first_rel, count):
    n = table.shape[0]
    first = first_rel + (n - 1) // 2
    n_lo = min(max(-first, 0), count)
    n_hi = min(max(first + count - n, 0), count)
    mid = count - n_lo - n_hi
    parts = [jnp.broadcast_to(table[:1], (n_lo, table.shape[1])),
             table[first + n_lo:first + n_lo + mid] if mid > 0 else table[:0],
             jnp.broadcast_to(table[n - 1:], (n_hi, table.shape[1]))]
    return jnp.concatenate(parts, axis=0)


def _far_bucket(n):
    half = T5_BUCKETS // 2
    return min(T5_MAX_EXACT + int(math.log(n / T5_MAX_EXACT) / math.log(T5_MAX_DIST / T5_MAX_EXACT) * (half - T5_MAX_EXACT)), half - 1)


def _nt(a, b):
    return lax.dot_general(a, b, (((1,), (1,)), ((), ())), preferred_element_type=F32)


def _silu(x):
    return x * (1.0 / (1.0 + jnp.exp(-x)))


def _sigmoid(x):
    return 1.0 / (1.0 + jnp.exp(-x))


def _mod_kernel(c_ref, w_ref, b_ref, o_ref):
    a, w = _silu(c_ref[...]), w_ref[...]
    a_hi, w_hi = a.astype(BF16), w.astype(BF16)
    a_lo, w_lo = (a - a_hi.astype(F32)).astype(BF16), (w - w_hi.astype(F32)).astype(BF16)
    dot = functools.partial(jnp.dot, preferred_element_type=F32)
    o_ref[...] = dot(a_hi, w_hi) + (dot(a_hi, w_lo) + dot(a_lo, w_hi)) + b_ref[...]


def _modulation(c_all, w_ada, b_ada):
    rows, d = c_all.shape
    n_out = w_ada.shape[1]
    tn = d
    return pl.pallas_call(
        _mod_kernel,
        grid=(n_out // tn,),
        in_specs=[pl.BlockSpec((rows, d), lambda j: (0, 0)),
                  pl.BlockSpec((d, tn), lambda j: (0, j)),
                  pl.BlockSpec((1, tn), lambda j: (0, j))],
        out_specs=pl.BlockSpec((rows, tn), lambda j: (0, j)),
        out_shape=jax.ShapeDtypeStruct((rows, n_out), F32),
        name="adaln_mod",
    )(c_all, w_ada, b_ada.reshape(1, n_out))


def _proj_kernel(x_ref, shift_ref, scale_ref, gn_ref, w_ref, gqa_ref, gka_ref, gqb_ref, gkb_ref, pm_ref,
                 qa_ref, ka32_ref, kab_ref, va32_ref, vab_ref, ga_ref,
                 qb_ref, kb32_ref, kbb_ref, vb32_ref, vbb_ref, gb_ref, mg_ref):
    nb, t, d = x_ref.shape
    rows = nb * t
    x = x_ref[...]
    ms = jnp.mean(x * x, axis=-1, keepdims=True)
    xn = x * lax.rsqrt(ms + EPS) * gn_ref[...]
    h = xn * (1.0 + scale_ref[...]) + shift_ref[...]
    hb = h.reshape(rows, d).astype(BF16)

    def seg(c, width=SEG):
        return jnp.dot(hb, w_ref[:, c * SEG:c * SEG + width], preferred_element_type=F32)

    def head_norm(y, g_ref):
        sq = y * y
        hi = sq.astype(BF16)
        pm = pm_ref[...]
        parts = []
        for c in range(SEG // NORM_GROUP):
            sl = slice(c * NORM_GROUP, (c + 1) * NORM_GROUP)
            gms = jnp.dot(hi[:, sl], pm, preferred_element_type=F32)
            parts.append(y[:, sl] * lax.rsqrt(gms + EPS))
        return jnp.concatenate(parts, axis=1) * g_ref[...]

    def put(ref, y):
        ref[...] = y.astype(ref.dtype).reshape(ref.shape)

    def put_by_head(ref, y):
        for hh in range(H_A):
            ref[:, pl.ds(hh, t, stride=H_A), :] = y[:, hh * LANES:(hh + 1) * LANES].reshape(nb, t, LANES)

    put(qa_ref, head_norm(seg(0), gqa_ref) * (HD_A ** -0.5 * LOG2E))
    ka = head_norm(seg(1), gka_ref)
    put_by_head(ka32_ref, ka)
    put(kab_ref, ka)
    va = seg(2)
    put_by_head(va32_ref, va)
    put(vab_ref, va)
    put(ga_ref, seg(3))
    put(qb_ref, head_norm(seg(4), gqb_ref) * (HD_B ** -0.5 * LOG2E))
    kb = head_norm(seg(5), gkb_ref)
    put(kb32_ref, kb)
    put(kbb_ref, kb)
    vb = seg(6)
    put(vb32_ref, vb)
    put(vbb_ref, vb)
    put(gb_ref, seg(7))
    for c in range(8, 12):
        mg_ref[:, :, (c - 8) * SEG:(c - 7) * SEG] = seg(c).astype(mg_ref.dtype).reshape(nb, t, SEG)


def _project(x, shift, scale, g_norm, w_in_bf, gqa, gka, gqb, gkb, pm, nb, t):
    bx, sx, d = x.shape
    n_cols = w_in_bf.shape[1]
    grid = (bx // nb, sx // t)
    tok = lambda w: pl.BlockSpec((nb, t, w), lambda b, i: (b, i, 0))
    per_b = pl.BlockSpec((nb, 1, d), lambda b, i: (b, 0, 0))
    const = lambda shp: pl.BlockSpec(shp, lambda b, i: (0,) * len(shp))
    sds = lambda w, dt: jax.ShapeDtypeStruct((bx, sx, w), dt)
    by_head = jax.ShapeDtypeStruct((bx, sx * H_A, LANES), F32)
    out_shape = (sds(SEG, BF16), by_head, sds(SEG, BF16), by_head, sds(SEG, BF16), sds(SEG, BF16),
                 sds(SEG, BF16), sds(SEG, F32), sds(SEG, BF16), sds(SEG, F32), sds(SEG, BF16), sds(SEG, BF16),
                 sds(4 * SEG, BF16))
    out_specs = tuple(pl.BlockSpec((nb, s.shape[1] // (sx // t), s.shape[2]), lambda b, i: (b, i, 0)) for s in out_shape)
    return pl.pallas_call(
        _proj_kernel,
        grid=grid,
        in_specs=[tok(d), per_b, per_b, const((1, d)),
                  pl.BlockSpec((d, n_cols), lambda b, i: (0, 0), pipeline_mode=pl.Buffered(1)),
                  const((1, SEG)), const((1, SEG)), const((1, SEG)), const((1, SEG)),
                  const((NORM_GROUP, NORM_GROUP))],
        out_specs=out_specs,
        out_shape=out_shape,
        compiler_params=pltpu.CompilerParams(dimension_semantics=("arbitrary", "arbitrary"),
                                             vmem_limit_bytes=VMEM_LIMIT),
        name="in_proj",
    )(x, shift, scale, g_norm.reshape(1, d), w_in_bf, gqa, gka, gqb, gkb, pm)


def _bias_tile_kernel(off_ref, gen_ref, o_ref, *, tq, nk, qbase, kbase0, kstep, band, nvalid):
    v = pl.program_id(1)
    g = gen_ref[0, 0]
    x = jnp.broadcast_to(g, (tq, g.shape[-1]))
    y = pltpu.roll(x, 0, 1, stride=1, stride_axis=0)
    y = (y[:, :nk] - off_ref[pl.program_id(0)]) * LOG2E
    row = lax.broadcasted_iota(jnp.int32, (tq, nk), 0)
    col = lax.broadcasted_iota(jnp.int32, (tq, nk), 1)
    qc = (qbase + row) // CHUNK
    kc = (kbase0 - v * kstep + col) // CHUNK
    y = jnp.where(kc <= qc, y, NEG)
    if band:
        y = jnp.where(kc >= qc - BAND_CHUNKS, y, NEG)
    y = jnp.where(col < nvalid, y, NEG)
    o_ref[0, 0] = y


def _bias_tiles(table_fn, offset, n_heads, tq, nk, n_var, qbase, kbase0, kstep, band, nvalid):
    length = 1 << (tq + nk - 1).bit_length()
    assert length >= tq + nk - 1 and min(kbase0 - (n_var - 1) * kstep, qbase) >= 0
    gen = jnp.stack([jnp.concatenate([table_fn(kbase0 - v * kstep - qbase, nk),
                                      table_fn(kbase0 - v * kstep - qbase + nk - length, length - nk)], axis=0)
                     for v in range(n_var)])
    gen = jnp.transpose(gen, (2, 0, 1)).astype(F32).reshape(n_heads, n_var, 1, length)
    return pl.pallas_call(
        functools.partial(_bias_tile_kernel, tq=tq, nk=nk, qbase=qbase, kbase0=kbase0, kstep=kstep,
                          band=band, nvalid=nvalid),
        grid=(n_heads, n_var),
        in_specs=[pl.BlockSpec(memory_space=pltpu.SMEM),
                  pl.BlockSpec((1, 1, 1, length), lambda h, v: (h, v, 0, 0))],
        out_specs=pl.BlockSpec((1, 1, tq, nk), lambda h, v: (h, v, 0, 0)),
        out_shape=jax.ShapeDtypeStruct((n_heads, n_var, tq, nk), F32),
        name="bias_tiles",
    )(offset.astype(F32), gen)


def _split_halves(q):
    lane = lax.broadcasted_iota(jnp.int32, q.shape, 1)
    zero = jnp.zeros_like(q)
    return jnp.where(lane < LANES // 2, q, zero), jnp.where(lane >= LANES // 2, q, zero)


def _online_update(carry, s, v):
    m, l, acc = carry
    m_new = jnp.maximum(m, jnp.max(s, axis=-1, keepdims=True))
    alpha = jnp.exp2(m - m_new)
    p = jnp.exp2(s - m_new)
    l = alpha * l + jnp.sum(p, axis=-1, keepdims=True)
    acc = alpha * acc + jnp.dot(p.astype(BF16), v, preferred_element_type=F32)
    return m_new, l, acc


def _diff_init(tq):
    return (jnp.full((tq, 1), -jnp.inf, F32), jnp.zeros((tq, 1), F32), jnp.zeros((tq, LANES), F32))


def _diff_finish(c0, c1, lamv_ref, gsub_ref, g, lam_init):
    lamv = lamv_ref[...]
    e1 = jnp.exp(jnp.sum(lamv[0:1] * lamv[1:2], axis=-1, keepdims=True))
    e2 = jnp.exp(jnp.sum(lamv[2:3] * lamv[3:4], axis=-1, keepdims=True))
    lam = e1 - e2 + lam_init
    o = c0[2] / c0[1] - lam * (c1[2] / c1[1])
    o = o * lax.rsqrt(jnp.mean(o * o, axis=-1, keepdims=True) + EPS) * gsub_ref[...]
    o = o * (1.0 - lam_init)
    return o * _silu(g.astype(F32))


def _lane_partial_sum(p):
    out = p[:, :LANES]
    for c in range(1, p.shape[1] // LANES):
        out = out + p[:, c * LANES:(c + 1) * LANES]
    return out


def _attn_a_prompt_kernel(par_ref, q_ref, k_ref, v_ref, g_ref, bias_ref, lamv_ref, gsub_ref, o_ref, l_ref, acc_ref,
                          *, tq, lam_init):
    i = pl.program_id(2)
    has_near = i > 0
    n_far = jnp.maximum(i - 1, 0)
    near = pl.multiple_of(n_far * tq, tq)
    diag = pl.multiple_of(i * tq, tq)
    half = tq // 2
    bounded = par_ref[0] > 0.5
    n_heads = q_ref.shape[2] // LANES
    hs = [slice(hh * LANES, (hh + 1) * LANES) for hh in range(n_heads)]

    def finish_head(hh, c0, c1):
        o = _diff_finish(c0, c1, lamv_ref, gsub_ref, g_ref[0, :, hs[hh]], lam_init)
        o_ref[0, :, hs[hh]] = o.astype(o_ref.dtype)

    def run_online():
        for hh in range(n_heads):
            q0, q1 = _split_halves(q_ref[0, :, hs[hh]])

            def tile(carry, start, bias, mask=None):
                c0, c1 = carry
                k = k_ref[0, pl.ds(start, tq), hs[hh]]
                v = v_ref[0, pl.ds(start, tq), hs[hh]]
                s0, s1 = _nt(q0, k), _nt(q1, k)
                if bias is not None:
                    b = bias_ref[hh, 0, :, bias] if mask is None else bias_ref[hh, 0, :, bias] + mask
                    s0, s1 = s0 + b, s1 + b
                return _online_update(c0, s0, v), _online_update(c1, s1, v)

            init = _diff_init(tq)
            carry = lax.fori_loop(0, n_far, lambda j, c: tile(c, pl.multiple_of(j * tq, tq), None), (init, init))
            carry = tile(carry, near, slice(0, tq), jnp.where(has_near, 0.0, NEG))
            c0, c1 = tile(carry, diag, slice(tq, 2 * tq))
            finish_head(hh, c0, c1)

    def run_bounded():
        qs = [_split_halves(q_ref[0, :, sl]) for sl in hs]
        q_all = [jnp.concatenate(pair, axis=0) for pair in qs]

        def accumulate(hh, start, nk, rows=None, bias_cols=None, mask=None, init=False):
            k = k_ref[0, pl.ds(start, nk), hs[hh]]
            v = v_ref[0, pl.ds(start, nk), hs[hh]]
            if rows is None:
                rows, q2 = slice(0, tq), q_all[hh]
            else:
                q2 = jnp.concatenate([qs[hh][0][rows], qs[hh][1][rows]], axis=0)
            s = _nt(q2, k)
            if bias_cols is not None:
                bias = bias_ref[hh, 0, rows, bias_cols]
                if mask is not None:
                    bias = bias + mask
                s = s + jnp.concatenate([bias, bias], axis=0)
            p = jnp.exp2(s)
            dl = _lane_partial_sum(p)
            da = jnp.dot(p.astype(BF16), v, preferred_element_type=F32)
            r = q2.shape[0] // 2
            for mp in range(2):
                if init:
                    l_ref[hh, mp, rows, :] = dl[mp * r:(mp + 1) * r]
                    acc_ref[hh, mp, rows, :] = da[mp * r:(mp + 1) * r]
                else:
                    l_ref[hh, mp, rows, :] += dl[mp * r:(mp + 1) * r]
                    acc_ref[hh, mp, rows, :] += da[mp * r:(mp + 1) * r]

        near_mask = jnp.where(has_near, 0.0, NEG)
        for hh in range(n_heads):
            accumulate(hh, near, tq, bias_cols=slice(0, tq), mask=near_mask, init=True)

        def far(j, carry):
            for hh in range(n_heads):
                accumulate(hh, pl.multiple_of(j * tq, tq), tq)
            return carry

        lax.fori_loop(0, n_far, far, 0)
        for hh in range(n_heads):
            accumulate(hh, diag, half, slice(0, half), slice(tq, tq + half))
            accumulate(hh, diag, tq, slice(half, tq), slice(tq, 2 * tq))
            c0 = (None, jnp.sum(l_ref[hh, 0], axis=-1, keepdims=True), acc_ref[hh, 0])
            c1 = (None, jnp.sum(l_ref[hh, 1], axis=-1, keepdims=True), acc_ref[hh, 1])
            finish_head(hh, c0, c1)

    pl.when(bounded)(run_bounded)
    pl.when(jnp.logical_not(bounded))(run_online)


def _logits_bounded(g_q, g_k, head_dim, table, offset):
    qk = math.sqrt(head_dim) * LOG2E * jnp.max(jnp.abs(g_q)) * jnp.max(jnp.abs(g_k)) * BF16_ROUND_MARGIN
    bias = jnp.max(jnp.abs(table - offset[None, :])) * LOG2E
    return (qk + bias <= MAX_DIRECT_LOGIT).astype(F32).reshape(1)


def _attn_a_prompt(qa, kab, vab, ga, bias, bounded, lamv, gsub, tq, lam_init):
    b, s, _ = qa.shape
    assert bias.shape[1] == 1 and bias.shape[3] == 2 * tq
    hps = HEADS_PER_STEP_A
    tile = pl.BlockSpec((1, tq, hps * LANES), lambda bi, h, i: (bi, i, h))
    single = pl.Buffered(1)
    full = pl.BlockSpec((1, s, hps * LANES), lambda bi, h, i: (bi, 0, h))
    return pl.pallas_call(
        functools.partial(_attn_a_prompt_kernel, tq=tq, lam_init=lam_init),
        grid=(b, H_A // hps, s // tq),
        in_specs=[pl.BlockSpec(memory_space=pltpu.SMEM), tile, full, full, tile,
                  pl.BlockSpec((hps, 1, tq, 2 * tq), lambda bi, h, i: (h, 0, 0, 0), pipeline_mode=single),
                  pl.BlockSpec((4, HD_A), lambda bi, h, i: (0, 0)),
                  pl.BlockSpec((1, LANES), lambda bi, h, i: (0, 0))],
        out_specs=tile,
        out_shape=jax.ShapeDtypeStruct((b, s, W_A), BF16),
        scratch_shapes=[pltpu.VMEM((hps, 2, tq, LANES), F32), pltpu.VMEM((hps, 2, tq, LANES), F32)],
        compiler_params=pltpu.CompilerParams(dimension_semantics=("arbitrary", "arbitrary", "arbitrary"),
                                             vmem_limit_bytes=VMEM_LIMIT),
        name="diff_attn_prompt",
    )(bounded, qa, kab, vab, ga, bias, lamv, gsub)


def _attn_a_sample_kernel(q_ref, kc_ref, vc_ref, kn_ref, vn_ref, g_ref, bias_ref, lamv_ref, gsub_ref, o_ref, *, lam_init):
    tq = q_ref.shape[1]
    nk = bias_ref.shape[3]
    past = kc_ref.shape[1] // H_A
    pad = jnp.zeros((nk - past - tq, LANES), BF16)
    for hh in range(H_A):
        sl = slice(hh * LANES, (hh + 1) * LANES)
        rows = pl.ds(hh, past, stride=H_A)
        qst = jnp.concatenate(_split_halves(q_ref[0, :, sl]), axis=0)
        k = jnp.concatenate([kc_ref[0, rows, :].astype(BF16), kn_ref[0, :, sl], pad], axis=0)
        v = jnp.concatenate([vc_ref[0, rows, :].astype(BF16), vn_ref[0, :, sl], pad], axis=0)
        bias = bias_ref[hh, 0]
        c = _online_update(_diff_init(2 * tq), _nt(qst, k) + jnp.concatenate([bias, bias], axis=0), v)
        c0, c1 = tuple(x[:tq] for x in c), tuple(x[tq:] for x in c)
        o_ref[0, :, sl] = _diff_finish(c0, c1, lamv_ref, gsub_ref, g_ref[0, :, sl], lam_init).astype(o_ref.dtype)


def _band_pair(q, k, v, g, bias_ref, bounded, first_head, piece_masks):
    t = q.shape[0]
    qst = jnp.concatenate(_split_halves(q), axis=0)
    bias = jnp.concatenate([bias_ref[first_head, 0], bias_ref[first_head + 1, 0]], axis=0)
    w = bias.shape[1] // (len(piece_masks) + 1)
    bias = jnp.concatenate([bias[:, c * w:(c + 1) * w] + m for c, m in enumerate(piece_masks)]
                           + [bias[:, len(piece_masks) * w:]], axis=1)
    s = _nt(qst, k) + bias
    if not bounded:
        s = s - jnp.max(s, axis=-1, keepdims=True)
    p = jnp.exp2(s)
    l = jnp.sum(p, axis=-1, keepdims=True)
    o2 = jnp.dot(p.astype(BF16), v, preferred_element_type=F32) / l
    lane = lax.broadcasted_iota(jnp.int32, (t, LANES), 1)
    o = jnp.where(lane < LANES // 2, o2[:t], o2[t:])
    return o * _silu(g.astype(F32))


def _attn_b_prompt_kernel(par_ref, q_ref, k_ref, v_ref, g_ref, bias_ref, o_ref, *, tq, n_prev):
    i = pl.program_id(2)
    tiles = q_ref.shape[1] // tq
    bounded = par_ref[0] > 0.5

    def run(direct):
        for sub in range(tiles):
            first = tiles * i + sub - n_prev
            rows = slice(sub * tq, (sub + 1) * tq)
            starts = [pl.multiple_of(jnp.maximum(first + c, 0) * tq, tq) for c in range(n_prev + 1)]
            masks = [jnp.where(first + c >= 0, 0.0, NEG) for c in range(n_prev)]
            for pp in range(q_ref.shape[2] // LANES):
                sl = slice(pp * LANES, (pp + 1) * LANES)
                k = jnp.concatenate([k_ref[0, pl.ds(st, tq), sl] for st in starts], axis=0)
                v = jnp.concatenate([v_ref[0, pl.ds(st, tq), sl] for st in starts], axis=0)
                o = _band_pair(q_ref[0, rows, sl], k, v, g_ref[0, rows, sl], bias_ref, direct, 2 * pp, masks)
                o_ref[0, rows, sl] = o.astype(o_ref.dtype)

    pl.when(bounded)(lambda: run(True))
    pl.when(jnp.logical_not(bounded))(lambda: run(False))


def _attn_b_prompt(qb, kbb, vbb, gb, bias, bounded, tq):
    b, s, w = qb.shape
    n_prev = bias.shape[3] // tq - 1
    assert bias.shape[1] == 1 and bias.shape[3] == (n_prev + 1) * tq
    tps = TILES_PER_STEP_B
    tile = pl.BlockSpec((1, tps * tq, w), lambda bi, p, i: (bi, i, p))
    full = pl.BlockSpec((1, s, w), lambda bi, p, i: (bi, 0, p))
    return pl.pallas_call(
        functools.partial(_attn_b_prompt_kernel, tq=tq, n_prev=n_prev),
        grid=(b, 1, s // (tps * tq)),
        in_specs=[pl.BlockSpec(memory_space=pltpu.SMEM), tile, full, full, tile,
                  pl.BlockSpec((H_B, 1, tq, (n_prev + 1) * tq), lambda bi, p, i: (0, 0, 0, 0),
                               pipeline_mode=pl.Buffered(1))],
        out_specs=tile,
        out_shape=jax.ShapeDtypeStruct((b, s, W_B), BF16),
        compiler_params=pltpu.CompilerParams(dimension_semantics=("arbitrary", "arbitrary", "arbitrary"),
                                             vmem_limit_bytes=VMEM_LIMIT),
        name="band_attn_prompt",
    )(bounded, qb, kbb, vbb, gb, bias)


def _attn_b_sample_kernel(q_ref, kc_ref, vc_ref, kn_ref, vn_ref, g_ref, bias_ref, o_ref):
    nk = bias_ref.shape[3]
    past, t = kc_ref.shape[2], kn_ref.shape[1]
    pad = jnp.zeros((nk - past - t, LANES), BF16)
    lane = lax.broadcasted_iota(jnp.int32, (t, LANES), 1)
    for pp in range(q_ref.shape[2] // LANES):
        sl = slice(pp * LANES, (pp + 1) * LANES)
        qst = jnp.concatenate(_split_halves(q_ref[0, :, sl]), axis=0)
        k_new = jnp.concatenate([kn_ref[0, :, sl], pad], axis=0)
        v_new = jnp.concatenate([vn_ref[0, :, sl], pad], axis=0)
        s = jnp.concatenate([jnp.dot(qst, kc_ref[0, sl, :].astype(BF16), preferred_element_type=F32),
                             _nt(qst, k_new)], axis=1)
        s = s + jnp.concatenate([bias_ref[2 * pp, 0], bias_ref[2 * pp + 1, 0]], axis=0)
        p = jnp.exp2(s - jnp.max(s, axis=-1, keepdims=True))
        l = jnp.sum(p, axis=-1, keepdims=True)
        pb = p.astype(BF16)
        o2 = (_nt(pb[:, :past], vc_ref[0, sl, :].astype(BF16))
              + jnp.dot(pb[:, past:], v_new, preferred_element_type=F32)) / l
        o = jnp.where(lane < LANES // 2, o2[:t], o2[t:])
        o_ref[0, :, sl] = (o * _silu(g_ref[0, :, sl].astype(F32))).astype(o_ref.dtype)


N_A_SAMPLE_INPUTS = 9
N_B_SAMPLE_INPUTS = 7


def _attn_sample_kernel(*refs, lam_init):
    a_in = refs[:N_A_SAMPLE_INPUTS]
    b_in = refs[N_A_SAMPLE_INPUTS:N_A_SAMPLE_INPUTS + N_B_SAMPLE_INPUTS]
    oa_ref, ob_ref = refs[N_A_SAMPLE_INPUTS + N_B_SAMPLE_INPUTS:]
    _attn_a_sample_kernel(*a_in, oa_ref, lam_init=lam_init)
    _attn_b_sample_kernel(*b_in, ob_ref)


def _attn_sample(qa, kab, vab, ga, cache_a_k, cache_a_v, bias_a, lamv, gsub,
                 qb, kbb, vbb, gb, cache_b_kt, cache_b_vt, bias_b, lam_init):
    b, t, _ = qa.shape
    assert cache_b_kt.shape[2] % LANES == 0
    new = pl.BlockSpec((1, t, SEG), lambda bi: (bi, 0, 0))
    whole = lambda x: pl.BlockSpec((1,) + x.shape[1:], lambda bi: (bi,) + (0,) * (x.ndim - 1))
    const = lambda x: pl.BlockSpec(x.shape, lambda bi: (0,) * x.ndim)
    return pl.pallas_call(
        functools.partial(_attn_sample_kernel, lam_init=lam_init),
        grid=(b,),
        in_specs=[new, whole(cache_a_k), whole(cache_a_v), new, new, new, const(bias_a), const(lamv), const(gsub),
                  new, whole(cache_b_kt), whole(cache_b_vt), new, new, new, const(bias_b)],
        out_specs=(new, new),
        out_shape=(jax.ShapeDtypeStruct((b, t, W_A), BF16), jax.ShapeDtypeStruct((b, t, W_B), BF16)),
        name="attn_sample",
    )(qa, cache_a_k, cache_a_v, kab, vab, ga, bias_a, lamv, gsub, qb, cache_b_kt, cache_b_vt, kbb, vbb, gb, bias_b)


def _merge_kernel(x_ref, gate_ref, oa_ref, ob_ref, mg_ref, woa32_ref, wob32_ref, wout32_ref, o_ref,
                  woa_ref, wob_ref, wout_ref):
    nb, t, d = x_ref.shape
    rows = nb * t

    @pl.when((pl.program_id(0) == 0) & (pl.program_id(1) == 0))
    def _():
        woa_ref[...] = woa32_ref[...].astype(BF16)
        wob_ref[...] = wob32_ref[...].astype(BF16)
        wout_ref[...] = wout32_ref[...].astype(BF16)

    ya = jnp.dot(oa_ref[...].reshape(rows, W_A), woa_ref[...], preferred_element_type=F32)
    yb = jnp.dot(ob_ref[...].reshape(rows, W_B), wob_ref[...], preferred_element_type=F32)
    mg = mg_ref[...].reshape(rows, 2 * d).astype(F32)
    m = _sigmoid(mg[:, :d]) * ya + _sigmoid(mg[:, d:]) * yb
    y = jnp.dot(m.astype(BF16), wout_ref[...], preferred_element_type=F32)
    o_ref[...] = x_ref[...] + gate_ref[...] * y.reshape(nb, t, d)


def _merge(x, gate, oa, ob, mg, w_oa, w_ob, w_out, nb, t):
    bx, sx, d = x.shape
    tok = lambda w: pl.BlockSpec((nb, t, w), lambda b, i: (b, i, 0))
    const = lambda shp: pl.BlockSpec(shp, lambda b, i: (0,) * len(shp), pipeline_mode=pl.Buffered(1))
    return pl.pallas_call(
        _merge_kernel,
        grid=(bx // nb, sx // t),
        in_specs=[tok(d), pl.BlockSpec((nb, 1, d), lambda b, i: (b, 0, 0)), tok(W_A), tok(W_B), tok(2 * d),
                  const((W_A, d)), const((W_B, d)), const((d, d))],
        out_specs=tok(d),
        out_shape=jax.ShapeDtypeStruct((bx, sx, d), F32),
        scratch_shapes=[pltpu.VMEM((W_A, d), BF16), pltpu.VMEM((W_B, d), BF16), pltpu.VMEM((d, d), BF16)],
        compiler_params=pltpu.CompilerParams(dimension_semantics=("arbitrary", "arbitrary"),
                                             vmem_limit_bytes=VMEM_LIMIT),
        name="merge_out",
    )(x, gate, oa, ob, mg, w_oa, w_ob, w_out)


TQ_A = 512
HEADS_PER_STEP_A = 4
TQ_B = 256
PREV_TILES_B = BAND_PAST // TQ_B
TILES_PER_STEP_B = 2
TM_PROJ = 512
TM_MERGE = 1024
QBASE = 2048


def kernel(x_prompt, x_sample, cache_a_k, cache_a_v, cache_b_k, cache_b_v, c_prompt, c_sample, g_norm, w_ada, b_ada, w_in, g_qa, g_ka, lam_q1, lam_k1, lam_q2, lam_k2, g_subln, t5_bias, g_qb, g_kb, rel_bias_b, w_oa, w_ob, w_out):
    xp, xs = x_prompt, x_sample
    bp, s, d = xp.shape
    bs, t, _ = xs.shape
    depth = w_in.shape[0]
    past = cache_a_k.shape[2]
    lb = cache_b_k.shape[2]
    n_keep = min(BAND_PAST, s)
    assert s % TM_PROJ == 0 and s % TM_MERGE == 0 and s % TQ_A == 0 and s % TQ_B == 0 and (TQ_A // 2) % CHUNK == 0 and TQ_B % CHUNK == 0
    assert PREV_TILES_B * TQ_B >= BAND_PAST and s % (TILES_PER_STEP_B * TQ_B) == 0 and past >= lb
    assert _far_bucket(TQ_A + 1) == T5_BUCKETS // 2 - 1

    r = jnp.arange(NORM_GROUP)
    pm = ((r[:, None] // HD_A) == (r[None, :] // HD_A)).astype(BF16) * (1.0 / HD_A)
    c_rows = bp + bs
    c_pad = -(-c_rows // SUBLANES) * SUBLANES
    c_all = jnp.concatenate([c_prompt, c_sample, jnp.zeros((c_pad - c_rows, d), F32)], axis=0)

    nk_as = -(-(past + t) // LANES) * LANES
    nk_bs = -(-(lb + t) // LANES) * LANES
    t5_fn = functools.partial(_t5_run, t5_bias)
    outs = [[] for _ in range(8)]
    for l in range(depth):
        lam_init = 0.8 - 0.6 * math.exp(-0.3 * l)
        rel_fn = functools.partial(_clipped_run, rel_bias_b[l])
        cfar = t5_bias[_t5_bucket(jnp.int32(-(TQ_A + 1)))]
        bias_ap = _bias_tiles(t5_fn, cfar, H_A, TQ_A, 2 * TQ_A, 1, QBASE, QBASE - TQ_A, 0, False, 2 * TQ_A)
        bias_bp = _bias_tiles(rel_fn, jnp.zeros((H_B,)), H_B, TQ_B, (PREV_TILES_B + 1) * TQ_B, 1, QBASE,
                              QBASE - PREV_TILES_B * TQ_B, 0, True, (PREV_TILES_B + 1) * TQ_B)
        bias_as = _bias_tiles(t5_fn, jnp.zeros((H_A,)), H_A, t, nk_as, 1, past, 0, 0, False, past + t)
        bias_bs = _bias_tiles(rel_fn, jnp.zeros((H_B,)), H_B, t, nk_bs, 1, past, past - lb, 0, True, lb + t)

        mod = _modulation(c_all, w_ada[l], b_ada[l])
        shift = mod[:, :d].reshape(c_pad, 1, d)
        scale = mod[:, d:2 * d].reshape(c_pad, 1, d)
        gate = mod[:, 2 * d:].reshape(c_pad, 1, d)
        w_in_bf = w_in[l].astype(BF16)
        tile8 = lambda g: jnp.tile(g, SEG // g.shape[0]).reshape(1, SEG)
        gains = (tile8(g_qa[l]), tile8(g_ka[l]), tile8(g_qb[l]), tile8(g_kb[l]))
        lamv = jnp.stack([lam_q1[l], lam_k1[l], lam_q2[l], lam_k2[l]])
        gsub = g_subln[l].reshape(1, LANES)

        (qa, ka32, kab, va32, vab, ga, qb, kb32, kbb, vb32, vbb, gb, mg) = _project(
            xp, shift[:bp], scale[:bp], g_norm[l], w_in_bf, *gains, pm, 1, TM_PROJ)
        bounded_a = _logits_bounded(g_qa[l], g_ka[l], HD_A, t5_bias, cfar)
        bounded_b = _logits_bounded(g_qb[l], g_kb[l], HD_B, rel_bias_b[l], jnp.zeros((H_B,)))
        oa = _attn_a_prompt(qa, kab, vab, ga, bias_ap, bounded_a, lamv, gsub, TQ_A, lam_init)
        ob = _attn_b_prompt(qb, kbb, vbb, gb, bias_bp, bounded_b, TQ_B)
        xp = _merge(xp, gate[:bp], oa, ob, mg, w_oa[l], w_ob[l], w_out[l], 1, TM_MERGE)
        outs[0].append(ka32.reshape(bp, s, H_A, 2 * HD_A))
        outs[1].append(va32.reshape(bp, s, H_A, 2 * HD_A))
        outs[2].append(kb32[:, s - n_keep:].reshape(bp, n_keep, H_B, HD_B))
        outs[3].append(vb32[:, s - n_keep:].reshape(bp, n_keep, H_B, HD_B))

        (qa, ka32, kab, va32, vab, ga, qb, kb32, kbb, vb32, vbb, gb, mg) = _project(
            xs, shift[bp:c_rows], scale[bp:c_rows], g_norm[l], w_in_bf, *gains, pm, bs, t)
        by_head = lambda c: c.reshape(bs, past * H_A, LANES)
        to_rows = lambda c: jnp.transpose(c, (0, 2, 3, 1)).reshape(bs, W_B, lb)
        oa, ob = _attn_sample(qa, kab, vab, ga, by_head(cache_a_k[l]), by_head(cache_a_v[l]), bias_as, lamv, gsub,
                              qb, kbb, vbb, gb, to_rows(cache_b_k[l]), to_rows(cache_b_v[l]), bias_bs, lam_init)
        xs = _merge(xs, gate[bp:c_rows], oa, ob, mg, w_oa[l], w_ob[l], w_out[l], bs, t)
        outs[4].append(ka32.reshape(bs, t, H_A, 2 * HD_A))
        outs[5].append(va32.reshape(bs, t, H_A, 2 * HD_A))
        outs[6].append(kb32.reshape(bs, t, H_B, HD_B))
        outs[7].append(vb32.reshape(bs, t, H_B, HD_B))

    return (xp, xs) + tuple(jnp.stack(o) for o in outs)
```
